```python
import jax
import jax.numpy as jnp
from jax import lax
import numpy as np

D_MODEL = 2048
BATCH = 4
SEQ = 2048
DEPTH = 1

MIX_WIDTH = D_MODEL
M_HEADS = 4
M_DV = MIX_WIDTH // 2 // M_HEADS
M_DQK = M_DV // 2
M_CHUNK = 64
CONV_WIDTH = 4
A_HEADS = 16
A_DH = MIX_WIDTH // 2 // A_HEADS
DILATED_GROUPS = ((128, 1), (512, 4), (2048, 16))
N_EXPERTS = 32
TOP_K = 4
D_EXPERT = D_MODEL
SWIGLU_ALPHA = 1.702
SWIGLU_LIMIT = 7.0
MOE_BLOCK = 128
DEEPNORM_ALPHA = (2 * DEPTH) ** 0.25
DEEPNORM_BETA = (8 * DEPTH) ** -0.25
EPS = 1e-5
M_QK_W = 2 * M_HEADS * M_DQK
M_V_W = M_HEADS * M_DV
A_W = A_HEADS * A_DH
PROJ_DIM = M_QK_W + 2 * M_V_W + 2 * M_HEADS + 3 * A_W

kernel_name = 'hymba_mlstm_dilated_alibi_moe_deepnorm'


def layer_norm(x, g, b):
    xf = x.astype(jnp.float32)
    mu = xf.mean(-1, keepdims=True)
    var = jnp.square(xf - mu).mean(-1, keepdims=True)
    return ((xf - mu) * lax.rsqrt(var + EPS) * g + b).astype(x.dtype)


def head_rms_norm(t, g):
    tf = t.astype(jnp.float32)
    return tf * lax.rsqrt(jnp.square(tf).mean(-1, keepdims=True) + EPS) * g


def causal_depthwise_conv(t, w, b):
    C = t.shape[-1]
    y = lax.conv_general_dilated(
        t, w[:, None, :].astype(t.dtype), window_strides=(1,),
        padding=((CONV_WIDTH - 1, 0),), dimension_numbers=('NWC', 'WIO', 'NWC'),
        feature_group_count=C)
    return y + b


def mlstm_chunkwise(q, k, v, ig, fg):
    B, S, H, DQK = q.shape
    DV = v.shape[-1]
    nc = S // M_CHUNK
    f32 = jnp.float32

    def to_chunks(t):
        t = t.astype(f32).reshape((B, nc, M_CHUNK) + t.shape[2:])
        return jnp.moveaxis(jnp.moveaxis(t, 3, 2), 1, 0)

    qc = to_chunks(q)
    kc = to_chunks(k) * (DQK ** -0.5)
    vc = to_chunks(v)
    igc = to_chunks(ig)
    lfc = to_chunks(jax.nn.log_sigmoid(fg.astype(f32)))
    causal = jnp.tril(jnp.ones((M_CHUNK, M_CHUNK), dtype=bool))

    def step(carry, xs):
        C, n, m = carry
        qx, kx, vx, i_, lf = xs
        b = jnp.cumsum(lf, axis=-1)
        dmat = jnp.where(causal, b[..., :, None] - b[..., None, :] + i_[..., None, :], -jnp.inf)
        inter = b + m[..., None]
        m_t = jnp.maximum(inter, dmat.max(-1))
        a = jnp.exp(dmat - m_t[..., None]) * jnp.einsum('bhtd,bhsd->bhts', qx, kx)
        e_inter = jnp.exp(inter - m_t)
        num = jnp.einsum('bhts,bhsv->bhtv', a, vx) + e_inter[..., None] * jnp.einsum('bhvd,bhtd->bhtv', C, qx)
        den = a.sum(-1) + e_inter * jnp.einsum('bhd,bhtd->bht', n, qx)
        h = num / jnp.maximum(jnp.abs(den), jnp.exp(-m_t))[..., None]
        g = b[..., -1:] - b + i_
        m_new = jnp.maximum(b[..., -1] + m, g.max(-1))
        w_s = jnp.exp(g - m_new[..., None])
        decay = jnp.exp(b[..., -1] + m - m_new)
        C = decay[..., None, None] * C + jnp.einsum('bhs,bhsv,bhsd->bhvd', w_s, vx, kx)
        n = decay[..., None] * n + jnp.einsum('bhs,bhsd->bhd', w_s, kx)
        return (C, n, m_new), h

    init = (jnp.zeros((B, H, DV, DQK), f32), jnp.zeros((B, H, DQK), f32), jnp.zeros((B, H), f32))
    _, hs = lax.scan(step, init, (qc, kc, vc, igc, lfc))
    hs = jnp.moveaxis(jnp.moveaxis(hs, 0, 1), 2, 3)
    return hs.reshape(B, S, H, DV)


def dilated_window_group(q, k, v, slopes, window, dilation):
    B, H, S, Dh = q.shape
    L = S // dilation
    nk = window // dilation
    nb = -(-L // nk)
    Lp = nb * nk

    def to_blocks(t):
        t = t.reshape(B, H, L, dilation, Dh).transpose(0, 1, 3, 2, 4)
        t = jnp.pad(t, ((0, 0), (0, 0), (0, 0), (0, Lp - L), (0, 0)))
        return t.reshape(B, H, dilation, nb, nk, Dh)

    qb, kb, vb = to_blocks(q), to_blocks(k), to_blocks(v)

    def with_prev(t):
        prev = jnp.pad(t[:, :, :, :-1], ((0, 0), (0, 0), (0, 0), (1, 0), (0, 0), (0, 0)))
        return jnp.concatenate([prev, t], axis=4)

    kk, vv = with_prev(kb), with_prev(vb)
    s = jnp.einsum('bhrnqd,bhrnkd->bhrnqk', qb, kk) * (Dh ** -0.5)
    qi = jnp.arange(nk)[:, None]
    ki = jnp.arange(2 * nk)[None, :]
    dist = nk + qi - ki
    first = (jnp.arange(nb) == 0)[:, None, None]
    valid = (dist >= 0) & (dist <= nk) & ~(first & (ki < nk))
    alibi = -slopes[:, None, None, None, None] * (dist * dilation).astype(jnp.float32)
    s = jnp.where(valid, s + alibi, -jnp.inf)
    m = s.max(-1)
    p = jnp.exp(s - m[..., None])
    l = p.sum(-1)
    num = jnp.einsum('bhrnqk,bhrnkd->bhrnqd', p, vv)

    def from_blocks(t):
        t = t.reshape((B, H, dilation, Lp) + t.shape[5:])[:, :, :, :L]
        t = jnp.swapaxes(t, 2, 3)
        return t.reshape((B, H, S) + t.shape[4:])

    return from_blocks(num), from_blocks(m), from_blocks(l)


def dilated_attention(q, k, v):
    H = q.shape[1]
    slopes = jnp.exp2(-8.0 * jnp.arange(1, H + 1, dtype=jnp.float32) / H)
    nums, ms, ls = [], [], []
    for window, dilation in DILATED_GROUPS:
        num, m, l = dilated_window_group(q, k, v, slopes, window, dilation)
        nums.append(num)
        ms.append(m)
        ls.append(l)
    m_all = jnp.stack(ms)
    w = jnp.exp(m_all - m_all.max(0))
    num = (w[..., None] * jnp.stack(nums)).sum(0)
    den = (w * jnp.stack(ls)).sum(0)
    return num / den[..., None]


def token_mixer(h, w_in, b_in, conv_w, conv_b, m_norm_g, a_norm_g, w_out):
    B, S, _ = h.shape
    proj = h @ w_in + b_in
    cuts = [int(i) for i in np.cumsum([M_QK_W, M_V_W, M_V_W, M_HEADS, M_HEADS, A_W, A_W])]
    qk_m, v_m, o_m, ig, fg, q_a, k_a, v_a = jnp.split(proj, cuts, axis=-1)

    qk_m = jax.nn.silu(causal_depthwise_conv(qk_m, conv_w, conv_b))
    q_m, k_m = jnp.split(qk_m, 2, axis=-1)
    h_m = mlstm_chunkwise(q_m.reshape(B, S, M_HEADS, M_DQK), k_m.reshape(B, S, M_HEADS, M_DQK),
                          v_m.reshape(B, S, M_HEADS, M_DV), ig, fg)
    h_m = head_rms_norm(h_m, m_norm_g.reshape(M_HEADS, M_DV)) * jax.nn.sigmoid(
        o_m.astype(jnp.float32)).reshape(B, S, M_HEADS, M_DV)
    y_m = h_m.reshape(B, S, M_V_W).astype(h.dtype)

    def heads(t):
        return t.reshape(B, S, A_HEADS, A_DH).transpose(0, 2, 1, 3).astype(jnp.float32)
    o_a = dilated_attention(heads(q_a), heads(k_a), heads(v_a))
    y_a = head_rms_norm(o_a.transpose(0, 2, 1, 3), a_norm_g.reshape(A_HEADS, A_DH))
    y_a = y_a.reshape(B, S, A_W).astype(h.dtype)

    return jnp.concatenate([y_m, y_a], axis=-1) @ w_out


def moe(h, router_w, router_b, w_gu, b_gu, w_dn, b_dn):
    B, S, D = h.shape
    T = B * S
    TK = T * TOP_K
    xt = h.reshape(T, D)
    logits = (xt @ router_w + router_b).astype(jnp.float32)
    top_val, top_idx = lax.top_k(logits, TOP_K)
    gates = jax.nn.softmax(top_val, axis=-1)
    e_flat = top_idx.reshape(TK).astype(jnp.int32)
    order = jnp.argsort(e_flat)
    e_sorted = e_flat[order]
    tok_sorted = (order // TOP_K).astype(jnp.int32)
    gate_sorted = gates.reshape(TK)[order].astype(h.dtype)
    counts = jnp.bincount(e_flat, length=N_EXPERTS)
    starts = jnp.cumsum(counts) - counts
    padded = ((counts + MOE_BLOCK - 1) // MOE_BLOCK) * MOE_BLOCK
    pad_end = jnp.cumsum(padded)
    pad_start = pad_end - padded
    dest = pad_start[e_sorted] + jnp.arange(TK, dtype=jnp.int32) - starts[e_sorted]
    R = TK + N_EXPERTS * MOE_BLOCK
    x_pad = jnp.zeros((R, D), h.dtype).at[dest].set(xt[tok_sorted])
    tok_pad = jnp.zeros((R,), jnp.int32).at[dest].set(tok_sorted)
    gate_pad = jnp.zeros((R,), h.dtype).at[dest].set(gate_sorted)
    n_blocks = R // MOE_BLOCK
    block_expert = jnp.minimum(
        jnp.searchsorted(pad_end, jnp.arange(n_blocks, dtype=jnp.int32) * MOE_BLOCK, side='right'),
        N_EXPERTS - 1)

    def expert_block(args):
        xb, e = args
        gu = xb @ w_gu[e] + b_gu[e]
        x_glu = jnp.minimum(gu[:, 0::2], SWIGLU_LIMIT)
        x_lin = jnp.clip(gu[:, 1::2], -SWIGLU_LIMIT, SWIGLU_LIMIT)
        act = x_glu * jax.nn.sigmoid(SWIGLU_ALPHA * x_glu) * (x_lin + 1)
        return act @ w_dn[e] + b_dn[e]

    y_pad = lax.map(expert_block, (x_pad.reshape(n_blocks, MOE_BLOCK, D), block_expert))
    y_pad = y_pad.reshape(R, D) * gate_pad[:, None]
    return jax.ops.segment_sum(y_pad, tok_pad, num_segments=T).reshape(B, S, D)


def setup_inputs(seed: int = 0) -> dict:
    key = jax.random.key(seed)
    ks = jax.random.split(key, 21)
    f32 = jnp.float32

    def nrm(k, shape, scale):
        return jax.random.normal(k, shape, f32) * scale

    x = nrm(ks[0], (BATCH, SEQ, D_MODEL), 1.0)
    c = nrm(ks[1], (BATCH, D_MODEL), 1.0)
    w_ada = nrm(ks[2], (DEPTH, D_MODEL, 6 * D_MODEL), 0.1 * D_MODEL ** -0.5)
    b_ada = nrm(ks[3], (DEPTH, 6 * D_MODEL), 0.02)
    col_scale = jnp.concatenate([
        jnp.ones((M_QK_W,), f32), jnp.full((M_V_W,), DEEPNORM_BETA, f32),
        jnp.ones((M_V_W + 2 * M_HEADS + 2 * A_W,), f32), jnp.full((A_W,), DEEPNORM_BETA, f32)])
    w_in = nrm(ks[4], (DEPTH, D_MODEL, PROJ_DIM), D_MODEL ** -0.5) * col_scale
    fg_start = M_QK_W + 2 * M_V_W + M_HEADS
    b_in = nrm(ks[5], (DEPTH, PROJ_DIM), 0.02)
    b_in = b_in.at[:, fg_start:fg_start + M_HEADS].add(jnp.linspace(3.0, 6.0, M_HEADS, dtype=f32))
    conv_w = nrm(ks[6], (DEPTH, CONV_WIDTH, M_QK_W), CONV_WIDTH ** -0.5)
    conv_b = nrm(ks[7], (DEPTH, M_QK_W), 0.02)
    m_norm_g = 1.0 + nrm(ks[8], (DEPTH, M_V_W), 0.02)
    a_norm_g = 1.0 + nrm(ks[9], (DEPTH, A_W), 0.02)
    w_out = nrm(ks[10], (DEPTH, MIX_WIDTH, D_MODEL), DEEPNORM_BETA * MIX_WIDTH ** -0.5)
    ln1_g = 1.0 + nrm(ks[11], (DEPTH, D_MODEL), 0.02)
    ln1_b = nrm(ks[12], (DEPTH, D_MODEL), 0.02)
    router_w = nrm(ks[13], (DEPTH, D_MODEL, N_EXPERTS), D_MODEL ** -0.5)
    router_b = nrm(ks[14], (DEPTH, N_EXPERTS), 0.01)
    w_gu = nrm(ks[15], (DEPTH, N_EXPERTS, D_MODEL, 2 * D_EXPERT), D_MODEL ** -0.5)
    b_gu = nrm(ks[16], (DEPTH, N_EXPERTS, 2 * D_EXPERT), 0.02)
    w_dn = nrm(ks[17], (DEPTH, N_EXPERTS, D_EXPERT, D_MODEL), DEEPNORM_BETA * D_EXPERT ** -0.5)
    b_dn = nrm(ks[18], (DEPTH, N_EXPERTS, D_MODEL), 0.02)
    ln2_g = 1.0 + nrm(ks[19], (DEPTH, D_MODEL), 0.02)
    ln2_b = nrm(ks[20], (DEPTH, D_MODEL), 0.02)
    return {'x': x, 'c': c, 'w_ada': w_ada, 'b_ada': b_ada, 'w_in': w_in, 'b_in': b_in,
            'conv_w': conv_w, 'conv_b': conv_b, 'm_norm_g': m_norm_g, 'a_norm_g': a_norm_g,
            'w_out': w_out, 'ln1_g': ln1_g, 'ln1_b': ln1_b, 'router_w': router_w,
            'router_b': router_b, 'w_gu': w_gu, 'b_gu': b_gu, 'w_dn': w_dn, 'b_dn': b_dn,
            'ln2_g': ln2_g, 'ln2_b': ln2_b}


def reference(x, c, w_ada, b_ada, w_in, b_in, conv_w, conv_b, m_norm_g, a_norm_g, w_out,
              ln1_g, ln1_b, router_w, router_b, w_gu, b_gu, w_dn, b_dn, ln2_g, ln2_b):
    cond = jax.nn.silu(c)
    for l in range(DEPTH):
        mod = cond @ w_ada[l] + b_ada[l]
        sh1, sc1, g1, sh2, sc2, g2 = jnp.split(mod[:, None, :], 6, axis=-1)
        h = x * (1 + sc1) + sh1
        y = token_mixer(h, w_in[l], b_in[l], conv_w[l], conv_b[l], m_norm_g[l], a_norm_g[l], w_out[l])
        x = layer_norm(DEEPNORM_ALPHA * x + (1 + g1) * y, ln1_g[l], ln1_b[l])
        h = x * (1 + sc2) + sh2
        y = moe(h, router_w[l], router_b[l], w_gu[l], b_gu[l], w_dn[l], b_dn[l])
        x = layer_norm(DEEPNORM_ALPHA * x + (1 + g2) * y, ln2_g[l], ln2_b[l])
    return x
```

```python
import functools

import jax
import jax.numpy as jnp
import numpy as np
from jax import lax
from jax.experimental import pallas as pl
from jax.experimental.pallas import tpu as pltpu

F32 = jnp.float32
BF16 = jnp.bfloat16
I32 = jnp.int32

M_HEADS = 4
M_DQK = 128
M_DV = 256
CONV_WIDTH = 4
A_HEADS = 16
A_DH = 64
ATTN_BLOCK = 128
DILATIONS = (1, 4, 16)
N_EXPERTS = 32
TOP_K = 4
SWIGLU_ALPHA = 1.702
SWIGLU_LIMIT = 7.0
EPS = 1e-5

LANES = 128
VMEM_LIMIT = 56 * 1024 * 1024

MLSTM_CHUNK = 256
MOE_ROW_PAD = 256
MOE_CHUNK = 256
MOE_ITEM_ROWS = 1280
MOE_TN = 512
COMBINE_TOKENS = 128


def _cparams(sem, vmem=VMEM_LIMIT):
    return pltpu.CompilerParams(dimension_semantics=sem, vmem_limit_bytes=vmem)


def _sigmoid(x):
    return 1.0 / (1.0 + jnp.exp(-x))


def _log_sigmoid(x):
    return jnp.minimum(x, 0.0) - jnp.log(1.0 + jnp.exp(-jnp.abs(x)))


def _layer_norm(z, g, b):
    mu = jnp.mean(z, axis=-1, keepdims=True)
    zc = z - mu
    var = jnp.mean(zc * zc, axis=-1, keepdims=True)
    return zc * lax.rsqrt(var + EPS) * g + b


def _dot(a, b):
    return jnp.dot(a, b, preferred_element_type=F32)


def _dot_nt(a, b):
    return lax.dot_general(a, b, (((1,), (1,)), ((), ())), preferred_element_type=F32)


def _dot_hilo(a, sel):
    hi = a.astype(BF16)
    lo = (a - hi.astype(F32)).astype(BF16)
    return _dot(hi, sel) + _dot(lo, sel)


def _ada_kernel(c_ref, w_ref, b_ref, o_ref):
    c = c_ref[...]
    cond = c * _sigmoid(c)
    o_ref[...] = _dot(cond.astype(BF16), w_ref[...].astype(BF16)) + b_ref[...]


def _ada_mod(c, w_ada, b_ada):
    bsz, d = c.shape
    n = w_ada.shape[1]
    tn = 1024
    rows = 8
    c_pad = jnp.zeros((rows, d), F32).at[:bsz].set(c)
    out = pl.pallas_call(
        _ada_kernel,
        grid=(n // tn,),
        in_specs=[pl.BlockSpec((rows, d), lambda j: (0, 0)),
                  pl.BlockSpec((d, tn), lambda j: (0, j)),
                  pl.BlockSpec((1, tn), lambda j: (0, j))],
        out_specs=pl.BlockSpec((rows, tn), lambda j: (0, j)),
        out_shape=jax.ShapeDtypeStruct((rows, n), F32),
        compiler_params=_cparams(("arbitrary",)),
        name="ada_mod",
    )(c_pad, w_ada, b_ada.reshape(1, n))
    return out[:bsz]


def _inproj_kernel(x_ref, sc_ref, sh_ref, w_ref, b_ref, wg_ref, bg_ref, o_ref, g_ref, h_ref):
    j = pl.program_id(1)

    @pl.when(j == 0)
    def _():
        h = x_ref[...] * (1.0 + sc_ref[0]) + sh_ref[0]
        hb = h.astype(BF16)
        h_ref[...] = hb
        g_ref[...] = _dot(hb, wg_ref[...]) + bg_ref[...]

    o_ref[...] = (_dot(h_ref[...], w_ref[...]) + b_ref[...]).astype(BF16)


def _in_proj(x2, sc, sh, w_main, b_main, w_gate, b_gate, seq):
    t, d = x2.shape
    n = w_main.shape[1]
    tm, tn = 512, 1024
    per_b = seq // tm
    return pl.pallas_call(
        _inproj_kernel,
        grid=(t // tm, n // tn),
        in_specs=[pl.BlockSpec((tm, d), lambda i, j: (i, 0)),
                  pl.BlockSpec((1, 1, d), lambda i, j: (i // per_b, 0, 0)),
                  pl.BlockSpec((1, 1, d), lambda i, j: (i // per_b, 0, 0)),
                  pl.BlockSpec((d, tn), lambda i, j: (0, j)),
                  pl.BlockSpec((1, tn), lambda i, j: (0, j)),
                  pl.BlockSpec((d, LANES), lambda i, j: (0, 0)),
                  pl.BlockSpec((1, LANES), lambda i, j: (0, 0))],
        out_specs=[pl.BlockSpec((tm, tn), lambda i, j: (i, j)),
                   pl.BlockSpec((tm, LANES), lambda i, j: (i, 0))],
        out_shape=[jax.ShapeDtypeStruct((t, n), BF16),
                   jax.ShapeDtypeStruct((t, LANES), F32)],
        scratch_shapes=[pltpu.VMEM((tm, d), BF16)],
        compiler_params=_cparams(("arbitrary", "arbitrary")),
        name="in_proj",
    )(x2, sc, sh, w_main, b_main, w_gate, b_gate)


def _mlstm_kernel(qk_ref, v_ref, o_ref, gc_ref, gr_ref, cw_ref, cb_ref, ng_ref, out_ref,
                  ct_ref, n_ref, m_ref, prev_ref):
    c = pl.program_id(1)
    L = MLSTM_CHUNK

    @pl.when(c == 0)
    def _():
        ct_ref[...] = jnp.zeros_like(ct_ref)
        n_ref[...] = jnp.zeros_like(n_ref)
        m_ref[...] = jnp.zeros_like(m_ref)
        prev_ref[...] = jnp.zeros_like(prev_ref)

    x = qk_ref[...].astype(F32)
    prev = prev_ref[...]
    row = lax.broadcasted_iota(I32, (L, 1), 0)
    y = cw_ref[CONV_WIDTH - 1:CONV_WIDTH, :] * x + cb_ref[...]
    for k in range(1, CONV_WIDTH):
        xs = jnp.where(row < k, pltpu.roll(prev, k, 0), pltpu.roll(x, k, 0))
        y = y + cw_ref[CONV_WIDTH - 1 - k:CONV_WIDTH - k, :] * xs
    prev_ref[...] = x
    y = y * _sigmoid(y)

    gc = gc_ref[...]
    gr = gr_ref[0]
    ti = lax.broadcasted_iota(I32, (L, L), 0)
    si = lax.broadcasted_iota(I32, (L, L), 1)
    causal = si <= ti
    tril = causal.astype(F32)
    triu = (ti <= si).astype(F32)
    b_cols = jnp.dot(tril, _log_sigmoid(gc), precision=lax.Precision.HIGHEST, preferred_element_type=F32)
    b_rows = jnp.dot(_log_sigmoid(gr), triu, precision=lax.Precision.HIGHEST, preferred_element_type=F32)

    qk_w = M_HEADS * M_DQK
    for h in range(M_HEADS):
        qf = y[:, h * M_DQK:(h + 1) * M_DQK]
        kf = y[:, qk_w + h * M_DQK:qk_w + (h + 1) * M_DQK] * (M_DQK ** -0.5)
        vb = v_ref[:, h * M_DV:(h + 1) * M_DV]
        qb = qf.astype(BF16)
        kb = kf.astype(BF16)

        bc = b_cols[:, M_HEADS + h:M_HEADS + h + 1]
        ic = gc[:, h:h + 1]
        br = b_rows[M_HEADS + h:M_HEADS + h + 1, :]
        ir = gr[h:h + 1, :]
        m_prev = m_ref[h][0:1, 0:1]
        n_prev = n_ref[h][0:1, :]

        dm = jnp.where(causal, bc - br + ir, -jnp.inf)
        inter = bc + m_prev
        mt = jnp.maximum(inter, jnp.max(dm, axis=1, keepdims=True))
        a = jnp.exp(dm - mt) * _dot_nt(qb, kb)
        e_int = jnp.exp(inter - mt)
        num = _dot(a.astype(BF16), vb) + e_int * _dot(qb, ct_ref[h].astype(BF16))
        den = jnp.sum(a, axis=1, keepdims=True) + e_int * jnp.sum(qf * n_prev, axis=1, keepdims=True)
        hh = num / jnp.maximum(jnp.abs(den), jnp.exp(-mt))

        b_last = bc[L - 1:L, :]
        g_col = b_last - bc + ic
        m_new = jnp.maximum(b_last + m_prev, jnp.max(g_col, axis=0, keepdims=True))
        w_col = jnp.exp(g_col - m_new)
        decay = jnp.exp(b_last + m_prev - m_new)
        wv = (w_col * vb.astype(F32)).astype(BF16)
        ct_ref[h] = decay * ct_ref[h] + _dot(kf.T.astype(BF16), wv)
        n_new = decay * n_prev + jnp.sum(w_col * kf, axis=0, keepdims=True)
        n_ref[h] = jnp.broadcast_to(n_new, n_ref.shape[1:])
        m_ref[h] = jnp.broadcast_to(m_new, m_ref.shape[1:])

        ms = jnp.mean(hh * hh, axis=1, keepdims=True)
        og = o_ref[:, h * M_DV:(h + 1) * M_DV].astype(F32)
        yh = hh * lax.rsqrt(ms + EPS) * ng_ref[:, h * M_DV:(h + 1) * M_DV] * _sigmoid(og)
        out_ref[:, h * M_DV:(h + 1) * M_DV] = yh.astype(BF16)


def _mlstm(proj, gates_c, gates_r, conv_w, conv_b, norm_g, bsz, seq):
    L = MLSTM_CHUNK
    nc = seq // L
    t = bsz * seq
    w = M_HEADS * M_DV
    return pl.pallas_call(
        _mlstm_kernel,
        grid=(bsz, nc),
        in_specs=[pl.BlockSpec((L, w), lambda b, c: (b * nc + c, 0)),
                  pl.BlockSpec((L, w), lambda b, c: (b * nc + c, 1)),
                  pl.BlockSpec((L, w), lambda b, c: (b * nc + c, 2)),
                  pl.BlockSpec((L, LANES), lambda b, c: (b * nc + c, 0)),
                  pl.BlockSpec((1, 8, L), lambda b, c: (b, 0, c)),
                  pl.BlockSpec((CONV_WIDTH, w), lambda b, c: (0, 0)),
                  pl.BlockSpec((1, w), lambda b, c: (0, 0)),
                  pl.BlockSpec((1, w), lambda b, c: (0, 0))],
        out_specs=pl.BlockSpec((L, w), lambda b, c: (b * nc + c, 0)),
        out_shape=jax.ShapeDtypeStruct((t, w), BF16),
        scratch_shapes=[pltpu.VMEM((M_HEADS, M_DQK, M_DV), F32),
                        pltpu.VMEM((M_HEADS, 8, M_DQK), F32),
                        pltpu.VMEM((M_HEADS, 8, LANES), F32),
                        pltpu.VMEM((L, w), F32)],
        compiler_params=_cparams(("arbitrary", "arbitrary")),
        name="mlstm",
    )(proj, proj, proj, gates_c, gates_r, conv_w, conv_b, norm_g)


def _attn_kernel(dilation, has_prev, *refs):
    if has_prev:
        q_ref, kc_ref, vc_ref, kp_ref, vp_ref, o_ref, lse_ref = refs
    else:
        q_ref, kc_ref, vc_ref, o_ref, lse_ref = refs
    n = pl.program_id(2)
    nq = ATTN_BLOCK
    qi = lax.broadcasted_iota(I32, (nq, nq), 0)
    ki = lax.broadcasted_iota(I32, (nq, nq), 1)
    d_cur = (qi - ki).astype(F32)
    ok_cur = ki <= qi
    if has_prev:
        d_prev = d_cur + float(nq)
        ok_prev = jnp.logical_and(ki >= qi, n > 0)
    lane = lax.broadcasted_iota(I32, (nq, LANES), 1)
    lse_tile = jnp.zeros((nq, LANES), F32)
    for h in range(A_HEADS):
        sl = slice(h * A_DH, (h + 1) * A_DH)
        coef = -(2.0 ** (-8.0 * (h + 1) / A_HEADS)) * dilation
        qh = q_ref[0, 0, :, sl] * (A_DH ** -0.5)
        s_cur = jnp.where(ok_cur, _dot_nt(qh, kc_ref[0, 0, :, sl]) + d_cur * coef, -jnp.inf)
        m = jnp.max(s_cur, axis=1, keepdims=True)
        if has_prev:
            s_prev = jnp.where(ok_prev, _dot_nt(qh, kp_ref[0, 0, :, sl]) + d_prev * coef, -jnp.inf)
            m = jnp.maximum(m, jnp.max(s_prev, axis=1, keepdims=True))
        p_cur = jnp.exp(s_cur - m)
        l = jnp.sum(p_cur, axis=1, keepdims=True)
        num = _dot(p_cur.astype(BF16), vc_ref[0, 0, :, sl])
        if has_prev:
            p_prev = jnp.exp(s_prev - m)
            l = l + jnp.sum(p_prev, axis=1, keepdims=True)
            num = num + _dot(p_prev.astype(BF16), vp_ref[0, 0, :, sl])
        o_ref[0, 0, :, sl] = (num / l).astype(BF16)
        lse_tile = jnp.where(lane == h, m + jnp.log(l), lse_tile)
    lse_ref[0, 0] = lse_tile


def _attn_group(qkv, dilation):
    bsz, d, ls, _ = qkv.shape
    aw = A_HEADS * A_DH
    nq = ATTN_BLOCK
    nb = ls // nq
    has_prev = nb > 1
    blk = (1, 1, nq, aw)
    in_specs = [pl.BlockSpec(blk, lambda b, r, n: (b, r, n, 0)),
                pl.BlockSpec(blk, lambda b, r, n: (b, r, n, 1)),
                pl.BlockSpec(blk, lambda b, r, n: (b, r, n, 2))]
    args = [qkv, qkv, qkv]
    if has_prev:
        in_specs += [pl.BlockSpec(blk, lambda b, r, n: (b, r, jnp.maximum(n - 1, 0), 1)),
                     pl.BlockSpec(blk, lambda b, r, n: (b, r, jnp.maximum(n - 1, 0), 2))]
        args += [qkv, qkv]
    return pl.pallas_call(
        functools.partial(_attn_kernel, dilation, has_prev),
        grid=(bsz, d, nb),
        in_specs=in_specs,
        out_specs=[pl.BlockSpec(blk, lambda b, r, n: (b, r, n, 0)),
                   pl.BlockSpec((1, 1, nq, LANES), lambda b, r, n: (b, r, n, 0))],
        out_shape=[jax.ShapeDtypeStruct((bsz, d, ls, aw), BF16),
                   jax.ShapeDtypeStruct((bsz, d, ls, LANES), F32)],
        compiler_params=_cparams(("arbitrary", "arbitrary", "arbitrary")),
        name=f"dilated_attn_d{dilation}",
    )(*args)


def _head_maps(n_heads, dh):
    w = n_heads * dh
    e = np.zeros((LANES, w), np.float32)
    for h in range(n_heads):
        e[h, h * dh:(h + 1) * dh] = 1.0
    return jnp.asarray(e, BF16), jnp.asarray(e.T.copy(), BF16)


def _merge_kernel(o1_ref, o2_ref, o3_ref, l1_ref, l2_ref, l3_ref, g_ref, e_ref, p_ref, y_ref):
    l1, l2, l3 = l1_ref[...], l2_ref[...], l3_ref[...]
    mx = jnp.maximum(jnp.maximum(l1, l2), l3)
    w1, w2, w3 = jnp.exp(l1 - mx), jnp.exp(l2 - mx), jnp.exp(l3 - mx)
    inv = 1.0 / (w1 + w2 + w3)
    e = e_ref[...]
    o = (_dot_hilo(w1 * inv, e) * o1_ref[...].astype(F32)
         + _dot_hilo(w2 * inv, e) * o2_ref[...].astype(F32)
         + _dot_hilo(w3 * inv, e) * o3_ref[...].astype(F32))
    ms = _dot_hilo(o * o, p_ref[...]) * (1.0 / A_DH)
    scale = _dot_hilo(lax.rsqrt(ms + EPS), e)
    y_ref[...] = (o * scale * g_ref[...]).astype(BF16)


def _attn_merge(outs, lses, norm_g):
    t, aw = outs[0].shape
    tm = 512
    expand, pool = _head_maps(A_HEADS, A_DH)
    row = lambda i: (i, 0)
    const = lambda i: (0, 0)
    return pl.pallas_call(
        _merge_kernel,
        grid=(t // tm,),
        in_specs=[pl.BlockSpec((tm, aw), row)] * 3 + [pl.BlockSpec((tm, LANES), row)] * 3
        + [pl.BlockSpec((1, aw), const), pl.BlockSpec((LANES, aw), const), pl.BlockSpec((aw, LANES), const)],
        out_specs=pl.BlockSpec((tm, aw), row),
        out_shape=jax.ShapeDtypeStruct((t, aw), BF16),
        compiler_params=_cparams(("arbitrary",)),
        name="attn_merge",
    )(*outs, *lses, norm_g, expand, pool)


def _outproj_kernel(alpha, ym_ref, ya_ref, w_ref, x_ref, g1_ref, sc_ref, sh_ref, lg_ref, lb_ref,
                    rw_ref, rb_ref, x1_ref, h2_ref, ti_ref, tg_ref):
    half = ym_ref.shape[1]
    y = _dot(ym_ref[...], w_ref[0:half, :]) + _dot(ya_ref[...], w_ref[half:2 * half, :])
    z = alpha * x_ref[...] + (1.0 + g1_ref[0]) * y
    x1 = _layer_norm(z, lg_ref[...], lb_ref[...])
    x1_ref[...] = x1
    h2 = x1 * (1.0 + sc_ref[0]) + sh_ref[0]
    h2_ref[...] = h2
    logits = _dot(h2.astype(BF16), rw_ref[...]) + rb_ref[...]
    lane = lax.broadcasted_iota(I32, logits.shape, 1)
    lane_f = lane.astype(F32)
    work = jnp.where(lane < N_EXPERTS, logits, -jnp.inf)
    idx_tile = jnp.zeros(logits.shape, F32)
    val_tile = jnp.zeros(logits.shape, F32)
    top = None
    denom = None
    for k in range(TOP_K):
        mk = jnp.max(work, axis=1, keepdims=True)
        ik = jnp.min(jnp.where(work == mk, lane_f, float(LANES)), axis=1, keepdims=True)
        work = jnp.where(lane_f == ik, -jnp.inf, work)
        if k == 0:
            top = mk
        ek = jnp.exp(mk - top)
        denom = ek if k == 0 else denom + ek
        idx_tile = jnp.where(lane == k, ik, idx_tile)
        val_tile = jnp.where(lane == k, ek, val_tile)
    ti_ref[...] = idx_tile.astype(I32)
    tg_ref[...] = val_tile / denom


def _out_proj(y_m, y_a, w_out, x2, g1, sc2, sh2, ln_g, ln_b, rw, rb, alpha, seq):
    t, d = x2.shape
    half = y_m.shape[1]
    tm = 256
    per_b = seq // tm
    row = lambda i: (i, 0)
    const = lambda i: (0, 0)
    mod = lambda i: (i // per_b, 0, 0)
    return pl.pallas_call(
        functools.partial(_outproj_kernel, alpha),
        grid=(t // tm,),
        in_specs=[pl.BlockSpec((tm, half), row), pl.BlockSpec((tm, half), row),
                  pl.BlockSpec((2 * half, d), const), pl.BlockSpec((tm, d), row),
                  pl.BlockSpec((1, 1, d), mod), pl.BlockSpec((1, 1, d), mod), pl.BlockSpec((1, 1, d), mod),
                  pl.BlockSpec((1, d), const), pl.BlockSpec((1, d), const),
                  pl.BlockSpec((d, LANES), const), pl.BlockSpec((1, LANES), const)],
        out_specs=[pl.BlockSpec((tm, d), row), pl.BlockSpec((tm, d), row),
                   pl.BlockSpec((tm, LANES), row), pl.BlockSpec((tm, LANES), row)],
        out_shape=[jax.ShapeDtypeStruct((t, d), F32), jax.ShapeDtypeStruct((t, d), F32),
                   jax.ShapeDtypeStruct((t, LANES), I32), jax.ShapeDtypeStruct((t, LANES), F32)],
        compiler_params=_cparams(("arbitrary",)),
        name="out_proj_ln_router",
    )(y_m, y_a, w_out, x2, g1, sc2, sh2, ln_g, ln_b, rw, rb)


def _expert_kernel(ie_ref, ir_ref, ic_ref, iv_ref, tok_hbm, h_hbm, wgu_ref, bgu_ref, wdn_ref, bdn_ref,
                   sel_ref, y_hbm, tok_smem, xbuf, yacc, wgu_b, wdn_b, sems):
    i = pl.program_id(0)
    j = pl.program_id(1)
    nj = pl.num_programs(1)
    row0 = ir_ref[i]
    nch = ic_ref[i]
    ch = MOE_CHUNK

    def row_copy(r):
        t = tok_smem[lax.shift_right_logical(r, 7), jnp.bitwise_and(r, LANES - 1)]
        return pltpu.make_async_copy(h_hbm.at[pl.ds(t, 1), :], xbuf.at[pl.ds(r, 1), :], sems.at[1])

    @pl.when(jnp.logical_and(j == 0, nch > 0))
    def _():
        tok_rows = pl.ds(lax.shift_right_logical(row0, 7), MOE_ITEM_ROWS // LANES)
        cp = pltpu.make_async_copy(tok_hbm.at[tok_rows, :], tok_smem, sems.at[0])
        cp.start()
        cp.wait()

        def issue(r, carry):
            row_copy(r).start()
            return carry

        lax.fori_loop(0, nch * ch, issue, 0)

        def drain(r, carry):
            row_copy(r).wait()
            return carry

        lax.fori_loop(0, nch * ch, drain, 0)

    @pl.when(nch > 0)
    def _():
        wgu_b[...] = wgu_ref[0].astype(BF16)
        wdn_b[...] = wdn_ref[0].astype(BF16)
        bgu = bgu_ref[0]
        last = j == nj - 1

        def chunk(m, carry):
            rows = pl.ds(pl.multiple_of(m * ch, ch), ch)
            gu = _dot(xbuf[rows, :].astype(BF16), wgu_b[...]) + bgu
            glu = jnp.minimum(gu, SWIGLU_LIMIT)
            f_glu = glu * _sigmoid(SWIGLU_ALPHA * glu)
            f_lin = jnp.clip(gu, -SWIGLU_LIMIT, SWIGLU_LIMIT) + 1.0
            prod = (pltpu.roll(f_glu, 1, 1) * f_lin).astype(BF16)
            parts = [_dot(prod[:, q * 2 * LANES:(q + 1) * 2 * LANES], sel_ref[...])
                     for q in range(MOE_TN // (2 * LANES))]
            act = jnp.concatenate(parts, axis=1).astype(BF16)
            yc = _dot(act, wdn_b[...])

            @pl.when(j == 0)
            def _():
                yacc[rows, :] = yc

            @pl.when(jnp.logical_and(j > 0, jnp.logical_not(last)))
            def _():
                yacc[rows, :] += yc

            @pl.when(jnp.logical_and(j > 0, last))
            def _():
                yacc[rows, :] += yc + bdn_ref[0]

            return carry

        lax.fori_loop(0, nch, chunk, 0)

        @pl.when(last)
        def _():
            def out_copy(m):
                rows = pl.ds(pl.multiple_of(m * ch, ch), ch)
                dst = pl.ds(pl.multiple_of(row0 + m * ch, ch), ch)
                return pltpu.make_async_copy(yacc.at[rows, :], y_hbm.at[dst, :], sems.at[2])

            def issue(m, carry):
                out_copy(m).start()
                return carry

            lax.fori_loop(0, nch, issue, 0)

            def drain(m, carry):
                out_copy(m).wait()
                return carry

            lax.fori_loop(0, nch, drain, 0)


def _experts(h2, tok_pad, item_e, item_row0, item_nch, item_valid, w_gu, b_gu, w_dn, b_dn, n_rows):
    ne, d, two_de = w_gu.shape
    tn = MOE_TN
    nj = two_de // tn
    assert nj > 1, "the kernel separates its first and last column-tile steps"
    ni = item_e.shape[0]
    sel = np.zeros((2 * LANES, LANES), np.float32)
    sel[2 * np.arange(LANES) + 1, np.arange(LANES)] = 1.0

    def jmap(i, j, iv):
        return jnp.where(iv[i] > 0, j, nj - 1)

    grid_spec = pltpu.PrefetchScalarGridSpec(
        num_scalar_prefetch=4,
        grid=(ni, nj),
        in_specs=[pl.BlockSpec(memory_space=pl.ANY),
                  pl.BlockSpec(memory_space=pl.ANY),
                  pl.BlockSpec((1, d, tn), lambda i, j, ie, ir, ic, iv: (ie[i], 0, jmap(i, j, iv))),
                  pl.BlockSpec((1, 1, tn), lambda i, j, ie, ir, ic, iv: (ie[i], 0, jmap(i, j, iv))),
                  pl.BlockSpec((1, tn // 2, d), lambda i, j, ie, ir, ic, iv: (ie[i], jmap(i, j, iv), 0)),
                  pl.BlockSpec((1, 1, d), lambda i, j, ie, ir, ic, iv: (ie[i], 0, 0)),
                  pl.BlockSpec((2 * LANES, LANES), lambda i, j, ie, ir, ic, iv: (0, 0))],
        out_specs=pl.BlockSpec(memory_space=pl.ANY),
        scratch_shapes=[pltpu.SMEM((MOE_ITEM_ROWS // LANES, LANES), I32),
                        pltpu.VMEM((MOE_ITEM_ROWS, d), F32),
                        pltpu.VMEM((MOE_ITEM_ROWS, d), F32),
                        pltpu.VMEM((d, tn), BF16),
                        pltpu.VMEM((tn // 2, d), BF16),
                        pltpu.SemaphoreType.DMA((3,))],
    )
    return pl.pallas_call(
        _expert_kernel,
        grid_spec=grid_spec,
        out_shape=jax.ShapeDtypeStruct((n_rows, d), F32),
        compiler_params=_cparams(("arbitrary", "arbitrary")),
        name="moe_experts",
    )(item_e, item_row0, item_nch, item_valid, tok_pad, h2, w_gu, b_gu.reshape(ne, 1, two_de),
      w_dn, b_dn.reshape(ne, 1, d), jnp.asarray(sel, BF16))


def _combine_kernel(alpha, dest_ref, y_hbm, gate_ref, x1_ref, g2_ref, lg_ref, lb_ref, out_ref, buf, sem):
    tc = COMBINE_TOKENS

    def row_copy(s):
        k = jnp.bitwise_and(s, TOP_K - 1)
        r = lax.shift_right_logical(s, 2)
        src = dest_ref[0, 0, s]
        return pltpu.make_async_copy(y_hbm.at[pl.ds(src, 1), :], buf.at[k, pl.ds(r, 1), :], sem.at[0])

    def issue(s, carry):
        row_copy(s).start()
        return carry

    lax.fori_loop(0, tc * TOP_K, issue, 0)

    def drain(s, carry):
        row_copy(s).wait()
        return carry

    lax.fori_loop(0, tc * TOP_K, drain, 0)

    gates = gate_ref[...]
    y = gates[:, 0:1] * buf[0]
    for k in range(1, TOP_K):
        y = y + gates[:, k:k + 1] * buf[k]
    z = alpha * x1_ref[...] + (1.0 + g2_ref[0]) * y
    out_ref[...] = _layer_norm(z, lg_ref[...], lb_ref[...])


def _combine(dest, y_pad, gates, x1, g2, ln_g, ln_b, alpha, seq):
    t, d = x1.shape
    tc = COMBINE_TOKENS
    per_b = seq // tc
    row = lambda i: (i, 0)
    const = lambda i: (0, 0)
    return pl.pallas_call(
        functools.partial(_combine_kernel, alpha),
        grid=(t // tc,),
        in_specs=[pl.BlockSpec((1, 1, tc * TOP_K), lambda i: (i, 0, 0), memory_space=pltpu.SMEM),
                  pl.BlockSpec(memory_space=pl.ANY),
                  pl.BlockSpec((tc, LANES), row), pl.BlockSpec((tc, d), row),
                  pl.BlockSpec((1, 1, d), lambda i: (i // per_b, 0, 0)),
                  pl.BlockSpec((1, d), const), pl.BlockSpec((1, d), const)],
        out_specs=pl.BlockSpec((tc, d), row),
        out_shape=jax.ShapeDtypeStruct((t, d), F32),
        scratch_shapes=[pltpu.VMEM((TOP_K, tc, d), F32), pltpu.SemaphoreType.DMA((1,))],
        compiler_params=_cparams(("arbitrary",)),
        name="moe_combine_ln",
    )(dest.reshape(t // tc, 1, tc * TOP_K), y_pad, gates, x1, g2, ln_g, ln_b)


def _routing_tables(top_idx):
    t = top_idx.shape[0]
    tk = t * TOP_K
    e_flat = top_idx.reshape(tk)
    order = jnp.argsort(e_flat).astype(I32)
    e_sorted = e_flat[order]
    tok_sorted = order // TOP_K
    counts = jnp.bincount(e_flat, length=N_EXPERTS).astype(I32)
    starts = jnp.cumsum(counts) - counts
    padded = ((counts + MOE_ROW_PAD - 1) // MOE_ROW_PAD) * MOE_ROW_PAD
    pad_end = jnp.cumsum(padded)
    pad_start = pad_end - padded
    dest_sorted = pad_start[e_sorted] + jnp.arange(tk, dtype=I32) - starts[e_sorted]
    n_rows = tk + N_EXPERTS * MOE_ROW_PAD
    tok_pad = jnp.zeros((n_rows + MOE_ITEM_ROWS,), I32).at[dest_sorted].set(tok_sorted).reshape(-1, LANES)
    dest = jnp.zeros((tk,), I32).at[order].set(dest_sorted)

    items_per = (padded + MOE_ITEM_ROWS - 1) // MOE_ITEM_ROWS
    item_end = jnp.cumsum(items_per)
    item_start = item_end - items_per
    n_items = N_EXPERTS + n_rows // MOE_ITEM_ROWS
    idx = jnp.arange(n_items, dtype=I32)
    valid = idx < item_end[-1]
    e_i = jnp.minimum(jnp.searchsorted(item_end, idx, side='right'), N_EXPERTS - 1).astype(I32)
    k_i = idx - item_start[e_i]
    row0 = pad_start[e_i] + k_i * MOE_ITEM_ROWS
    nrows = jnp.clip(padded[e_i] - k_i * MOE_ITEM_ROWS, 0, MOE_ITEM_ROWS)
    e_last = e_i[jnp.maximum(item_end[-1] - 1, 0)]
    item_e = jnp.where(valid, e_i, e_last).astype(I32)
    item_row0 = jnp.where(valid, row0, 0).astype(I32)
    item_nch = jnp.where(valid, nrows // MOE_CHUNK, 0).astype(I32)
    return tok_pad, dest, item_e, item_row0, item_nch, valid.astype(I32), n_rows


def kernel(x, c, w_ada, b_ada, w_in, b_in, conv_w, conv_b, m_norm_g, a_norm_g, w_out, ln1_g, ln1_b,
           router_w, router_b, w_gu, b_gu, w_dn, b_dn, ln2_g, ln2_b):
    bsz, seq, d = x.shape
    depth = w_ada.shape[0]
    t = bsz * seq
    alpha = float((2 * depth) ** 0.25)
    qk_w = 2 * M_HEADS * M_DQK
    mv_w = M_HEADS * M_DV
    aw = A_HEADS * A_DH
    gate_lo = qk_w + 2 * mv_w
    gate_hi = gate_lo + 2 * M_HEADS

    x2 = x.reshape(t, d)
    for l in range(depth):
        mod = _ada_mod(c, w_ada[l], b_ada[l]).reshape(bsz, 6, 1, d)
        sh1, sc1, g1, sh2, sc2, g2 = (mod[:, i] for i in range(6))

        w_main = jnp.concatenate([w_in[l][:, :gate_lo], w_in[l][:, gate_hi:]], axis=1).astype(BF16)
        b_main = jnp.concatenate([b_in[l][:gate_lo], b_in[l][gate_hi:]]).reshape(1, -1)
        w_gate = jnp.zeros((d, LANES), BF16).at[:, :2 * M_HEADS].set(w_in[l][:, gate_lo:gate_hi].astype(BF16))
        b_gate = jnp.zeros((1, LANES), F32).at[0, :2 * M_HEADS].set(b_in[l][gate_lo:gate_hi])
        proj, gates_c = _in_proj(x2, sc1, sh1, w_main, b_main, w_gate, b_gate, seq)

        gates_r = gates_c[:, :2 * M_HEADS].reshape(bsz, seq, 2 * M_HEADS).transpose(0, 2, 1)
        y_m = _mlstm(proj, gates_c, gates_r, conv_w[l], conv_b[l].reshape(1, -1),
                     m_norm_g[l].reshape(1, -1), bsz, seq)

        qkv = proj[:, gate_lo:].reshape(bsz, seq, 3 * aw)
        outs, lses = [], []
        for dil in DILATIONS:
            ls = seq // dil
            qkv_d = qkv.reshape(bsz, ls, dil, 3 * aw).transpose(0, 2, 1, 3)
            o_d, lse_d = _attn_group(qkv_d, dil)
            outs.append(o_d.transpose(0, 2, 1, 3).reshape(t, aw))
            lses.append(lse_d.transpose(0, 2, 1, 3).reshape(t, LANES))
        y_a = _attn_merge(outs, lses, a_norm_g[l].reshape(1, -1))

        rw = jnp.zeros((d, LANES), BF16).at[:, :N_EXPERTS].set(router_w[l].astype(BF16))
        rb = jnp.zeros((1, LANES), F32).at[0, :N_EXPERTS].set(router_b[l])
        x1, h2, top_idx, gates = _out_proj(y_m, y_a, w_out[l].astype(BF16), x2, g1, sc2, sh2,
                                           ln1_g[l].reshape(1, -1), ln1_b[l].reshape(1, -1), rw, rb, alpha, seq)

        tok_pad, dest, item_e, item_row0, item_nch, item_valid, n_rows = _routing_tables(top_idx[:, :TOP_K])
        y_pad = _experts(h2, tok_pad, item_e, item_row0, item_nch, item_valid,
                         w_gu[l], b_gu[l], w_dn[l], b_dn[l], n_rows)
        x2 = _combine(dest, y_pad, gates, x1, g2, ln2_g[l].reshape(1, -1), ln2_b[l].reshape(1, -1), alpha, seq)
    return x2.reshape(bsz, seq, d)
```

```python
import functools

import jax
import jax.numpy as jnp
import numpy as np
from jax import lax
from jax.experimental import pallas as pl
from jax.experimental.pallas import tpu as pltpu

F32 = jnp.float32
BF16 = jnp.bfloat16
I32 = jnp.int32

M_HEADS = 4
M_DQK = 128
M_DV = 256
CONV_WIDTH = 4
A_HEADS = 16
A_DH = 64
ATTN_BLOCK = 128
DILATIONS = (1, 4, 16)
N_EXPERTS = 32
TOP_K = 4
SWIGLU_ALPHA = 1.702
SWIGLU_LIMIT = 7.0
EPS = 1e-5

LANES = 128
VMEM_LIMIT = 56 * 1024 * 1024

MLSTM_CHUNK = 256
MOE_ROW_PAD = 256
MOE_CHUNK = 256
MOE_ITEM_ROWS = 1280
MOE_TN = 512
COMBINE_TOKENS = 128


def _cparams(sem, vmem=VMEM_LIMIT):
    return pltpu.CompilerParams(dimension_semantics=sem, vmem_limit_bytes=vmem)


def _sigmoid(x):
    return 1.0 / (1.0 + jnp.exp(-x))


def _log_sigmoid(x):
    return jnp.minimum(x, 0.0) - jnp.log(1.0 + jnp.exp(-jnp.abs(x)))


def _layer_norm(z, g, b):
    mu = jnp.mean(z, axis=-1, keepdims=True)
    zc = z - mu
    var = jnp.mean(zc * zc, axis=-1, keepdims=True)
    return zc * lax.rsqrt(var + EPS) * g + b


def _dot(a, b):
    return jnp.dot(a, b, preferred_element_type=F32)


def _dot_nt(a, b):
    return lax.dot_general(a, b, (((1,), (1,)), ((), ())), preferred_element_type=F32)


def _dot_hilo(a, sel):
    hi = a.astype(BF16)
    lo = (a - hi.astype(F32)).astype(BF16)
    return _dot(hi, sel) + _dot(lo, sel)


def _ada_kernel(c_ref, w_ref, b_ref, o_ref):
    c = c_ref[...]
    cond = c * _sigmoid(c)
    o_ref[...] = _dot(cond.astype(BF16), w_ref[...].astype(BF16)) + b_ref[...]


def _ada_mod(c, w_ada, b_ada):
    bsz, d = c.shape
    n = w_ada.shape[1]
    tn = 1024
    rows = 8
    c_pad = jnp.zeros((rows, d), F32).at[:bsz].set(c)
    out = pl.pallas_call(
        _ada_kernel,
        grid=(n // tn,),
        in_specs=[pl.BlockSpec((rows, d), lambda j: (0, 0)),
                  pl.BlockSpec((d, tn), lambda j: (0, j)),
                  pl.BlockSpec((1, tn), lambda j: (0, j))],
        out_specs=pl.BlockSpec((rows, tn), lambda j: (0, j)),
        out_shape=jax.ShapeDtypeStruct((rows, n), F32),
        compiler_params=_cparams(("arbitrary",)),
        name="ada_mod",
    )(c_pad, w_ada, b_ada.reshape(1, n))
    return out[:bsz]


def _inproj_kernel(x_ref, sc_ref, sh_ref, w_ref, b_ref, wg_ref, bg_ref, o_ref, g_ref, h_ref):
    j = pl.program_id(1)

    @pl.when(j == 0)
    def _():
        h = x_ref[...] * (1.0 + sc_ref[0]) + sh_ref[0]
        hb = h.astype(BF16)
        h_ref[...] = hb
        g_ref[...] = _dot(hb, wg_ref[...]) + bg_ref[...]

    o_ref[...] = (_dot(h_ref[...], w_ref[...]) + b_ref[...]).astype(BF16)


def _in_proj(x2, sc, sh, w_main, b_main, w_gate, b_gate, seq):
    t, d = x2.shape
    n = w_main.shape[1]
    tm, tn = 512, 1024
    per_b = seq // tm
    return pl.pallas_call(
        _inproj_kernel,
        grid=(t // tm, n // tn),
        in_specs=[pl.BlockSpec((tm, d), lambda i, j: (i, 0)),
                  pl.BlockSpec((1, 1, d), lambda i, j: (i // per_b, 0, 0)),
                  pl.BlockSpec((1, 1, d), lambda i, j: (i // per_b, 0, 0)),
                  pl.BlockSpec((d, tn), lambda i, j: (0, j)),
                  pl.BlockSpec((1, tn), lambda i, j: (0, j)),
                  pl.BlockSpec((d, LANES), lambda i, j: (0, 0)),
                  pl.BlockSpec((1, LANES), lambda i, j: (0, 0))],
        out_specs=[pl.BlockSpec((tm, tn), lambda i, j: (i, j)),
                   pl.BlockSpec((tm, LANES), lambda i, j: (i, 0))],
        out_shape=[jax.ShapeDtypeStruct((t, n), BF16),
                   jax.ShapeDtypeStruct((t, LANES), F32)],
        scratch_shapes=[pltpu.VMEM((tm, d), BF16)],
        compiler_params=_cparams(("arbitrary", "arbitrary")),
        name="in_proj",
    )(x2, sc, sh, w_main, b_main, w_gate, b_gate)


def _mlstm_kernel(qk_ref, v_ref, o_ref, gc_ref, gr_ref, cw_ref, cb_ref, ng_ref, out_ref,
                  ct_ref, n_ref, m_ref, prev_ref):
    c = pl.program_id(1)
    L = MLSTM_CHUNK

    @pl.when(c == 0)
    def _():
        ct_ref[...] = jnp.zeros_like(ct_ref)
        n_ref[...] = jnp.zeros_like(n_ref)
        m_ref[...] = jnp.zeros_like(m_ref)
        prev_ref[...] = jnp.zeros_like(prev_ref)

    x = qk_ref[...].astype(F32)
    prev = prev_ref[...]
    row = lax.broadcasted_iota(I32, (L, 1), 0)
    y = cw_ref[CONV_WIDTH - 1:CONV_WIDTH, :] * x + cb_ref[...]
    for k in range(1, CONV_WIDTH):
        xs = jnp.where(row < k, pltpu.roll(prev, k, 0), pltpu.roll(x, k, 0))
        y = y + cw_ref[CONV_WIDTH - 1 - k:CONV_WIDTH - k, :] * xs
    prev_ref[...] = x
    y = y * _sigmoid(y)

    gc = gc_ref[...]
    gr = gr_ref[0]
    ti = lax.broadcasted_iota(I32, (L, L), 0)
    si = lax.broadcasted_iota(I32, (L, L), 1)
    causal = si <= ti
    tril = causal.astype(F32)
    triu = (ti <= si).astype(F32)
    b_cols = jnp.dot(tril, _log_sigmoid(gc), precision=lax.Precision.HIGHEST, preferred_element_type=F32)
    b_rows = jnp.dot(_log_sigmoid(gr), triu, precision=lax.Precision.HIGHEST, preferred_element_type=F32)

    qk_w = M_HEADS * M_DQK
    for h in range(M_HEADS):
        qf = y[:, h * M_DQK:(h + 1) * M_DQK]
        kf = y[:, qk_w + h * M_DQK:qk_w + (h + 1) * M_DQK] * (M_DQK ** -0.5)
        vb = v_ref[:, h * M_DV:(h + 1) * M_DV]
        qb = qf.astype(BF16)
        kb = kf.astype(BF16)

        bc = b_cols[:, M_HEADS + h:M_HEADS + h + 1]
        ic = gc[:, h:h + 1]
        br = b_rows[M_HEADS + h:M_HEADS + h + 1, :]
        ir = gr[h:h + 1, :]
        m_prev = m_ref[h][0:1, 0:1]
        n_prev = n_ref[h][0:1, :]

        dm = jnp.where(causal, bc - br + ir, -jnp.inf)
        inter = bc + m_prev
        mt = jnp.maximum(inter, jnp.max(dm, axis=1, keepdims=True))
        a = jnp.exp(dm - mt) * _dot_nt(qb, kb)
        e_int = jnp.exp(inter - mt)
        num = _dot(a.astype(BF16), vb) + e_int * _dot(qb, ct_ref[h].astype(BF16))
        den = jnp.sum(a, axis=1, keepdims=True) + e_int * jnp.sum(qf * n_prev, axis=1, keepdims=True)
        hh = num / jnp.maximum(jnp.abs(den), jnp.exp(-mt))

        b_last = bc[L - 1:L, :]
        g_col = b_last - bc + ic
        m_new = jnp.maximum(b_last + m_prev, jnp.max(g_col, axis=0, keepdims=True))
        w_col = jnp.exp(g_col - m_new)
        decay = jnp.exp(b_last + m_prev - m_new)
        wv = (w_col * vb.astype(F32)).astype(BF16)
        ct_ref[h] = decay * ct_ref[h] + _dot(kf.T.astype(BF16), wv)
        n_new = decay * n_prev + jnp.sum(w_col * kf, axis=0, keepdims=True)
        n_ref[h] = jnp.broadcast_to(n_new, n_ref.shape[1:])
        m_ref[h] = jnp.broadcast_to(m_new, m_ref.shape[1:])

        ms = jnp.mean(hh * hh, axis=1, keepdims=True)
        og = o_ref[:, h * M_DV:(h + 1) * M_DV].astype(F32)
        yh = hh * lax.rsqrt(ms + EPS) * ng_ref[:, h * M_DV:(h + 1) * M_DV] * _sigmoid(og)
        out_ref[:, h * M_DV:(h + 1) * M_DV] = yh.astype(BF16)


def _mlstm(proj, gates_c, gates_r, conv_w, conv_b, norm_g, bsz, seq):
    L = MLSTM_CHUNK
    nc = seq // L
    t = bsz * seq
    w = M_HEADS * M_DV
    return pl.pallas_call(
        _mlstm_kernel,
        grid=(bsz, nc),
        in_specs=[pl.BlockSpec((L, w), lambda b, c: (b * nc + c, 0)),
                  pl.BlockSpec((L, w), lambda b, c: (b * nc + c, 1)),
                  pl.BlockSpec((L, w), lambda b, c: (b * nc + c, 2)),
                  pl.BlockSpec((L, LANES), lambda b, c: (b * nc + c, 0)),
                  pl.BlockSpec((1, 8, L), lambda b, c: (b, 0, c)),
                  pl.BlockSpec((CONV_WIDTH, w), lambda b, c: (0, 0)),
                  pl.BlockSpec((1, w), lambda b, c: (0, 0)),
                  pl.BlockSpec((1, w), lambda b, c: (0, 0))],
        out_specs=pl.BlockSpec((L, w), lambda b, c: (b * nc + c, 0)),
        out_shape=jax.ShapeDtypeStruct((t, w), BF16),
        scratch_shapes=[pltpu.VMEM((M_HEADS, M_DQK, M_DV), F32),
                        pltpu.VMEM((M_HEADS, 8, M_DQK), F32),
                        pltpu.VMEM((M_HEADS, 8, LANES), F32),
                        pltpu.VMEM((L, w), F32)],
        compiler_params=_cparams(("arbitrary", "arbitrary")),
        name="mlstm",
    )(proj, proj, proj, gates_c, gates_r, conv_w, conv_b, norm_g)


def _attn_kernel(dilation, has_prev, *refs):
    nq = ATTN_BLOCK
    if has_prev:
        q_ref, kc_ref, vc_ref, kp_ref, vp_ref, o_ref, lse_ref, k_all, v_all = refs
        k_all[0:nq, :] = kp_ref[0, 0]
        k_all[nq:2 * nq, :] = kc_ref[0, 0]
        v_all[0:nq, :] = vp_ref[0, 0]
        v_all[nq:2 * nq, :] = vc_ref[0, 0]
        nk = 2 * nq
    else:
        q_ref, k_all, v_all, o_ref, lse_ref = refs
        k_all, v_all = k_all.at[0, 0], v_all.at[0, 0]
        nk = nq
    n = pl.program_id(2)
    qi = lax.broadcasted_iota(I32, (nq, nk), 0)
    ki = lax.broadcasted_iota(I32, (nq, nk), 1)
    dist = qi - ki + (nk - nq)
    ok = jnp.logical_and(dist >= 0, dist <= nq)
    if has_prev:
        ok = jnp.logical_and(ok, jnp.logical_or(ki >= nq, n > 0))
    dist_f = dist.astype(F32)
    lane = lax.broadcasted_iota(I32, (nq, LANES), 1)
    left_q = lane < A_DH
    left_k = lax.broadcasted_iota(I32, (nk, LANES), 1) < A_DH
    n_pairs = A_HEADS // 2

    scores = []
    for p in range(n_pairs):
        cols = slice(p * LANES, (p + 1) * LANES)
        qp = q_ref[0, 0, :, cols] * (A_DH ** -0.5)
        kp = k_all[:, cols]
        zero = jnp.zeros_like(qp)
        scores.append(_dot_nt(jnp.where(left_q, qp, zero), kp))
        scores.append(_dot_nt(jnp.where(left_q, zero, qp), kp))
    probs, maxes = [], []
    for h in range(A_HEADS):
        coef = -(2.0 ** (-8.0 * (h + 1) / A_HEADS)) * dilation
        s = jnp.where(ok, scores[h] + dist_f * coef, -jnp.inf)
        m = jnp.max(s, axis=1, keepdims=True)
        probs.append(jnp.exp(s - m).astype(BF16))
        maxes.append(m)
    lse_tile = jnp.zeros((nq, LANES), F32)
    for p in range(n_pairs):
        cols = slice(p * LANES, (p + 1) * LANES)
        vp = v_all[:, cols]
        one = jnp.ones_like(vp)
        pv_e = _dot(probs[2 * p], jnp.where(left_k, vp, one))
        pv_o = _dot(probs[2 * p + 1], jnp.where(left_k, one, vp))
        num = jnp.where(left_q, pv_e, pv_o)
        den = pltpu.roll(jnp.where(left_q, pv_o, pv_e), A_DH, 1)
        o_ref[0, 0, :, cols] = (num / den).astype(BF16)
        lse_tile = jnp.where(lane == 2 * p, maxes[2 * p] + jnp.log(pv_e[:, A_DH:A_DH + 1]), lse_tile)
        lse_tile = jnp.where(lane == 2 * p + 1, maxes[2 * p + 1] + jnp.log(pv_o[:, 0:1]), lse_tile)
    lse_ref[0, 0] = lse_tile


def _attn_group(qkv, dilation):
    bsz, d, ls, _ = qkv.shape
    aw = A_HEADS * A_DH
    nq = ATTN_BLOCK
    nb = ls // nq
    has_prev = nb > 1
    blk = (1, 1, nq, aw)
    in_specs = [pl.BlockSpec(blk, lambda b, r, n: (b, r, n, 0)),
                pl.BlockSpec(blk, lambda b, r, n: (b, r, n, 1)),
                pl.BlockSpec(blk, lambda b, r, n: (b, r, n, 2))]
    args = [qkv, qkv, qkv]
    if has_prev:
        in_specs += [pl.BlockSpec(blk, lambda b, r, n: (b, r, jnp.maximum(n - 1, 0), 1)),
                     pl.BlockSpec(blk, lambda b, r, n: (b, r, jnp.maximum(n - 1, 0), 2))]
        args += [qkv, qkv]
    return pl.pallas_call(
        functools.partial(_attn_kernel, dilation, has_prev),
        grid=(bsz, d, nb),
        in_specs=in_specs,
        out_specs=[pl.BlockSpec(blk, lambda b, r, n: (b, r, n, 0)),
                   pl.BlockSpec((1, 1, nq, LANES), lambda b, r, n: (b, r, n, 0))],
        out_shape=[jax.ShapeDtypeStruct((bsz, d, ls, aw), BF16),
                   jax.ShapeDtypeStruct((bsz, d, ls, LANES), F32)],
        scratch_shapes=[pltpu.VMEM((2 * nq, aw), BF16)] * 2 if has_prev else [],
        compiler_params=_cparams(("arbitrary", "arbitrary", "arbitrary")),
        name=f"dilated_attn_d{dilation}",
    )(*args)


def _head_maps(n_heads, dh):
    w = n_heads * dh
    e = np.zeros((LANES, w), np.float32)
    for h in range(n_heads):
        e[h, h * dh:(h + 1) * dh] = 1.0
    return jnp.asarray(e, BF16), jnp.asarray(e.T.copy(), BF16)


def _merge_kernel(o1_ref, o2_ref, o3_ref, l1_ref, l2_ref, l3_ref, g_ref, e_ref, p_ref, y_ref):
    l1, l2, l3 = l1_ref[...], l2_ref[...], l3_ref[...]
    mx = jnp.maximum(jnp.maximum(l1, l2), l3)
    w1, w2, w3 = jnp.exp(l1 - mx), jnp.exp(l2 - mx), jnp.exp(l3 - mx)
    inv = 1.0 / (w1 + w2 + w3)
    e = e_ref[...]
    o = (_dot_hilo(w1 * inv, e) * o1_ref[...].astype(F32)
         + _dot_hilo(w2 * inv, e) * o2_ref[...].astype(F32)
         + _dot_hilo(w3 * inv, e) * o3_ref[...].astype(F32))
    ms = _dot_hilo(o * o, p_ref[...]) * (1.0 / A_DH)
    scale = _dot_hilo(lax.rsqrt(ms + EPS), e)
    y_ref[...] = (o * scale * g_ref[...]).astype(BF16)


def _attn_merge(outs, lses, norm_g):
    t, aw = outs[0].shape
    tm = 512
    expand, pool = _head_maps(A_HEADS, A_DH)
    row = lambda i: (i, 0)
    const = lambda i: (0, 0)
    return pl.pallas_call(
        _merge_kernel,
        grid=(t // tm,),
        in_specs=[pl.BlockSpec((tm, aw), row)] * 3 + [pl.BlockSpec((tm, LANES), row)] * 3
        + [pl.BlockSpec((1, aw), const), pl.BlockSpec((LANES, aw), const), pl.BlockSpec((aw, LANES), const)],
        out_specs=pl.BlockSpec((tm, aw), row),
        out_shape=jax.ShapeDtypeStruct((t, aw), BF16),
        compiler_params=_cparams(("arbitrary",)),
        name="attn_merge",
    )(*outs, *lses, norm_g, expand, pool)


def _outproj_kernel(alpha, ym_ref, ya_ref, w_ref, x_ref, g1_ref, sc_ref, sh_ref, lg_ref, lb_ref,
                    rw_ref, rb_ref, x1_ref, h2_ref, ti_ref, tg_ref):
    half = ym_ref.shape[1]
    y = _dot(ym_ref[...], w_ref[0:half, :]) + _dot(ya_ref[...], w_ref[half:2 * half, :])
    z = alpha * x_ref[...] + (1.0 + g1_ref[0]) * y
    x1 = _layer_norm(z, lg_ref[...], lb_ref[...])
    x1_ref[...] = x1
    h2 = x1 * (1.0 + sc_ref[0]) + sh_ref[0]
    h2_ref[...] = h2
    logits = _dot(h2.astype(BF16), rw_ref[...]) + rb_ref[...]
    lane = lax.broadcasted_iota(I32, logits.shape, 1)
    lane_f = lane.astype(F32)
    work = jnp.where(lane < N_EXPERTS, logits, -jnp.inf)
    idx_tile = jnp.zeros(logits.shape, F32)
    val_tile = jnp.zeros(logits.shape, F32)
    top = None
    denom = None
    for k in range(TOP_K):
        mk = jnp.max(work, axis=1, keepdims=True)
        ik = jnp.min(jnp.where(work == mk, lane_f, float(LANES)), axis=1, keepdims=True)
        work = jnp.where(lane_f == ik, -jnp.inf, work)
        if k == 0:
            top = mk
        ek = jnp.exp(mk - top)
        denom = ek if k == 0 else denom + ek
        idx_tile = jnp.where(lane == k, ik, idx_tile)
        val_tile = jnp.where(lane == k, ek, val_tile)
    ti_ref[...] = idx_tile.astype(I32)
    tg_ref[...] = val_tile / denom


def _out_proj(y_m, y_a, w_out, x2, g1, sc2, sh2, ln_g, ln_b, rw, rb, alpha, seq):
    t, d = x2.shape
    half = y_m.shape[1]
    tm = 256
    per_b = seq // tm
    row = lambda i: (i, 0)
    const = lambda i: (0, 0)
    mod = lambda i: (i // per_b, 0, 0)
    return pl.pallas_call(
        functools.partial(_outproj_kernel, alpha),
        grid=(t // tm,),
        in_specs=[pl.BlockSpec((tm, half), row), pl.BlockSpec((tm, half), row),
                  pl.BlockSpec((2 * half, d), const), pl.BlockSpec((tm, d), row),
                  pl.BlockSpec((1, 1, d), mod), pl.BlockSpec((1, 1, d), mod), pl.BlockSpec((1, 1, d), mod),
                  pl.BlockSpec((1, d), const), pl.BlockSpec((1, d), const),
                  pl.BlockSpec((d, LANES), const), pl.BlockSpec((1, LANES), const)],
        out_specs=[pl.BlockSpec((tm, d), row), pl.BlockSpec((tm, d), row),
                   pl.BlockSpec((tm, LANES), row), pl.BlockSpec((tm, LANES), row)],
        out_shape=[jax.ShapeDtypeStruct((t, d), F32), jax.ShapeDtypeStruct((t, d), F32),
                   jax.ShapeDtypeStruct((t, LANES), I32), jax.ShapeDtypeStruct((t, LANES), F32)],
        compiler_params=_cparams(("arbitrary",)),
        name="out_proj_ln_router",
    )(y_m, y_a, w_out, x2, g1, sc2, sh2, ln_g, ln_b, rw, rb)


def _expert_kernel(ie_ref, ir_ref, ic_ref, iv_ref, tok_hbm, h_hbm, wgu_ref, bgu_ref, wdn_ref, bdn_ref,
                   sel_ref, y_hbm, tok_smem, xbuf, xb, yacc, wgu_b, wdn_b, gu_scr, sems):
    i = pl.program_id(0)
    j = pl.program_id(1)
    nj = pl.num_programs(1)
    row0 = ir_ref[i]
    nch = ic_ref[i]
    ch = MOE_CHUNK

    def row_copy(r):
        t = tok_smem[lax.shift_right_logical(r, 7), jnp.bitwise_and(r, LANES - 1)]
        return pltpu.make_async_copy(h_hbm.at[pl.ds(t, 1), :], xbuf.at[pl.ds(r, 1), :], sems.at[1])

    @pl.when(jnp.logical_and(j == 0, nch > 0))
    def _():
        tok_rows = pl.ds(lax.shift_right_logical(row0, 7), MOE_ITEM_ROWS // LANES)
        cp = pltpu.make_async_copy(tok_hbm.at[tok_rows, :], tok_smem, sems.at[0])
        cp.start()
        cp.wait()

        def issue(r, carry):
            row_copy(r).start()
            return carry

        lax.fori_loop(0, nch * ch, issue, 0)

        def drain(r, carry):
            row_copy(r).wait()
            return carry

        lax.fori_loop(0, nch * ch, drain, 0)

        def prep(m, carry):
            rows = pl.ds(pl.multiple_of(m * ch, ch), ch)
            xb[rows, :] = xbuf[rows, :].astype(BF16)
            yacc[rows, :] = jnp.broadcast_to(bdn_ref[0], (ch, yacc.shape[1]))
            return carry

        lax.fori_loop(0, nch, prep, 0)

    @pl.when(nch > 0)
    def _():
        wgu_b[...] = wgu_ref[0].astype(BF16)
        wdn_b[...] = wdn_ref[0].astype(BF16)
        bgu = bgu_ref[0]
        last = j == nj - 1

        def gate_up(m):
            rows = pl.ds(pl.multiple_of(m * ch, ch), ch)
            return _dot(xb[rows, :], wgu_b[...]) + bgu

        def finish(m, gu):
            rows = pl.ds(pl.multiple_of(m * ch, ch), ch)
            glu = jnp.minimum(gu, SWIGLU_LIMIT)
            f_glu = glu * _sigmoid(SWIGLU_ALPHA * glu)
            f_lin = jnp.clip(gu, -SWIGLU_LIMIT, SWIGLU_LIMIT) + 1.0
            prod = (pltpu.roll(f_glu, 1, 1) * f_lin).astype(BF16)
            parts = [_dot(prod[:, q * 2 * LANES:(q + 1) * 2 * LANES], sel_ref[...])
                     for q in range(MOE_TN // (2 * LANES))]
            act = jnp.concatenate(parts, axis=1).astype(BF16)
            yacc[rows, :] += _dot(act, wdn_b[...])

        gu_scr[...] = gate_up(0)

        def chunk(m, carry):
            gu = gu_scr[...]
            gu_scr[...] = gate_up(m + 1)
            finish(m, gu)
            return carry

        lax.fori_loop(0, nch - 1, chunk, 0)
        finish(nch - 1, gu_scr[...])

        @pl.when(last)
        def _():
            def out_copy(m):
                rows = pl.ds(pl.multiple_of(m * ch, ch), ch)
                dst = pl.ds(pl.multiple_of(row0 + m * ch, ch), ch)
                return pltpu.make_async_copy(yacc.at[rows, :], y_hbm.at[dst, :], sems.at[2])

            def issue(m, carry):
                out_copy(m).start()
                return carry

            lax.fori_loop(0, nch, issue, 0)

            def drain(m, carry):
                out_copy(m).wait()
                return carry

            lax.fori_loop(0, nch, drain, 0)


def _experts(h2, tok_pad, item_e, item_row0, item_nch, item_valid, w_gu, b_gu, w_dn, b_dn, n_rows):
    ne, d, two_de = w_gu.shape
    tn = MOE_TN
    nj = two_de // tn
    assert nj > 1, "the kernel separates its first and last column-tile steps"
    ni = item_e.shape[0]
    sel = np.zeros((2 * LANES, LANES), np.float32)
    sel[2 * np.arange(LANES) + 1, np.arange(LANES)] = 1.0

    def jmap(i, j, iv):
        return jnp.where(iv[i] > 0, j, nj - 1)

    grid_spec = pltpu.PrefetchScalarGridSpec(
        num_scalar_prefetch=4,
        grid=(ni, nj),
        in_specs=[pl.BlockSpec(memory_space=pl.ANY),
                  pl.BlockSpec(memory_space=pl.ANY),
                  pl.BlockSpec((1, d, tn), lambda i, j, ie, ir, ic, iv: (ie[i], 0, jmap(i, j, iv))),
                  pl.BlockSpec((1, 1, tn), lambda i, j, ie, ir, ic, iv: (ie[i], 0, jmap(i, j, iv))),
                  pl.BlockSpec((1, tn // 2, d), lambda i, j, ie, ir, ic, iv: (ie[i], jmap(i, j, iv), 0)),
                  pl.BlockSpec((1, 1, d), lambda i, j, ie, ir, ic, iv: (ie[i], 0, 0)),
                  pl.BlockSpec((2 * LANES, LANES), lambda i, j, ie, ir, ic, iv: (0, 0))],
        out_specs=pl.BlockSpec(memory_space=pl.ANY),
        scratch_shapes=[pltpu.SMEM((MOE_ITEM_ROWS // LANES, LANES), I32),
                        pltpu.VMEM((MOE_ITEM_ROWS, d), F32),
                        pltpu.VMEM((MOE_ITEM_ROWS, d), BF16),
                        pltpu.VMEM((MOE_ITEM_ROWS, d), F32),
                        pltpu.VMEM((d, tn), BF16),
                        pltpu.VMEM((tn // 2, d), BF16),
                        pltpu.VMEM((MOE_CHUNK, tn), F32),
                        pltpu.SemaphoreType.DMA((3,))],
    )
    return pl.pallas_call(
        _expert_kernel,
        grid_spec=grid_spec,
        out_shape=jax.ShapeDtypeStruct((n_rows, d), F32),
        compiler_params=_cparams(("arbitrary", "arbitrary")),
        name="moe_experts",
    )(item_e, item_row0, item_nch, item_valid, tok_pad, h2, w_gu, b_gu.reshape(ne, 1, two_de),
      w_dn, b_dn.reshape(ne, 1, d), jnp.asarray(sel, BF16))


def _combine_kernel(alpha, dest_ref, y_hbm, gate_ref, x1_ref, g2_ref, lg_ref, lb_ref, out_ref, buf, sem):
    tc = COMBINE_TOKENS

    def row_copy(s):
        k = jnp.bitwise_and(s, TOP_K - 1)
        r = lax.shift_right_logical(s, 2)
        src = dest_ref[0, 0, s]
        return pltpu.make_async_copy(y_hbm.at[pl.ds(src, 1), :], buf.at[k, pl.ds(r, 1), :], sem.at[0])

    def issue(s, carry):
        row_copy(s).start()
        return carry

    lax.fori_loop(0, tc * TOP_K, issue, 0)

    def drain(s, carry):
        row_copy(s).wait()
        return carry

    lax.fori_loop(0, tc * TOP_K, drain, 0)

    gates = gate_ref[...]
    y = gates[:, 0:1] * buf[0]
    for k in range(1, TOP_K):
        y = y + gates[:, k:k + 1] * buf[k]
    z = alpha * x1_ref[...] + (1.0 + g2_ref[0]) * y
    out_ref[...] = _layer_norm(z, lg_ref[...], lb_ref[...])


def _combine(dest, y_pad, gates, x1, g2, ln_g, ln_b, alpha, seq):
    t, d = x1.shape
    tc = COMBINE_TOKENS
    per_b = seq // tc
    row = lambda i: (i, 0)
    const = lambda i: (0, 0)
    return pl.pallas_call(
        functools.partial(_combine_kernel, alpha),
        grid=(t // tc,),
        in_specs=[pl.BlockSpec((1, 1, tc * TOP_K), lambda i: (i, 0, 0), memory_space=pltpu.SMEM),
                  pl.BlockSpec(memory_space=pl.ANY),
                  pl.BlockSpec((tc, LANES), row), pl.BlockSpec((tc, d), row),
                  pl.BlockSpec((1, 1, d), lambda i: (i // per_b, 0, 0)),
                  pl.BlockSpec((1, d), const), pl.BlockSpec((1, d), const)],
        out_specs=pl.BlockSpec((tc, d), row),
        out_shape=jax.ShapeDtypeStruct((t, d), F32),
        scratch_shapes=[pltpu.VMEM((TOP_K, tc, d), F32), pltpu.SemaphoreType.DMA((1,))],
        compiler_params=_cparams(("arbitrary",)),
        name="moe_combine_ln",
    )(dest.reshape(t // tc, 1, tc * TOP_K), y_pad, gates, x1, g2, ln_g, ln_b)


def _routing_tables(top_idx):
    t = top_idx.shape[0]
    tk = t * TOP_K
    e_flat = top_idx.reshape(tk)
    order = jnp.argsort(e_flat).astype(I32)
    e_sorted = e_flat[order]
    tok_sorted = order // TOP_K
    counts = jnp.bincount(e_flat, length=N_EXPERTS).astype(I32)
    starts = jnp.cumsum(counts) - counts
    padded = ((counts + MOE_ROW_PAD - 1) // MOE_ROW_PAD) * MOE_ROW_PAD
    pad_end = jnp.cumsum(padded)
    pad_start = pad_end - padded
    dest_sorted = pad_start[e_sorted] + jnp.arange(tk, dtype=I32) - starts[e_sorted]
    n_rows = tk + N_EXPERTS * MOE_ROW_PAD
    tok_pad = jnp.zeros((n_rows + MOE_ITEM_ROWS,), I32).at[dest_sorted].set(tok_sorted).reshape(-1, LANES)
    dest = jnp.zeros((tk,), I32).at[order].set(dest_sorted)

    items_per = (padded + MOE_ITEM_ROWS - 1) // MOE_ITEM_ROWS
    item_end = jnp.cumsum(items_per)
    item_start = item_end - items_per
    n_items = N_EXPERTS + n_rows // MOE_ITEM_ROWS
    idx = jnp.arange(n_items, dtype=I32)
    valid = idx < item_end[-1]
    e_i = jnp.minimum(jnp.searchsorted(item_end, idx, side='right'), N_EXPERTS - 1).astype(I32)
    k_i = idx - item_start[e_i]
    row0 = pad_start[e_i] + k_i * MOE_ITEM_ROWS
    nrows = jnp.clip(padded[e_i] - k_i * MOE_ITEM_ROWS, 0, MOE_ITEM_ROWS)
    e_last = e_i[jnp.maximum(item_end[-1] - 1, 0)]
    item_e = jnp.where(valid, e_i, e_last).astype(I32)
    item_row0 = jnp.where(valid, row0, 0).astype(I32)
    item_nch = jnp.where(valid, nrows // MOE_CHUNK, 0).astype(I32)
    return tok_pad, dest, item_e, item_row0, item_nch, valid.astype(I32), n_rows


def kernel(x, c, w_ada, b_ada, w_in, b_in, conv_w, conv_b, m_norm_g, a_norm_g, w_out, ln1_g, ln1_b,
           router_w, router_b, w_gu, b_gu, w_dn, b_dn, ln2_g, ln2_b):
    bsz, seq, d = x.shape
    depth = w_ada.shape[0]
    t = bsz * seq
    alpha = float((2 * depth) ** 0.25)
    qk_w = 2 * M_HEADS * M_DQK
    mv_w = M_HEADS * M_DV
    aw = A_HEADS * A_DH
    gate_lo = qk_w + 2 * mv_w
    gate_hi = gate_lo + 2 * M_HEADS

    x2 = x.reshape(t, d)
    for l in range(depth):
        mod = _ada_mod(c, w_ada[l], b_ada[l]).reshape(bsz, 6, 1, d)
        sh1, sc1, g1, sh2, sc2, g2 = (mod[:, i] for i in range(6))

        w_main = jnp.concatenate([w_in[l][:, :gate_lo], w_in[l][:, gate_hi:]], axis=1).astype(BF16)
        b_main = jnp.concatenate([b_in[l][:gate_lo], b_in[l][gate_hi:]]).reshape(1, -1)
        w_gate = jnp.zeros((d, LANES), BF16).at[:, :2 * M_HEADS].set(w_in[l][:, gate_lo:gate_hi].astype(BF16))
        b_gate = jnp.zeros((1, LANES), F32).at[0, :2 * M_HEADS].set(b_in[l][gate_lo:gate_hi])
        proj, gates_c = _in_proj(x2, sc1, sh1, w_main, b_main, w_gate, b_gate, seq)

        gates_r = gates_c[:, :2 * M_HEADS].reshape(bsz, seq, 2 * M_HEADS).transpose(0, 2, 1)
        y_m = _mlstm(proj, gates_c, gates_r, conv_w[l], conv_b[l].reshape(1, -1),
                     m_norm_g[l].reshape(1, -1), bsz, seq)

        qkv = proj[:, gate_lo:].reshape(bsz, seq, 3 * aw)
        outs, lses = [], []
        for dil in DILATIONS:
            ls = seq // dil
            qkv_d = qkv.reshape(bsz, ls, dil, 3 * aw).transpose(0, 2, 1, 3)
            o_d, lse_d = _attn_group(qkv_d, dil)
            outs.append(o_d.transpose(0, 2, 1, 3).reshape(t, aw))
            lses.append(lse_d.transpose(0, 2, 1, 3).reshape(t, LANES))
        y_a = _attn_merge(outs, lses, a_norm_g[l].reshape(1, -1))

        rw = jnp.zeros((d, LANES), BF16).at[:, :N_EXPERTS].set(router_w[l].astype(BF16))
        rb = jnp.zeros((1, LANES), F32).at[0, :N_EXPERTS].set(router_b[l])
        x1, h2, top_idx, gates = _out_proj(y_m, y_a, w_out[l].astype(BF16), x2, g1, sc2, sh2,
                                           ln1_g[l].reshape(1, -1), ln1_b[l].reshape(1, -1), rw, rb, alpha, seq)

        tok_pad, dest, item_e, item_row0, item_nch, item_valid, n_rows = _routing_tables(top_idx[:, :TOP_K])
        y_pad = _experts(h2, tok_pad, item_e, item_row0, item_nch, item_valid,
                         w_gu[l], b_gu[l], w_dn[l], b_dn[l], n_rows)
        x2 = _combine(dest, y_pad, gates, x1, g2, ln2_g[l].reshape(1, -1), ln2_b[l].reshape(1, -1), alpha, seq)
    return x2.reshape(bsz, seq, d)
```

```python
import functools

import jax
import jax.numpy as jnp
import numpy as np
from jax import lax
from jax.experimental import pallas as pl
from jax.experimental.pallas import tpu as pltpu

F32 = jnp.float32
BF16 = jnp.bfloat16
I32 = jnp.int32

M_HEADS = 4
M_DQK = 128
M_DV = 256
CONV_WIDTH = 4
A_HEADS = 16
A_DH = 64
ATTN_BLOCK = 128
DILATIONS = (1, 4, 16)
N_EXPERTS = 32
TOP_K = 4
SWIGLU_ALPHA = 1.702
SWIGLU_LIMIT = 7.0
EPS = 1e-5

LANES = 128
VMEM_LIMIT = 56 * 1024 * 1024

MLSTM_CHUNK = 256
MOE_ROW_PAD = 256
MOE_CHUNK = 256
MOE_ITEM_ROWS = 1280
MOE_TN = 512
COMBINE_TOKENS = 128
DMA_UNROLL = 8


def _cparams(sem, vmem=VMEM_LIMIT):
    return pltpu.CompilerParams(dimension_semantics=sem, vmem_limit_bytes=vmem)


def _sigmoid(x):
    return 1.0 / (1.0 + jnp.exp(-x))


def _log_sigmoid(x):
    return jnp.minimum(x, 0.0) - jnp.log(1.0 + jnp.exp(-jnp.abs(x)))


def _layer_norm(z, g, b):
    mu = jnp.mean(z, axis=-1, keepdims=True)
    zc = z - mu
    var = jnp.mean(zc * zc, axis=-1, keepdims=True)
    return zc * lax.rsqrt(var + EPS) * g + b


def _dot(a, b):
    return jnp.dot(a, b, preferred_element_type=F32)


def _dot_nt(a, b):
    return lax.dot_general(a, b, (((1,), (1,)), ((), ())), preferred_element_type=F32)


def _dot_hilo(a, sel):
    hi = a.astype(BF16)
    lo = (a - hi.astype(F32)).astype(BF16)
    return _dot(hi, sel) + _dot(lo, sel)


def _ada_kernel(c_ref, w_ref, b_ref, o_ref):
    c = c_ref[...]
    cond = c * _sigmoid(c)
    o_ref[...] = _dot(cond.astype(BF16), w_ref[...].astype(BF16)) + b_ref[...]


def _ada_mod(c, w_ada, b_ada):
    bsz, d = c.shape
    n = w_ada.shape[1]
    tn = 1024
    rows = 8
    c_pad = jnp.zeros((rows, d), F32).at[:bsz].set(c)
    out = pl.pallas_call(
        _ada_kernel,
        grid=(n // tn,),
        in_specs=[pl.BlockSpec((rows, d), lambda j: (0, 0)),
                  pl.BlockSpec((d, tn), lambda j: (0, j)),
                  pl.BlockSpec((1, tn), lambda j: (0, j))],
        out_specs=pl.BlockSpec((rows, tn), lambda j: (0, j)),
        out_shape=jax.ShapeDtypeStruct((rows, n), F32),
        compiler_params=_cparams(("arbitrary",)),
        name="ada_mod",
    )(c_pad, w_ada, b_ada.reshape(1, n))
    return out[:bsz]


def _inproj_kernel(x_ref, sc_ref, sh_ref, w_ref, b_ref, wg_ref, bg_ref, o_ref, g_ref, h_ref):
    j = pl.program_id(1)

    @pl.when(j == 0)
    def _():
        h = x_ref[...] * (1.0 + sc_ref[0]) + sh_ref[0]
        hb = h.astype(BF16)
        h_ref[...] = hb
        g_ref[...] = _dot(hb, wg_ref[...]) + bg_ref[...]

    o_ref[...] = (_dot(h_ref[...], w_ref[...]) + b_ref[...]).astype(BF16)


def _in_proj(x2, sc, sh, w_main, b_main, w_gate, b_gate, seq):
    t, d = x2.shape
    n = w_main.shape[1]
    tm, tn = 512, 1024
    per_b = seq // tm
    return pl.pallas_call(
        _inproj_kernel,
        grid=(t // tm, n // tn),
        in_specs=[pl.BlockSpec((tm, d), lambda i, j: (i, 0)),
                  pl.BlockSpec((1, 1, d), lambda i, j: (i // per_b, 0, 0)),
                  pl.BlockSpec((1, 1, d), lambda i, j: (i // per_b, 0, 0)),
                  pl.BlockSpec((d, tn), lambda i, j: (0, j)),
                  pl.BlockSpec((1, tn), lambda i, j: (0, j)),
                  pl.BlockSpec((d, LANES), lambda i, j: (0, 0)),
                  pl.BlockSpec((1, LANES), lambda i, j: (0, 0))],
        out_specs=[pl.BlockSpec((tm, tn), lambda i, j: (i, j)),
                   pl.BlockSpec((tm, LANES), lambda i, j: (i, 0))],
        out_shape=[jax.ShapeDtypeStruct((t, n), BF16),
                   jax.ShapeDtypeStruct((t, LANES), F32)],
        scratch_shapes=[pltpu.VMEM((tm, d), BF16)],
        compiler_params=_cparams(("arbitrary", "arbitrary")),
        name="in_proj",
    )(x2, sc, sh, w_main, b_main, w_gate, b_gate)


def _mlstm_kernel(qk_ref, v_ref, o_ref, gc_ref, gr_ref, cw_ref, cb_ref, ng_ref, out_ref,
                  ct_ref, n_ref, m_ref, prev_ref):
    c = pl.program_id(1)
    L = MLSTM_CHUNK

    @pl.when(c == 0)
    def _():
        ct_ref[...] = jnp.zeros_like(ct_ref)
        n_ref[...] = jnp.zeros_like(n_ref)
        m_ref[...] = jnp.zeros_like(m_ref)
        prev_ref[...] = jnp.zeros_like(prev_ref)

    x = qk_ref[...].astype(F32)
    prev = prev_ref[...]
    row = lax.broadcasted_iota(I32, (L, 1), 0)
    y = cw_ref[CONV_WIDTH - 1:CONV_WIDTH, :] * x + cb_ref[...]
    for k in range(1, CONV_WIDTH):
        xs = jnp.where(row < k, pltpu.roll(prev, k, 0), pltpu.roll(x, k, 0))
        y = y + cw_ref[CONV_WIDTH - 1 - k:CONV_WIDTH - k, :] * xs
    prev_ref[...] = x
    y = y * _sigmoid(y)

    gc = gc_ref[...]
    gr = gr_ref[0]
    ti = lax.broadcasted_iota(I32, (L, L), 0)
    si = lax.broadcasted_iota(I32, (L, L), 1)
    causal = si <= ti
    tril = causal.astype(F32)
    triu = (ti <= si).astype(F32)
    b_cols = jnp.dot(tril, _log_sigmoid(gc), precision=lax.Precision.HIGHEST, preferred_element_type=F32)
    b_rows = jnp.dot(_log_sigmoid(gr), triu, precision=lax.Precision.HIGHEST, preferred_element_type=F32)

    qk_w = M_HEADS * M_DQK
    for h in range(M_HEADS):
        qf = y[:, h * M_DQK:(h + 1) * M_DQK]
        kf = y[:, qk_w + h * M_DQK:qk_w + (h + 1) * M_DQK] * (M_DQK ** -0.5)
        vb = v_ref[:, h * M_DV:(h + 1) * M_DV]
        qb = qf.astype(BF16)
        kb = kf.astype(BF16)

        bc = b_cols[:, M_HEADS + h:M_HEADS + h + 1]
        ic = gc[:, h:h + 1]
        br = b_rows[M_HEADS + h:M_HEADS + h + 1, :]
        ir = gr[h:h + 1, :]
        m_prev = m_ref[h][0:1, 0:1]
        n_prev = n_ref[h][0:1, :]

        dm = jnp.where(causal, bc - br + ir, -jnp.inf)
        inter = bc + m_prev
        mt = jnp.maximum(inter, jnp.max(dm, axis=1, keepdims=True))
        a = jnp.exp(dm - mt) * _dot_nt(qb, kb)
        e_int = jnp.exp(inter - mt)
        num = _dot(a.astype(BF16), vb) + e_int * _dot(qb, ct_ref[h].astype(BF16))
        den = jnp.sum(a, axis=1, keepdims=True) + e_int * jnp.sum(qf * n_prev, axis=1, keepdims=True)
        hh = num / jnp.maximum(jnp.abs(den), jnp.exp(-mt))

        b_last = bc[L - 1:L, :]
        g_col = b_last - bc + ic
        m_new = jnp.maximum(b_last + m_prev, jnp.max(g_col, axis=0, keepdims=True))
        w_col = jnp.exp(g_col - m_new)
        decay = jnp.exp(b_last + m_prev - m_new)
        wv = (w_col * vb.astype(F32)).astype(BF16)
        ct_ref[h] = decay * ct_ref[h] + _dot(kf.T.astype(BF16), wv)
        n_new = decay * n_prev + jnp.sum(w_col * kf, axis=0, keepdims=True)
        n_ref[h] = jnp.broadcast_to(n_new, n_ref.shape[1:])
        m_ref[h] = jnp.broadcast_to(m_new, m_ref.shape[1:])

        ms = jnp.mean(hh * hh, axis=1, keepdims=True)
        og = o_ref[:, h * M_DV:(h + 1) * M_DV].astype(F32)
        yh = hh * lax.rsqrt(ms + EPS) * ng_ref[:, h * M_DV:(h + 1) * M_DV] * _sigmoid(og)
        out_ref[:, h * M_DV:(h + 1) * M_DV] = yh.astype(BF16)


def _mlstm(proj, gates_c, gates_r, conv_w, conv_b, norm_g, bsz, seq):
    L = MLSTM_CHUNK
    nc = seq // L
    t = bsz * seq
    w = M_HEADS * M_DV
    return pl.pallas_call(
        _mlstm_kernel,
        grid=(bsz, nc),
        in_specs=[pl.BlockSpec((L, w), lambda b, c: (b * nc + c, 0)),
                  pl.BlockSpec((L, w), lambda b, c: (b * nc + c, 1)),
                  pl.BlockSpec((L, w), lambda b, c: (b * nc + c, 2)),
                  pl.BlockSpec((L, LANES), lambda b, c: (b * nc + c, 0)),
                  pl.BlockSpec((1, 8, L), lambda b, c: (b, 0, c)),
                  pl.BlockSpec((CONV_WIDTH, w), lambda b, c: (0, 0)),
                  pl.BlockSpec((1, w), lambda b, c: (0, 0)),
                  pl.BlockSpec((1, w), lambda b, c: (0, 0))],
        out_specs=pl.BlockSpec((L, w), lambda b, c: (b * nc + c, 0)),
        out_shape=jax.ShapeDtypeStruct((t, w), BF16),
        scratch_shapes=[pltpu.VMEM((M_HEADS, M_DQK, M_DV), F32),
                        pltpu.VMEM((M_HEADS, 8, M_DQK), F32),
                        pltpu.VMEM((M_HEADS, 8, LANES), F32),
                        pltpu.VMEM((L, w), F32)],
        compiler_params=_cparams(("arbitrary", "arbitrary")),
        name="mlstm",
    )(proj, proj, proj, gates_c, gates_r, conv_w, conv_b, norm_g)


def _attn_kernel(dilation, has_prev, *refs):
    nq = ATTN_BLOCK
    if has_prev:
        q_ref, kc_ref, vc_ref, kp_ref, vp_ref, o_ref, lse_ref, k_all, v_all = refs
        k_all[0:nq, :] = kp_ref[0, 0]
        k_all[nq:2 * nq, :] = kc_ref[0, 0]
        v_all[0:nq, :] = vp_ref[0, 0]
        v_all[nq:2 * nq, :] = vc_ref[0, 0]
        nk = 2 * nq
    else:
        q_ref, k_all, v_all, o_ref, lse_ref = refs
        k_all, v_all = k_all.at[0, 0], v_all.at[0, 0]
        nk = nq
    n = pl.program_id(2)
    qi = lax.broadcasted_iota(I32, (nq, nk), 0)
    ki = lax.broadcasted_iota(I32, (nq, nk), 1)
    dist = qi - ki + (nk - nq)
    ok = jnp.logical_and(dist >= 0, dist <= nq)
    if has_prev:
        ok = jnp.logical_and(ok, jnp.logical_or(ki >= nq, n > 0))
    dist_f = dist.astype(F32)
    lane = lax.broadcasted_iota(I32, (nq, LANES), 1)
    left_q = lane < A_DH
    left_k = lax.broadcasted_iota(I32, (nk, LANES), 1) < A_DH
    n_pairs = A_HEADS // 2

    scores = []
    for p in range(n_pairs):
        cols = slice(p * LANES, (p + 1) * LANES)
        qp = q_ref[0, 0, :, cols] * (A_DH ** -0.5)
        kp = k_all[:, cols]
        zero = jnp.zeros_like(qp)
        scores.append(_dot_nt(jnp.where(left_q, qp, zero), kp))
        scores.append(_dot_nt(jnp.where(left_q, zero, qp), kp))
    probs, maxes = [], []
    for h in range(A_HEADS):
        coef = -(2.0 ** (-8.0 * (h + 1) / A_HEADS)) * dilation
        s = jnp.where(ok, scores[h] + dist_f * coef, -jnp.inf)
        m = jnp.max(s, axis=1, keepdims=True)
        probs.append(jnp.exp(s - m).astype(BF16))
        maxes.append(m)
    lse_tile = jnp.zeros((nq, LANES), F32)
    for p in range(n_pairs):
        cols = slice(p * LANES, (p + 1) * LANES)
        vp = v_all[:, cols]
        one = jnp.ones_like(vp)
        pv_e = _dot(probs[2 * p], jnp.where(left_k, vp, one))
        pv_o = _dot(probs[2 * p + 1], jnp.where(left_k, one, vp))
        num = jnp.where(left_q, pv_e, pv_o)
        den = pltpu.roll(jnp.where(left_q, pv_o, pv_e), A_DH, 1)
        o_ref[0, 0, :, cols] = (num / den).astype(BF16)
        lse_tile = jnp.where(lane == 2 * p, maxes[2 * p] + jnp.log(pv_e[:, A_DH:A_DH + 1]), lse_tile)
        lse_tile = jnp.where(lane == 2 * p + 1, maxes[2 * p + 1] + jnp.log(pv_o[:, 0:1]), lse_tile)
    lse_ref[0, 0] = lse_tile


def _attn_group(qkv, dilation):
    bsz, d, ls, _ = qkv.shape
    aw = A_HEADS * A_DH
    nq = ATTN_BLOCK
    nb = ls // nq
    has_prev = nb > 1
    blk = (1, 1, nq, aw)
    in_specs = [pl.BlockSpec(blk, lambda b, r, n: (b, r, n, 0)),
                pl.BlockSpec(blk, lambda b, r, n: (b, r, n, 1)),
                pl.BlockSpec(blk, lambda b, r, n: (b, r, n, 2))]
    args = [qkv, qkv, qkv]
    if has_prev:
        in_specs += [pl.BlockSpec(blk, lambda b, r, n: (b, r, jnp.maximum(n - 1, 0), 1)),
                     pl.BlockSpec(blk, lambda b, r, n: (b, r, jnp.maximum(n - 1, 0), 2))]
        args += [qkv, qkv]
    return pl.pallas_call(
        functools.partial(_attn_kernel, dilation, has_prev),
        grid=(bsz, d, nb),
        in_specs=in_specs,
        out_specs=[pl.BlockSpec(blk, lambda b, r, n: (b, r, n, 0)),
                   pl.BlockSpec((1, 1, nq, LANES), lambda b, r, n: (b, r, n, 0))],
        out_shape=[jax.ShapeDtypeStruct((bsz, d, ls, aw), BF16),
                   jax.ShapeDtypeStruct((bsz, d, ls, LANES), F32)],
        scratch_shapes=[pltpu.VMEM((2 * nq, aw), BF16)] * 2 if has_prev else [],
        compiler_params=_cparams(("arbitrary", "arbitrary", "arbitrary")),
        name=f"dilated_attn_d{dilation}",
    )(*args)


def _head_maps(n_heads, dh):
    w = n_heads * dh
    e = np.zeros((LANES, w), np.float32)
    for h in range(n_heads):
        e[h, h * dh:(h + 1) * dh] = 1.0
    return jnp.asarray(e, BF16), jnp.asarray(e.T.copy(), BF16)


def _merge_kernel(o1_ref, o2_ref, o3_ref, l1_ref, l2_ref, l3_ref, g_ref, e_ref, p_ref, y_ref):
    l1, l2, l3 = l1_ref[...], l2_ref[...], l3_ref[...]
    mx = jnp.maximum(jnp.maximum(l1, l2), l3)
    w1, w2, w3 = jnp.exp(l1 - mx), jnp.exp(l2 - mx), jnp.exp(l3 - mx)
    inv = 1.0 / (w1 + w2 + w3)
    e = e_ref[...]
    o = (_dot_hilo(w1 * inv, e) * o1_ref[...].astype(F32)
         + _dot_hilo(w2 * inv, e) * o2_ref[...].astype(F32)
         + _dot_hilo(w3 * inv, e) * o3_ref[...].astype(F32))
    ms = _dot_hilo(o * o, p_ref[...]) * (1.0 / A_DH)
    scale = _dot_hilo(lax.rsqrt(ms + EPS), e)
    y_ref[...] = (o * scale * g_ref[...]).astype(BF16)


def _attn_merge(outs, lses, norm_g):
    t, aw = outs[0].shape
    tm = 512
    expand, pool = _head_maps(A_HEADS, A_DH)
    row = lambda i: (i, 0)
    const = lambda i: (0, 0)
    return pl.pallas_call(
        _merge_kernel,
        grid=(t // tm,),
        in_specs=[pl.BlockSpec((tm, aw), row)] * 3 + [pl.BlockSpec((tm, LANES), row)] * 3
        + [pl.BlockSpec((1, aw), const), pl.BlockSpec((LANES, aw), const), pl.BlockSpec((aw, LANES), const)],
        out_specs=pl.BlockSpec((tm, aw), row),
        out_shape=jax.ShapeDtypeStruct((t, aw), BF16),
        compiler_params=_cparams(("arbitrary",)),
        name="attn_merge",
    )(*outs, *lses, norm_g, expand, pool)


def _outproj_kernel(alpha, ym_ref, ya_ref, w_ref, x_ref, g1_ref, sc_ref, sh_ref, lg_ref, lb_ref,
                    rw_ref, rb_ref, x1_ref, h2_ref, ti_ref, tg_ref):
    half = ym_ref.shape[1]
    y = _dot(ym_ref[...], w_ref[0:half, :]) + _dot(ya_ref[...], w_ref[half:2 * half, :])
    z = alpha * x_ref[...] + (1.0 + g1_ref[0]) * y
    x1 = _layer_norm(z, lg_ref[...], lb_ref[...])
    x1_ref[...] = x1
    h2 = x1 * (1.0 + sc_ref[0]) + sh_ref[0]
    h2_ref[...] = h2
    logits = _dot(h2.astype(BF16), rw_ref[...]) + rb_ref[...]
    lane = lax.broadcasted_iota(I32, logits.shape, 1)
    lane_f = lane.astype(F32)
    work = jnp.where(lane < N_EXPERTS, logits, -jnp.inf)
    idx_tile = jnp.zeros(logits.shape, F32)
    val_tile = jnp.zeros(logits.shape, F32)
    top = None
    denom = None
    for k in range(TOP_K):
        mk = jnp.max(work, axis=1, keepdims=True)
        ik = jnp.min(jnp.where(work == mk, lane_f, float(LANES)), axis=1, keepdims=True)
        work = jnp.where(lane_f == ik, -jnp.inf, work)
        if k == 0:
            top = mk
        ek = jnp.exp(mk - top)
        denom = ek if k == 0 else denom + ek
        idx_tile = jnp.where(lane == k, ik, idx_tile)
        val_tile = jnp.where(lane == k, ek, val_tile)
    ti_ref[...] = idx_tile.astype(I32)
    tg_ref[...] = val_tile / denom


def _out_proj(y_m, y_a, w_out, x2, g1, sc2, sh2, ln_g, ln_b, rw, rb, alpha, seq):
    t, d = x2.shape
    half = y_m.shape[1]
    tm = 256
    per_b = seq // tm
    row = lambda i: (i, 0)
    const = lambda i: (0, 0)
    mod = lambda i: (i // per_b, 0, 0)
    return pl.pallas_call(
        functools.partial(_outproj_kernel, alpha),
        grid=(t // tm,),
        in_specs=[pl.BlockSpec((tm, half), row), pl.BlockSpec((tm, half), row),
                  pl.BlockSpec((2 * half, d), const), pl.BlockSpec((tm, d), row),
                  pl.BlockSpec((1, 1, d), mod), pl.BlockSpec((1, 1, d), mod), pl.BlockSpec((1, 1, d), mod),
                  pl.BlockSpec((1, d), const), pl.BlockSpec((1, d), const),
                  pl.BlockSpec((d, LANES), const), pl.BlockSpec((1, LANES), const)],
        out_specs=[pl.BlockSpec((tm, d), row), pl.BlockSpec((tm, d), row),
                   pl.BlockSpec((tm, LANES), row), pl.BlockSpec((tm, LANES), row)],
        out_shape=[jax.ShapeDtypeStruct((t, d), F32), jax.ShapeDtypeStruct((t, d), F32),
                   jax.ShapeDtypeStruct((t, LANES), I32), jax.ShapeDtypeStruct((t, LANES), F32)],
        compiler_params=_cparams(("arbitrary",)),
        name="out_proj_ln_router",
    )(y_m, y_a, w_out, x2, g1, sc2, sh2, ln_g, ln_b, rw, rb)


def _expert_kernel(ie_ref, ir_ref, ic_ref, iv_ref, tok_hbm, h_hbm, wgu_ref, bgu_ref, wdn_ref, bdn_ref,
                   sel_ref, y_hbm, tok_smem, xbuf, xb, yacc, wgu_b, wdn_b, gu_scr, sems):
    i = pl.program_id(0)
    j = pl.program_id(1)
    nj = pl.num_programs(1)
    row0 = ir_ref[i]
    nch = ic_ref[i]
    ch = MOE_CHUNK

    def row_copy(r):
        t = tok_smem[lax.shift_right_logical(r, 7), jnp.bitwise_and(r, LANES - 1)]
        return pltpu.make_async_copy(h_hbm.at[pl.ds(t, 1), :], xbuf.at[pl.ds(r, 1), :], sems.at[1])

    @pl.when(jnp.logical_and(j == 0, nch > 0))
    def _():
        tok_rows = pl.ds(lax.shift_right_logical(row0, 7), MOE_ITEM_ROWS // LANES)
        cp = pltpu.make_async_copy(tok_hbm.at[tok_rows, :], tok_smem, sems.at[0])
        cp.start()
        cp.wait()

        def issue(g, carry):
            for u in range(DMA_UNROLL):
                row_copy(g * DMA_UNROLL + u).start()
            return carry

        lax.fori_loop(0, nch * (ch // DMA_UNROLL), issue, 0)

        def drain(m, carry):
            rows = pl.ds(pl.multiple_of(m * ch, ch), ch)
            pltpu.make_async_copy(h_hbm.at[pl.ds(0, ch), :], xbuf.at[rows, :], sems.at[1]).wait()
            return carry

        lax.fori_loop(0, nch, drain, 0)

        def prep(m, carry):
            rows = pl.ds(pl.multiple_of(m * ch, ch), ch)
            xb[rows, :] = xbuf[rows, :].astype(BF16)
            yacc[rows, :] = jnp.broadcast_to(bdn_ref[0], (ch, yacc.shape[1]))
            return carry

        lax.fori_loop(0, nch, prep, 0)

    @pl.when(nch > 0)
    def _():
        wgu_b[...] = wgu_ref[0].astype(BF16)
        wdn_b[...] = wdn_ref[0].astype(BF16)
        bgu = bgu_ref[0]
        last = j == nj - 1

        def gate_up(m):
            rows = pl.ds(pl.multiple_of(m * ch, ch), ch)
            return _dot(xb[rows, :], wgu_b[...]) + bgu

        def finish(m, gu):
            rows = pl.ds(pl.multiple_of(m * ch, ch), ch)
            glu = jnp.minimum(gu, SWIGLU_LIMIT)
            f_glu = glu * _sigmoid(SWIGLU_ALPHA * glu)
            f_lin = jnp.clip(gu, -SWIGLU_LIMIT, SWIGLU_LIMIT) + 1.0
            prod = (pltpu.roll(f_glu, 1, 1) * f_lin).astype(BF16)
            parts = [_dot(prod[:, q * 2 * LANES:(q + 1) * 2 * LANES], sel_ref[...])
                     for q in range(MOE_TN // (2 * LANES))]
            act = jnp.concatenate(parts, axis=1).astype(BF16)
            yacc[rows, :] += _dot(act, wdn_b[...])

        gu_scr[...] = gate_up(0)

        def chunk(m, carry):
            gu = gu_scr[...]
            gu_scr[...] = gate_up(m + 1)
            finish(m, gu)
            return carry

        lax.fori_loop(0, nch - 1, chunk, 0)
        finish(nch - 1, gu_scr[...])

        @pl.when(last)
        def _():
            def out_copy(m):
                rows = pl.ds(pl.multiple_of(m * ch, ch), ch)
                dst = pl.ds(pl.multiple_of(row0 + m * ch, ch), ch)
                return pltpu.make_async_copy(yacc.at[rows, :], y_hbm.at[dst, :], sems.at[2])

            def issue(m, carry):
                out_copy(m).start()
                return carry

            lax.fori_loop(0, nch, issue, 0)

            def drain(m, carry):
                out_copy(m).wait()
                return carry

            lax.fori_loop(0, nch, drain, 0)


def _experts(h2, tok_pad, item_e, item_row0, item_nch, item_valid, w_gu, b_gu, w_dn, b_dn, n_rows):
    ne, d, two_de = w_gu.shape
    tn = MOE_TN
    nj = two_de // tn
    assert nj > 1, "the kernel separates its first and last column-tile steps"
    ni = item_e.shape[0]
    sel = np.zeros((2 * LANES, LANES), np.float32)
    sel[2 * np.arange(LANES) + 1, np.arange(LANES)] = 1.0

    def jmap(i, j, iv):
        return jnp.where(iv[i] > 0, j, nj - 1)

    grid_spec = pltpu.PrefetchScalarGridSpec(
        num_scalar_prefetch=4,
        grid=(ni, nj),
        in_specs=[pl.BlockSpec(memory_space=pl.ANY),
                  pl.BlockSpec(memory_space=pl.ANY),
                  pl.BlockSpec((1, d, tn), lambda i, j, ie, ir, ic, iv: (ie[i], 0, jmap(i, j, iv))),
                  pl.BlockSpec((1, 1, tn), lambda i, j, ie, ir, ic, iv: (ie[i], 0, jmap(i, j, iv))),
                  pl.BlockSpec((1, tn // 2, d), lambda i, j, ie, ir, ic, iv: (ie[i], jmap(i, j, iv), 0)),
                  pl.BlockSpec((1, 1, d), lambda i, j, ie, ir, ic, iv: (ie[i], 0, 0)),
                  pl.BlockSpec((2 * LANES, LANES), lambda i, j, ie, ir, ic, iv: (0, 0))],
        out_specs=pl.BlockSpec(memory_space=pl.ANY),
        scratch_shapes=[pltpu.SMEM((MOE_ITEM_ROWS // LANES, LANES), I32),
                        pltpu.VMEM((MOE_ITEM_ROWS, d), F32),
                        pltpu.VMEM((MOE_ITEM_ROWS, d), BF16),
                        pltpu.VMEM((MOE_ITEM_ROWS, d), F32),
                        pltpu.VMEM((d, tn), BF16),
                        pltpu.VMEM((tn // 2, d), BF16),
                        pltpu.VMEM((MOE_CHUNK, tn), F32),
                        pltpu.SemaphoreType.DMA((3,))],
    )
    return pl.pallas_call(
        _expert_kernel,
        grid_spec=grid_spec,
        out_shape=jax.ShapeDtypeStruct((n_rows, d), F32),
        compiler_params=_cparams(("arbitrary", "arbitrary")),
        name="moe_experts",
    )(item_e, item_row0, item_nch, item_valid, tok_pad, h2, w_gu, b_gu.reshape(ne, 1, two_de),
      w_dn, b_dn.reshape(ne, 1, d), jnp.asarray(sel, BF16))


def _combine_kernel(alpha, dest_ref, y_hbm, gate_ref, x1_ref, g2_ref, lg_ref, lb_ref, out_ref, buf, sem):
    tc = COMBINE_TOKENS

    def row_copy(s):
        k = jnp.bitwise_and(s, TOP_K - 1)
        r = lax.shift_right_logical(s, 2)
        src = dest_ref[0, 0, s]
        return pltpu.make_async_copy(y_hbm.at[pl.ds(src, 1), :], buf.at[k, pl.ds(r, 1), :], sem.at[0])

    def issue(g, carry):
        for u in range(DMA_UNROLL):
            row_copy(g * DMA_UNROLL + u).start()
        return carry

    lax.fori_loop(0, tc * TOP_K // DMA_UNROLL, issue, 0)
    for k in range(TOP_K):
        pltpu.make_async_copy(y_hbm.at[pl.ds(0, tc), :], buf.at[k], sem.at[0]).wait()

    gates = gate_ref[...]
    y = gates[:, 0:1] * buf[0]
    for k in range(1, TOP_K):
        y = y + gates[:, k:k + 1] * buf[k]
    z = alpha * x1_ref[...] + (1.0 + g2_ref[0]) * y
    out_ref[...] = _layer_norm(z, lg_ref[...], lb_ref[...])


def _combine(dest, y_pad, gates, x1, g2, ln_g, ln_b, alpha, seq):
    t, d = x1.shape
    tc = COMBINE_TOKENS
    per_b = seq // tc
    row = lambda i: (i, 0)
    const = lambda i: (0, 0)
    return pl.pallas_call(
        functools.partial(_combine_kernel, alpha),
        grid=(t // tc,),
        in_specs=[pl.BlockSpec((1, 1, tc * TOP_K), lambda i: (i, 0, 0), memory_space=pltpu.SMEM),
                  pl.BlockSpec(memory_space=pl.ANY),
                  pl.BlockSpec((tc, LANES), row), pl.BlockSpec((tc, d), row),
                  pl.BlockSpec((1, 1, d), lambda i: (i // per_b, 0, 0)),
                  pl.BlockSpec((1, d), const), pl.BlockSpec((1, d), const)],
        out_specs=pl.BlockSpec((tc, d), row),
        out_shape=jax.ShapeDtypeStruct((t, d), F32),
        scratch_shapes=[pltpu.VMEM((TOP_K, tc, d), F32), pltpu.SemaphoreType.DMA((1,))],
        compiler_params=_cparams(("arbitrary",)),
        name="moe_combine_ln",
    )(dest.reshape(t // tc, 1, tc * TOP_K), y_pad, gates, x1, g2, ln_g, ln_b)


def _routing_tables(top_idx):
    t = top_idx.shape[0]
    tk = t * TOP_K
    experts = jnp.arange(N_EXPERTS, dtype=I32)
    hits = [top_idx[:, k:k + 1] == experts for k in range(TOP_K)]
    onehot = sum(h.astype(I32) for h in hits)
    csum = jnp.cumsum(onehot, axis=0)
    counts = csum[-1]
    earlier = csum - onehot
    starts = jnp.cumsum(counts) - counts
    padded = ((counts + MOE_ROW_PAD - 1) // MOE_ROW_PAD) * MOE_ROW_PAD
    pad_end = jnp.cumsum(padded)
    pad_start = pad_end - padded
    dest = jnp.stack([jnp.sum(jnp.where(h, earlier + pad_start, 0), axis=1) for h in hits], axis=1).reshape(tk)

    n_rows = tk + N_EXPERTS * MOE_ROW_PAD
    n_tab = n_rows + MOE_ITEM_ROWS
    tok_sorted = (jnp.argsort(top_idx.reshape(tk)) // TOP_K).astype(I32)
    blk = jnp.arange(n_tab // MOE_ROW_PAD, dtype=I32) * MOE_ROW_PAD
    blk_e = jnp.minimum(jnp.searchsorted(pad_end, blk, side='right'), N_EXPERTS - 1).astype(I32)
    local = (blk - pad_start[blk_e])[:, None] + jnp.arange(MOE_ROW_PAD, dtype=I32)
    src = jnp.clip(starts[blk_e][:, None] + local, 0, tk - 1)
    live = jnp.logical_and(local < counts[blk_e][:, None], (blk < pad_end[-1])[:, None])
    tok_pad = jnp.where(live, tok_sorted[src], 0).reshape(-1, LANES)

    items_per = (padded + MOE_ITEM_ROWS - 1) // MOE_ITEM_ROWS
    item_end = jnp.cumsum(items_per)
    item_start = item_end - items_per
    n_items = N_EXPERTS + n_rows // MOE_ITEM_ROWS
    idx = jnp.arange(n_items, dtype=I32)
    valid = idx < item_end[-1]
    e_i = jnp.minimum(jnp.searchsorted(item_end, idx, side='right'), N_EXPERTS - 1).astype(I32)
    k_i = idx - item_start[e_i]
    row0 = pad_start[e_i] + k_i * MOE_ITEM_ROWS
    nrows = jnp.clip(padded[e_i] - k_i * MOE_ITEM_ROWS, 0, MOE_ITEM_ROWS)
    e_last = e_i[jnp.maximum(item_end[-1] - 1, 0)]
    item_e = jnp.where(valid, e_i, e_last).astype(I32)
    item_row0 = jnp.where(valid, row0, 0).astype(I32)
    item_nch = jnp.where(valid, nrows // MOE_CHUNK, 0).astype(I32)
    return tok_pad, dest, item_e, item_row0, item_nch, valid.astype(I32), n_rows


def kernel(x, c, w_ada, b_ada, w_in, b_in, conv_w, conv_b, m_norm_g, a_norm_g, w_out, ln1_g, ln1_b,
           router_w, router_b, w_gu, b_gu, w_dn, b_dn, ln2_g, ln2_b):
    bsz, seq, d = x.shape
    depth = w_ada.shape[0]
    t = bsz * seq
    alpha = float((2 * depth) ** 0.25)
    qk_w = 2 * M_HEADS * M_DQK
    mv_w = M_HEADS * M_DV
    aw = A_HEADS * A_DH
    gate_lo = qk_w + 2 * mv_w
    gate_hi = gate_lo + 2 * M_HEADS

    x2 = x.reshape(t, d)
    for l in range(depth):
        mod = _ada_mod(c, w_ada[l], b_ada[l]).reshape(bsz, 6, 1, d)
        sh1, sc1, g1, sh2, sc2, g2 = (mod[:, i] for i in range(6))

        w_main = jnp.concatenate([w_in[l][:, :gate_lo], w_in[l][:, gate_hi:]], axis=1).astype(BF16)
        b_main = jnp.concatenate([b_in[l][:gate_lo], b_in[l][gate_hi:]]).reshape(1, -1)
        w_gate = jnp.zeros((d, LANES), BF16).at[:, :2 * M_HEADS].set(w_in[l][:, gate_lo:gate_hi].astype(BF16))
        b_gate = jnp.zeros((1, LANES), F32).at[0, :2 * M_HEADS].set(b_in[l][gate_lo:gate_hi])
        proj, gates_c = _in_proj(x2, sc1, sh1, w_main, b_main, w_gate, b_gate, seq)

        gates_r = gates_c[:, :2 * M_HEADS].reshape(bsz, seq, 2 * M_HEADS).transpose(0, 2, 1)
        y_m = _mlstm(proj, gates_c, gates_r, conv_w[l], conv_b[l].reshape(1, -1),
                     m_norm_g[l].reshape(1, -1), bsz, seq)

        qkv = proj[:, gate_lo:].reshape(bsz, seq, 3 * aw)
        outs, lses = [], []
        for dil in DILATIONS:
            ls = seq // dil
            qkv_d = qkv.reshape(bsz, ls, dil, 3 * aw).transpose(0, 2, 1, 3)
            o_d, lse_d = _attn_group(qkv_d, dil)
            outs.append(o_d.transpose(0, 2, 1, 3).reshape(t, aw))
            lses.append(lse_d.transpose(0, 2, 1, 3).reshape(t, LANES))
        y_a = _attn_merge(outs, lses, a_norm_g[l].reshape(1, -1))

        rw = jnp.zeros((d, LANES), BF16).at[:, :N_EXPERTS].set(router_w[l].astype(BF16))
        rb = jnp.zeros((1, LANES), F32).at[0, :N_EXPERTS].set(router_b[l])
        x1, h2, top_idx, gates = _out_proj(y_m, y_a, w_out[l].astype(BF16), x2, g1, sc2, sh2,
                                           ln1_g[l].reshape(1, -1), ln1_b[l].reshape(1, -1), rw, rb, alpha, seq)

        tok_pad, dest, item_e, item_row0, item_nch, item_valid, n_rows = _routing_tables(top_idx[:, :TOP_K])
        y_pad = _experts(h2, tok_pad, item_e, item_row0, item_nch, item_valid,
                         w_gu[l], b_gu[l], w_dn[l], b_dn[l], n_rows)
        x2 = _combine(dest, y_pad, gates, x1, g2, ln2_g[l].reshape(1, -1), ln2_b[l].reshape(1, -1), alpha, seq)
    return x2.reshape(bsz, seq, d)
```

```python
import functools

import jax
import jax.numpy as jnp
import numpy as np
from jax import lax
from jax.experimental import pallas as pl
from jax.experimental.pallas import tpu as pltpu

F32 = jnp.float32
BF16 = jnp.bfloat16
I32 = jnp.int32

M_HEADS = 4
M_DQK = 128
M_DV = 256
CONV_WIDTH = 4
A_HEADS = 16
A_DH = 64
ATTN_BLOCK = 128
DILATIONS = (1, 4, 16)
N_EXPERTS = 32
TOP_K = 4
SWIGLU_ALPHA = 1.702
SWIGLU_LIMIT = 7.0
EPS = 1e-5

LANES = 128
VMEM_LIMIT = 56 * 1024 * 1024

MLSTM_CHUNK = 256
MOE_ROW_PAD = 256
MOE_CHUNK = 256
MOE_ITEM_ROWS = 1280
MOE_TN = 512
COMBINE_TOKENS = 128
DMA_UNROLL = 8


def _cparams(sem, vmem=VMEM_LIMIT):
    return pltpu.CompilerParams(dimension_semantics=sem, vmem_limit_bytes=vmem)


def _sigmoid(x):
    return 1.0 / (1.0 + jnp.exp(-x))


def _log_sigmoid(x):
    return jnp.minimum(x, 0.0) - jnp.log(1.0 + jnp.exp(-jnp.abs(x)))


def _layer_norm(z, g, b):
    mu = jnp.mean(z, axis=-1, keepdims=True)
    zc = z - mu
    var = jnp.mean(zc * zc, axis=-1, keepdims=True)
    return zc * lax.rsqrt(var + EPS) * g + b


def _dot(a, b):
    return jnp.dot(a, b, preferred_element_type=F32)


def _dot_nt(a, b):
    return lax.dot_general(a, b, (((1,), (1,)), ((), ())), preferred_element_type=F32)


def _store_token_major(ref, first_row, val):
    rows, d = val.shape
    segs = d // LANES
    for s in range(segs):
        ref[pl.ds(first_row * segs + s, rows, stride=segs), :] = val[:, s * LANES:(s + 1) * LANES]


def _load_token_major(ref, first_row, rows, segs):
    return jnp.concatenate([ref[pl.ds(first_row * segs + s, rows, stride=segs), :] for s in range(segs)], axis=1)


def _dot_hilo(a, sel):
    hi = a.astype(BF16)
    lo = (a - hi.astype(F32)).astype(BF16)
    return _dot(hi, sel) + _dot(lo, sel)


def _ada_kernel(c_ref, w_ref, b_ref, o_ref):
    c = c_ref[...]
    cond = c * _sigmoid(c)
    o_ref[...] = _dot(cond.astype(BF16), w_ref[...].astype(BF16)) + b_ref[...]


def _ada_mod(c, w_ada, b_ada):
    bsz, d = c.shape
    n = w_ada.shape[1]
    tn = 1024
    rows = 8
    c_pad = jnp.zeros((rows, d), F32).at[:bsz].set(c)
    out = pl.pallas_call(
        _ada_kernel,
        grid=(n // tn,),
        in_specs=[pl.BlockSpec((rows, d), lambda j: (0, 0)),
                  pl.BlockSpec((d, tn), lambda j: (0, j)),
                  pl.BlockSpec((1, tn), lambda j: (0, j))],
        out_specs=pl.BlockSpec((rows, tn), lambda j: (0, j)),
        out_shape=jax.ShapeDtypeStruct((rows, n), F32),
        compiler_params=_cparams(("arbitrary",)),
        name="ada_mod",
    )(c_pad, w_ada, b_ada.reshape(1, n))
    return out[:bsz]


def _inproj_kernel(x_ref, sc_ref, sh_ref, w_ref, b_ref, wg_ref, bg_ref, o_ref, g_ref, h_ref):
    j = pl.program_id(1)

    @pl.when(j == 0)
    def _():
        h = x_ref[...] * (1.0 + sc_ref[0]) + sh_ref[0]
        hb = h.astype(BF16)
        h_ref[...] = hb
        g_ref[...] = _dot(hb, wg_ref[...]) + bg_ref[...]

    o_ref[...] = (_dot(h_ref[...], w_ref[...]) + b_ref[...]).astype(BF16)


def _in_proj(x2, sc, sh, w_main, b_main, w_gate, b_gate, seq):
    t, d = x2.shape
    n = w_main.shape[1]
    tm, tn = 512, 1024
    per_b = seq // tm
    return pl.pallas_call(
        _inproj_kernel,
        grid=(t // tm, n // tn),
        in_specs=[pl.BlockSpec((tm, d), lambda i, j: (i, 0)),
                  pl.BlockSpec((1, 1, d), lambda i, j: (i // per_b, 0, 0)),
                  pl.BlockSpec((1, 1, d), lambda i, j: (i // per_b, 0, 0)),
                  pl.BlockSpec((d, tn), lambda i, j: (0, j)),
                  pl.BlockSpec((1, tn), lambda i, j: (0, j)),
                  pl.BlockSpec((d, LANES), lambda i, j: (0, 0)),
                  pl.BlockSpec((1, LANES), lambda i, j: (0, 0))],
        out_specs=[pl.BlockSpec((tm, tn), lambda i, j: (i, j)),
                   pl.BlockSpec((tm, LANES), lambda i, j: (i, 0))],
        out_shape=[jax.ShapeDtypeStruct((t, n), BF16),
                   jax.ShapeDtypeStruct((t, LANES), F32)],
        scratch_shapes=[pltpu.VMEM((tm, d), BF16)],
        compiler_params=_cparams(("arbitrary", "arbitrary")),
        name="in_proj",
    )(x2, sc, sh, w_main, b_main, w_gate, b_gate)


def _mlstm_kernel(qk_ref, v_ref, o_ref, gc_ref, gr_ref, cw_ref, cb_ref, ng_ref, out_ref,
                  ct_ref, n_ref, m_ref, prev_ref):
    c = pl.program_id(1)
    L = MLSTM_CHUNK

    @pl.when(c == 0)
    def _():
        ct_ref[...] = jnp.zeros_like(ct_ref)
        n_ref[...] = jnp.zeros_like(n_ref)
        m_ref[...] = jnp.zeros_like(m_ref)
        prev_ref[...] = jnp.zeros_like(prev_ref)

    x = qk_ref[...].astype(F32)
    prev = prev_ref[...]
    row = lax.broadcasted_iota(I32, (L, 1), 0)
    y = cw_ref[CONV_WIDTH - 1:CONV_WIDTH, :] * x + cb_ref[...]
    for k in range(1, CONV_WIDTH):
        xs = jnp.where(row < k, pltpu.roll(prev, k, 0), pltpu.roll(x, k, 0))
        y = y + cw_ref[CONV_WIDTH - 1 - k:CONV_WIDTH - k, :] * xs
    prev_ref[...] = x
    y = y * _sigmoid(y)

    gc = gc_ref[...]
    gr = gr_ref[0]
    ti = lax.broadcasted_iota(I32, (L, L), 0)
    si = lax.broadcasted_iota(I32, (L, L), 1)
    causal = si <= ti
    tril = causal.astype(F32)
    triu = (ti <= si).astype(F32)
    b_cols = jnp.dot(tril, _log_sigmoid(gc), precision=lax.Precision.HIGHEST, preferred_element_type=F32)
    b_rows = jnp.dot(_log_sigmoid(gr), triu, precision=lax.Precision.HIGHEST, preferred_element_type=F32)

    qk_w = M_HEADS * M_DQK
    for h in range(M_HEADS):
        qf = y[:, h * M_DQK:(h + 1) * M_DQK]
        kf = y[:, qk_w + h * M_DQK:qk_w + (h + 1) * M_DQK] * (M_DQK ** -0.5)
        vb = v_ref[:, h * M_DV:(h + 1) * M_DV]
        qb = qf.astype(BF16)
        kb = kf.astype(BF16)

        bc = b_cols[:, M_HEADS + h:M_HEADS + h + 1]
        ic = gc[:, h:h + 1]
        br = b_rows[M_HEADS + h:M_HEADS + h + 1, :]
        ir = gr[h:h + 1, :]
        m_prev = m_ref[h][0:1, 0:1]
        n_prev = n_ref[h][0:1, :]

        dm = jnp.where(causal, bc - br + ir, -jnp.inf)
        inter = bc + m_prev
        mt = jnp.maximum(inter, jnp.max(dm, axis=1, keepdims=True))
        a = jnp.exp(dm - mt) * _dot_nt(qb, kb)
        e_int = jnp.exp(inter - mt)
        num = _dot(a.astype(BF16), vb) + e_int * _dot(qb, ct_ref[h].astype(BF16))
        den = jnp.sum(a, axis=1, keepdims=True) + e_int * jnp.sum(qf * n_prev, axis=1, keepdims=True)
        hh = num / jnp.maximum(jnp.abs(den), jnp.exp(-mt))

        b_last = bc[L - 1:L, :]
        g_col = b_last - bc + ic
        m_new = jnp.maximum(b_last + m_prev, jnp.max(g_col, axis=0, keepdims=True))
        w_col = jnp.exp(g_col - m_new)
        decay = jnp.exp(b_last + m_prev - m_new)
        wv = (w_col * vb.astype(F32)).astype(BF16)
        ct_ref[h] = decay * ct_ref[h] + _dot(kf.T.astype(BF16), wv)
        n_new = decay * n_prev + jnp.sum(w_col * kf, axis=0, keepdims=True)
        n_ref[h] = jnp.broadcast_to(n_new, n_ref.shape[1:])
        m_ref[h] = jnp.broadcast_to(m_new, m_ref.shape[1:])

        ms = jnp.mean(hh * hh, axis=1, keepdims=True)
        og = o_ref[:, h * M_DV:(h + 1) * M_DV].astype(F32)
        yh = hh * lax.rsqrt(ms + EPS) * ng_ref[:, h * M_DV:(h + 1) * M_DV] * _sigmoid(og)
        out_ref[:, h * M_DV:(h + 1) * M_DV] = yh.astype(BF16)


def _mlstm(proj, gates_c, gates_r, conv_w, conv_b, norm_g, bsz, seq):
    L = MLSTM_CHUNK
    nc = seq // L
    t = bsz * seq
    w = M_HEADS * M_DV
    return pl.pallas_call(
        _mlstm_kernel,
        grid=(bsz, nc),
        in_specs=[pl.BlockSpec((L, w), lambda b, c: (b * nc + c, 0)),
                  pl.BlockSpec((L, w), lambda b, c: (b * nc + c, 1)),
                  pl.BlockSpec((L, w), lambda b, c: (b * nc + c, 2)),
                  pl.BlockSpec((L, LANES), lambda b, c: (b * nc + c, 0)),
                  pl.BlockSpec((1, 8, L), lambda b, c: (b, 0, c)),
                  pl.BlockSpec((CONV_WIDTH, w), lambda b, c: (0, 0)),
                  pl.BlockSpec((1, w), lambda b, c: (0, 0)),
                  pl.BlockSpec((1, w), lambda b, c: (0, 0))],
        out_specs=pl.BlockSpec((L, w), lambda b, c: (b * nc + c, 0)),
        out_shape=jax.ShapeDtypeStruct((t, w), BF16),
        scratch_shapes=[pltpu.VMEM((M_HEADS, M_DQK, M_DV), F32),
                        pltpu.VMEM((M_HEADS, 8, M_DQK), F32),
                        pltpu.VMEM((M_HEADS, 8, LANES), F32),
                        pltpu.VMEM((L, w), F32)],
        compiler_params=_cparams(("arbitrary", "arbitrary")),
        name="mlstm",
    )(proj, proj, proj, gates_c, gates_r, conv_w, conv_b, norm_g)


def _attn_kernel(dilation, has_prev, *refs):
    nq = ATTN_BLOCK
    if has_prev:
        q_ref, kc_ref, vc_ref, kp_ref, vp_ref, o_ref, lse_ref, k_all, v_all = refs
        k_all[0:nq, :] = kp_ref[0, 0]
        k_all[nq:2 * nq, :] = kc_ref[0, 0]
        v_all[0:nq, :] = vp_ref[0, 0]
        v_all[nq:2 * nq, :] = vc_ref[0, 0]
        nk = 2 * nq
    else:
        q_ref, k_all, v_all, o_ref, lse_ref = refs
        k_all, v_all = k_all.at[0, 0], v_all.at[0, 0]
        nk = nq
    n = pl.program_id(2)
    qi = lax.broadcasted_iota(I32, (nq, nk), 0)
    ki = lax.broadcasted_iota(I32, (nq, nk), 1)
    dist = qi - ki + (nk - nq)
    ok = jnp.logical_and(dist >= 0, dist <= nq)
    if has_prev:
        ok = jnp.logical_and(ok, jnp.logical_or(ki >= nq, n > 0))
    dist_f = dist.astype(F32)
    lane = lax.broadcasted_iota(I32, (nq, LANES), 1)
    left_q = lane < A_DH
    left_k = lax.broadcasted_iota(I32, (nk, LANES), 1) < A_DH
    n_pairs = A_HEADS // 2

    scores = []
    for p in range(n_pairs):
        cols = slice(p * LANES, (p + 1) * LANES)
        qp = q_ref[0, 0, :, cols] * (A_DH ** -0.5)
        kp = k_all[:, cols]
        zero = jnp.zeros_like(qp)
        scores.append(_dot_nt(jnp.where(left_q, qp, zero), kp))
        scores.append(_dot_nt(jnp.where(left_q, zero, qp), kp))
    probs, maxes = [], []
    for h in range(A_HEADS):
        coef = -(2.0 ** (-8.0 * (h + 1) / A_HEADS)) * dilation
        s = jnp.where(ok, scores[h] + dist_f * coef, -jnp.inf)
        m = jnp.max(s, axis=1, keepdims=True)
        probs.append(jnp.exp(s - m).astype(BF16))
        maxes.append(m)
    lse_tile = jnp.zeros((nq, LANES), F32)
    for p in range(n_pairs):
        cols = slice(p * LANES, (p + 1) * LANES)
        vp = v_all[:, cols]
        one = jnp.ones_like(vp)
        pv_e = _dot(probs[2 * p], jnp.where(left_k, vp, one))
        pv_o = _dot(probs[2 * p + 1], jnp.where(left_k, one, vp))
        num = jnp.where(left_q, pv_e, pv_o)
        den = pltpu.roll(jnp.where(left_q, pv_o, pv_e), A_DH, 1)
        o_ref[0, 0, :, cols] = (num / den).astype(BF16)
        lse_tile = jnp.where(lane == 2 * p, maxes[2 * p] + jnp.log(pv_e[:, A_DH:A_DH + 1]), lse_tile)
        lse_tile = jnp.where(lane == 2 * p + 1, maxes[2 * p + 1] + jnp.log(pv_o[:, 0:1]), lse_tile)
    lse_ref[0, 0] = lse_tile


def _attn_group(qkv, dilation):
    bsz, d, ls, _ = qkv.shape
    aw = A_HEADS * A_DH
    nq = ATTN_BLOCK
    nb = ls // nq
    has_prev = nb > 1
    blk = (1, 1, nq, aw)
    in_specs = [pl.BlockSpec(blk, lambda b, r, n: (b, r, n, 0)),
                pl.BlockSpec(blk, lambda b, r, n: (b, r, n, 1)),
                pl.BlockSpec(blk, lambda b, r, n: (b, r, n, 2))]
    args = [qkv, qkv, qkv]
    if has_prev:
        in_specs += [pl.BlockSpec(blk, lambda b, r, n: (b, r, jnp.maximum(n - 1, 0), 1)),
                     pl.BlockSpec(blk, lambda b, r, n: (b, r, jnp.maximum(n - 1, 0), 2))]
        args += [qkv, qkv]
    return pl.pallas_call(
        functools.partial(_attn_kernel, dilation, has_prev),
        grid=(bsz, d, nb),
        in_specs=in_specs,
        out_specs=[pl.BlockSpec(blk, lambda b, r, n: (b, r, n, 0)),
                   pl.BlockSpec((1, 1, nq, LANES), lambda b, r, n: (b, r, n, 0))],
        out_shape=[jax.ShapeDtypeStruct((bsz, d, ls, aw), BF16),
                   jax.ShapeDtypeStruct((bsz, d, ls, LANES), F32)],
        scratch_shapes=[pltpu.VMEM((2 * nq, aw), BF16)] * 2 if has_prev else [],
        compiler_params=_cparams(("arbitrary", "arbitrary", "arbitrary")),
        name=f"dilated_attn_d{dilation}",
    )(*args)


def _head_maps(n_heads, dh):
    w = n_heads * dh
    e = np.zeros((LANES, w), np.float32)
    for h in range(n_heads):
        e[h, h * dh:(h + 1) * dh] = 1.0
    return jnp.asarray(e, BF16), jnp.asarray(e.T.copy(), BF16)


def _merge_kernel(o1_ref, o2_ref, o3_ref, l1_ref, l2_ref, l3_ref, g_ref, e_ref, p_ref, y_ref):
    l1, l2, l3 = l1_ref[...], l2_ref[...], l3_ref[...]
    mx = jnp.maximum(jnp.maximum(l1, l2), l3)
    w1, w2, w3 = jnp.exp(l1 - mx), jnp.exp(l2 - mx), jnp.exp(l3 - mx)
    inv = 1.0 / (w1 + w2 + w3)
    e = e_ref[...]
    o = (_dot_hilo(w1 * inv, e) * o1_ref[...].astype(F32)
         + _dot_hilo(w2 * inv, e) * o2_ref[...].astype(F32)
         + _dot_hilo(w3 * inv, e) * o3_ref[...].astype(F32))
    ms = _dot_hilo(o * o, p_ref[...]) * (1.0 / A_DH)
    scale = _dot_hilo(lax.rsqrt(ms + EPS), e)
    y_ref[...] = (o * scale * g_ref[...]).astype(BF16)


def _attn_merge(outs, lses, norm_g):
    t, aw = outs[0].shape
    tm = 512
    expand, pool = _head_maps(A_HEADS, A_DH)
    row = lambda i: (i, 0)
    const = lambda i: (0, 0)
    return pl.pallas_call(
        _merge_kernel,
        grid=(t // tm,),
        in_specs=[pl.BlockSpec((tm, aw), row)] * 3 + [pl.BlockSpec((tm, LANES), row)] * 3
        + [pl.BlockSpec((1, aw), const), pl.BlockSpec((LANES, aw), const), pl.BlockSpec((aw, LANES), const)],
        out_specs=pl.BlockSpec((tm, aw), row),
        out_shape=jax.ShapeDtypeStruct((t, aw), BF16),
        compiler_params=_cparams(("arbitrary",)),
        name="attn_merge",
    )(*outs, *lses, norm_g, expand, pool)


def _outproj_kernel(alpha, ym_ref, ya_ref, w_ref, x_ref, g1_ref, sc_ref, sh_ref, lg_ref, lb_ref,
                    rw_ref, rb_ref, x1_ref, h2_ref, ti_ref, tg_ref):
    half = ym_ref.shape[1]
    y = _dot(ym_ref[...], w_ref[0:half, :]) + _dot(ya_ref[...], w_ref[half:2 * half, :])
    z = alpha * x_ref[...] + (1.0 + g1_ref[0]) * y
    x1 = _layer_norm(z, lg_ref[...], lb_ref[...])
    x1_ref[...] = x1
    h2 = x1 * (1.0 + sc_ref[0]) + sh_ref[0]
    _store_token_major(h2_ref, 0, h2)
    logits = _dot(h2.astype(BF16), rw_ref[...]) + rb_ref[...]
    lane = lax.broadcasted_iota(I32, logits.shape, 1)
    lane_f = lane.astype(F32)
    work = jnp.where(lane < N_EXPERTS, logits, -jnp.inf)
    idx_tile = jnp.zeros(logits.shape, F32)
    val_tile = jnp.zeros(logits.shape, F32)
    top = None
    denom = None
    for k in range(TOP_K):
        mk = jnp.max(work, axis=1, keepdims=True)
        ik = jnp.min(jnp.where(work == mk, lane_f, float(LANES)), axis=1, keepdims=True)
        work = jnp.where(lane_f == ik, -jnp.inf, work)
        if k == 0:
            top = mk
        ek = jnp.exp(mk - top)
        denom = ek if k == 0 else denom + ek
        idx_tile = jnp.where(lane == k, ik, idx_tile)
        val_tile = jnp.where(lane == k, ek, val_tile)
    ti_ref[...] = idx_tile.astype(I32)
    tg_ref[...] = val_tile / denom


def _out_proj(y_m, y_a, w_out, x2, g1, sc2, sh2, ln_g, ln_b, rw, rb, alpha, seq):
    t, d = x2.shape
    half = y_m.shape[1]
    tm = 256
    per_b = seq // tm
    row = lambda i: (i, 0)
    const = lambda i: (0, 0)
    mod = lambda i: (i // per_b, 0, 0)
    return pl.pallas_call(
        functools.partial(_outproj_kernel, alpha),
        grid=(t // tm,),
        in_specs=[pl.BlockSpec((tm, half), row), pl.BlockSpec((tm, half), row),
                  pl.BlockSpec((2 * half, d), const), pl.BlockSpec((tm, d), row),
                  pl.BlockSpec((1, 1, d), mod), pl.BlockSpec((1, 1, d), mod), pl.BlockSpec((1, 1, d), mod),
                  pl.BlockSpec((1, d), const), pl.BlockSpec((1, d), const),
                  pl.BlockSpec((d, LANES), const), pl.BlockSpec((1, LANES), const)],
        out_specs=[pl.BlockSpec((tm, d), row), pl.BlockSpec((tm * (d // LANES), LANES), row),
                   pl.BlockSpec((tm, LANES), row), pl.BlockSpec((tm, LANES), row)],
        out_shape=[jax.ShapeDtypeStruct((t, d), F32), jax.ShapeDtypeStruct((t * (d // LANES), LANES), F32),
                   jax.ShapeDtypeStruct((t, LANES), I32), jax.ShapeDtypeStruct((t, LANES), F32)],
        compiler_params=_cparams(("arbitrary",)),
        name="out_proj_ln_router",
    )(y_m, y_a, w_out, x2, g1, sc2, sh2, ln_g, ln_b, rw, rb)


def _expert_kernel(ie_ref, ir_ref, ic_ref, iv_ref, tok_hbm, h_hbm, wgu_ref, bgu_ref, wdn_ref, bdn_ref,
                   sel_ref, y_hbm, tok_smem, xbuf, xb, yacc, wgu_b, wdn_b, gu_scr, sems):
    i = pl.program_id(0)
    j = pl.program_id(1)
    nj = pl.num_programs(1)
    row0 = ir_ref[i]
    nch = ic_ref[i]
    ch = MOE_CHUNK

    segs = yacc.shape[1] // LANES

    def row_copy(r):
        t = tok_smem[lax.shift_right_logical(r, 7), jnp.bitwise_and(r, LANES - 1)]
        return pltpu.make_async_copy(h_hbm.at[pl.ds(pl.multiple_of(t * segs, segs), segs), :],
                                     xbuf.at[pl.ds(pl.multiple_of(r * segs, segs), segs), :], sems.at[1])

    @pl.when(jnp.logical_and(j == 0, nch > 0))
    def _():
        tok_rows = pl.ds(lax.shift_right_logical(row0, 7), MOE_ITEM_ROWS // LANES)
        cp = pltpu.make_async_copy(tok_hbm.at[tok_rows, :], tok_smem, sems.at[0])
        cp.start()
        cp.wait()

        def issue(g, carry):
            for u in range(DMA_UNROLL):
                row_copy(g * DMA_UNROLL + u).start()
            return carry

        lax.fori_loop(0, nch * (ch // DMA_UNROLL), issue, 0)

        def drain(m, carry):
            slab = pl.ds(pl.multiple_of(m * ch * segs, ch * segs), ch * segs)
            pltpu.make_async_copy(h_hbm.at[pl.ds(0, ch * segs), :], xbuf.at[slab, :], sems.at[1]).wait()
            return carry

        lax.fori_loop(0, nch, drain, 0)

        def prep(m, carry):
            rows = pl.ds(pl.multiple_of(m * ch, ch), ch)
            xb[rows, :] = _load_token_major(xbuf, pl.multiple_of(m * ch, ch), ch, segs).astype(BF16)
            yacc[rows, :] = jnp.broadcast_to(bdn_ref[0], (ch, yacc.shape[1]))
            return carry

        lax.fori_loop(0, nch, prep, 0)

    @pl.when(nch > 0)
    def _():
        wgu_b[...] = wgu_ref[0].astype(BF16)
        wdn_b[...] = wdn_ref[0].astype(BF16)
        bgu = bgu_ref[0]
        last = j == nj - 1

        def gate_up(m):
            rows = pl.ds(pl.multiple_of(m * ch, ch), ch)
            return _dot(xb[rows, :], wgu_b[...]) + bgu

        def finish(m, gu):
            rows = pl.ds(pl.multiple_of(m * ch, ch), ch)
            glu = jnp.minimum(gu, SWIGLU_LIMIT)
            f_glu = glu * _sigmoid(SWIGLU_ALPHA * glu)
            f_lin = jnp.clip(gu, -SWIGLU_LIMIT, SWIGLU_LIMIT) + 1.0
            prod = (pltpu.roll(f_glu, 1, 1) * f_lin).astype(BF16)
            parts = [_dot(prod[:, q * 2 * LANES:(q + 1) * 2 * LANES], sel_ref[...])
                     for q in range(MOE_TN // (2 * LANES))]
            act = jnp.concatenate(parts, axis=1).astype(BF16)
            yacc[rows, :] += _dot(act, wdn_b[...])

        gu_scr[...] = gate_up(0)

        def chunk(m, carry):
            gu = gu_scr[...]
            gu_scr[...] = gate_up(m + 1)
            finish(m, gu)
            return carry

        lax.fori_loop(0, nch - 1, chunk, 0)
        finish(nch - 1, gu_scr[...])

        @pl.when(last)
        def _():
            def out_copy(m):
                src = pl.ds(pl.multiple_of(m * ch * segs, ch * segs), ch * segs)
                dst = pl.ds(pl.multiple_of((row0 + m * ch) * segs, ch * segs), ch * segs)
                return pltpu.make_async_copy(xbuf.at[src, :], y_hbm.at[dst, :], sems.at[2])

            def issue(m, carry):
                rows = pl.ds(pl.multiple_of(m * ch, ch), ch)
                _store_token_major(xbuf, pl.multiple_of(m * ch, ch), yacc[rows, :])
                out_copy(m).start()
                return carry

            lax.fori_loop(0, nch, issue, 0)

            def drain(m, carry):
                out_copy(m).wait()
                return carry

            lax.fori_loop(0, nch, drain, 0)


def _experts(h2, tok_pad, item_e, item_row0, item_nch, item_valid, w_gu, b_gu, w_dn, b_dn, n_rows):
    ne, d, two_de = w_gu.shape
    tn = MOE_TN
    nj = two_de // tn
    assert nj > 1, "the kernel separates its first and last column-tile steps"
    ni = item_e.shape[0]
    sel = np.zeros((2 * LANES, LANES), np.float32)
    sel[2 * np.arange(LANES) + 1, np.arange(LANES)] = 1.0

    def jmap(i, j, iv):
        return jnp.where(iv[i] > 0, j, nj - 1)

    grid_spec = pltpu.PrefetchScalarGridSpec(
        num_scalar_prefetch=4,
        grid=(ni, nj),
        in_specs=[pl.BlockSpec(memory_space=pl.ANY),
                  pl.BlockSpec(memory_space=pl.ANY),
                  pl.BlockSpec((1, d, tn), lambda i, j, ie, ir, ic, iv: (ie[i], 0, jmap(i, j, iv))),
                  pl.BlockSpec((1, 1, tn), lambda i, j, ie, ir, ic, iv: (ie[i], 0, jmap(i, j, iv))),
                  pl.BlockSpec((1, tn // 2, d), lambda i, j, ie, ir, ic, iv: (ie[i], jmap(i, j, iv), 0)),
                  pl.BlockSpec((1, 1, d), lambda i, j, ie, ir, ic, iv: (ie[i], 0, 0)),
                  pl.BlockSpec((2 * LANES, LANES), lambda i, j, ie, ir, ic, iv: (0, 0))],
        out_specs=pl.BlockSpec(memory_space=pl.ANY),
        scratch_shapes=[pltpu.SMEM((MOE_ITEM_ROWS // LANES, LANES), I32),
                        pltpu.VMEM((MOE_ITEM_ROWS * (d // LANES), LANES), F32),
                        pltpu.VMEM((MOE_ITEM_ROWS, d), BF16),
                        pltpu.VMEM((MOE_ITEM_ROWS, d), F32),
                        pltpu.VMEM((d, tn), BF16),
                        pltpu.VMEM((tn // 2, d), BF16),
                        pltpu.VMEM((MOE_CHUNK, tn), F32),
                        pltpu.SemaphoreType.DMA((3,))],
    )
    return pl.pallas_call(
        _expert_kernel,
        grid_spec=grid_spec,
        out_shape=jax.ShapeDtypeStruct((n_rows * (d // LANES), LANES), F32),
        compiler_params=_cparams(("arbitrary", "arbitrary")),
        name="moe_experts",
    )(item_e, item_row0, item_nch, item_valid, tok_pad, h2, w_gu, b_gu.reshape(ne, 1, two_de),
      w_dn, b_dn.reshape(ne, 1, d), jnp.asarray(sel, BF16))


def _combine_kernel(alpha, dest_ref, y_hbm, gate_ref, x1_ref, g2_ref, lg_ref, lb_ref, out_ref, buf, sem):
    tc = COMBINE_TOKENS
    segs = x1_ref.shape[1] // LANES

    def row_copy(s):
        k = jnp.bitwise_and(s, TOP_K - 1)
        r = lax.shift_right_logical(s, 2)
        src = dest_ref[0, 0, s]
        return pltpu.make_async_copy(y_hbm.at[pl.ds(pl.multiple_of(src * segs, segs), segs), :],
                                     buf.at[k, pl.ds(pl.multiple_of(r * segs, segs), segs), :], sem.at[0])

    def issue(g, carry):
        for u in range(DMA_UNROLL):
            row_copy(g * DMA_UNROLL + u).start()
        return carry

    lax.fori_loop(0, tc * TOP_K // DMA_UNROLL, issue, 0)
    for k in range(TOP_K):
        pltpu.make_async_copy(y_hbm.at[pl.ds(0, tc * segs), :], buf.at[k], sem.at[0]).wait()

    gates = gate_ref[...]
    y = gates[:, 0:1] * _load_token_major(buf.at[0], 0, tc, segs)
    for k in range(1, TOP_K):
        y = y + gates[:, k:k + 1] * _load_token_major(buf.at[k], 0, tc, segs)
    z = alpha * x1_ref[...] + (1.0 + g2_ref[0]) * y
    out_ref[...] = _layer_norm(z, lg_ref[...], lb_ref[...])


def _combine(dest, y_pad, gates, x1, g2, ln_g, ln_b, alpha, seq):
    t, d = x1.shape
    tc = COMBINE_TOKENS
    per_b = seq // tc
    row = lambda i: (i, 0)
    const = lambda i: (0, 0)
    return pl.pallas_call(
        functools.partial(_combine_kernel, alpha),
        grid=(t // tc,),
        in_specs=[pl.BlockSpec((1, 1, tc * TOP_K), lambda i: (i, 0, 0), memory_space=pltpu.SMEM),
                  pl.BlockSpec(memory_space=pl.ANY),
                  pl.BlockSpec((tc, LANES), row), pl.BlockSpec((tc, d), row),
                  pl.BlockSpec((1, 1, d), lambda i: (i // per_b, 0, 0)),
                  pl.BlockSpec((1, d), const), pl.BlockSpec((1, d), const)],
        out_specs=pl.BlockSpec((tc, d), row),
        out_shape=jax.ShapeDtypeStruct((t, d), F32),
        scratch_shapes=[pltpu.VMEM((TOP_K, tc * (d // LANES), LANES), F32), pltpu.SemaphoreType.DMA((1,))],
        compiler_params=_cparams(("arbitrary",)),
        name="moe_combine_ln",
    )(dest.reshape(t // tc, 1, tc * TOP_K), y_pad, gates, x1, g2, ln_g, ln_b)


def _routing_tables(top_idx):
    t = top_idx.shape[0]
    tk = t * TOP_K
    experts = jnp.arange(N_EXPERTS, dtype=I32)
    hits = [top_idx[:, k:k + 1] == experts for k in range(TOP_K)]
    onehot = sum(h.astype(I32) for h in hits)
    csum = jnp.cumsum(onehot, axis=0)
    counts = csum[-1]
    earlier = csum - onehot
    starts = jnp.cumsum(counts) - counts
    padded = ((counts + MOE_ROW_PAD - 1) // MOE_ROW_PAD) * MOE_ROW_PAD
    pad_end = jnp.cumsum(padded)
    pad_start = pad_end - padded
    dest = jnp.stack([jnp.sum(jnp.where(h, earlier + pad_start, 0), axis=1) for h in hits], axis=1).reshape(tk)

    n_rows = tk + N_EXPERTS * MOE_ROW_PAD
    n_tab = n_rows + MOE_ITEM_ROWS
    tok_sorted = (jnp.argsort(top_idx.reshape(tk)) // TOP_K).astype(I32)
    blk = jnp.arange(n_tab // MOE_ROW_PAD, dtype=I32) * MOE_ROW_PAD
    blk_e = jnp.minimum(jnp.searchsorted(pad_end, blk, side='right'), N_EXPERTS - 1).astype(I32)
    local = (blk - pad_start[blk_e])[:, None] + jnp.arange(MOE_ROW_PAD, dtype=I32)
    src = jnp.clip(starts[blk_e][:, None] + local, 0, tk - 1)
    live = jnp.logical_and(local < counts[blk_e][:, None], (blk < pad_end[-1])[:, None])
    tok_pad = jnp.where(live, tok_sorted[src], 0).reshape(-1, LANES)

    items_per = (padded + MOE_ITEM_ROWS - 1) // MOE_ITEM_ROWS
    item_end = jnp.cumsum(items_per)
    item_start = item_end - items_per
    n_items = N_EXPERTS + n_rows // MOE_ITEM_ROWS
    idx = jnp.arange(n_items, dtype=I32)
    valid = idx < item_end[-1]
    e_i = jnp.minimum(jnp.searchsorted(item_end, idx, side='right'), N_EXPERTS - 1).astype(I32)
    k_i = idx - item_start[e_i]
    row0 = pad_start[e_i] + k_i * MOE_ITEM_ROWS
    nrows = jnp.clip(padded[e_i] - k_i * MOE_ITEM_ROWS, 0, MOE_ITEM_ROWS)
    e_last = e_i[jnp.maximum(item_end[-1] - 1, 0)]
    item_e = jnp.where(valid, e_i, e_last).astype(I32)
    item_row0 = jnp.where(valid, row0, 0).astype(I32)
    item_nch = jnp.where(valid, nrows // MOE_CHUNK, 0).astype(I32)
    return tok_pad, dest, item_e, item_row0, item_nch, valid.astype(I32), n_rows


def kernel(x, c, w_ada, b_ada, w_in, b_in, conv_w, conv_b, m_norm_g, a_norm_g, w_out, ln1_g, ln1_b,
           router_w, router_b, w_gu, b_gu, w_dn, b_dn, ln2_g, ln2_b):
    bsz, seq, d = x.shape
    depth = w_ada.shape[0]
    t = bsz * seq
    alpha = float((2 * depth) ** 0.25)
    qk_w = 2 * M_HEADS * M_DQK
    mv_w = M_HEADS * M_DV
    aw = A_HEADS * A_DH
    gate_lo = qk_w + 2 * mv_w
    gate_hi = gate_lo + 2 * M_HEADS

    x2 = x.reshape(t, d)
    for l in range(depth):
        mod = _ada_mod(c, w_ada[l], b_ada[l]).reshape(bsz, 6, 1, d)
        sh1, sc1, g1, sh2, sc2, g2 = (mod[:, i] for i in range(6))

        w_main = jnp.concatenate([w_in[l][:, :gate_lo], w_in[l][:, gate_hi:]], axis=1).astype(BF16)
        b_main = jnp.concatenate([b_in[l][:gate_lo], b_in[l][gate_hi:]]).reshape(1, -1)
        w_gate = jnp.zeros((d, LANES), BF16).at[:, :2 * M_HEADS].set(w_in[l][:, gate_lo:gate_hi].astype(BF16))
        b_gate = jnp.zeros((1, LANES), F32).at[0, :2 * M_HEADS].set(b_in[l][gate_lo:gate_hi])
        proj, gates_c = _in_proj(x2, sc1, sh1, w_main, b_main, w_gate, b_gate, seq)

        gates_r = gates_c[:, :2 * M_HEADS].reshape(bsz, seq, 2 * M_HEADS).transpose(0, 2, 1)
        y_m = _mlstm(proj, gates_c, gates_r, conv_w[l], conv_b[l].reshape(1, -1),
                     m_norm_g[l].reshape(1, -1), bsz, seq)

        qkv = proj[:, gate_lo:].reshape(bsz, seq, 3 * aw)
        outs, lses = [], []
        for dil in DILATIONS:
            ls = seq // dil
            qkv_d = qkv.reshape(bsz, ls, dil, 3 * aw).transpose(0, 2, 1, 3)
            o_d, lse_d = _attn_group(qkv_d, dil)
            outs.append(o_d.transpose(0, 2, 1, 3).reshape(t, aw))
            lses.append(lse_d.transpose(0, 2, 1, 3).reshape(t, LANES))
        y_a = _attn_merge(outs, lses, a_norm_g[l].reshape(1, -1))

        rw = jnp.zeros((d, LANES), BF16).at[:, :N_EXPERTS].set(router_w[l].astype(BF16))
        rb = jnp.zeros((1, LANES), F32).at[0, :N_EXPERTS].set(router_b[l])
        x1, h2, top_idx, gates = _out_proj(y_m, y_a, w_out[l].astype(BF16), x2, g1, sc2, sh2,
                                           ln1_g[l].reshape(1, -1), ln1_b[l].reshape(1, -1), rw, rb, alpha, seq)

        tok_pad, dest, item_e, item_row0, item_nch, item_valid, n_rows = _routing_tables(top_idx[:, :TOP_K])
        y_pad = _experts(h2, tok_pad, item_e, item_row0, item_nch, item_valid,
                         w_gu[l], b_gu[l], w_dn[l], b_dn[l], n_rows)
        x2 = _combine(dest, y_pad, gates, x1, g2, ln2_g[l].reshape(1, -1), ln2_b[l].reshape(1, -1), alpha, seq)
    return x2.reshape(bsz, seq, d)
```

```python
import functools

import jax
import jax.numpy as jnp
import numpy as np
from jax import lax
from jax.experimental import pallas as pl
from jax.experimental.pallas import tpu as pltpu

F32 = jnp.float32
BF16 = jnp.bfloat16
I32 = jnp.int32

M_HEADS = 4
M_DQK = 128
M_DV = 256
CONV_WIDTH = 4
A_HEADS = 16
A_DH = 64
ATTN_BLOCK = 128
DILATIONS = (1, 4, 16)
N_EXPERTS = 32
TOP_K = 4
SWIGLU_ALPHA = 1.702
SWIGLU_LIMIT = 7.0
EPS = 1e-5

LANES = 128
VMEM_LIMIT = 56 * 1024 * 1024

MLSTM_CHUNK = 256
MOE_ROW_PAD = 256
MOE_CHUNK = 256
MOE_ITEM_ROWS = 1280
MOE_TN = 512
COMBINE_TOKENS = 128
DMA_UNROLL = 8


def _cparams(sem, vmem=VMEM_LIMIT):
    return pltpu.CompilerParams(dimension_semantics=sem, vmem_limit_bytes=vmem)


def _sigmoid(x):
    return 1.0 / (1.0 + jnp.exp(-x))


def _log_sigmoid(x):
    return jnp.minimum(x, 0.0) - jnp.log(1.0 + jnp.exp(-jnp.abs(x)))


def _layer_norm(z, g, b):
    mu = jnp.mean(z, axis=-1, keepdims=True)
    zc = z - mu
    var = jnp.mean(zc * zc, axis=-1, keepdims=True)
    return zc * lax.rsqrt(var + EPS) * g + b


def _dot(a, b):
    return jnp.dot(a, b, preferred_element_type=F32)


def _dot_nt(a, b):
    return lax.dot_general(a, b, (((1,), (1,)), ((), ())), preferred_element_type=F32)


def _dot_hilo(a, sel):
    hi = a.astype(BF16)
    lo = (a - hi.astype(F32)).astype(BF16)
    return _dot(hi, sel) + _dot(lo, sel)


def _ada_kernel(c_ref, w_ref, b_ref, o_ref):
    c = c_ref[...]
    cond = c * _sigmoid(c)
    o_ref[...] = _dot(cond.astype(BF16), w_ref[...].astype(BF16)) + b_ref[...]


def _ada_mod(c, w_ada, b_ada):
    bsz, d = c.shape
    n = w_ada.shape[1]
    tn = 1024
    rows = 8
    c_pad = jnp.zeros((rows, d), F32).at[:bsz].set(c)
    out = pl.pallas_call(
        _ada_kernel,
        grid=(n // tn,),
        in_specs=[pl.BlockSpec((rows, d), lambda j: (0, 0)),
                  pl.BlockSpec((d, tn), lambda j: (0, j)),
                  pl.BlockSpec((1, tn), lambda j: (0, j))],
        out_specs=pl.BlockSpec((rows, tn), lambda j: (0, j)),
        out_shape=jax.ShapeDtypeStruct((rows, n), F32),
        compiler_params=_cparams(("arbitrary",)),
        name="ada_mod",
    )(c_pad, w_ada, b_ada.reshape(1, n))
    return out[:bsz]


def _inproj_kernel(attn_col0, x_ref, sc_ref, sh_ref, w_ref, b_ref, wg_ref, bg_ref, o_ref, g_ref, *rest):
    dil_refs, (h_ref, r_scr) = rest[:-2], rest[-2:]
    j = pl.program_id(1)

    @pl.when(j == 0)
    def _():
        h = x_ref[...] * (1.0 + sc_ref[0]) + sh_ref[0]
        hb = h.astype(BF16)
        h_ref[...] = hb
        g_ref[...] = _dot(hb, wg_ref[...]) + bg_ref[...]

    res = _dot(h_ref[...], w_ref[...]) + b_ref[...]
    o_ref[...] = res.astype(BF16)

    @pl.when(j >= attn_col0)
    def _():
        cols = res.shape[1] // LANES
        for c in range(cols):
            r_scr[c] = res[:, c * LANES:(c + 1) * LANES]
        for ref in dil_refs:
            d, n = ref.shape[1], ref.shape[2]
            for r in range(d):
                for c in range(cols):
                    ref[0, r, :, c * LANES:(c + 1) * LANES] = r_scr[c, pl.ds(r, n, stride=d), :].astype(BF16)


def _in_proj(x2, sc, sh, w_main, b_main, w_gate, b_gate, seq, attn_col0):
    t, d = x2.shape
    n = w_main.shape[1]
    tm, tn = 512, 1024
    per_b = seq // tm
    bsz = t // seq
    dils = [dl for dl in DILATIONS if dl > 1]
    aw3 = n - attn_col0 * tn
    dil_specs = [pl.BlockSpec((1, dl, tm // dl, tn),
                              lambda i, j: (i // per_b, 0, i % per_b, jnp.maximum(j - attn_col0, 0))) for dl in dils]
    dil_shapes = [jax.ShapeDtypeStruct((bsz, dl, seq // dl, aw3), BF16) for dl in dils]
    return pl.pallas_call(
        functools.partial(_inproj_kernel, attn_col0),
        grid=(t // tm, n // tn),
        in_specs=[pl.BlockSpec((tm, d), lambda i, j: (i, 0)),
                  pl.BlockSpec((1, 1, d), lambda i, j: (i // per_b, 0, 0)),
                  pl.BlockSpec((1, 1, d), lambda i, j: (i // per_b, 0, 0)),
                  pl.BlockSpec((d, tn), lambda i, j: (0, j)),
                  pl.BlockSpec((1, tn), lambda i, j: (0, j)),
                  pl.BlockSpec((d, LANES), lambda i, j: (0, 0)),
                  pl.BlockSpec((1, LANES), lambda i, j: (0, 0))],
        out_specs=[pl.BlockSpec((tm, tn), lambda i, j: (i, j)),
                   pl.BlockSpec((tm, LANES), lambda i, j: (i, 0))] + dil_specs,
        out_shape=[jax.ShapeDtypeStruct((t, n), BF16),
                   jax.ShapeDtypeStruct((t, LANES), F32)] + dil_shapes,
        scratch_shapes=[pltpu.VMEM((tm, d), BF16), pltpu.VMEM((tn // LANES, tm, LANES), F32)],
        compiler_params=_cparams(("arbitrary", "arbitrary")),
        name="in_proj",
    )(x2, sc, sh, w_main, b_main, w_gate, b_gate)


def _mlstm_kernel(qk_ref, v_ref, o_ref, gc_ref, gr_ref, cw_ref, cb_ref, ng_ref, out_ref,
                  ct_ref, n_ref, m_ref, prev_ref):
    c = pl.program_id(1)
    L = MLSTM_CHUNK

    @pl.when(c == 0)
    def _():
        ct_ref[...] = jnp.zeros_like(ct_ref)
        n_ref[...] = jnp.zeros_like(n_ref)
        m_ref[...] = jnp.zeros_like(m_ref)
        prev_ref[...] = jnp.zeros_like(prev_ref)

    x = qk_ref[...].astype(F32)
    prev = prev_ref[...]
    row = lax.broadcasted_iota(I32, (L, 1), 0)
    y = cw_ref[CONV_WIDTH - 1:CONV_WIDTH, :] * x + cb_ref[...]
    for k in range(1, CONV_WIDTH):
        xs = jnp.where(row < k, pltpu.roll(prev, k, 0), pltpu.roll(x, k, 0))
        y = y + cw_ref[CONV_WIDTH - 1 - k:CONV_WIDTH - k, :] * xs
    prev_ref[...] = x
    y = y * _sigmoid(y)

    gc = gc_ref[...]
    gr = gr_ref[0]
    ti = lax.broadcasted_iota(I32, (L, L), 0)
    si = lax.broadcasted_iota(I32, (L, L), 1)
    causal = si <= ti
    tril = causal.astype(F32)
    triu = (ti <= si).astype(F32)
    b_cols = jnp.dot(tril, _log_sigmoid(gc), precision=lax.Precision.HIGHEST, preferred_element_type=F32)
    b_rows = jnp.dot(_log_sigmoid(gr), triu, precision=lax.Precision.HIGHEST, preferred_element_type=F32)

    qk_w = M_HEADS * M_DQK
    for h in range(M_HEADS):
        qf = y[:, h * M_DQK:(h + 1) * M_DQK]
        kf = y[:, qk_w + h * M_DQK:qk_w + (h + 1) * M_DQK] * (M_DQK ** -0.5)
        vb = v_ref[:, h * M_DV:(h + 1) * M_DV]
        qb = qf.astype(BF16)
        kb = kf.astype(BF16)

        bc = b_cols[:, M_HEADS + h:M_HEADS + h + 1]
        ic = gc[:, h:h + 1]
        br = b_rows[M_HEADS + h:M_HEADS + h + 1, :]
        ir = gr[h:h + 1, :]
        m_prev = m_ref[h][0:1, 0:1]
        n_prev = n_ref[h][0:1, :]

        dm = jnp.where(causal, bc - br + ir, -jnp.inf)
        inter = bc + m_prev
        mt = jnp.maximum(inter, jnp.max(dm, axis=1, keepdims=True))
        a = jnp.exp(dm - mt) * _dot_nt(qb, kb)
        e_int = jnp.exp(inter - mt)
        num = _dot(a.astype(BF16), vb) + e_int * _dot(qb, ct_ref[h].astype(BF16))
        den = jnp.sum(a, axis=1, keepdims=True) + e_int * jnp.sum(qf * n_prev, axis=1, keepdims=True)
        hh = num / jnp.maximum(jnp.abs(den), jnp.exp(-mt))

        b_last = bc[L - 1:L, :]
        g_col = b_last - bc + ic
        m_new = jnp.maximum(b_last + m_prev, jnp.max(g_col, axis=0, keepdims=True))
        w_col = jnp.exp(g_col - m_new)
        decay = jnp.exp(b_last + m_prev - m_new)
        wv = (w_col * vb.astype(F32)).astype(BF16)
        ct_ref[h] = decay * ct_ref[h] + _dot(kf.T.astype(BF16), wv)
        n_new = decay * n_prev + jnp.sum(w_col * kf, axis=0, keepdims=True)
        n_ref[h] = jnp.broadcast_to(n_new, n_ref.shape[1:])
        m_ref[h] = jnp.broadcast_to(m_new, m_ref.shape[1:])

        ms = jnp.mean(hh * hh, axis=1, keepdims=True)
        og = o_ref[:, h * M_DV:(h + 1) * M_DV].astype(F32)
        yh = hh * lax.rsqrt(ms + EPS) * ng_ref[:, h * M_DV:(h + 1) * M_DV] * _sigmoid(og)
        out_ref[:, h * M_DV:(h + 1) * M_DV] = yh.astype(BF16)


def _mlstm(proj, gates_c, gates_r, conv_w, conv_b, norm_g, bsz, seq):
    L = MLSTM_CHUNK
    nc = seq // L
    t = bsz * seq
    w = M_HEADS * M_DV
    return pl.pallas_call(
        _mlstm_kernel,
        grid=(bsz, nc),
        in_specs=[pl.BlockSpec((L, w), lambda b, c: (b * nc + c, 0)),
                  pl.BlockSpec((L, w), lambda b, c: (b * nc + c, 1)),
                  pl.BlockSpec((L, w), lambda b, c: (b * nc + c, 2)),
                  pl.BlockSpec((L, LANES), lambda b, c: (b * nc + c, 0)),
                  pl.BlockSpec((1, 8, L), lambda b, c: (b, 0, c)),
                  pl.BlockSpec((CONV_WIDTH, w), lambda b, c: (0, 0)),
                  pl.BlockSpec((1, w), lambda b, c: (0, 0)),
                  pl.BlockSpec((1, w), lambda b, c: (0, 0))],
        out_specs=pl.BlockSpec((L, w), lambda b, c: (b * nc + c, 0)),
        out_shape=jax.ShapeDtypeStruct((t, w), BF16),
        scratch_shapes=[pltpu.VMEM((M_HEADS, M_DQK, M_DV), F32),
                        pltpu.VMEM((M_HEADS, 8, M_DQK), F32),
                        pltpu.VMEM((M_HEADS, 8, LANES), F32),
                        pltpu.VMEM((L, w), F32)],
        compiler_params=_cparams(("arbitrary", "arbitrary")),
        name="mlstm",
    )(proj, proj, proj, gates_c, gates_r, conv_w, conv_b, norm_g)


def _attn_kernel(dilation, has_prev, *refs):
    nq = ATTN_BLOCK
    if has_prev:
        q_ref, kc_ref, vc_ref, kp_ref, vp_ref, o_ref, lse_ref, k_all, v_all = refs
        k_all[0:nq, :] = kp_ref[0, 0]
        k_all[nq:2 * nq, :] = kc_ref[0, 0]
        v_all[0:nq, :] = vp_ref[0, 0]
        v_all[nq:2 * nq, :] = vc_ref[0, 0]
        nk = 2 * nq
    else:
        q_ref, k_all, v_all, o_ref, lse_ref = refs
        k_all, v_all = k_all.at[0, 0], v_all.at[0, 0]
        nk = nq
    n = pl.program_id(2)
    qi = lax.broadcasted_iota(I32, (nq, nk), 0)
    ki = lax.broadcasted_iota(I32, (nq, nk), 1)
    dist = qi - ki + (nk - nq)
    ok = jnp.logical_and(dist >= 0, dist <= nq)
    if has_prev:
        ok = jnp.logical_and(ok, jnp.logical_or(ki >= nq, n > 0))
    dist_f = dist.astype(F32)
    lane = lax.broadcasted_iota(I32, (nq, LANES), 1)
    left_q = lane < A_DH
    left_k = lax.broadcasted_iota(I32, (nk, LANES), 1) < A_DH
    n_pairs = A_HEADS // 2

    scores = []
    for p in range(n_pairs):
        cols = slice(p * LANES, (p + 1) * LANES)
        qp = q_ref[0, 0, :, cols] * (A_DH ** -0.5)
        kp = k_all[:, cols]
        zero = jnp.zeros_like(qp)
        scores.append(_dot_nt(jnp.where(left_q, qp, zero), kp))
        scores.append(_dot_nt(jnp.where(left_q, zero, qp), kp))
    probs, maxes = [], []
    for h in range(A_HEADS):
        coef = -(2.0 ** (-8.0 * (h + 1) / A_HEADS)) * dilation
        s = jnp.where(ok, scores[h] + dist_f * coef, -jnp.inf)
        m = jnp.max(s, axis=1, keepdims=True)
        probs.append(jnp.exp(s - m).astype(BF16))
        maxes.append(m)
    lse_tile = jnp.zeros((nq, LANES), F32)
    for p in range(n_pairs):
        cols = slice(p * LANES, (p + 1) * LANES)
        vp = v_all[:, cols]
        one = jnp.ones_like(vp)
        pv_e = _dot(probs[2 * p], jnp.where(left_k, vp, one))
        pv_o = _dot(probs[2 * p + 1], jnp.where(left_k, one, vp))
        num = jnp.where(left_q, pv_e, pv_o)
        den = pltpu.roll(jnp.where(left_q, pv_o, pv_e), A_DH, 1)
        o_ref[0, 0, :, cols] = (num / den).astype(BF16)
        lse_tile = jnp.where(lane == 2 * p, maxes[2 * p] + jnp.log(pv_e[:, A_DH:A_DH + 1]), lse_tile)
        lse_tile = jnp.where(lane == 2 * p + 1, maxes[2 * p + 1] + jnp.log(pv_o[:, 0:1]), lse_tile)
    lse_ref[0, 0] = lse_tile


def _attn_group(qkv, dilation, col0):
    bsz, d, ls, _ = qkv.shape
    aw = A_HEADS * A_DH
    nq = ATTN_BLOCK
    nb = ls // nq
    has_prev = nb > 1
    blk = (1, 1, nq, aw)
    in_specs = [pl.BlockSpec(blk, lambda b, r, n: (b, r, n, col0)),
                pl.BlockSpec(blk, lambda b, r, n: (b, r, n, col0 + 1)),
                pl.BlockSpec(blk, lambda b, r, n: (b, r, n, col0 + 2))]
    args = [qkv, qkv, qkv]
    if has_prev:
        in_specs += [pl.BlockSpec(blk, lambda b, r, n: (b, r, jnp.maximum(n - 1, 0), col0 + 1)),
                     pl.BlockSpec(blk, lambda b, r, n: (b, r, jnp.maximum(n - 1, 0), col0 + 2))]
        args += [qkv, qkv]
    return pl.pallas_call(
        functools.partial(_attn_kernel, dilation, has_prev),
        grid=(bsz, d, nb),
        in_specs=in_specs,
        out_specs=[pl.BlockSpec(blk, lambda b, r, n: (b, r, n, 0)),
                   pl.BlockSpec((1, 1, nq, LANES), lambda b, r, n: (b, r, n, 0))],
        out_shape=[jax.ShapeDtypeStruct((bsz, d, ls, aw), BF16),
                   jax.ShapeDtypeStruct((bsz, d, ls, LANES), F32)],
        scratch_shapes=[pltpu.VMEM((2 * nq, aw), BF16)] * 2 if has_prev else [],
        compiler_params=_cparams(("arbitrary", "arbitrary", "arbitrary")),
        name=f"dilated_attn_d{dilation}",
    )(*args)


def _head_maps(n_heads, dh):
    w = n_heads * dh
    e = np.zeros((LANES, w), np.float32)
    for h in range(n_heads):
        e[h, h * dh:(h + 1) * dh] = 1.0
    return jnp.asarray(e, BF16), jnp.asarray(e.T.copy(), BF16)


def _natural_rows(ref, scr):
    d, n, w = ref.shape[1:]
    if d == 1:
        return ref[0, 0].astype(F32)
    cols = w // LANES
    for r in range(d):
        blk = ref[0, r].astype(F32)
        for c in range(cols):
            scr[c, pl.ds(r, n, stride=d), :] = blk[:, c * LANES:(c + 1) * LANES]
    return jnp.concatenate([scr[c] for c in range(cols)], axis=1)


def _merge_kernel(o1_ref, o2_ref, o3_ref, l1_ref, l2_ref, l3_ref, g_ref, e_ref, p_ref, y_ref, o_scr, l_scr):
    l1, l2, l3 = (_natural_rows(ref, l_scr.at[g]) for g, ref in enumerate((l1_ref, l2_ref, l3_ref)))
    mx = jnp.maximum(jnp.maximum(l1, l2), l3)
    w1, w2, w3 = jnp.exp(l1 - mx), jnp.exp(l2 - mx), jnp.exp(l3 - mx)
    inv = 1.0 / (w1 + w2 + w3)
    e = e_ref[...]
    o = (_dot_hilo(w1 * inv, e) * _natural_rows(o1_ref, o_scr.at[0])
         + _dot_hilo(w2 * inv, e) * _natural_rows(o2_ref, o_scr.at[1])
         + _dot_hilo(w3 * inv, e) * _natural_rows(o3_ref, o_scr.at[2]))
    ms = _dot_hilo(o * o, p_ref[...]) * (1.0 / A_DH)
    scale = _dot_hilo(lax.rsqrt(ms + EPS), e)
    y_ref[...] = (o * scale * g_ref[...]).astype(BF16)


def _attn_merge(outs, lses, norm_g):
    bsz, _, seq, aw = outs[0].shape
    t = bsz * seq
    tm = 512
    per_b = seq // tm
    expand, pool = _head_maps(A_HEADS, A_DH)
    const = lambda i: (0, 0)

    def grouped(arr):
        d, w = arr.shape[1], arr.shape[3]
        return pl.BlockSpec((1, d, tm // d, w), lambda i: (i // per_b, 0, i % per_b, 0))

    return pl.pallas_call(
        _merge_kernel,
        grid=(t // tm,),
        in_specs=[grouped(a) for a in outs] + [grouped(a) for a in lses]
        + [pl.BlockSpec((1, aw), const), pl.BlockSpec((LANES, aw), const), pl.BlockSpec((aw, LANES), const)],
        out_specs=pl.BlockSpec((tm, aw), lambda i: (i, 0)),
        out_shape=jax.ShapeDtypeStruct((t, aw), BF16),
        scratch_shapes=[pltpu.VMEM((len(outs), aw // LANES, tm, LANES), F32),
                        pltpu.VMEM((len(lses), 1, tm, LANES), F32)],
        compiler_params=_cparams(("arbitrary",)),
        name="attn_merge",
    )(*outs, *lses, norm_g, expand, pool)


def _outproj_kernel(alpha, ym_ref, ya_ref, w_ref, x_ref, g1_ref, sc_ref, sh_ref, lg_ref, lb_ref,
                    rw_ref, rb_ref, x1_ref, h2_ref, ti_ref, tg_ref):
    half = ym_ref.shape[1]
    y = _dot(ym_ref[...], w_ref[0:half, :]) + _dot(ya_ref[...], w_ref[half:2 * half, :])
    z = alpha * x_ref[...] + (1.0 + g1_ref[0]) * y
    x1 = _layer_norm(z, lg_ref[...], lb_ref[...])
    x1_ref[...] = x1
    h2 = x1 * (1.0 + sc_ref[0]) + sh_ref[0]
    h2_ref[...] = h2
    logits = _dot(h2.astype(BF16), rw_ref[...]) + rb_ref[...]
    lane = lax.broadcasted_iota(I32, logits.shape, 1)
    lane_f = lane.astype(F32)
    work = jnp.where(lane < N_EXPERTS, logits, -jnp.inf)
    idx_tile = jnp.zeros(logits.shape, F32)
    val_tile = jnp.zeros(logits.shape, F32)
    top = None
    denom = None
    for k in range(TOP_K):
        mk = jnp.max(work, axis=1, keepdims=True)
        ik = jnp.min(jnp.where(work == mk, lane_f, float(LANES)), axis=1, keepdims=True)
        work = jnp.where(lane_f == ik, -jnp.inf, work)
        if k == 0:
            top = mk
        ek = jnp.exp(mk - top)
        denom = ek if k == 0 else denom + ek
        idx_tile = jnp.where(lane == k, ik, idx_tile)
        val_tile = jnp.where(lane == k, ek, val_tile)
    ti_ref[...] = idx_tile.astype(I32)
    tg_ref[...] = val_tile / denom


def _out_proj(y_m, y_a, w_out, x2, g1, sc2, sh2, ln_g, ln_b, rw, rb, alpha, seq):
    t, d = x2.shape
    half = y_m.shape[1]
    tm = 256
    per_b = seq // tm
    row = lambda i: (i, 0)
    const = lambda i: (0, 0)
    mod = lambda i: (i // per_b, 0, 0)
    return pl.pallas_call(
        functools.partial(_outproj_kernel, alpha),
        grid=(t // tm,),
        in_specs=[pl.BlockSpec((tm, half), row), pl.BlockSpec((tm, half), row),
                  pl.BlockSpec((2 * half, d), const), pl.BlockSpec((tm, d), row),
                  pl.BlockSpec((1, 1, d), mod), pl.BlockSpec((1, 1, d), mod), pl.BlockSpec((1, 1, d), mod),
                  pl.BlockSpec((1, d), const), pl.BlockSpec((1, d), const),
                  pl.BlockSpec((d, LANES), const), pl.BlockSpec((1, LANES), const)],
        out_specs=[pl.BlockSpec((tm, d), row), pl.BlockSpec((tm, d), row),
                   pl.BlockSpec((tm, LANES), row), pl.BlockSpec((tm, LANES), row)],
        out_shape=[jax.ShapeDtypeStruct((t, d), F32), jax.ShapeDtypeStruct((t, d), F32),
                   jax.ShapeDtypeStruct((t, LANES), I32), jax.ShapeDtypeStruct((t, LANES), F32)],
        compiler_params=_cparams(("arbitrary",)),
        name="out_proj_ln_router",
    )(y_m, y_a, w_out, x2, g1, sc2, sh2, ln_g, ln_b, rw, rb)


def _expert_kernel(ie_ref, ir_ref, ic_ref, iv_ref, tok_hbm, h_hbm, wgu_ref, bgu_ref, wdn_ref, bdn_ref,
                   sel_ref, y_hbm, tok_smem, xbuf, xb, yacc, wgu_b, wdn_b, gu_scr, sems):
    i = pl.program_id(0)
    j = pl.program_id(1)
    nj = pl.num_programs(1)
    row0 = ir_ref[i]
    nch = ic_ref[i]
    ch = MOE_CHUNK

    def row_copy(r):
        t = tok_smem[lax.shift_right_logical(r, 7), jnp.bitwise_and(r, LANES - 1)]
        return pltpu.make_async_copy(h_hbm.at[pl.ds(t, 1), :], xbuf.at[pl.ds(r, 1), :], sems.at[1])

    @pl.when(jnp.logical_and(j == 0, nch > 0))
    def _():
        tok_rows = pl.ds(lax.shift_right_logical(row0, 7), MOE_ITEM_ROWS // LANES)
        cp = pltpu.make_async_copy(tok_hbm.at[tok_rows, :], tok_smem, sems.at[0])
        cp.start()
        cp.wait()

        def issue(g, carry):
            for u in range(DMA_UNROLL):
                row_copy(g * DMA_UNROLL + u).start()
            return carry

        lax.fori_loop(0, nch * (ch // DMA_UNROLL), issue, 0)

        def drain(m, carry):
            rows = pl.ds(pl.multiple_of(m * ch, ch), ch)
            pltpu.make_async_copy(h_hbm.at[pl.ds(0, ch), :], xbuf.at[rows, :], sems.at[1]).wait()
            return carry

        lax.fori_loop(0, nch, drain, 0)

        def prep(m, carry):
            rows = pl.ds(pl.multiple_of(m * ch, ch), ch)
            xb[rows, :] = xbuf[rows, :].astype(BF16)
            yacc[rows, :] = jnp.broadcast_to(bdn_ref[0], (ch, yacc.shape[1]))
            return carry

        lax.fori_loop(0, nch, prep, 0)

    @pl.when(nch > 0)
    def _():
        wgu_b[...] = wgu_ref[0].astype(BF16)
        wdn_b[...] = wdn_ref[0].astype(BF16)
        bgu = bgu_ref[0]
        last = j == nj - 1

        def gate_up(m):
            rows = pl.ds(pl.multiple_of(m * ch, ch), ch)
            return _dot(xb[rows, :], wgu_b[...]) + bgu

        def finish(m, gu):
            rows = pl.ds(pl.multiple_of(m * ch, ch), ch)
            glu = jnp.minimum(gu, SWIGLU_LIMIT)
            f_glu = glu * _sigmoid(SWIGLU_ALPHA * glu)
            f_lin = jnp.clip(gu, -SWIGLU_LIMIT, SWIGLU_LIMIT) + 1.0
            prod = (pltpu.roll(f_glu, 1, 1) * f_lin).astype(BF16)
            parts = [_dot(prod[:, q * 2 * LANES:(q + 1) * 2 * LANES], sel_ref[...])
                     for q in range(MOE_TN // (2 * LANES))]
            act = jnp.concatenate(parts, axis=1).astype(BF16)
            yacc[rows, :] += _dot(act, wdn_b[...])

        def step(m):
            gu = gu_scr[...]
            gu_scr[...] = gate_up(m + 1)
            finish(m, gu)

        gu_scr[...] = gate_up(0)
        n_steps = nch - 1

        def pair(p, carry):
            step(2 * p)
            step(2 * p + 1)
            return carry

        lax.fori_loop(0, lax.shift_right_logical(n_steps, 1), pair, 0)

        @pl.when(jnp.bitwise_and(n_steps, 1) == 1)
        def _():
            step(n_steps - 1)

        finish(nch - 1, gu_scr[...])

        @pl.when(last)
        def _():
            def out_copy(m):
                rows = pl.ds(pl.multiple_of(m * ch, ch), ch)
                dst = pl.ds(pl.multiple_of(row0 + m * ch, ch), ch)
                return pltpu.make_async_copy(yacc.at[rows, :], y_hbm.at[dst, :], sems.at[2])

            def issue(m, carry):
                out_copy(m).start()
                return carry

            lax.fori_loop(0, nch, issue, 0)

            def drain(m, carry):
                out_copy(m).wait()
                return carry

            lax.fori_loop(0, nch, drain, 0)


def _experts(h2, tok_pad, item_e, item_row0, item_nch, item_valid, w_gu, b_gu, w_dn, b_dn, n_rows):
    ne, d, two_de = w_gu.shape
    tn = MOE_TN
    nj = two_de // tn
    assert nj > 1, "the kernel separates its first and last column-tile steps"
    ni = item_e.shape[0]
    sel = np.zeros((2 * LANES, LANES), np.float32)
    sel[2 * np.arange(LANES) + 1, np.arange(LANES)] = 1.0

    def jmap(i, j, iv):
        return jnp.where(iv[i] > 0, j, nj - 1)

    grid_spec = pltpu.PrefetchScalarGridSpec(
        num_scalar_prefetch=4,
        grid=(ni, nj),
        in_specs=[pl.BlockSpec(memory_space=pl.ANY),
                  pl.BlockSpec(memory_space=pl.ANY),
                  pl.BlockSpec((1, d, tn), lambda i, j, ie, ir, ic, iv: (ie[i], 0, jmap(i, j, iv))),
                  pl.BlockSpec((1, 1, tn), lambda i, j, ie, ir, ic, iv: (ie[i], 0, jmap(i, j, iv))),
                  pl.BlockSpec((1, tn // 2, d), lambda i, j, ie, ir, ic, iv: (ie[i], jmap(i, j, iv), 0)),
                  pl.BlockSpec((1, 1, d), lambda i, j, ie, ir, ic, iv: (ie[i], 0, 0)),
                  pl.BlockSpec((2 * LANES, LANES), lambda i, j, ie, ir, ic, iv: (0, 0))],
        out_specs=pl.BlockSpec(memory_space=pl.ANY),
        scratch_shapes=[pltpu.SMEM((MOE_ITEM_ROWS // LANES, LANES), I32),
                        pltpu.VMEM((MOE_ITEM_ROWS, d), F32),
                        pltpu.VMEM((MOE_ITEM_ROWS, d), BF16),
                        pltpu.VMEM((MOE_ITEM_ROWS, d), F32),
                        pltpu.VMEM((d, tn), BF16),
                        pltpu.VMEM((tn // 2, d), BF16),
                        pltpu.VMEM((MOE_CHUNK, tn), F32),
                        pltpu.SemaphoreType.DMA((3,))],
    )
    return pl.pallas_call(
        _expert_kernel,
        grid_spec=grid_spec,
        out_shape=jax.ShapeDtypeStruct((n_rows, d), F32),
        compiler_params=_cparams(("arbitrary", "arbitrary")),
        name="moe_experts",
    )(item_e, item_row0, item_nch, item_valid, tok_pad, h2, w_gu, b_gu.reshape(ne, 1, two_de),
      w_dn, b_dn.reshape(ne, 1, d), jnp.asarray(sel, BF16))


def _combine_kernel(alpha, dest_ref, y_hbm, gate_ref, x1_ref, g2_ref, lg_ref, lb_ref, out_ref, buf, sem):
    tc = COMBINE_TOKENS

    def row_copy(s):
        k = jnp.bitwise_and(s, TOP_K - 1)
        r = lax.shift_right_logical(s, 2)
        src = dest_ref[0, 0, s]
        return pltpu.make_async_copy(y_hbm.at[pl.ds(src, 1), :], buf.at[k, pl.ds(r, 1), :], sem.at[0])

    def issue(g, carry):
        for u in range(DMA_UNROLL):
            row_copy(g * DMA_UNROLL + u).start()
        return carry

    lax.fori_loop(0, tc * TOP_K // DMA_UNROLL, issue, 0)
    for k in range(TOP_K):
        pltpu.make_async_copy(y_hbm.at[pl.ds(0, tc), :], buf.at[k], sem.at[0]).wait()

    gates = gate_ref[...]
    y = gates[:, 0:1] * buf[0]
    for k in range(1, TOP_K):
        y = y + gates[:, k:k + 1] * buf[k]
    z = alpha * x1_ref[...] + (1.0 + g2_ref[0]) * y
    out_ref[...] = _layer_norm(z, lg_ref[...], lb_ref[...])


def _combine(dest, y_pad, gates, x1, g2, ln_g, ln_b, alpha, seq):
    t, d = x1.shape
    tc = COMBINE_TOKENS
    per_b = seq // tc
    row = lambda i: (i, 0)
    const = lambda i: (0, 0)
    return pl.pallas_call(
        functools.partial(_combine_kernel, alpha),
        grid=(t // tc,),
        in_specs=[pl.BlockSpec((1, 1, tc * TOP_K), lambda i: (i, 0, 0), memory_space=pltpu.SMEM),
                  pl.BlockSpec(memory_space=pl.ANY),
                  pl.BlockSpec((tc, LANES), row), pl.BlockSpec((tc, d), row),
                  pl.BlockSpec((1, 1, d), lambda i: (i // per_b, 0, 0)),
                  pl.BlockSpec((1, d), const), pl.BlockSpec((1, d), const)],
        out_specs=pl.BlockSpec((tc, d), row),
        out_shape=jax.ShapeDtypeStruct((t, d), F32),
        scratch_shapes=[pltpu.VMEM((TOP_K, tc, d), F32), pltpu.SemaphoreType.DMA((1,))],
        compiler_params=_cparams(("arbitrary",)),
        name="moe_combine_ln",
    )(dest.reshape(t // tc, 1, tc * TOP_K), y_pad, gates, x1, g2, ln_g, ln_b)


def _count_le(ends, q):
    return jnp.sum((ends[None, :] <= q[:, None]).astype(I32), axis=1)


def _lookup(table, idx):
    hit = idx[:, None] == jnp.arange(table.shape[0], dtype=I32)
    return jnp.sum(jnp.where(hit, table[None, :], 0), axis=1)


def _routing_tables(top_idx):
    t = top_idx.shape[0]
    tk = t * TOP_K
    experts = jnp.arange(N_EXPERTS, dtype=I32)
    hits = [top_idx[:, k:k + 1] == experts for k in range(TOP_K)]
    onehot = sum(h.astype(I32) for h in hits)
    csum = jnp.cumsum(onehot, axis=0)
    counts = csum[-1]
    earlier = csum - onehot
    starts = jnp.cumsum(counts) - counts
    padded = ((counts + MOE_ROW_PAD - 1) // MOE_ROW_PAD) * MOE_ROW_PAD
    pad_end = jnp.cumsum(padded)
    pad_start = pad_end - padded
    dest = jnp.stack([jnp.sum(jnp.where(h, earlier + pad_start, 0), axis=1) for h in hits], axis=1).reshape(tk)

    n_rows = tk + N_EXPERTS * MOE_ROW_PAD
    n_tab = n_rows + MOE_ITEM_ROWS
    tok_sorted = (jnp.argsort(top_idx.reshape(tk)) // TOP_K).astype(I32)
    blk = jnp.arange(n_tab // MOE_ROW_PAD, dtype=I32) * MOE_ROW_PAD
    blk_e = jnp.minimum(_count_le(pad_end, blk), N_EXPERTS - 1)
    local = (blk - _lookup(pad_start, blk_e))[:, None] + jnp.arange(MOE_ROW_PAD, dtype=I32)
    src = jnp.clip(_lookup(starts, blk_e)[:, None] + local, 0, tk - 1)
    live = jnp.logical_and(local < _lookup(counts, blk_e)[:, None], (blk < pad_end[-1])[:, None])
    tok_pad = jnp.where(live, tok_sorted[src], 0).reshape(-1, LANES)

    items_per = (padded + MOE_ITEM_ROWS - 1) // MOE_ITEM_ROWS
    item_end = jnp.cumsum(items_per)
    item_start = item_end - items_per
    n_items = N_EXPERTS + n_rows // MOE_ITEM_ROWS
    idx = jnp.arange(n_items, dtype=I32)
    valid = idx < item_end[-1]
    e_i = jnp.minimum(_count_le(item_end, idx), N_EXPERTS - 1)
    k_i = idx - _lookup(item_start, e_i)
    row0 = _lookup(pad_start, e_i) + k_i * MOE_ITEM_ROWS
    nrows = jnp.clip(_lookup(padded, e_i) - k_i * MOE_ITEM_ROWS, 0, MOE_ITEM_ROWS)
    e_last = jnp.sum(jnp.where(idx == item_end[-1] - 1, e_i, 0))
    item_e = jnp.where(valid, e_i, e_last).astype(I32)
    item_row0 = jnp.where(valid, row0, 0).astype(I32)
    item_nch = jnp.where(valid, nrows // MOE_CHUNK, 0).astype(I32)
    return tok_pad, dest, item_e, item_row0, item_nch, valid.astype(I32), n_rows


def kernel(x, c, w_ada, b_ada, w_in, b_in, conv_w, conv_b, m_norm_g, a_norm_g, w_out, ln1_g, ln1_b,
           router_w, router_b, w_gu, b_gu, w_dn, b_dn, ln2_g, ln2_b):
    bsz, seq, d = x.shape
    depth = w_ada.shape[0]
    t = bsz * seq
    alpha = float((2 * depth) ** 0.25)
    qk_w = 2 * M_HEADS * M_DQK
    mv_w = M_HEADS * M_DV
    aw = A_HEADS * A_DH
    gate_lo = qk_w + 2 * mv_w
    gate_hi = gate_lo + 2 * M_HEADS

    x2 = x.reshape(t, d)
    for l in range(depth):
        mod = _ada_mod(c, w_ada[l], b_ada[l]).reshape(bsz, 6, 1, d)
        sh1, sc1, g1, sh2, sc2, g2 = (mod[:, i] for i in range(6))

        w_main = jnp.concatenate([w_in[l][:, :gate_lo], w_in[l][:, gate_hi:]], axis=1).astype(BF16)
        b_main = jnp.concatenate([b_in[l][:gate_lo], b_in[l][gate_hi:]]).reshape(1, -1)
        w_gate = jnp.zeros((d, LANES), BF16).at[:, :2 * M_HEADS].set(w_in[l][:, gate_lo:gate_hi].astype(BF16))
        b_gate = jnp.zeros((1, LANES), F32).at[0, :2 * M_HEADS].set(b_in[l][gate_lo:gate_hi])
        attn_col0 = gate_lo // aw
        proj, gates_c, *qkv_dil = _in_proj(x2, sc1, sh1, w_main, b_main, w_gate, b_gate, seq, attn_col0)

        gates_r = gates_c[:, :2 * M_HEADS].reshape(bsz, seq, 2 * M_HEADS).transpose(0, 2, 1)
        y_m = _mlstm(proj, gates_c, gates_r, conv_w[l], conv_b[l].reshape(1, -1),
                     m_norm_g[l].reshape(1, -1), bsz, seq)

        outs, lses = [], []
        for dil in DILATIONS:
            if dil == 1:
                o_d, lse_d = _attn_group(proj.reshape(bsz, 1, seq, -1), dil, attn_col0)
            else:
                o_d, lse_d = _attn_group(qkv_dil[DILATIONS.index(dil) - 1], dil, 0)
            outs.append(o_d)
            lses.append(lse_d)
        y_a = _attn_merge(outs, lses, a_norm_g[l].reshape(1, -1))

        rw = jnp.zeros((d, LANES), BF16).at[:, :N_EXPERTS].set(router_w[l].astype(BF16))
        rb = jnp.zeros((1, LANES), F32).at[0, :N_EXPERTS].set(router_b[l])
        x1, h2, top_idx, gates = _out_proj(y_m, y_a, w_out[l].astype(BF16), x2, g1, sc2, sh2,
                                           ln1_g[l].reshape(1, -1), ln1_b[l].reshape(1, -1), rw, rb, alpha, seq)

        tok_pad, dest, item_e, item_row0, item_nch, item_valid, n_rows = _routing_tables(top_idx[:, :TOP_K])
        y_pad = _experts(h2, tok_pad, item_e, item_row0, item_nch, item_valid,
                         w_gu[l], b_gu[l], w_dn[l], b_dn[l], n_rows)
        x2 = _combine(dest, y_pad, gates, x1, g2, ln2_g[l].reshape(1, -1), ln2_b[l].reshape(1, -1), alpha, seq)
    return x2.reshape(bsz, seq, d)
```

```python
import functools

import jax
import jax.numpy as jnp
import numpy as np
from jax import lax
from jax.experimental import pallas as pl
from jax.experimental.pallas import tpu as pltpu

F32 = jnp.float32
BF16 = jnp.bfloat16
I32 = jnp.int32

M_HEADS = 4
M_DQK = 128
M_DV = 256
CONV_WIDTH = 4
A_HEADS = 16
A_DH = 64
ATTN_BLOCK = 128
DILATIONS = (1, 4, 16)
N_EXPERTS = 32
TOP_K = 4
SWIGLU_ALPHA = 1.702
SWIGLU_LIMIT = 7.0
EPS = 1e-5

LANES = 128
VMEM_LIMIT = 56 * 1024 * 1024

MLSTM_CHUNK = 256
MOE_ROW_PAD = 256
MOE_CHUNK = 256
MOE_ITEM_ROWS = 1280
MOE_TN = 512
COMBINE_TOKENS = 128
DMA_UNROLL = 8


def _cparams(sem, vmem=VMEM_LIMIT):
    return pltpu.CompilerParams(dimension_semantics=sem, vmem_limit_bytes=vmem)


def _sigmoid(x):
    return 1.0 / (1.0 + jnp.exp(-x))


def _log_sigmoid(x):
    return jnp.minimum(x, 0.0) - jnp.log(1.0 + jnp.exp(-jnp.abs(x)))


def _layer_norm(z, g, b):
    mu = jnp.mean(z, axis=-1, keepdims=True)
    zc = z - mu
    var = jnp.mean(zc * zc, axis=-1, keepdims=True)
    return zc * lax.rsqrt(var + EPS) * g + b


def _dot(a, b):
    return jnp.dot(a, b, preferred_element_type=F32)


def _dot_nt(a, b):
    return lax.dot_general(a, b, (((1,), (1,)), ((), ())), preferred_element_type=F32)


def _dot_hilo(a, sel):
    hi = a.astype(BF16)
    lo = (a - hi.astype(F32)).astype(BF16)
    return _dot(hi, sel) + _dot(lo, sel)


def _ada_kernel(c_ref, w_ref, b_ref, o_ref):
    c = c_ref[...]
    cond = c * _sigmoid(c)
    o_ref[...] = _dot(cond.astype(BF16), w_ref[...].astype(BF16)) + b_ref[...]


def _ada_mod(c, w_ada, b_ada):
    bsz, d = c.shape
    n = w_ada.shape[1]
    tn = 1024
    rows = 8
    c_pad = jnp.zeros((rows, d), F32).at[:bsz].set(c)
    out = pl.pallas_call(
        _ada_kernel,
        grid=(n // tn,),
        in_specs=[pl.BlockSpec((rows, d), lambda j: (0, 0)),
                  pl.BlockSpec((d, tn), lambda j: (0, j)),
                  pl.BlockSpec((1, tn), lambda j: (0, j))],
        out_specs=pl.BlockSpec((rows, tn), lambda j: (0, j)),
        out_shape=jax.ShapeDtypeStruct((rows, n), F32),
        compiler_params=_cparams(("arbitrary",)),
        name="ada_mod",
    )(c_pad, w_ada, b_ada.reshape(1, n))
    return out[:bsz]


def _inproj_kernel(attn_col0, x_ref, sc_ref, sh_ref, w_ref, b_ref, wg_ref, bg_ref, o_ref, g_ref, *rest):
    dil_refs, (h_ref, r_scr) = rest[:-2], rest[-2:]
    j = pl.program_id(1)

    @pl.when(j == 0)
    def _():
        h = x_ref[...] * (1.0 + sc_ref[0]) + sh_ref[0]
        hb = h.astype(BF16)
        h_ref[...] = hb
        g_ref[...] = _dot(hb, wg_ref[...]) + bg_ref[...]

    res = _dot(h_ref[...], w_ref[...]) + b_ref[...]
    o_ref[...] = res.astype(BF16)

    @pl.when(j >= attn_col0)
    def _():
        cols = res.shape[1] // LANES
        for c in range(cols):
            r_scr[c] = res[:, c * LANES:(c + 1) * LANES]
        for ref in dil_refs:
            d, n = ref.shape[1], ref.shape[2]
            for r in range(d):
                for c in range(cols):
                    ref[0, r, :, c * LANES:(c + 1) * LANES] = r_scr[c, pl.ds(r, n, stride=d), :].astype(BF16)


def _in_proj(x2, sc, sh, w_main, b_main, w_gate, b_gate, seq, attn_col0):
    t, d = x2.shape
    n = w_main.shape[1]
    tm, tn = 512, 1024
    per_b = seq // tm
    bsz = t // seq
    dils = [dl for dl in DILATIONS if dl > 1]
    aw3 = n - attn_col0 * tn
    dil_specs = [pl.BlockSpec((1, dl, tm // dl, tn),
                              lambda i, j: (i // per_b, 0, i % per_b, jnp.maximum(j - attn_col0, 0))) for dl in dils]
    dil_shapes = [jax.ShapeDtypeStruct((bsz, dl, seq // dl, aw3), BF16) for dl in dils]
    return pl.pallas_call(
        functools.partial(_inproj_kernel, attn_col0),
        grid=(t // tm, n // tn),
        in_specs=[pl.BlockSpec((tm, d), lambda i, j: (i, 0)),
                  pl.BlockSpec((1, 1, d), lambda i, j: (i // per_b, 0, 0)),
                  pl.BlockSpec((1, 1, d), lambda i, j: (i // per_b, 0, 0)),
                  pl.BlockSpec((d, tn), lambda i, j: (0, j)),
                  pl.BlockSpec((1, tn), lambda i, j: (0, j)),
                  pl.BlockSpec((d, LANES), lambda i, j: (0, 0)),
                  pl.BlockSpec((1, LANES), lambda i, j: (0, 0))],
        out_specs=[pl.BlockSpec((tm, tn), lambda i, j: (i, j)),
                   pl.BlockSpec((tm, LANES), lambda i, j: (i, 0))] + dil_specs,
        out_shape=[jax.ShapeDtypeStruct((t, n), BF16),
                   jax.ShapeDtypeStruct((t, LANES), F32)] + dil_shapes,
        scratch_shapes=[pltpu.VMEM((tm, d), BF16), pltpu.VMEM((tn // LANES, tm, LANES), F32)],
        compiler_params=_cparams(("arbitrary", "arbitrary")),
        name="in_proj",
    )(x2, sc, sh, w_main, b_main, w_gate, b_gate)


def _mlstm_kernel(qk_ref, v_ref, o_ref, gc_ref, gr_ref, cw_ref, cb_ref, ng_ref, out_ref,
                  ct_ref, n_ref, m_ref, prev_ref):
    c = pl.program_id(1)
    L = MLSTM_CHUNK

    @pl.when(c == 0)
    def _():
        ct_ref[...] = jnp.zeros_like(ct_ref)
        n_ref[...] = jnp.zeros_like(n_ref)
        m_ref[...] = jnp.zeros_like(m_ref)
        prev_ref[...] = jnp.zeros_like(prev_ref)

    x = qk_ref[...].astype(F32)
    prev = prev_ref[...]
    row = lax.broadcasted_iota(I32, (L, 1), 0)
    y = cw_ref[CONV_WIDTH - 1:CONV_WIDTH, :] * x + cb_ref[...]
    for k in range(1, CONV_WIDTH):
        xs = jnp.where(row < k, pltpu.roll(prev, k, 0), pltpu.roll(x, k, 0))
        y = y + cw_ref[CONV_WIDTH - 1 - k:CONV_WIDTH - k, :] * xs
    prev_ref[...] = x
    y = y * _sigmoid(y)

    gc = gc_ref[...]
    gr = gr_ref[0]
    ti = lax.broadcasted_iota(I32, (L, L), 0)
    si = lax.broadcasted_iota(I32, (L, L), 1)
    causal = si <= ti
    tril = causal.astype(F32)
    triu = (ti <= si).astype(F32)
    b_cols = jnp.dot(tril, _log_sigmoid(gc), precision=lax.Precision.HIGHEST, preferred_element_type=F32)
    b_rows = jnp.dot(_log_sigmoid(gr), triu, precision=lax.Precision.HIGHEST, preferred_element_type=F32)

    qk_w = M_HEADS * M_DQK
    for h in range(M_HEADS):
        qf = y[:, h * M_DQK:(h + 1) * M_DQK]
        kf = y[:, qk_w + h * M_DQK:qk_w + (h + 1) * M_DQK] * (M_DQK ** -0.5)
        vb = v_ref[:, h * M_DV:(h + 1) * M_DV]
        qb = qf.astype(BF16)
        kb = kf.astype(BF16)

        bc = b_cols[:, M_HEADS + h:M_HEADS + h + 1]
        ic = gc[:, h:h + 1]
        br = b_rows[M_HEADS + h:M_HEADS + h + 1, :]
        ir = gr[h:h + 1, :]
        m_prev = m_ref[h][0:1, 0:1]
        n_prev = n_ref[h][0:1, :]

        dm = jnp.where(causal, bc - br + ir, -jnp.inf)
        inter = bc + m_prev
        mt = jnp.maximum(inter, jnp.max(dm, axis=1, keepdims=True))
        a = jnp.exp(dm - mt) * _dot_nt(qb, kb)
        e_int = jnp.exp(inter - mt)
        num = _dot(a.astype(BF16), vb) + e_int * _dot(qb, ct_ref[h].astype(BF16))
        den = jnp.sum(a, axis=1, keepdims=True) + e_int * jnp.sum(qf * n_prev, axis=1, keepdims=True)
        hh = num / jnp.maximum(jnp.abs(den), jnp.exp(-mt))

        b_last = bc[L - 1:L, :]
        g_col = b_last - bc + ic
        m_new = jnp.maximum(b_last + m_prev, jnp.max(g_col, axis=0, keepdims=True))
        w_col = jnp.exp(g_col - m_new)
        decay = jnp.exp(b_last + m_prev - m_new)
        wv = (w_col * vb.astype(F32)).astype(BF16)
        ct_ref[h] = decay * ct_ref[h] + _dot(kf.T.astype(BF16), wv)
        n_new = decay * n_prev + jnp.sum(w_col * kf, axis=0, keepdims=True)
        n_ref[h] = jnp.broadcast_to(n_new, n_ref.shape[1:])
        m_ref[h] = jnp.broadcast_to(m_new, m_ref.shape[1:])

        ms = jnp.mean(hh * hh, axis=1, keepdims=True)
        og = o_ref[:, h * M_DV:(h + 1) * M_DV].astype(F32)
        yh = hh * lax.rsqrt(ms + EPS) * ng_ref[:, h * M_DV:(h + 1) * M_DV] * _sigmoid(og)
        out_ref[:, h * M_DV:(h + 1) * M_DV] = yh.astype(BF16)


def _mlstm(proj, gates_c, gates_r, conv_w, conv_b, norm_g, bsz, seq):
    L = MLSTM_CHUNK
    nc = seq // L
    t = bsz * seq
    w = M_HEADS * M_DV
    return pl.pallas_call(
        _mlstm_kernel,
        grid=(bsz, nc),
        in_specs=[pl.BlockSpec((L, w), lambda b, c: (b * nc + c, 0)),
                  pl.BlockSpec((L, w), lambda b, c: (b * nc + c, 1)),
                  pl.BlockSpec((L, w), lambda b, c: (b * nc + c, 2)),
                  pl.BlockSpec((L, LANES), lambda b, c: (b * nc + c, 0)),
                  pl.BlockSpec((1, 8, L), lambda b, c: (b, 0, c)),
                  pl.BlockSpec((CONV_WIDTH, w), lambda b, c: (0, 0)),
                  pl.BlockSpec((1, w), lambda b, c: (0, 0)),
                  pl.BlockSpec((1, w), lambda b, c: (0, 0))],
        out_specs=pl.BlockSpec((L, w), lambda b, c: (b * nc + c, 0)),
        out_shape=jax.ShapeDtypeStruct((t, w), BF16),
        scratch_shapes=[pltpu.VMEM((M_HEADS, M_DQK, M_DV), F32),
                        pltpu.VMEM((M_HEADS, 8, M_DQK), F32),
                        pltpu.VMEM((M_HEADS, 8, LANES), F32),
                        pltpu.VMEM((L, w), F32)],
        compiler_params=_cparams(("arbitrary", "arbitrary")),
        name="mlstm",
    )(proj, proj, proj, gates_c, gates_r, conv_w, conv_b, norm_g)


def _attn_kernel(dilation, has_prev, *refs):
    nq = ATTN_BLOCK
    if has_prev:
        q_ref, kc_ref, vc_ref, kp_ref, vp_ref, o_ref, lse_ref, k_all, v_all = refs
        k_all[0:nq, :] = kp_ref[0, 0]
        k_all[nq:2 * nq, :] = kc_ref[0, 0]
        v_all[0:nq, :] = vp_ref[0, 0]
        v_all[nq:2 * nq, :] = vc_ref[0, 0]
        nk = 2 * nq
    else:
        q_ref, k_all, v_all, o_ref, lse_ref = refs
        k_all, v_all = k_all.at[0, 0], v_all.at[0, 0]
        nk = nq
    n = pl.program_id(2)
    qi = lax.broadcasted_iota(I32, (nq, nk), 0)
    ki = lax.broadcasted_iota(I32, (nq, nk), 1)
    dist = qi - ki + (nk - nq)
    ok = jnp.logical_and(dist >= 0, dist <= nq)
    if has_prev:
        ok = jnp.logical_and(ok, jnp.logical_or(ki >= nq, n > 0))
    dist_f = dist.astype(F32)
    lane = lax.broadcasted_iota(I32, (nq, LANES), 1)
    left_q = lane < A_DH
    left_k = lax.broadcasted_iota(I32, (nk, LANES), 1) < A_DH
    n_pairs = A_HEADS // 2

    scores = []
    for p in range(n_pairs):
        cols = slice(p * LANES, (p + 1) * LANES)
        qp = q_ref[0, 0, :, cols] * (A_DH ** -0.5)
        kp = k_all[:, cols]
        zero = jnp.zeros_like(qp)
        scores.append(_dot_nt(jnp.where(left_q, qp, zero), kp))
        scores.append(_dot_nt(jnp.where(left_q, zero, qp), kp))
    probs, maxes = [], []
    for h in range(A_HEADS):
        coef = -(2.0 ** (-8.0 * (h + 1) / A_HEADS)) * dilation
        s = jnp.where(ok, scores[h] + dist_f * coef, -jnp.inf)
        m = jnp.max(s, axis=1, keepdims=True)
        probs.append(jnp.exp(s - m).astype(BF16))
        maxes.append(m)
    lse_tile = jnp.zeros((nq, LANES), F32)
    for p in range(n_pairs):
        cols = slice(p * LANES, (p + 1) * LANES)
        vp = v_all[:, cols]
        one = jnp.ones_like(vp)
        pv_e = _dot(probs[2 * p], jnp.where(left_k, vp, one))
        pv_o = _dot(probs[2 * p + 1], jnp.where(left_k, one, vp))
        num = jnp.where(left_q, pv_e, pv_o)
        den = pltpu.roll(jnp.where(left_q, pv_o, pv_e), A_DH, 1)
        o_ref[0, 0, :, cols] = (num / den).astype(BF16)
        lse_tile = jnp.where(lane == 2 * p, maxes[2 * p] + jnp.log(pv_e[:, A_DH:A_DH + 1]), lse_tile)
        lse_tile = jnp.where(lane == 2 * p + 1, maxes[2 * p + 1] + jnp.log(pv_o[:, 0:1]), lse_tile)
    lse_ref[0, 0] = lse_tile


def _attn_group(qkv, dilation, col0):
    bsz, d, ls, _ = qkv.shape
    aw = A_HEADS * A_DH
    nq = ATTN_BLOCK
    nb = ls // nq
    has_prev = nb > 1
    blk = (1, 1, nq, aw)
    in_specs = [pl.BlockSpec(blk, lambda b, r, n: (b, r, n, col0)),
                pl.BlockSpec(blk, lambda b, r, n: (b, r, n, col0 + 1)),
                pl.BlockSpec(blk, lambda b, r, n: (b, r, n, col0 + 2))]
    args = [qkv, qkv, qkv]
    if has_prev:
        in_specs += [pl.BlockSpec(blk, lambda b, r, n: (b, r, jnp.maximum(n - 1, 0), col0 + 1)),
                     pl.BlockSpec(blk, lambda b, r, n: (b, r, jnp.maximum(n - 1, 0), col0 + 2))]
        args += [qkv, qkv]
    return pl.pallas_call(
        functools.partial(_attn_kernel, dilation, has_prev),
        grid=(bsz, d, nb),
        in_specs=in_specs,
        out_specs=[pl.BlockSpec(blk, lambda b, r, n: (b, r, n, 0)),
                   pl.BlockSpec((1, 1, nq, LANES), lambda b, r, n: (b, r, n, 0))],
        out_shape=[jax.ShapeDtypeStruct((bsz, d, ls, aw), BF16),
                   jax.ShapeDtypeStruct((bsz, d, ls, LANES), F32)],
        scratch_shapes=[pltpu.VMEM((2 * nq, aw), BF16)] * 2 if has_prev else [],
        compiler_params=_cparams(("arbitrary", "arbitrary", "arbitrary")),
        name=f"dilated_attn_d{dilation}",
    )(*args)


def _head_maps(n_heads, dh):
    w = n_heads * dh
    e = np.zeros((LANES, w), np.float32)
    for h in range(n_heads):
        e[h, h * dh:(h + 1) * dh] = 1.0
    return jnp.asarray(e, BF16), jnp.asarray(e.T.copy(), BF16)


def _natural_rows(ref, scr):
    d, n, w = ref.shape[1:]
    if d == 1:
        return ref[0, 0].astype(F32)
    cols = w // LANES
    for r in range(d):
        blk = ref[0, r].astype(F32)
        for c in range(cols):
            scr[c, pl.ds(r, n, stride=d), :] = blk[:, c * LANES:(c + 1) * LANES]
    return jnp.concatenate([scr[c] for c in range(cols)], axis=1)


def _merge_kernel(o1_ref, o2_ref, o3_ref, l1_ref, l2_ref, l3_ref, g_ref, e_ref, p_ref, y_ref, o_scr, l_scr):
    l1, l2, l3 = (_natural_rows(ref, l_scr.at[g]) for g, ref in enumerate((l1_ref, l2_ref, l3_ref)))
    mx = jnp.maximum(jnp.maximum(l1, l2), l3)
    w1, w2, w3 = jnp.exp(l1 - mx), jnp.exp(l2 - mx), jnp.exp(l3 - mx)
    inv = 1.0 / (w1 + w2 + w3)
    e = e_ref[...]
    o = (_dot_hilo(w1 * inv, e) * _natural_rows(o1_ref, o_scr.at[0])
         + _dot_hilo(w2 * inv, e) * _natural_rows(o2_ref, o_scr.at[1])
         + _dot_hilo(w3 * inv, e) * _natural_rows(o3_ref, o_scr.at[2]))
    ms = _dot_hilo(o * o, p_ref[...]) * (1.0 / A_DH)
    scale = _dot_hilo(lax.rsqrt(ms + EPS), e)
    y_ref[...] = (o * scale * g_ref[...]).astype(BF16)


def _attn_merge(outs, lses, norm_g):
    bsz, _, seq, aw = outs[0].shape
    t = bsz * seq
    tm = 512
    per_b = seq // tm
    expand, pool = _head_maps(A_HEADS, A_DH)
    const = lambda i: (0, 0)

    def grouped(arr):
        d, w = arr.shape[1], arr.shape[3]
        return pl.BlockSpec((1, d, tm // d, w), lambda i: (i // per_b, 0, i % per_b, 0))

    return pl.pallas_call(
        _merge_kernel,
        grid=(t // tm,),
        in_specs=[grouped(a) for a in outs] + [grouped(a) for a in lses]
        + [pl.BlockSpec((1, aw), const), pl.BlockSpec((LANES, aw), const), pl.BlockSpec((aw, LANES), const)],
        out_specs=pl.BlockSpec((tm, aw), lambda i: (i, 0)),
        out_shape=jax.ShapeDtypeStruct((t, aw), BF16),
        scratch_shapes=[pltpu.VMEM((len(outs), aw // LANES, tm, LANES), F32),
                        pltpu.VMEM((len(lses), 1, tm, LANES), F32)],
        compiler_params=_cparams(("arbitrary",)),
        name="attn_merge",
    )(*outs, *lses, norm_g, expand, pool)


def _outproj_kernel(alpha, ym_ref, ya_ref, w_ref, x_ref, g1_ref, sc_ref, sh_ref, lg_ref, lb_ref,
                    rw_ref, rb_ref, x1_ref, h2_ref, ti_ref, tg_ref):
    half = ym_ref.shape[1]
    y = _dot(ym_ref[...], w_ref[0:half, :]) + _dot(ya_ref[...], w_ref[half:2 * half, :])
    z = alpha * x_ref[...] + (1.0 + g1_ref[0]) * y
    x1 = _layer_norm(z, lg_ref[...], lb_ref[...])
    x1_ref[...] = x1
    h2 = x1 * (1.0 + sc_ref[0]) + sh_ref[0]
    h2_ref[...] = h2
    logits = _dot(h2.astype(BF16), rw_ref[...]) + rb_ref[...]
    lane = lax.broadcasted_iota(I32, logits.shape, 1)
    lane_f = lane.astype(F32)
    work = jnp.where(lane < N_EXPERTS, logits, -jnp.inf)
    idx_tile = jnp.zeros(logits.shape, F32)
    val_tile = jnp.zeros(logits.shape, F32)
    top = None
    denom = None
    for k in range(TOP_K):
        mk = jnp.max(work, axis=1, keepdims=True)
        ik = jnp.min(jnp.where(work == mk, lane_f, float(LANES)), axis=1, keepdims=True)
        work = jnp.where(lane_f == ik, -jnp.inf, work)
        if k == 0:
            top = mk
        ek = jnp.exp(mk - top)
        denom = ek if k == 0 else denom + ek
        idx_tile = jnp.where(lane == k, ik, idx_tile)
        val_tile = jnp.where(lane == k, ek, val_tile)
    ti_ref[...] = idx_tile.astype(I32)
    tg_ref[...] = val_tile / denom


def _out_proj(y_m, y_a, w_out, x2, g1, sc2, sh2, ln_g, ln_b, rw, rb, alpha, seq):
    t, d = x2.shape
    half = y_m.shape[1]
    tm = 256
    per_b = seq // tm
    row = lambda i: (i, 0)
    const = lambda i: (0, 0)
    mod = lambda i: (i // per_b, 0, 0)
    return pl.pallas_call(
        functools.partial(_outproj_kernel, alpha),
        grid=(t // tm,),
        in_specs=[pl.BlockSpec((tm, half), row), pl.BlockSpec((tm, half), row),
                  pl.BlockSpec((2 * half, d), const), pl.BlockSpec((tm, d), row),
                  pl.BlockSpec((1, 1, d), mod), pl.BlockSpec((1, 1, d), mod), pl.BlockSpec((1, 1, d), mod),
                  pl.BlockSpec((1, d), const), pl.BlockSpec((1, d), const),
                  pl.BlockSpec((d, LANES), const), pl.BlockSpec((1, LANES), const)],
        out_specs=[pl.BlockSpec((tm, d), row), pl.BlockSpec((tm, d), row),
                   pl.BlockSpec((tm, LANES), row), pl.BlockSpec((tm, LANES), row)],
        out_shape=[jax.ShapeDtypeStruct((t, d), F32), jax.ShapeDtypeStruct((t, d), F32),
                   jax.ShapeDtypeStruct((t, LANES), I32), jax.ShapeDtypeStruct((t, LANES), F32)],
        compiler_params=_cparams(("arbitrary",)),
        name="out_proj_ln_router",
    )(y_m, y_a, w_out, x2, g1, sc2, sh2, ln_g, ln_b, rw, rb)


def _expert_kernel(nj, ie_ref, ir_ref, ic_ref, iv_ref, tail_ref, tok_hbm, h_hbm, wgu_ref, bgu_ref, wdn_ref,
                   bdn_ref, sel_ref, y_hbm, tok_smem, ring, xb, yacc, wgu_b, wdn_b, gu_scr, gsem, sems):
    i = pl.program_id(0)
    j = pl.program_id(1)
    row0 = ir_ref[i]
    nch = ic_ref[i]
    ch = MOE_CHUNK
    lead = MOE_ITEM_ROWS
    ring_chunks = ring.shape[0] // ch
    per_body = ch // nj

    def chunk_slot(first_row):
        return lax.rem(lax.shift_right_logical(first_row, 8), ring_chunks)

    def load_tokens(first_row):
        rows = pl.ds(lax.shift_right_logical(first_row, 7), lead // LANES)
        cp = pltpu.make_async_copy(tok_hbm.at[rows, :], tok_smem, sems.at[0])
        cp.start()
        cp.wait()

    def gather_rows(first_row, first_idx, count):
        slot = chunk_slot(first_row)
        base = slot * ch + jnp.bitwise_and(first_row, ch - 1)
        for u in range(count):
            idx = first_idx + u
            t = tok_smem[lax.shift_right_logical(idx, 7), jnp.bitwise_and(idx, LANES - 1)]
            pltpu.make_async_copy(h_hbm.at[pl.ds(t, 1), :], ring.at[pl.ds(base + u, 1), :], gsem.at[slot]).start()

    def chunk_wait(first_row):
        slot = chunk_slot(first_row)
        rows = pl.ds(pl.multiple_of(slot * ch, ch), ch)
        pltpu.make_async_copy(h_hbm.at[pl.ds(0, ch), :], ring.at[rows, :], gsem.at[slot]).wait()
        return rows

    @pl.when(jnp.logical_and(i == 0, j == 0))
    def _():
        load_tokens(0)

        def issue(g, carry):
            gather_rows(g * DMA_UNROLL, g * DMA_UNROLL, DMA_UNROLL)
            return carry

        lax.fori_loop(0, lead // DMA_UNROLL, issue, 0)

    @pl.when(jnp.logical_and(j == 0, nch > 0))
    def _():
        load_tokens(row0 + lead)

        def prep(m, carry):
            src = chunk_wait(row0 + m * ch)
            rows = pl.ds(pl.multiple_of(m * ch, ch), ch)
            xb[rows, :] = ring[src, :].astype(BF16)
            yacc[rows, :] = jnp.broadcast_to(bdn_ref[0], (ch, yacc.shape[1]))
            return carry

        lax.fori_loop(0, nch, prep, 0)

    @pl.when(nch > 0)
    def _():
        wgu_b[...] = wgu_ref[0].astype(BF16)
        wdn_b[...] = wdn_ref[0].astype(BF16)
        bgu = bgu_ref[0]
        last = j == nj - 1

        def prefetch(m):
            first_idx = (j * nch + m) * per_body
            gather_rows(row0 + lead + first_idx, first_idx, per_body)

        def gate_up(m):
            rows = pl.ds(pl.multiple_of(m * ch, ch), ch)
            return _dot(xb[rows, :], wgu_b[...]) + bgu

        def finish(m, gu):
            rows = pl.ds(pl.multiple_of(m * ch, ch), ch)
            glu = jnp.minimum(gu, SWIGLU_LIMIT)
            f_glu = glu * _sigmoid(SWIGLU_ALPHA * glu)
            f_lin = jnp.clip(gu, -SWIGLU_LIMIT, SWIGLU_LIMIT) + 1.0
            prod = (pltpu.roll(f_glu, 1, 1) * f_lin).astype(BF16)
            parts = [_dot(prod[:, q * 2 * LANES:(q + 1) * 2 * LANES], sel_ref[...])
                     for q in range(MOE_TN // (2 * LANES))]
            act = jnp.concatenate(parts, axis=1).astype(BF16)
            yacc[rows, :] += _dot(act, wdn_b[...])

        def step(m):
            gu = gu_scr[...]
            gu_scr[...] = gate_up(m + 1)
            finish(m, gu)
            prefetch(m)

        gu_scr[...] = gate_up(0)
        n_steps = nch - 1

        def pair(p, carry):
            step(2 * p)
            step(2 * p + 1)
            return carry

        lax.fori_loop(0, lax.shift_right_logical(n_steps, 1), pair, 0)

        @pl.when(jnp.bitwise_and(n_steps, 1) == 1)
        def _():
            step(n_steps - 1)

        finish(nch - 1, gu_scr[...])
        prefetch(nch - 1)

        @pl.when(last)
        def _():
            def out_copy(m):
                rows = pl.ds(pl.multiple_of(m * ch, ch), ch)
                dst = pl.ds(pl.multiple_of(row0 + m * ch, ch), ch)
                return pltpu.make_async_copy(yacc.at[rows, :], y_hbm.at[dst, :], sems.at[1])

            def issue(m, carry):
                out_copy(m).start()
                return carry

            lax.fori_loop(0, nch, issue, 0)

            def drain(m, carry):
                out_copy(m).wait()
                return carry

            lax.fori_loop(0, nch, drain, 0)

    @pl.when(jnp.logical_and(i == pl.num_programs(0) - 1, j == nj - 1))
    def _():
        def drain(m, carry):
            chunk_wait(tail_ref[0] + m * ch)
            return carry

        lax.fori_loop(0, lead // ch, drain, 0)


def _experts(h2, tok_pad, item_e, item_row0, item_nch, item_valid, item_tail, w_gu, b_gu, w_dn, b_dn, n_rows):
    ne, d, two_de = w_gu.shape
    tn = MOE_TN
    nj = two_de // tn
    assert nj > 1, "the kernel separates its first and last column-tile steps"
    assert MOE_CHUNK % nj == 0 and (MOE_CHUNK // nj) % 8 == 0, "row copies per chunk stage"
    ni = item_e.shape[0]
    ring_rows = 2 * MOE_ITEM_ROWS
    sel = np.zeros((2 * LANES, LANES), np.float32)
    sel[2 * np.arange(LANES) + 1, np.arange(LANES)] = 1.0

    def jmap(i, j, iv):
        return jnp.where(iv[i] > 0, j, nj - 1)

    grid_spec = pltpu.PrefetchScalarGridSpec(
        num_scalar_prefetch=5,
        grid=(ni, nj),
        in_specs=[pl.BlockSpec(memory_space=pl.ANY),
                  pl.BlockSpec(memory_space=pl.ANY),
                  pl.BlockSpec((1, d, tn), lambda i, j, ie, ir, ic, iv, it: (ie[i], 0, jmap(i, j, iv))),
                  pl.BlockSpec((1, 1, tn), lambda i, j, ie, ir, ic, iv, it: (ie[i], 0, jmap(i, j, iv))),
                  pl.BlockSpec((1, tn // 2, d), lambda i, j, ie, ir, ic, iv, it: (ie[i], jmap(i, j, iv), 0)),
                  pl.BlockSpec((1, 1, d), lambda i, j, ie, ir, ic, iv, it: (ie[i], 0, 0)),
                  pl.BlockSpec((2 * LANES, LANES), lambda i, j, ie, ir, ic, iv, it: (0, 0))],
        out_specs=pl.BlockSpec(memory_space=pl.ANY),
        scratch_shapes=[pltpu.SMEM((MOE_ITEM_ROWS // LANES, LANES), I32),
                        pltpu.VMEM((ring_rows, d), F32),
                        pltpu.VMEM((MOE_ITEM_ROWS, d), BF16),
                        pltpu.VMEM((MOE_ITEM_ROWS, d), F32),
                        pltpu.VMEM((d, tn), BF16),
                        pltpu.VMEM((tn // 2, d), BF16),
                        pltpu.VMEM((MOE_CHUNK, tn), F32),
                        pltpu.SemaphoreType.DMA((ring_rows // MOE_CHUNK,)),
                        pltpu.SemaphoreType.DMA((2,))],
    )
    return pl.pallas_call(
        functools.partial(_expert_kernel, nj),
        grid_spec=grid_spec,
        out_shape=jax.ShapeDtypeStruct((n_rows, d), F32),
        compiler_params=_cparams(("arbitrary", "arbitrary")),
        name="moe_experts",
    )(item_e, item_row0, item_nch, item_valid, item_tail, tok_pad, h2, w_gu, b_gu.reshape(ne, 1, two_de),
      w_dn, b_dn.reshape(ne, 1, d), jnp.asarray(sel, BF16))


def _combine_kernel(alpha, dest_ref, y_hbm, gate_ref, x1_ref, g2_ref, lg_ref, lb_ref, out_ref, buf, sem):
    tc = COMBINE_TOKENS

    def row_copy(s):
        k = jnp.bitwise_and(s, TOP_K - 1)
        r = lax.shift_right_logical(s, 2)
        src = dest_ref[0, 0, s]
        return pltpu.make_async_copy(y_hbm.at[pl.ds(src, 1), :], buf.at[k, pl.ds(r, 1), :], sem.at[0])

    def issue(g, carry):
        for u in range(DMA_UNROLL):
            row_copy(g * DMA_UNROLL + u).start()
        return carry

    lax.fori_loop(0, tc * TOP_K // DMA_UNROLL, issue, 0)
    for k in range(TOP_K):
        pltpu.make_async_copy(y_hbm.at[pl.ds(0, tc), :], buf.at[k], sem.at[0]).wait()

    gates = gate_ref[...]
    y = gates[:, 0:1] * buf[0]
    for k in range(1, TOP_K):
        y = y + gates[:, k:k + 1] * buf[k]
    z = alpha * x1_ref[...] + (1.0 + g2_ref[0]) * y
    out_ref[...] = _layer_norm(z, lg_ref[...], lb_ref[...])


def _combine(dest, y_pad, gates, x1, g2, ln_g, ln_b, alpha, seq):
    t, d = x1.shape
    tc = COMBINE_TOKENS
    per_b = seq // tc
    row = lambda i: (i, 0)
    const = lambda i: (0, 0)
    return pl.pallas_call(
        functools.partial(_combine_kernel, alpha),
        grid=(t // tc,),
        in_specs=[pl.BlockSpec((1, 1, tc * TOP_K), lambda i: (i, 0, 0), memory_space=pltpu.SMEM),
                  pl.BlockSpec(memory_space=pl.ANY),
                  pl.BlockSpec((tc, LANES), row), pl.BlockSpec((tc, d), row),
                  pl.BlockSpec((1, 1, d), lambda i: (i // per_b, 0, 0)),
                  pl.BlockSpec((1, d), const), pl.BlockSpec((1, d), const)],
        out_specs=pl.BlockSpec((tc, d), row),
        out_shape=jax.ShapeDtypeStruct((t, d), F32),
        scratch_shapes=[pltpu.VMEM((TOP_K, tc, d), F32), pltpu.SemaphoreType.DMA((1,))],
        compiler_params=_cparams(("arbitrary",)),
        name="moe_combine_ln",
    )(dest.reshape(t // tc, 1, tc * TOP_K), y_pad, gates, x1, g2, ln_g, ln_b)


def _count_le(ends, q):
    return jnp.sum((ends[None, :] <= q[:, None]).astype(I32), axis=1)


def _lookup(table, idx):
    hit = idx[:, None] == jnp.arange(table.shape[0], dtype=I32)
    return jnp.sum(jnp.where(hit, table[None, :], 0), axis=1)


def _routing_tables(top_idx):
    t = top_idx.shape[0]
    tk = t * TOP_K
    experts = jnp.arange(N_EXPERTS, dtype=I32)
    hits = [top_idx[:, k:k + 1] == experts for k in range(TOP_K)]
    onehot = sum(h.astype(I32) for h in hits)
    csum = jnp.cumsum(onehot, axis=0)
    counts = csum[-1]
    earlier = csum - onehot
    starts = jnp.cumsum(counts) - counts
    padded = ((counts + MOE_ROW_PAD - 1) // MOE_ROW_PAD) * MOE_ROW_PAD
    pad_end = jnp.cumsum(padded)
    pad_start = pad_end - padded
    dest = jnp.stack([jnp.sum(jnp.where(h, earlier + pad_start, 0), axis=1) for h in hits], axis=1).reshape(tk)

    n_rows = tk + N_EXPERTS * MOE_ROW_PAD
    n_tab = n_rows + 2 * MOE_ITEM_ROWS
    tok_sorted = (jnp.argsort(top_idx.reshape(tk)) // TOP_K).astype(I32)
    blk = jnp.arange(n_tab // MOE_ROW_PAD, dtype=I32) * MOE_ROW_PAD
    blk_e = jnp.minimum(_count_le(pad_end, blk), N_EXPERTS - 1)
    local = (blk - _lookup(pad_start, blk_e))[:, None] + jnp.arange(MOE_ROW_PAD, dtype=I32)
    src = jnp.clip(_lookup(starts, blk_e)[:, None] + local, 0, tk - 1)
    live = jnp.logical_and(local < _lookup(counts, blk_e)[:, None], (blk < pad_end[-1])[:, None])
    tok_pad = jnp.where(live, tok_sorted[src], 0).reshape(-1, LANES)

    items_per = (padded + MOE_ITEM_ROWS - 1) // MOE_ITEM_ROWS
    item_end = jnp.cumsum(items_per)
    item_start = item_end - items_per
    n_items = N_EXPERTS + n_rows // MOE_ITEM_ROWS
    idx = jnp.arange(n_items, dtype=I32)
    valid = idx < item_end[-1]
    e_i = jnp.minimum(_count_le(item_end, idx), N_EXPERTS - 1)
    k_i = idx - _lookup(item_start, e_i)
    row0 = _lookup(pad_start, e_i) + k_i * MOE_ITEM_ROWS
    nrows = jnp.clip(_lookup(padded, e_i) - k_i * MOE_ITEM_ROWS, 0, MOE_ITEM_ROWS)
    e_last = jnp.sum(jnp.where(idx == item_end[-1] - 1, e_i, 0))
    item_e = jnp.where(valid, e_i, e_last).astype(I32)
    item_row0 = jnp.where(valid, row0, 0).astype(I32)
    item_nch = jnp.where(valid, nrows // MOE_CHUNK, 0).astype(I32)
    item_tail = jnp.stack([pad_end[-1], pad_end[-1]]).astype(I32)
    return tok_pad, dest, item_e, item_row0, item_nch, valid.astype(I32), item_tail, n_rows


def kernel(x, c, w_ada, b_ada, w_in, b_in, conv_w, conv_b, m_norm_g, a_norm_g, w_out, ln1_g, ln1_b,
           router_w, router_b, w_gu, b_gu, w_dn, b_dn, ln2_g, ln2_b):
    bsz, seq, d = x.shape
    depth = w_ada.shape[0]
    t = bsz * seq
    alpha = float((2 * depth) ** 0.25)
    qk_w = 2 * M_HEADS * M_DQK
    mv_w = M_HEADS * M_DV
    aw = A_HEADS * A_DH
    gate_lo = qk_w + 2 * mv_w
    gate_hi = gate_lo + 2 * M_HEADS

    x2 = x.reshape(t, d)
    for l in range(depth):
        mod = _ada_mod(c, w_ada[l], b_ada[l]).reshape(bsz, 6, 1, d)
        sh1, sc1, g1, sh2, sc2, g2 = (mod[:, i] for i in range(6))

        w_main = jnp.concatenate([w_in[l][:, :gate_lo], w_in[l][:, gate_hi:]], axis=1).astype(BF16)
        b_main = jnp.concatenate([b_in[l][:gate_lo], b_in[l][gate_hi:]]).reshape(1, -1)
        w_gate = jnp.zeros((d, LANES), BF16).at[:, :2 * M_HEADS].set(w_in[l][:, gate_lo:gate_hi].astype(BF16))
        b_gate = jnp.zeros((1, LANES), F32).at[0, :2 * M_HEADS].set(b_in[l][gate_lo:gate_hi])
        attn_col0 = gate_lo // aw
        proj, gates_c, *qkv_dil = _in_proj(x2, sc1, sh1, w_main, b_main, w_gate, b_gate, seq, attn_col0)

        gates_r = gates_c[:, :2 * M_HEADS].reshape(bsz, seq, 2 * M_HEADS).transpose(0, 2, 1)
        y_m = _mlstm(proj, gates_c, gates_r, conv_w[l], conv_b[l].reshape(1, -1),
                     m_norm_g[l].reshape(1, -1), bsz, seq)

        outs, lses = [], []
        for dil in DILATIONS:
            if dil == 1:
                o_d, lse_d = _attn_group(proj.reshape(bsz, 1, seq, -1), dil, attn_col0)
            else:
                o_d, lse_d = _attn_group(qkv_dil[DILATIONS.index(dil) - 1], dil, 0)
            outs.append(o_d)
            lses.append(lse_d)
        y_a = _attn_merge(outs, lses, a_norm_g[l].reshape(1, -1))

        rw = jnp.zeros((d, LANES), BF16).at[:, :N_EXPERTS].set(router_w[l].astype(BF16))
        rb = jnp.zeros((1, LANES), F32).at[0, :N_EXPERTS].set(router_b[l])
        x1, h2, top_idx, gates = _out_proj(y_m, y_a, w_out[l].astype(BF16), x2, g1, sc2, sh2,
                                           ln1_g[l].reshape(1, -1), ln1_b[l].reshape(1, -1), rw, rb, alpha, seq)

        (tok_pad, dest, item_e, item_row0, item_nch, item_valid, item_tail,
         n_rows) = _routing_tables(top_idx[:, :TOP_K])
        y_pad = _experts(h2, tok_pad, item_e, item_row0, item_nch, item_valid, item_tail,
                         w_gu[l], b_gu[l], w_dn[l], b_dn[l], n_rows)
        x2 = _combine(dest, y_pad, gates, x1, g2, ln2_g[l].reshape(1, -1), ln2_b[l].reshape(1, -1), alpha, seq)
    return x2.reshape(bsz, seq, d)
```

```python
import functools

import jax
import jax.numpy as jnp
import numpy as np
from jax import lax
from jax.experimental import pallas as pl
from jax.experimental.pallas import tpu as pltpu

F32 = jnp.float32
BF16 = jnp.bfloat16
I32 = jnp.int32

M_HEADS = 4
M_DQK = 128
M_DV = 256
CONV_WIDTH = 4
A_HEADS = 16
A_DH = 64
ATTN_BLOCK = 128
DILATIONS = (1, 4, 16)
N_EXPERTS = 32
TOP_K = 4
SWIGLU_ALPHA = 1.702
SWIGLU_LIMIT = 7.0
EPS = 1e-5

LANES = 128
VMEM_LIMIT = 56 * 1024 * 1024

MLSTM_CHUNK = 256
MOE_ROW_PAD = 256
MOE_CHUNK = 256
MOE_ITEM_ROWS = 1280
MOE_TN = 512
COMBINE_TOKENS = 128
DMA_UNROLL = 8


def _cparams(sem, vmem=VMEM_LIMIT):
    return pltpu.CompilerParams(dimension_semantics=sem, vmem_limit_bytes=vmem)


def _sigmoid(x):
    return 1.0 / (1.0 + jnp.exp(-x))


def _log_sigmoid(x):
    return jnp.minimum(x, 0.0) - jnp.log(1.0 + jnp.exp(-jnp.abs(x)))


def _layer_norm(z, g, b):
    mu = jnp.mean(z, axis=-1, keepdims=True)
    zc = z - mu
    var = jnp.mean(zc * zc, axis=-1, keepdims=True)
    return zc * lax.rsqrt(var + EPS) * g + b


def _dot(a, b):
    return jnp.dot(a, b, preferred_element_type=F32)


def _dot_nt(a, b):
    return lax.dot_general(a, b, (((1,), (1,)), ((), ())), preferred_element_type=F32)


def _dot_hilo(a, sel):
    hi = a.astype(BF16)
    lo = (a - hi.astype(F32)).astype(BF16)
    return _dot(hi, sel) + _dot(lo, sel)


def _ada_kernel(c_ref, w_ref, b_ref, o_ref):
    c = c_ref[...]
    cond = c * _sigmoid(c)
    o_ref[...] = _dot(cond.astype(BF16), w_ref[...].astype(BF16)) + b_ref[...]


def _ada_mod(c, w_ada, b_ada):
    bsz, d = c.shape
    n = w_ada.shape[1]
    tn = 1024
    rows = 8
    c_pad = jnp.zeros((rows, d), F32).at[:bsz].set(c)
    out = pl.pallas_call(
        _ada_kernel,
        grid=(n // tn,),
        in_specs=[pl.BlockSpec((rows, d), lambda j: (0, 0)),
                  pl.BlockSpec((d, tn), lambda j: (0, j)),
                  pl.BlockSpec((1, tn), lambda j: (0, j))],
        out_specs=pl.BlockSpec((rows, tn), lambda j: (0, j)),
        out_shape=jax.ShapeDtypeStruct((rows, n), F32),
        compiler_params=_cparams(("arbitrary",)),
        name="ada_mod",
    )(c_pad, w_ada, b_ada.reshape(1, n))
    return out[:bsz]


def _inproj_kernel(attn_col0, x_ref, sc_ref, sh_ref, w_ref, b_ref, wg_ref, bg_ref, o_ref, g_ref, *rest):
    dil_refs, (h_ref, r_scr) = rest[:-2], rest[-2:]
    j = pl.program_id(1)

    @pl.when(j == 0)
    def _():
        h = x_ref[...] * (1.0 + sc_ref[0]) + sh_ref[0]
        hb = h.astype(BF16)
        h_ref[...] = hb
        g_ref[...] = _dot(hb, wg_ref[...]) + bg_ref[...]

    res = _dot(h_ref[...], w_ref[...]) + b_ref[...]
    o_ref[...] = res.astype(BF16)

    @pl.when(j >= attn_col0)
    def _():
        cols = res.shape[1] // LANES
        for c in range(cols):
            r_scr[c] = res[:, c * LANES:(c + 1) * LANES]
        for ref in dil_refs:
            d, n = ref.shape[1], ref.shape[2]
            for r in range(d):
                for c in range(cols):
                    ref[0, r, :, c * LANES:(c + 1) * LANES] = r_scr[c, pl.ds(r, n, stride=d), :].astype(BF16)


def _in_proj(x2, sc, sh, w_main, b_main, w_gate, b_gate, seq, attn_col0):
    t, d = x2.shape
    n = w_main.shape[1]
    tm, tn = 512, 1024
    per_b = seq // tm
    bsz = t // seq
    dils = [dl for dl in DILATIONS if dl > 1]
    aw3 = n - attn_col0 * tn
    dil_specs = [pl.BlockSpec((1, dl, tm // dl, tn),
                              lambda i, j: (i // per_b, 0, i % per_b, jnp.maximum(j - attn_col0, 0))) for dl in dils]
    dil_shapes = [jax.ShapeDtypeStruct((bsz, dl, seq // dl, aw3), BF16) for dl in dils]
    return pl.pallas_call(
        functools.partial(_inproj_kernel, attn_col0),
        grid=(t // tm, n // tn),
        in_specs=[pl.BlockSpec((tm, d), lambda i, j: (i, 0)),
                  pl.BlockSpec((1, 1, d), lambda i, j: (i // per_b, 0, 0)),
                  pl.BlockSpec((1, 1, d), lambda i, j: (i // per_b, 0, 0)),
                  pl.BlockSpec((d, tn), lambda i, j: (0, j)),
                  pl.BlockSpec((1, tn), lambda i, j: (0, j)),
                  pl.BlockSpec((d, LANES), lambda i, j: (0, 0)),
                  pl.BlockSpec((1, LANES), lambda i, j: (0, 0))],
        out_specs=[pl.BlockSpec((tm, tn), lambda i, j: (i, j)),
                   pl.BlockSpec((tm, LANES), lambda i, j: (i, 0))] + dil_specs,
        out_shape=[jax.ShapeDtypeStruct((t, n), BF16),
                   jax.ShapeDtypeStruct((t, LANES), F32)] + dil_shapes,
        scratch_shapes=[pltpu.VMEM((tm, d), BF16), pltpu.VMEM((tn // LANES, tm, LANES), F32)],
        compiler_params=_cparams(("arbitrary", "arbitrary")),
        name="in_proj",
    )(x2, sc, sh, w_main, b_main, w_gate, b_gate)


def _mlstm_kernel(qk_ref, v_ref, o_ref, gc_ref, gr_ref, cw_ref, cb_ref, ng_ref, out_ref,
                  ct_ref, n_ref, m_ref, prev_ref):
    c = pl.program_id(1)
    L = MLSTM_CHUNK

    @pl.when(c == 0)
    def _():
        ct_ref[...] = jnp.zeros_like(ct_ref)
        n_ref[...] = jnp.zeros_like(n_ref)
        m_ref[...] = jnp.zeros_like(m_ref)
        prev_ref[...] = jnp.zeros_like(prev_ref)

    x = qk_ref[...].astype(F32)
    prev = prev_ref[...]
    row = lax.broadcasted_iota(I32, (L, 1), 0)
    y = cw_ref[CONV_WIDTH - 1:CONV_WIDTH, :] * x + cb_ref[...]
    for k in range(1, CONV_WIDTH):
        xs = jnp.where(row < k, pltpu.roll(prev, k, 0), pltpu.roll(x, k, 0))
        y = y + cw_ref[CONV_WIDTH - 1 - k:CONV_WIDTH - k, :] * xs
    prev_ref[...] = x
    y = y * _sigmoid(y)

    gc = gc_ref[...]
    gr = gr_ref[0]
    ti = lax.broadcasted_iota(I32, (L, L), 0)
    si = lax.broadcasted_iota(I32, (L, L), 1)
    causal = si <= ti
    tril = causal.astype(F32)
    triu = (ti <= si).astype(F32)
    b_cols = jnp.dot(tril, _log_sigmoid(gc), precision=lax.Precision.HIGHEST, preferred_element_type=F32)
    b_rows = jnp.dot(_log_sigmoid(gr), triu, precision=lax.Precision.HIGHEST, preferred_element_type=F32)

    qk_w = M_HEADS * M_DQK
    heads = range(M_HEADS)
    qf = [y[:, h * M_DQK:(h + 1) * M_DQK] for h in heads]
    kf = [y[:, qk_w + h * M_DQK:qk_w + (h + 1) * M_DQK] * (M_DQK ** -0.5) for h in heads]
    qb = [t.astype(BF16) for t in qf]
    vb = [v_ref[:, h * M_DV:(h + 1) * M_DV] for h in heads]
    bc = [b_cols[:, M_HEADS + h:M_HEADS + h + 1] for h in heads]
    ic = [gc[:, h:h + 1] for h in heads]
    br = [b_rows[M_HEADS + h:M_HEADS + h + 1, :] for h in heads]
    ir = [gr[h:h + 1, :] for h in heads]
    m_prev = [m_ref[h][0:1, 0:1] for h in heads]
    n_prev = [n_ref[h][0:1, :] for h in heads]

    b_last = [bc[h][L - 1:L, :] for h in heads]
    g_col = [b_last[h] - bc[h] + ic[h] for h in heads]
    m_new = [jnp.maximum(b_last[h] + m_prev[h], jnp.max(g_col[h], axis=0, keepdims=True)) for h in heads]
    w_col = [jnp.exp(g_col[h] - m_new[h]) for h in heads]
    decay = [jnp.exp(b_last[h] + m_prev[h] - m_new[h]) for h in heads]

    qk = [_dot_nt(qb[h], kf[h].astype(BF16)) for h in heads]
    q_state = [_dot(qb[h], ct_ref[h].astype(BF16)) for h in heads]
    kv_new = [_dot(kf[h].T.astype(BF16), (w_col[h] * vb[h].astype(F32)).astype(BF16)) for h in heads]

    dm = [jnp.where(causal, bc[h] - br[h] + ir[h], -jnp.inf) for h in heads]
    inter = [bc[h] + m_prev[h] for h in heads]
    mt = [jnp.maximum(inter[h], jnp.max(dm[h], axis=1, keepdims=True)) for h in heads]
    a = [jnp.exp(dm[h] - mt[h]) * qk[h] for h in heads]
    e_int = [jnp.exp(inter[h] - mt[h]) for h in heads]
    av = [_dot(a[h].astype(BF16), vb[h]) for h in heads]

    for h in heads:
        num = av[h] + e_int[h] * q_state[h]
        den = (jnp.sum(a[h], axis=1, keepdims=True)
               + e_int[h] * jnp.sum(qf[h] * n_prev[h], axis=1, keepdims=True))
        hh = num / jnp.maximum(jnp.abs(den), jnp.exp(-mt[h]))

        ct_ref[h] = decay[h] * ct_ref[h] + kv_new[h]
        n_new = decay[h] * n_prev[h] + jnp.sum(w_col[h] * kf[h], axis=0, keepdims=True)
        n_ref[h] = jnp.broadcast_to(n_new, n_ref.shape[1:])
        m_ref[h] = jnp.broadcast_to(m_new[h], m_ref.shape[1:])

        ms = jnp.mean(hh * hh, axis=1, keepdims=True)
        og = o_ref[:, h * M_DV:(h + 1) * M_DV].astype(F32)
        yh = hh * lax.rsqrt(ms + EPS) * ng_ref[:, h * M_DV:(h + 1) * M_DV] * _sigmoid(og)
        out_ref[:, h * M_DV:(h + 1) * M_DV] = yh.astype(BF16)


def _mlstm(proj, gates_c, gates_r, conv_w, conv_b, norm_g, bsz, seq):
    L = MLSTM_CHUNK
    nc = seq // L
    t = bsz * seq
    w = M_HEADS * M_DV
    return pl.pallas_call(
        _mlstm_kernel,
        grid=(bsz, nc),
        in_specs=[pl.BlockSpec((L, w), lambda b, c: (b * nc + c, 0)),
                  pl.BlockSpec((L, w), lambda b, c: (b * nc + c, 1)),
                  pl.BlockSpec((L, w), lambda b, c: (b * nc + c, 2)),
                  pl.BlockSpec((L, LANES), lambda b, c: (b * nc + c, 0)),
                  pl.BlockSpec((1, 8, L), lambda b, c: (b, 0, c)),
                  pl.BlockSpec((CONV_WIDTH, w), lambda b, c: (0, 0)),
                  pl.BlockSpec((1, w), lambda b, c: (0, 0)),
                  pl.BlockSpec((1, w), lambda b, c: (0, 0))],
        out_specs=pl.BlockSpec((L, w), lambda b, c: (b * nc + c, 0)),
        out_shape=jax.ShapeDtypeStruct((t, w), BF16),
        scratch_shapes=[pltpu.VMEM((M_HEADS, M_DQK, M_DV), F32),
                        pltpu.VMEM((M_HEADS, 8, M_DQK), F32),
                        pltpu.VMEM((M_HEADS, 8, LANES), F32),
                        pltpu.VMEM((L, w), F32)],
        compiler_params=_cparams(("arbitrary", "arbitrary")),
        name="mlstm",
    )(proj, proj, proj, gates_c, gates_r, conv_w, conv_b, norm_g)


def _attn_kernel(dilation, has_prev, *refs):
    nq = ATTN_BLOCK
    if has_prev:
        q_ref, kc_ref, vc_ref, kp_ref, vp_ref, o_ref, lse_ref, k_all, v_all = refs
        k_all[0:nq, :] = kp_ref[0, 0]
        k_all[nq:2 * nq, :] = kc_ref[0, 0]
        v_all[0:nq, :] = vp_ref[0, 0]
        v_all[nq:2 * nq, :] = vc_ref[0, 0]
        nk = 2 * nq
    else:
        q_ref, k_all, v_all, o_ref, lse_ref = refs
        k_all, v_all = k_all.at[0, 0], v_all.at[0, 0]
        nk = nq
    n = pl.program_id(2)
    qi = lax.broadcasted_iota(I32, (nq, nk), 0)
    ki = lax.broadcasted_iota(I32, (nq, nk), 1)
    dist = qi - ki + (nk - nq)
    ok = jnp.logical_and(dist >= 0, dist <= nq)
    if has_prev:
        ok = jnp.logical_and(ok, jnp.logical_or(ki >= nq, n > 0))
    dist_f = jnp.where(ok, dist.astype(F32), jnp.inf)
    lane = lax.broadcasted_iota(I32, (nq, LANES), 1)
    left_q = lane < A_DH
    left_k = lax.broadcasted_iota(I32, (nk, LANES), 1) < A_DH
    n_pairs = A_HEADS // 2

    scores = []
    for p in range(n_pairs):
        cols = slice(p * LANES, (p + 1) * LANES)
        qp = q_ref[0, 0, :, cols] * (A_DH ** -0.5)
        kp = k_all[:, cols]
        zero = jnp.zeros_like(qp)
        scores.append(_dot_nt(jnp.where(left_q, qp, zero), kp))
        scores.append(_dot_nt(jnp.where(left_q, zero, qp), kp))
    probs, maxes = [], []
    for h in range(A_HEADS):
        coef = -(2.0 ** (-8.0 * (h + 1) / A_HEADS)) * dilation
        s = scores[h] + dist_f * coef
        m = jnp.max(s, axis=1, keepdims=True)
        probs.append(jnp.exp(s - m).astype(BF16))
        maxes.append(m)
    lse_tile = jnp.zeros((nq, LANES), F32)
    for p in range(n_pairs):
        cols = slice(p * LANES, (p + 1) * LANES)
        vp = v_all[:, cols]
        one = jnp.ones_like(vp)
        pv_e = _dot(probs[2 * p], jnp.where(left_k, vp, one))
        pv_o = _dot(probs[2 * p + 1], jnp.where(left_k, one, vp))
        num = jnp.where(left_q, pv_e, pv_o)
        den = pltpu.roll(jnp.where(left_q, pv_o, pv_e), A_DH, 1)
        o_ref[0, 0, :, cols] = (num / den).astype(BF16)
        lse_tile = jnp.where(lane == 2 * p, maxes[2 * p] + jnp.log(pv_e[:, A_DH:A_DH + 1]), lse_tile)
        lse_tile = jnp.where(lane == 2 * p + 1, maxes[2 * p + 1] + jnp.log(pv_o[:, 0:1]), lse_tile)
    lse_ref[0, 0] = lse_tile


def _attn_group(qkv, dilation, col0):
    bsz, d, ls, _ = qkv.shape
    aw = A_HEADS * A_DH
    nq = ATTN_BLOCK
    nb = ls // nq
    has_prev = nb > 1
    blk = (1, 1, nq, aw)
    in_specs = [pl.BlockSpec(blk, lambda b, r, n: (b, r, n, col0)),
                pl.BlockSpec(blk, lambda b, r, n: (b, r, n, col0 + 1)),
                pl.BlockSpec(blk, lambda b, r, n: (b, r, n, col0 + 2))]
    args = [qkv, qkv, qkv]
    if has_prev:
        in_specs += [pl.BlockSpec(blk, lambda b, r, n: (b, r, jnp.maximum(n - 1, 0), col0 + 1)),
                     pl.BlockSpec(blk, lambda b, r, n: (b, r, jnp.maximum(n - 1, 0), col0 + 2))]
        args += [qkv, qkv]
    return pl.pallas_call(
        functools.partial(_attn_kernel, dilation, has_prev),
        grid=(bsz, d, nb),
        in_specs=in_specs,
        out_specs=[pl.BlockSpec(blk, lambda b, r, n: (b, r, n, 0)),
                   pl.BlockSpec((1, 1, nq, LANES), lambda b, r, n: (b, r, n, 0))],
        out_shape=[jax.ShapeDtypeStruct((bsz, d, ls, aw), BF16),
                   jax.ShapeDtypeStruct((bsz, d, ls, LANES), F32)],
        scratch_shapes=[pltpu.VMEM((2 * nq, aw), BF16)] * 2 if has_prev else [],
        compiler_params=_cparams(("arbitrary", "arbitrary", "arbitrary")),
        name=f"dilated_attn_d{dilation}",
    )(*args)


def _head_maps(n_heads, dh):
    w = n_heads * dh
    e = np.zeros((LANES, w), np.float32)
    for h in range(n_heads):
        e[h, h * dh:(h + 1) * dh] = 1.0
    return jnp.asarray(e, BF16), jnp.asarray(e.T.copy(), BF16)


def _natural_rows(ref, scr):
    d, n, w = ref.shape[1:]
    if d == 1:
        return ref[0, 0].astype(F32)
    cols = w // LANES
    for r in range(d):
        blk = ref[0, r].astype(F32)
        for c in range(cols):
            scr[c, pl.ds(r, n, stride=d), :] = blk[:, c * LANES:(c + 1) * LANES]
    return jnp.concatenate([scr[c] for c in range(cols)], axis=1)


def _merge_kernel(o1_ref, o2_ref, o3_ref, l1_ref, l2_ref, l3_ref, g_ref, e_ref, p_ref, y_ref, o_scr, l_scr):
    l1, l2, l3 = (_natural_rows(ref, l_scr.at[g]) for g, ref in enumerate((l1_ref, l2_ref, l3_ref)))
    mx = jnp.maximum(jnp.maximum(l1, l2), l3)
    w1, w2, w3 = jnp.exp(l1 - mx), jnp.exp(l2 - mx), jnp.exp(l3 - mx)
    inv = 1.0 / (w1 + w2 + w3)
    e = e_ref[...]
    o = (_dot_hilo(w1 * inv, e) * _natural_rows(o1_ref, o_scr.at[0])
         + _dot_hilo(w2 * inv, e) * _natural_rows(o2_ref, o_scr.at[1])
         + _dot_hilo(w3 * inv, e) * _natural_rows(o3_ref, o_scr.at[2]))
    ms = _dot_hilo(o * o, p_ref[...]) * (1.0 / A_DH)
    scale = _dot_hilo(lax.rsqrt(ms + EPS), e)
    y_ref[...] = (o * scale * g_ref[...]).astype(BF16)


def _attn_merge(outs, lses, norm_g):
    bsz, _, seq, aw = outs[0].shape
    t = bsz * seq
    tm = 512
    per_b = seq // tm
    expand, pool = _head_maps(A_HEADS, A_DH)
    const = lambda i: (0, 0)

    def grouped(arr):
        d, w = arr.shape[1], arr.shape[3]
        return pl.BlockSpec((1, d, tm // d, w), lambda i: (i // per_b, 0, i % per_b, 0))

    return pl.pallas_call(
        _merge_kernel,
        grid=(t // tm,),
        in_specs=[grouped(a) for a in outs] + [grouped(a) for a in lses]
        + [pl.BlockSpec((1, aw), const), pl.BlockSpec((LANES, aw), const), pl.BlockSpec((aw, LANES), const)],
        out_specs=pl.BlockSpec((tm, aw), lambda i: (i, 0)),
        out_shape=jax.ShapeDtypeStruct((t, aw), BF16),
        scratch_shapes=[pltpu.VMEM((len(outs), aw // LANES, tm, LANES), F32),
                        pltpu.VMEM((len(lses), 1, tm, LANES), F32)],
        compiler_params=_cparams(("arbitrary",)),
        name="attn_merge",
    )(*outs, *lses, norm_g, expand, pool)


def _outproj_kernel(alpha, ym_ref, ya_ref, w_ref, x_ref, g1_ref, sc_ref, sh_ref, lg_ref, lb_ref,
                    rw_ref, rb_ref, x1_ref, h2_ref, ti_ref, tg_ref):
    half = ym_ref.shape[1]
    y = _dot(ym_ref[...], w_ref[0:half, :]) + _dot(ya_ref[...], w_ref[half:2 * half, :])
    z = alpha * x_ref[...] + (1.0 + g1_ref[0]) * y
    x1 = _layer_norm(z, lg_ref[...], lb_ref[...])
    x1_ref[...] = x1
    h2 = x1 * (1.0 + sc_ref[0]) + sh_ref[0]
    h2_ref[...] = h2
    logits = _dot(h2.astype(BF16), rw_ref[...]) + rb_ref[...]
    lane = lax.broadcasted_iota(I32, logits.shape, 1)
    lane_f = lane.astype(F32)
    work = jnp.where(lane < N_EXPERTS, logits, -jnp.inf)
    idx_tile = jnp.zeros(logits.shape, F32)
    val_tile = jnp.zeros(logits.shape, F32)
    top = None
    denom = None
    for k in range(TOP_K):
        mk = jnp.max(work, axis=1, keepdims=True)
        ik = jnp.min(jnp.where(work == mk, lane_f, float(LANES)), axis=1, keepdims=True)
        work = jnp.where(lane_f == ik, -jnp.inf, work)
        if k == 0:
            top = mk
        ek = jnp.exp(mk - top)
        denom = ek if k == 0 else denom + ek
        idx_tile = jnp.where(lane == k, ik, idx_tile)
        val_tile = jnp.where(lane == k, ek, val_tile)
    ti_ref[...] = idx_tile.astype(I32)
    tg_ref[...] = val_tile / denom


def _out_proj(y_m, y_a, w_out, x2, g1, sc2, sh2, ln_g, ln_b, rw, rb, alpha, seq):
    t, d = x2.shape
    half = y_m.shape[1]
    tm = 256
    per_b = seq // tm
    row = lambda i: (i, 0)
    const = lambda i: (0, 0)
    mod = lambda i: (i // per_b, 0, 0)
    return pl.pallas_call(
        functools.partial(_outproj_kernel, alpha),
        grid=(t // tm,),
        in_specs=[pl.BlockSpec((tm, half), row), pl.BlockSpec((tm, half), row),
                  pl.BlockSpec((2 * half, d), const), pl.BlockSpec((tm, d), row),
                  pl.BlockSpec((1, 1, d), mod), pl.BlockSpec((1, 1, d), mod), pl.BlockSpec((1, 1, d), mod),
                  pl.BlockSpec((1, d), const), pl.BlockSpec((1, d), const),
                  pl.BlockSpec((d, LANES), const), pl.BlockSpec((1, LANES), const)],
        out_specs=[pl.BlockSpec((tm, d), row), pl.BlockSpec((tm, d), row),
                   pl.BlockSpec((tm, LANES), row), pl.BlockSpec((tm, LANES), row)],
        out_shape=[jax.ShapeDtypeStruct((t, d), F32), jax.ShapeDtypeStruct((t, d), F32),
                   jax.ShapeDtypeStruct((t, LANES), I32), jax.ShapeDtypeStruct((t, LANES), F32)],
        compiler_params=_cparams(("arbitrary",)),
        name="out_proj_ln_router",
    )(y_m, y_a, w_out, x2, g1, sc2, sh2, ln_g, ln_b, rw, rb)


def _expert_kernel(nj, ie_ref, ir_ref, ic_ref, iv_ref, tail_ref, tok_hbm, h_hbm, wgu_ref, bgu_ref, wdn_ref,
                   bdn_ref, sel_ref, y_hbm, tok_smem, ring, xb, yacc, wgu_b, wdn_b, gu_scr, gsem, sems):
    i = pl.program_id(0)
    j = pl.program_id(1)
    row0 = ir_ref[i]
    nch = ic_ref[i]
    ch = MOE_CHUNK
    lead = MOE_ITEM_ROWS
    ring_chunks = ring.shape[0] // ch
    per_body = ch // nj

    def chunk_slot(first_row):
        return lax.rem(lax.shift_right_logical(first_row, 8), ring_chunks)

    def load_tokens(first_row):
        rows = pl.ds(lax.shift_right_logical(first_row, 7), lead // LANES)
        cp = pltpu.make_async_copy(tok_hbm.at[rows, :], tok_smem, sems.at[0])
        cp.start()
        cp.wait()

    def gather_rows(first_row, first_idx, count):
        slot = chunk_slot(first_row)
        base = slot * ch + jnp.bitwise_and(first_row, ch - 1)
        for u in range(count):
            idx = first_idx + u
            t = tok_smem[lax.shift_right_logical(idx, 7), jnp.bitwise_and(idx, LANES - 1)]
            pltpu.make_async_copy(h_hbm.at[pl.ds(t, 1), :], ring.at[pl.ds(base + u, 1), :], gsem.at[slot]).start()

    def chunk_wait(first_row):
        slot = chunk_slot(first_row)
        rows = pl.ds(pl.multiple_of(slot * ch, ch), ch)
        pltpu.make_async_copy(h_hbm.at[pl.ds(0, ch), :], ring.at[rows, :], gsem.at[slot]).wait()
        return rows

    @pl.when(jnp.logical_and(i == 0, j == 0))
    def _():
        load_tokens(0)

        def issue(g, carry):
            gather_rows(g * DMA_UNROLL, g * DMA_UNROLL, DMA_UNROLL)
            return carry

        lax.fori_loop(0, lead // DMA_UNROLL, issue, 0)

    def out_wait(count):
        def drain(m, carry):
            pltpu.make_async_copy(yacc.at[pl.ds(0, ch), :], y_hbm.at[pl.ds(0, ch), :], sems.at[1]).wait()
            return carry

        lax.fori_loop(0, count, drain, 0)

    @pl.when(jnp.logical_and(j == 0, nch > 0))
    def _():
        load_tokens(row0 + lead)

        def cast_rows(m, carry):
            src = chunk_wait(row0 + m * ch)
            xb[pl.ds(pl.multiple_of(m * ch, ch), ch), :] = ring[src, :].astype(BF16)
            return carry

        lax.fori_loop(0, nch, cast_rows, 0)

        @pl.when(i > 0)
        def _():
            out_wait(ic_ref[jnp.maximum(i - 1, 0)])

        def seed(m, carry):
            yacc[pl.ds(pl.multiple_of(m * ch, ch), ch), :] = jnp.broadcast_to(bdn_ref[0], (ch, yacc.shape[1]))
            return carry

        lax.fori_loop(0, nch, seed, 0)

    @pl.when(nch > 0)
    def _():
        wgu_b[...] = wgu_ref[0].astype(BF16)
        wdn_b[...] = wdn_ref[0].astype(BF16)
        bgu = bgu_ref[0]
        last = j == nj - 1

        def prefetch(m):
            first_idx = (j * nch + m) * per_body
            gather_rows(row0 + lead + first_idx, first_idx, per_body)

        def gate_up(m):
            rows = pl.ds(pl.multiple_of(m * ch, ch), ch)
            return _dot(xb[rows, :], wgu_b[...]) + bgu

        def finish(m, gu):
            rows = pl.ds(pl.multiple_of(m * ch, ch), ch)
            glu = jnp.minimum(gu, SWIGLU_LIMIT)
            f_glu = glu * _sigmoid(SWIGLU_ALPHA * glu)
            f_lin = jnp.clip(gu, -SWIGLU_LIMIT, SWIGLU_LIMIT) + 1.0
            prod = (pltpu.roll(f_glu, 1, 1) * f_lin).astype(BF16)
            parts = [_dot(prod[:, q * 2 * LANES:(q + 1) * 2 * LANES], sel_ref[...])
                     for q in range(MOE_TN // (2 * LANES))]
            act = jnp.concatenate(parts, axis=1).astype(BF16)
            yacc[rows, :] += _dot(act, wdn_b[...])

        def step(m):
            gu = gu_scr[...]
            gu_scr[...] = gate_up(m + 1)
            finish(m, gu)
            prefetch(m)

        gu_scr[...] = gate_up(0)
        n_steps = nch - 1

        def pair(p, carry):
            step(2 * p)
            step(2 * p + 1)
            return carry

        lax.fori_loop(0, lax.shift_right_logical(n_steps, 1), pair, 0)

        @pl.when(jnp.bitwise_and(n_steps, 1) == 1)
        def _():
            step(n_steps - 1)

        finish(nch - 1, gu_scr[...])
        prefetch(nch - 1)

        @pl.when(last)
        def _():
            def issue(m, carry):
                rows = pl.ds(pl.multiple_of(m * ch, ch), ch)
                dst = pl.ds(pl.multiple_of(row0 + m * ch, ch), ch)
                pltpu.make_async_copy(yacc.at[rows, :], y_hbm.at[dst, :], sems.at[1]).start()
                return carry

            lax.fori_loop(0, nch, issue, 0)

    @pl.when(jnp.logical_and(i == pl.num_programs(0) - 1, j == nj - 1))
    def _():
        out_wait(tail_ref[1])

        def drain(m, carry):
            chunk_wait(tail_ref[0] + m * ch)
            return carry

        lax.fori_loop(0, lead // ch, drain, 0)


def _experts(h2, tok_pad, item_e, item_row0, item_nch, item_valid, item_tail, w_gu, b_gu, w_dn, b_dn, n_rows):
    ne, d, two_de = w_gu.shape
    tn = MOE_TN
    nj = two_de // tn
    assert nj > 1, "the kernel separates its first and last column-tile steps"
    assert MOE_CHUNK % nj == 0 and (MOE_CHUNK // nj) % 8 == 0, "row copies per chunk stage"
    ni = item_e.shape[0]
    ring_rows = 2 * MOE_ITEM_ROWS
    sel = np.zeros((2 * LANES, LANES), np.float32)
    sel[2 * np.arange(LANES) + 1, np.arange(LANES)] = 1.0

    def jmap(i, j, iv):
        return jnp.where(iv[i] > 0, j, nj - 1)

    grid_spec = pltpu.PrefetchScalarGridSpec(
        num_scalar_prefetch=5,
        grid=(ni, nj),
        in_specs=[pl.BlockSpec(memory_space=pl.ANY),
                  pl.BlockSpec(memory_space=pl.ANY),
                  pl.BlockSpec((1, d, tn), lambda i, j, ie, ir, ic, iv, it: (ie[i], 0, jmap(i, j, iv))),
                  pl.BlockSpec((1, 1, tn), lambda i, j, ie, ir, ic, iv, it: (ie[i], 0, jmap(i, j, iv))),
                  pl.BlockSpec((1, tn // 2, d), lambda i, j, ie, ir, ic, iv, it: (ie[i], jmap(i, j, iv), 0)),
                  pl.BlockSpec((1, 1, d), lambda i, j, ie, ir, ic, iv, it: (ie[i], 0, 0)),
                  pl.BlockSpec((2 * LANES, LANES), lambda i, j, ie, ir, ic, iv, it: (0, 0))],
        out_specs=pl.BlockSpec(memory_space=pl.ANY),
        scratch_shapes=[pltpu.SMEM((MOE_ITEM_ROWS // LANES, LANES), I32),
                        pltpu.VMEM((ring_rows, d), F32),
                        pltpu.VMEM((MOE_ITEM_ROWS, d), BF16),
                        pltpu.VMEM((MOE_ITEM_ROWS, d), F32),
                        pltpu.VMEM((d, tn), BF16),
                        pltpu.VMEM((tn // 2, d), BF16),
                        pltpu.VMEM((MOE_CHUNK, tn), F32),
                        pltpu.SemaphoreType.DMA((ring_rows // MOE_CHUNK,)),
                        pltpu.SemaphoreType.DMA((2,))],
    )
    return pl.pallas_call(
        functools.partial(_expert_kernel, nj),
        grid_spec=grid_spec,
        out_shape=jax.ShapeDtypeStruct((n_rows, d), F32),
        compiler_params=_cparams(("arbitrary", "arbitrary")),
        name="moe_experts",
    )(item_e, item_row0, item_nch, item_valid, item_tail, tok_pad, h2, w_gu, b_gu.reshape(ne, 1, two_de),
      w_dn, b_dn.reshape(ne, 1, d), jnp.asarray(sel, BF16))


def _combine_kernel(alpha, dest_ref, y_hbm, gate_ref, x1_ref, g2_ref, lg_ref, lb_ref, out_ref, buf, sem):
    tc = COMBINE_TOKENS

    def row_copy(s):
        k = jnp.bitwise_and(s, TOP_K - 1)
        r = lax.shift_right_logical(s, 2)
        src = dest_ref[0, 0, s]
        return pltpu.make_async_copy(y_hbm.at[pl.ds(src, 1), :], buf.at[k, pl.ds(r, 1), :], sem.at[0])

    def issue(g, carry):
        for u in range(DMA_UNROLL):
            row_copy(g * DMA_UNROLL + u).start()
        return carry

    lax.fori_loop(0, tc * TOP_K // DMA_UNROLL, issue, 0)
    for k in range(TOP_K):
        pltpu.make_async_copy(y_hbm.at[pl.ds(0, tc), :], buf.at[k], sem.at[0]).wait()

    gates = gate_ref[...]
    y = gates[:, 0:1] * buf[0]
    for k in range(1, TOP_K):
        y = y + gates[:, k:k + 1] * buf[k]
    z = alpha * x1_ref[...] + (1.0 + g2_ref[0]) * y
    out_ref[...] = _layer_norm(z, lg_ref[...], lb_ref[...])


def _combine(dest, y_pad, gates, x1, g2, ln_g, ln_b, alpha, seq):
    t, d = x1.shape
    tc = COMBINE_TOKENS
    per_b = seq // tc
    row = lambda i: (i, 0)
    const = lambda i: (0, 0)
    return pl.pallas_call(
        functools.partial(_combine_kernel, alpha),
        grid=(t // tc,),
        in_specs=[pl.BlockSpec((1, 1, tc * TOP_K), lambda i: (i, 0, 0), memory_space=pltpu.SMEM),
                  pl.BlockSpec(memory_space=pl.ANY),
                  pl.BlockSpec((tc, LANES), row), pl.BlockSpec((tc, d), row),
                  pl.BlockSpec((1, 1, d), lambda i: (i // per_b, 0, 0)),
                  pl.BlockSpec((1, d), const), pl.BlockSpec((1, d), const)],
        out_specs=pl.BlockSpec((tc, d), row),
        out_shape=jax.ShapeDtypeStruct((t, d), F32),
        scratch_shapes=[pltpu.VMEM((TOP_K, tc, d), F32), pltpu.SemaphoreType.DMA((1,))],
        compiler_params=_cparams(("arbitrary",)),
        name="moe_combine_ln",
    )(dest.reshape(t // tc, 1, tc * TOP_K), y_pad, gates, x1, g2, ln_g, ln_b)


def _count_le(ends, q):
    return jnp.sum((ends[None, :] <= q[:, None]).astype(I32), axis=1)


def _lookup(table, idx):
    hit = idx[:, None] == jnp.arange(table.shape[0], dtype=I32)
    return jnp.sum(jnp.where(hit, table[None, :], 0), axis=1)


def _routing_tables(top_idx):
    t = top_idx.shape[0]
    tk = t * TOP_K
    experts = jnp.arange(N_EXPERTS, dtype=I32)
    hits = [top_idx[:, k:k + 1] == experts for k in range(TOP_K)]
    onehot = sum(h.astype(I32) for h in hits)
    csum = jnp.cumsum(onehot, axis=0)
    counts = csum[-1]
    earlier = csum - onehot
    starts = jnp.cumsum(counts) - counts
    padded = ((counts + MOE_ROW_PAD - 1) // MOE_ROW_PAD) * MOE_ROW_PAD
    pad_end = jnp.cumsum(padded)
    pad_start = pad_end - padded
    dest = jnp.stack([jnp.sum(jnp.where(h, earlier + pad_start, 0), axis=1) for h in hits], axis=1).reshape(tk)

    n_rows = tk + N_EXPERTS * MOE_ROW_PAD
    n_tab = n_rows + 2 * MOE_ITEM_ROWS
    tok_sorted = (jnp.argsort(top_idx.reshape(tk)) // TOP_K).astype(I32)
    blk = jnp.arange(n_tab // MOE_ROW_PAD, dtype=I32) * MOE_ROW_PAD
    blk_e = jnp.minimum(_count_le(pad_end, blk), N_EXPERTS - 1)
    local = (blk - _lookup(pad_start, blk_e))[:, None] + jnp.arange(MOE_ROW_PAD, dtype=I32)
    src = jnp.clip(_lookup(starts, blk_e)[:, None] + local, 0, tk - 1)
    live = jnp.logical_and(local < _lookup(counts, blk_e)[:, None], (blk < pad_end[-1])[:, None])
    tok_pad = jnp.where(live, tok_sorted[src], 0).reshape(-1, LANES)

    items_per = (padded + MOE_ITEM_ROWS - 1) // MOE_ITEM_ROWS
    item_end = jnp.cumsum(items_per)
    item_start = item_end - items_per
    n_items = N_EXPERTS + n_rows // MOE_ITEM_ROWS
    idx = jnp.arange(n_items, dtype=I32)
    valid = idx < item_end[-1]
    e_i = jnp.minimum(_count_le(item_end, idx), N_EXPERTS - 1)
    k_i = idx - _lookup(item_start, e_i)
    row0 = _lookup(pad_start, e_i) + k_i * MOE_ITEM_ROWS
    nrows = jnp.clip(_lookup(padded, e_i) - k_i * MOE_ITEM_ROWS, 0, MOE_ITEM_ROWS)
    e_last = jnp.sum(jnp.where(idx == item_end[-1] - 1, e_i, 0))
    item_e = jnp.where(valid, e_i, e_last).astype(I32)
    item_row0 = jnp.where(valid, row0, 0).astype(I32)
    item_nch = jnp.where(valid, nrows // MOE_CHUNK, 0).astype(I32)
    item_tail = jnp.stack([pad_end[-1], jnp.sum(jnp.where(idx == item_end[-1] - 1, item_nch, 0))]).astype(I32)
    return tok_pad, dest, item_e, item_row0, item_nch, valid.astype(I32), item_tail, n_rows


def kernel(x, c, w_ada, b_ada, w_in, b_in, conv_w, conv_b, m_norm_g, a_norm_g, w_out, ln1_g, ln1_b,
           router_w, router_b, w_gu, b_gu, w_dn, b_dn, ln2_g, ln2_b):
    bsz, seq, d = x.shape
    depth = w_ada.shape[0]
    t = bsz * seq
    alpha = float((2 * depth) ** 0.25)
    qk_w = 2 * M_HEADS * M_DQK
    mv_w = M_HEADS * M_DV
    aw = A_HEADS * A_DH
    gate_lo = qk_w + 2 * mv_w
    gate_hi = gate_lo + 2 * M_HEADS

    x2 = x.reshape(t, d)
    for l in range(depth):
        mod = _ada_mod(c, w_ada[l], b_ada[l]).reshape(bsz, 6, 1, d)
        sh1, sc1, g1, sh2, sc2, g2 = (mod[:, i] for i in range(6))

        w_main = jnp.concatenate([w_in[l][:, :gate_lo], w_in[l][:, gate_hi:]], axis=1).astype(BF16)
        b_main = jnp.concatenate([b_in[l][:gate_lo], b_in[l][gate_hi:]]).reshape(1, -1)
        w_gate = jnp.zeros((d, LANES), BF16).at[:, :2 * M_HEADS].set(w_in[l][:, gate_lo:gate_hi].astype(BF16))
        b_gate = jnp.zeros((1, LANES), F32).at[0, :2 * M_HEADS].set(b_in[l][gate_lo:gate_hi])
        attn_col0 = gate_lo // aw
        proj, gates_c, *qkv_dil = _in_proj(x2, sc1, sh1, w_main, b_main, w_gate, b_gate, seq, attn_col0)

        gates_r = gates_c[:, :2 * M_HEADS].reshape(bsz, seq, 2 * M_HEADS).transpose(0, 2, 1)
        y_m = _mlstm(proj, gates_c, gates_r, conv_w[l], conv_b[l].reshape(1, -1),
                     m_norm_g[l].reshape(1, -1), bsz, seq)

        outs, lses = [], []
        for dil in DILATIONS:
            if dil == 1:
                o_d, lse_d = _attn_group(proj.reshape(bsz, 1, seq, -1), dil, attn_col0)
            else:
                o_d, lse_d = _attn_group(qkv_dil[DILATIONS.index(dil) - 1], dil, 0)
            outs.append(o_d)
            lses.append(lse_d)
        y_a = _attn_merge(outs, lses, a_norm_g[l].reshape(1, -1))

        rw = jnp.zeros((d, LANES), BF16).at[:, :N_EXPERTS].set(router_w[l].astype(BF16))
        rb = jnp.zeros((1, LANES), F32).at[0, :N_EXPERTS].set(router_b[l])
        x1, h2, top_idx, gates = _out_proj(y_m, y_a, w_out[l].astype(BF16), x2, g1, sc2, sh2,
                                           ln1_g[l].reshape(1, -1), ln1_b[l].reshape(1, -1), rw, rb, alpha, seq)

        (tok_pad, dest, item_e, item_row0, item_nch, item_valid, item_tail,
         n_rows) = _routing_tables(top_idx[:, :TOP_K])
        y_pad = _experts(h2, tok_pad, item_e, item_row0, item_nch, item_valid, item_tail,
                         w_gu[l], b_gu[l], w_dn[l], b_dn[l], n_rows)
        x2 = _combine(dest, y_pad, gates, x1, g2, ln2_g[l].reshape(1, -1), ln2_b[l].reshape(1, -1), alpha, seq)
    return x2.reshape(bsz, seq, d)
```

```python
import functools

import jax
import jax.numpy as jnp
import numpy as np
from jax import lax
from jax.experimental import pallas as pl
from jax.experimental.pallas import tpu as pltpu

F32 = jnp.float32
BF16 = jnp.bfloat16
I32 = jnp.int32

M_HEADS = 4
M_DQK = 128
M_DV = 256
CONV_WIDTH = 4
A_HEADS = 16
A_DH = 64
ATTN_BLOCK = 128
DILATIONS = (1, 4, 16)
N_EXPERTS = 32
TOP_K = 4
SWIGLU_ALPHA = 1.702
SWIGLU_LIMIT = 7.0
EPS = 1e-5

LANES = 128
VMEM_LIMIT = 56 * 1024 * 1024

MLSTM_CHUNK = 256
MOE_ROW_PAD = 256
MOE_CHUNK = 256
MOE_ITEM_ROWS = 1280
MOE_TN = 512
COMBINE_TOKENS = 128
DMA_UNROLL = 8


def _cparams(sem, vmem=VMEM_LIMIT):
    return pltpu.CompilerParams(dimension_semantics=sem, vmem_limit_bytes=vmem)


def _sigmoid(x):
    return 1.0 / (1.0 + jnp.exp(-x))


def _log_sigmoid(x):
    return jnp.minimum(x, 0.0) - jnp.log(1.0 + jnp.exp(-jnp.abs(x)))


def _layer_norm(z, g, b):
    mu = jnp.mean(z, axis=-1, keepdims=True)
    zc = z - mu
    var = jnp.mean(zc * zc, axis=-1, keepdims=True)
    return zc * lax.rsqrt(var + EPS) * g + b


def _dot(a, b):
    return jnp.dot(a, b, preferred_element_type=F32)


def _dot_nt(a, b):
    return lax.dot_general(a, b, (((1,), (1,)), ((), ())), preferred_element_type=F32)


def _dot_hilo(a, sel):
    hi = a.astype(BF16)
    lo = (a - hi.astype(F32)).astype(BF16)
    return _dot(hi, sel) + _dot(lo, sel)


def _ada_kernel(c_ref, w_ref, b_ref, o_ref):
    c = c_ref[...]
    cond = c * _sigmoid(c)
    o_ref[...] = _dot(cond.astype(BF16), w_ref[...].astype(BF16)) + b_ref[...]


def _ada_mod(c, w_ada, b_ada):
    bsz, d = c.shape
    n = w_ada.shape[1]
    tn = 1024
    rows = 8
    c_pad = jnp.zeros((rows, d), F32).at[:bsz].set(c)
    out = pl.pallas_call(
        _ada_kernel,
        grid=(n // tn,),
        in_specs=[pl.BlockSpec((rows, d), lambda j: (0, 0)),
                  pl.BlockSpec((d, tn), lambda j: (0, j)),
                  pl.BlockSpec((1, tn), lambda j: (0, j))],
        out_specs=pl.BlockSpec((rows, tn), lambda j: (0, j)),
        out_shape=jax.ShapeDtypeStruct((rows, n), F32),
        compiler_params=_cparams(("arbitrary",)),
        name="ada_mod",
    )(c_pad, w_ada, b_ada.reshape(1, n))
    return out[:bsz]


def _inproj_kernel(attn_col0, x_ref, sc_ref, sh_ref, w_ref, b_ref, wg_ref, bg_ref, o_ref, g_ref, *rest):
    dil_refs, (h_ref, r_scr) = rest[:-2], rest[-2:]
    j = pl.program_id(1)

    @pl.when(j == 0)
    def _():
        h = x_ref[...] * (1.0 + sc_ref[0]) + sh_ref[0]
        hb = h.astype(BF16)
        h_ref[...] = hb
        g_ref[...] = _dot(hb, wg_ref[...]) + bg_ref[...]

    res = _dot(h_ref[...], w_ref[...]) + b_ref[...]
    o_ref[...] = res.astype(BF16)

    @pl.when(j >= attn_col0)
    def _():
        cols = res.shape[1] // LANES
        for c in range(cols):
            r_scr[c] = res[:, c * LANES:(c + 1) * LANES]
        for ref in dil_refs:
            d, n = ref.shape[1], ref.shape[2]
            for r in range(d):
                for c in range(cols):
                    ref[0, r, :, c * LANES:(c + 1) * LANES] = r_scr[c, pl.ds(r, n, stride=d), :].astype(BF16)


def _in_proj(x2, sc, sh, w_main, b_main, w_gate, b_gate, seq, attn_col0):
    t, d = x2.shape
    n = w_main.shape[1]
    tm, tn = 512, 1024
    per_b = seq // tm
    bsz = t // seq
    dils = [dl for dl in DILATIONS if dl > 1]
    aw3 = n - attn_col0 * tn
    dil_specs = [pl.BlockSpec((1, dl, tm // dl, tn),
                              lambda i, j: (i // per_b, 0, i % per_b, jnp.maximum(j - attn_col0, 0))) for dl in dils]
    dil_shapes = [jax.ShapeDtypeStruct((bsz, dl, seq // dl, aw3), BF16) for dl in dils]
    return pl.pallas_call(
        functools.partial(_inproj_kernel, attn_col0),
        grid=(t // tm, n // tn),
        in_specs=[pl.BlockSpec((tm, d), lambda i, j: (i, 0)),
                  pl.BlockSpec((1, 1, d), lambda i, j: (i // per_b, 0, 0)),
                  pl.BlockSpec((1, 1, d), lambda i, j: (i // per_b, 0, 0)),
                  pl.BlockSpec((d, tn), lambda i, j: (0, j)),
                  pl.BlockSpec((1, tn), lambda i, j: (0, j)),
                  pl.BlockSpec((d, LANES), lambda i, j: (0, 0)),
                  pl.BlockSpec((1, LANES), lambda i, j: (0, 0))],
        out_specs=[pl.BlockSpec((tm, tn), lambda i, j: (i, j)),
                   pl.BlockSpec((tm, LANES), lambda i, j: (i, 0))] + dil_specs,
        out_shape=[jax.ShapeDtypeStruct((t, n), BF16),
                   jax.ShapeDtypeStruct((t, LANES), F32)] + dil_shapes,
        scratch_shapes=[pltpu.VMEM((tm, d), BF16), pltpu.VMEM((tn // LANES, tm, LANES), F32)],
        compiler_params=_cparams(("arbitrary", "arbitrary")),
        name="in_proj",
    )(x2, sc, sh, w_main, b_main, w_gate, b_gate)


def _mlstm_kernel(qk_ref, v_ref, o_ref, gc_ref, gr_ref, cw_ref, cb_ref, ng_ref, out_ref,
                  ct_ref, n_ref, m_ref, prev_ref):
    c = pl.program_id(1)
    L = MLSTM_CHUNK

    @pl.when(c == 0)
    def _():
        ct_ref[...] = jnp.zeros_like(ct_ref)
        n_ref[...] = jnp.zeros_like(n_ref)
        m_ref[...] = jnp.zeros_like(m_ref)
        prev_ref[...] = jnp.zeros_like(prev_ref)

    x = qk_ref[...].astype(F32)
    prev = prev_ref[...]
    row = lax.broadcasted_iota(I32, (L, 1), 0)
    y = cw_ref[CONV_WIDTH - 1:CONV_WIDTH, :] * x + cb_ref[...]
    for k in range(1, CONV_WIDTH):
        xs = jnp.where(row < k, pltpu.roll(prev, k, 0), pltpu.roll(x, k, 0))
        y = y + cw_ref[CONV_WIDTH - 1 - k:CONV_WIDTH - k, :] * xs
    prev_ref[...] = x
    y = y * _sigmoid(y)

    gc = gc_ref[...]
    gr = gr_ref[0]
    ti = lax.broadcasted_iota(I32, (L, L), 0)
    si = lax.broadcasted_iota(I32, (L, L), 1)
    causal = si <= ti
    tril = causal.astype(F32)
    triu = (ti <= si).astype(F32)
    b_cols = jnp.dot(tril, _log_sigmoid(gc), precision=lax.Precision.HIGHEST, preferred_element_type=F32)
    b_rows = jnp.dot(_log_sigmoid(gr), triu, precision=lax.Precision.HIGHEST, preferred_element_type=F32)

    qk_w = M_HEADS * M_DQK
    heads = range(M_HEADS)
    qf = [y[:, h * M_DQK:(h + 1) * M_DQK] for h in heads]
    kf = [y[:, qk_w + h * M_DQK:qk_w + (h + 1) * M_DQK] * (M_DQK ** -0.5) for h in heads]
    qb = [t.astype(BF16) for t in qf]
    vb = [v_ref[:, h * M_DV:(h + 1) * M_DV] for h in heads]
    bc = [b_cols[:, M_HEADS + h:M_HEADS + h + 1] for h in heads]
    ic = [gc[:, h:h + 1] for h in heads]
    br = [b_rows[M_HEADS + h:M_HEADS + h + 1, :] for h in heads]
    ir = [gr[h:h + 1, :] for h in heads]
    m_prev = [m_ref[h][0:1, 0:1] for h in heads]
    n_prev = [n_ref[h][0:1, :] for h in heads]

    b_last = [bc[h][L - 1:L, :] for h in heads]
    g_col = [b_last[h] - bc[h] + ic[h] for h in heads]
    m_new = [jnp.maximum(b_last[h] + m_prev[h], jnp.max(g_col[h], axis=0, keepdims=True)) for h in heads]
    w_col = [jnp.exp(g_col[h] - m_new[h]) for h in heads]
    decay = [jnp.exp(b_last[h] + m_prev[h] - m_new[h]) for h in heads]

    qk = [_dot_nt(qb[h], kf[h].astype(BF16)) for h in heads]
    q_state = [_dot(qb[h], ct_ref[h].astype(BF16)) for h in heads]
    kv_new = [_dot(kf[h].T.astype(BF16), (w_col[h] * vb[h].astype(F32)).astype(BF16)) for h in heads]

    dm = [jnp.where(causal, bc[h] - br[h] + ir[h], -jnp.inf) for h in heads]
    inter = [bc[h] + m_prev[h] for h in heads]
    mt = [jnp.maximum(inter[h], jnp.max(dm[h], axis=1, keepdims=True)) for h in heads]
    a = [jnp.exp(dm[h] - mt[h]) * qk[h] for h in heads]
    e_int = [jnp.exp(inter[h] - mt[h]) for h in heads]
    av = [_dot(a[h].astype(BF16), vb[h]) for h in heads]

    for h in heads:
        num = av[h] + e_int[h] * q_state[h]
        den = (jnp.sum(a[h], axis=1, keepdims=True)
               + e_int[h] * jnp.sum(qf[h] * n_prev[h], axis=1, keepdims=True))
        hh = num / jnp.maximum(jnp.abs(den), jnp.exp(-mt[h]))

        ct_ref[h] = decay[h] * ct_ref[h] + kv_new[h]
        n_new = decay[h] * n_prev[h] + jnp.sum(w_col[h] * kf[h], axis=0, keepdims=True)
        n_ref[h] = jnp.broadcast_to(n_new, n_ref.shape[1:])
        m_ref[h] = jnp.broadcast_to(m_new[h], m_ref.shape[1:])

        ms = jnp.mean(hh * hh, axis=1, keepdims=True)
        og = o_ref[:, h * M_DV:(h + 1) * M_DV].astype(F32)
        yh = hh * lax.rsqrt(ms + EPS) * ng_ref[:, h * M_DV:(h + 1) * M_DV] * _sigmoid(og)
        out_ref[:, h * M_DV:(h + 1) * M_DV] = yh.astype(BF16)


def _mlstm(proj, gates_c, gates_r, conv_w, conv_b, norm_g, bsz, seq):
    L = MLSTM_CHUNK
    nc = seq // L
    t = bsz * seq
    w = M_HEADS * M_DV
    return pl.pallas_call(
        _mlstm_kernel,
        grid=(bsz, nc),
        in_specs=[pl.BlockSpec((L, w), lambda b, c: (b * nc + c, 0)),
                  pl.BlockSpec((L, w), lambda b, c: (b * nc + c, 1)),
                  pl.BlockSpec((L, w), lambda b, c: (b * nc + c, 2)),
                  pl.BlockSpec((L, LANES), lambda b, c: (b * nc + c, 0)),
                  pl.BlockSpec((1, 8, L), lambda b, c: (b, 0, c)),
                  pl.BlockSpec((CONV_WIDTH, w), lambda b, c: (0, 0)),
                  pl.BlockSpec((1, w), lambda b, c: (0, 0)),
                  pl.BlockSpec((1, w), lambda b, c: (0, 0))],
        out_specs=pl.BlockSpec((L, w), lambda b, c: (b * nc + c, 0)),
        out_shape=jax.ShapeDtypeStruct((t, w), BF16),
        scratch_shapes=[pltpu.VMEM((M_HEADS, M_DQK, M_DV), F32),
                        pltpu.VMEM((M_HEADS, 8, M_DQK), F32),
                        pltpu.VMEM((M_HEADS, 8, LANES), F32),
                        pltpu.VMEM((L, w), F32)],
        compiler_params=_cparams(("arbitrary", "arbitrary")),
        name="mlstm",
    )(proj, proj, proj, gates_c, gates_r, conv_w, conv_b, norm_g)


def _attn_kernel(dilation, has_prev, *refs):
    nq = ATTN_BLOCK
    if has_prev:
        q_ref, kc_ref, vc_ref, kp_ref, vp_ref, o_ref, lse_ref, k_all, v_all = refs
        k_all[0:nq, :] = kp_ref[0, 0]
        k_all[nq:2 * nq, :] = kc_ref[0, 0]
        v_all[0:nq, :] = vp_ref[0, 0]
        v_all[nq:2 * nq, :] = vc_ref[0, 0]
        nk = 2 * nq
    else:
        q_ref, k_all, v_all, o_ref, lse_ref = refs
        k_all, v_all = k_all.at[0, 0], v_all.at[0, 0]
        nk = nq
    n = pl.program_id(2)
    qi = lax.broadcasted_iota(I32, (nq, nk), 0)
    ki = lax.broadcasted_iota(I32, (nq, nk), 1)
    dist = qi - ki + (nk - nq)
    ok = jnp.logical_and(dist >= 0, dist <= nq)
    if has_prev:
        ok = jnp.logical_and(ok, jnp.logical_or(ki >= nq, n > 0))
    dist_f = jnp.where(ok, dist.astype(F32), jnp.inf)
    lane = lax.broadcasted_iota(I32, (nq, LANES), 1)
    left_q = lane < A_DH
    left_k = lax.broadcasted_iota(I32, (nk, LANES), 1) < A_DH
    n_pairs = A_HEADS // 2

    scores = []
    for p in range(n_pairs):
        cols = slice(p * LANES, (p + 1) * LANES)
        qp = q_ref[0, 0, :, cols] * (A_DH ** -0.5)
        kp = k_all[:, cols]
        zero = jnp.zeros_like(qp)
        scores.append(_dot_nt(jnp.where(left_q, qp, zero), kp))
        scores.append(_dot_nt(jnp.where(left_q, zero, qp), kp))
    probs, maxes = [], []
    for h in range(A_HEADS):
        coef = -(2.0 ** (-8.0 * (h + 1) / A_HEADS)) * dilation
        s = scores[h] + dist_f * coef
        m = jnp.max(s, axis=1, keepdims=True)
        probs.append(jnp.exp(s - m).astype(BF16))
        maxes.append(m)
    lse_tile = jnp.zeros((nq, LANES), F32)
    for p in range(n_pairs):
        cols = slice(p * LANES, (p + 1) * LANES)
        vp = v_all[:, cols]
        one = jnp.ones_like(vp)
        pv_e = _dot(probs[2 * p], jnp.where(left_k, vp, one))
        pv_o = _dot(probs[2 * p + 1], jnp.where(left_k, one, vp))
        num = jnp.where(left_q, pv_e, pv_o)
        den = pltpu.roll(jnp.where(left_q, pv_o, pv_e), A_DH, 1)
        o_ref[0, 0, :, cols] = (num / den).astype(BF16)
        lse_tile = jnp.where(lane == 2 * p, maxes[2 * p] + jnp.log(pv_e[:, A_DH:A_DH + 1]), lse_tile)
        lse_tile = jnp.where(lane == 2 * p + 1, maxes[2 * p + 1] + jnp.log(pv_o[:, 0:1]), lse_tile)
    lse_ref[0, 0] = lse_tile


def _attn_group(qkv, dilation, col0):
    bsz, d, ls, _ = qkv.shape
    aw = A_HEADS * A_DH
    nq = ATTN_BLOCK
    nb = ls // nq
    has_prev = nb > 1
    blk = (1, 1, nq, aw)
    in_specs = [pl.BlockSpec(blk, lambda b, r, n: (b, r, n, col0)),
                pl.BlockSpec(blk, lambda b, r, n: (b, r, n, col0 + 1)),
                pl.BlockSpec(blk, lambda b, r, n: (b, r, n, col0 + 2))]
    args = [qkv, qkv, qkv]
    if has_prev:
        in_specs += [pl.BlockSpec(blk, lambda b, r, n: (b, r, jnp.maximum(n - 1, 0), col0 + 1)),
                     pl.BlockSpec(blk, lambda b, r, n: (b, r, jnp.maximum(n - 1, 0), col0 + 2))]
        args += [qkv, qkv]
    return pl.pallas_call(
        functools.partial(_attn_kernel, dilation, has_prev),
        grid=(bsz, d, nb),
        in_specs=in_specs,
        out_specs=[pl.BlockSpec(blk, lambda b, r, n: (b, r, n, 0)),
                   pl.BlockSpec((1, 1, nq, LANES), lambda b, r, n: (b, r, n, 0))],
        out_shape=[jax.ShapeDtypeStruct((bsz, d, ls, aw), BF16),
                   jax.ShapeDtypeStruct((bsz, d, ls, LANES), F32)],
        scratch_shapes=[pltpu.VMEM((2 * nq, aw), BF16)] * 2 if has_prev else [],
        compiler_params=_cparams(("arbitrary", "arbitrary", "arbitrary")),
        name=f"dilated_attn_d{dilation}",
    )(*args)


def _head_maps(n_heads, dh):
    w = n_heads * dh
    e = np.zeros((LANES, w), np.float32)
    for h in range(n_heads):
        e[h, h * dh:(h + 1) * dh] = 1.0
    return jnp.asarray(e, BF16), jnp.asarray(e.T.copy(), BF16)


def _natural_rows(ref, scr):
    d, n, w = ref.shape[1:]
    if d == 1:
        return ref[0, 0].astype(F32)
    cols = w // LANES
    for r in range(d):
        blk = ref[0, r].astype(F32)
        for c in range(cols):
            scr[c, pl.ds(r, n, stride=d), :] = blk[:, c * LANES:(c + 1) * LANES]
    return jnp.concatenate([scr[c] for c in range(cols)], axis=1)


def _merge_kernel(o1_ref, o2_ref, o3_ref, l1_ref, l2_ref, l3_ref, g_ref, e_ref, p_ref, y_ref, o_scr, l_scr):
    l1, l2, l3 = (_natural_rows(ref, l_scr.at[g]) for g, ref in enumerate((l1_ref, l2_ref, l3_ref)))
    mx = jnp.maximum(jnp.maximum(l1, l2), l3)
    w1, w2, w3 = jnp.exp(l1 - mx), jnp.exp(l2 - mx), jnp.exp(l3 - mx)
    inv = 1.0 / (w1 + w2 + w3)
    e = e_ref[...]
    o = (_dot_hilo(w1 * inv, e) * _natural_rows(o1_ref, o_scr.at[0])
         + _dot_hilo(w2 * inv, e) * _natural_rows(o2_ref, o_scr.at[1])
         + _dot_hilo(w3 * inv, e) * _natural_rows(o3_ref, o_scr.at[2]))
    ms = _dot_hilo(o * o, p_ref[...]) * (1.0 / A_DH)
    scale = _dot_hilo(lax.rsqrt(ms + EPS), e)
    y_ref[...] = (o * scale * g_ref[...]).astype(BF16)


def _attn_merge(outs, lses, norm_g):
    bsz, _, seq, aw = outs[0].shape
    t = bsz * seq
    tm = 512
    per_b = seq // tm
    expand, pool = _head_maps(A_HEADS, A_DH)
    const = lambda i: (0, 0)

    def grouped(arr):
        d, w = arr.shape[1], arr.shape[3]
        return pl.BlockSpec((1, d, tm // d, w), lambda i: (i // per_b, 0, i % per_b, 0))

    return pl.pallas_call(
        _merge_kernel,
        grid=(t // tm,),
        in_specs=[grouped(a) for a in outs] + [grouped(a) for a in lses]
        + [pl.BlockSpec((1, aw), const), pl.BlockSpec((LANES, aw), const), pl.BlockSpec((aw, LANES), const)],
        out_specs=pl.BlockSpec((tm, aw), lambda i: (i, 0)),
        out_shape=jax.ShapeDtypeStruct((t, aw), BF16),
        scratch_shapes=[pltpu.VMEM((len(outs), aw // LANES, tm, LANES), F32),
                        pltpu.VMEM((len(lses), 1, tm, LANES), F32)],
        compiler_params=_cparams(("arbitrary",)),
        name="attn_merge",
    )(*outs, *lses, norm_g, expand, pool)


def _outproj_kernel(alpha, ym_ref, ya_ref, w_ref, x_ref, g1_ref, sc_ref, sh_ref, lg_ref, lb_ref,
                    rw_ref, rb_ref, x1_ref, h2_ref, ti_ref, tg_ref):
    half = ym_ref.shape[1]
    y = _dot(ym_ref[...], w_ref[0:half, :]) + _dot(ya_ref[...], w_ref[half:2 * half, :])
    z = alpha * x_ref[...] + (1.0 + g1_ref[0]) * y
    x1 = _layer_norm(z, lg_ref[...], lb_ref[...])
    x1_ref[...] = x1
    h2 = x1 * (1.0 + sc_ref[0]) + sh_ref[0]
    h2_ref[...] = h2
    logits = _dot(h2.astype(BF16), rw_ref[...]) + rb_ref[...]
    lane = lax.broadcasted_iota(I32, logits.shape, 1)
    lane_f = lane.astype(F32)
    work = jnp.where(lane < N_EXPERTS, logits, -jnp.inf)
    idx_tile = jnp.zeros(logits.shape, F32)
    val_tile = jnp.zeros(logits.shape, F32)
    top = None
    denom = None
    for k in range(TOP_K):
        mk = jnp.max(work, axis=1, keepdims=True)
        ik = jnp.min(jnp.where(work == mk, lane_f, float(LANES)), axis=1, keepdims=True)
        work = jnp.where(lane_f == ik, -jnp.inf, work)
        if k == 0:
            top = mk
        ek = jnp.exp(mk - top)
        denom = ek if k == 0 else denom + ek
        idx_tile = jnp.where(lane == k, ik, idx_tile)
        val_tile = jnp.where(lane == k, ek, val_tile)
    ti_ref[...] = idx_tile.astype(I32)
    tg_ref[...] = val_tile / denom


def _out_proj(y_m, y_a, w_out, x2, g1, sc2, sh2, ln_g, ln_b, rw, rb, alpha, seq):
    t, d = x2.shape
    half = y_m.shape[1]
    tm = 256
    per_b = seq // tm
    row = lambda i: (i, 0)
    const = lambda i: (0, 0)
    mod = lambda i: (i // per_b, 0, 0)
    return pl.pallas_call(
        functools.partial(_outproj_kernel, alpha),
        grid=(t // tm,),
        in_specs=[pl.BlockSpec((tm, half), row), pl.BlockSpec((tm, half), row),
                  pl.BlockSpec((2 * half, d), const), pl.BlockSpec((tm, d), row),
                  pl.BlockSpec((1, 1, d), mod), pl.BlockSpec((1, 1, d), mod), pl.BlockSpec((1, 1, d), mod),
                  pl.BlockSpec((1, d), const), pl.BlockSpec((1, d), const),
                  pl.BlockSpec((d, LANES), const), pl.BlockSpec((1, LANES), const)],
        out_specs=[pl.BlockSpec((tm, d), row), pl.BlockSpec((tm, d), row),
                   pl.BlockSpec((tm, LANES), row), pl.BlockSpec((tm, LANES), row)],
        out_shape=[jax.ShapeDtypeStruct((t, d), F32), jax.ShapeDtypeStruct((t, d), F32),
                   jax.ShapeDtypeStruct((t, LANES), I32), jax.ShapeDtypeStruct((t, LANES), F32)],
        compiler_params=_cparams(("arbitrary",)),
        name="out_proj_ln_router",
    )(y_m, y_a, w_out, x2, g1, sc2, sh2, ln_g, ln_b, rw, rb)


def _expert_kernel(nj, ie_ref, ir_ref, ic_ref, iv_ref, tail_ref, tok_hbm, h_hbm, wgu_ref, bgu_ref, wdn_ref,
                   bdn_ref, sel_ref, y_hbm, tok_smem, ring, xb, yacc, wgu_b, wdn_b, gu_scr, gsem, sems):
    i = pl.program_id(0)
    j = pl.program_id(1)
    row0 = ir_ref[i]
    nch = ic_ref[i]
    ch = MOE_CHUNK
    lead = MOE_ITEM_ROWS
    ring_chunks = ring.shape[0] // ch
    per_body = ch // nj

    def chunk_slot(first_row):
        return lax.rem(lax.shift_right_logical(first_row, 8), ring_chunks)

    def load_tokens(first_row):
        rows = pl.ds(lax.shift_right_logical(first_row, 7), lead // LANES)
        cp = pltpu.make_async_copy(tok_hbm.at[rows, :], tok_smem, sems.at[0])
        cp.start()
        cp.wait()

    def gather_rows(first_row, first_idx, count):
        slot = chunk_slot(first_row)
        base = slot * ch + jnp.bitwise_and(first_row, ch - 1)
        for u in range(count):
            idx = first_idx + u
            t = tok_smem[lax.shift_right_logical(idx, 7), jnp.bitwise_and(idx, LANES - 1)]
            pltpu.make_async_copy(h_hbm.at[pl.ds(t, 1), :], ring.at[pl.ds(base + u, 1), :],
                                  gsem.at[slot]).start(priority=1)

    def chunk_wait(first_row):
        slot = chunk_slot(first_row)
        rows = pl.ds(pl.multiple_of(slot * ch, ch), ch)
        pltpu.make_async_copy(h_hbm.at[pl.ds(0, ch), :], ring.at[rows, :], gsem.at[slot]).wait()
        return rows

    @pl.when(jnp.logical_and(i == 0, j == 0))
    def _():
        load_tokens(0)

        def issue(g, carry):
            gather_rows(g * DMA_UNROLL, g * DMA_UNROLL, DMA_UNROLL)
            return carry

        lax.fori_loop(0, lead // DMA_UNROLL, issue, 0)

    def out_wait(count):
        def drain(m, carry):
            pltpu.make_async_copy(yacc.at[pl.ds(0, ch), :], y_hbm.at[pl.ds(0, ch), :], sems.at[1]).wait()
            return carry

        lax.fori_loop(0, count, drain, 0)

    @pl.when(jnp.logical_and(j == 0, nch > 0))
    def _():
        load_tokens(row0 + lead)

        def cast_rows(m, carry):
            src = chunk_wait(row0 + m * ch)
            xb[pl.ds(pl.multiple_of(m * ch, ch), ch), :] = ring[src, :].astype(BF16)
            return carry

        lax.fori_loop(0, nch, cast_rows, 0)

        @pl.when(i > 0)
        def _():
            out_wait(ic_ref[jnp.maximum(i - 1, 0)])

        def seed(m, carry):
            yacc[pl.ds(pl.multiple_of(m * ch, ch), ch), :] = jnp.broadcast_to(bdn_ref[0], (ch, yacc.shape[1]))
            return carry

        lax.fori_loop(0, nch, seed, 0)

    @pl.when(nch > 0)
    def _():
        wgu_b[...] = wgu_ref[0].astype(BF16)
        wdn_b[...] = wdn_ref[0].astype(BF16)
        bgu = bgu_ref[0]
        last = j == nj - 1

        def prefetch(m):
            first_idx = (j * nch + m) * per_body
            gather_rows(row0 + lead + first_idx, first_idx, per_body)

        def gate_up(m):
            rows = pl.ds(pl.multiple_of(m * ch, ch), ch)
            return _dot(xb[rows, :], wgu_b[...]) + bgu

        def finish(m, gu):
            rows = pl.ds(pl.multiple_of(m * ch, ch), ch)
            glu = jnp.minimum(gu, SWIGLU_LIMIT)
            f_glu = glu * _sigmoid(SWIGLU_ALPHA * glu)
            f_lin = jnp.clip(gu, -SWIGLU_LIMIT, SWIGLU_LIMIT) + 1.0
            prod = (pltpu.roll(f_glu, 1, 1) * f_lin).astype(BF16)
            parts = [_dot(prod[:, q * 2 * LANES:(q + 1) * 2 * LANES], sel_ref[...])
                     for q in range(MOE_TN // (2 * LANES))]
            act = jnp.concatenate(parts, axis=1).astype(BF16)
            yacc[rows, :] += _dot(act, wdn_b[...])

        def step(m):
            gu = gu_scr[...]
            gu_scr[...] = gate_up(m + 1)
            finish(m, gu)
            prefetch(m)

        gu_scr[...] = gate_up(0)
        n_steps = nch - 1

        def pair(p, carry):
            step(2 * p)
            step(2 * p + 1)
            return carry

        lax.fori_loop(0, lax.shift_right_logical(n_steps, 1), pair, 0)

        @pl.when(jnp.bitwise_and(n_steps, 1) == 1)
        def _():
            step(n_steps - 1)

        finish(nch - 1, gu_scr[...])
        prefetch(nch - 1)

        @pl.when(last)
        def _():
            def issue(m, carry):
                rows = pl.ds(pl.multiple_of(m * ch, ch), ch)
                dst = pl.ds(pl.multiple_of(row0 + m * ch, ch), ch)
                pltpu.make_async_copy(yacc.at[rows, :], y_hbm.at[dst, :], sems.at[1]).start()
                return carry

            lax.fori_loop(0, nch, issue, 0)

    @pl.when(jnp.logical_and(i == pl.num_programs(0) - 1, j == nj - 1))
    def _():
        out_wait(tail_ref[1])

        def drain(m, carry):
            chunk_wait(tail_ref[0] + m * ch)
            return carry

        lax.fori_loop(0, lead // ch, drain, 0)


def _experts(h2, tok_pad, item_e, item_row0, item_nch, item_valid, item_tail, w_gu, b_gu, w_dn, b_dn, n_rows):
    ne, d, two_de = w_gu.shape
    tn = MOE_TN
    nj = two_de // tn
    assert nj > 1, "the kernel separates its first and last column-tile steps"
    assert MOE_CHUNK % nj == 0 and (MOE_CHUNK // nj) % 8 == 0, "row copies per chunk stage"
    ni = item_e.shape[0]
    ring_rows = 2 * MOE_ITEM_ROWS
    sel = np.zeros((2 * LANES, LANES), np.float32)
    sel[2 * np.arange(LANES) + 1, np.arange(LANES)] = 1.0

    def jmap(i, j, iv):
        return jnp.where(iv[i] > 0, j, nj - 1)

    grid_spec = pltpu.PrefetchScalarGridSpec(
        num_scalar_prefetch=5,
        grid=(ni, nj),
        in_specs=[pl.BlockSpec(memory_space=pl.ANY),
                  pl.BlockSpec(memory_space=pl.ANY),
                  pl.BlockSpec((1, d, tn), lambda i, j, ie, ir, ic, iv, it: (ie[i], 0, jmap(i, j, iv))),
                  pl.BlockSpec((1, 1, tn), lambda i, j, ie, ir, ic, iv, it: (ie[i], 0, jmap(i, j, iv))),
                  pl.BlockSpec((1, tn // 2, d), lambda i, j, ie, ir, ic, iv, it: (ie[i], jmap(i, j, iv), 0)),
                  pl.BlockSpec((1, 1, d), lambda i, j, ie, ir, ic, iv, it: (ie[i], 0, 0)),
                  pl.BlockSpec((2 * LANES, LANES), lambda i, j, ie, ir, ic, iv, it: (0, 0))],
        out_specs=pl.BlockSpec(memory_space=pl.ANY),
        scratch_shapes=[pltpu.SMEM((MOE_ITEM_ROWS // LANES, LANES), I32),
                        pltpu.VMEM((ring_rows, d), F32),
                        pltpu.VMEM((MOE_ITEM_ROWS, d), BF16),
                        pltpu.VMEM((MOE_ITEM_ROWS, d), F32),
                        pltpu.VMEM((d, tn), BF16),
                        pltpu.VMEM((tn // 2, d), BF16),
                        pltpu.VMEM((MOE_CHUNK, tn), F32),
                        pltpu.SemaphoreType.DMA((ring_rows // MOE_CHUNK,)),
                        pltpu.SemaphoreType.DMA((2,))],
    )
    return pl.pallas_call(
        functools.partial(_expert_kernel, nj),
        grid_spec=grid_spec,
        out_shape=jax.ShapeDtypeStruct((n_rows, d), F32),
        compiler_params=_cparams(("arbitrary", "arbitrary")),
        name="moe_experts",
    )(item_e, item_row0, item_nch, item_valid, item_tail, tok_pad, h2, w_gu, b_gu.reshape(ne, 1, two_de),
      w_dn, b_dn.reshape(ne, 1, d), jnp.asarray(sel, BF16))


def _combine_kernel(alpha, dest_ref, y_hbm, gate_ref, x1_ref, g2_ref, lg_ref, lb_ref, out_ref, buf, sem):
    tc = COMBINE_TOKENS

    def row_copy(s):
        k = jnp.bitwise_and(s, TOP_K - 1)
        r = lax.shift_right_logical(s, 2)
        src = dest_ref[0, 0, s]
        return pltpu.make_async_copy(y_hbm.at[pl.ds(src, 1), :], buf.at[k, pl.ds(r, 1), :], sem.at[0])

    def issue(g, carry):
        for u in range(DMA_UNROLL):
            row_copy(g * DMA_UNROLL + u).start(priority=u % 2)
        return carry

    lax.fori_loop(0, tc * TOP_K // DMA_UNROLL, issue, 0)
    for k in range(TOP_K):
        pltpu.make_async_copy(y_hbm.at[pl.ds(0, tc), :], buf.at[k], sem.at[0]).wait()

    gates = gate_ref[...]
    y = gates[:, 0:1] * buf[0]
    for k in range(1, TOP_K):
        y = y + gates[:, k:k + 1] * buf[k]
    z = alpha * x1_ref[...] + (1.0 + g2_ref[0]) * y
    out_ref[...] = _layer_norm(z, lg_ref[...], lb_ref[...])


def _combine(dest, y_pad, gates, x1, g2, ln_g, ln_b, alpha, seq):
    t, d = x1.shape
    tc = COMBINE_TOKENS
    per_b = seq // tc
    row = lambda i: (i, 0)
    const = lambda i: (0, 0)
    return pl.pallas_call(
        functools.partial(_combine_kernel, alpha),
        grid=(t // tc,),
        in_specs=[pl.BlockSpec((1, 1, tc * TOP_K), lambda i: (i, 0, 0), memory_space=pltpu.SMEM),
                  pl.BlockSpec(memory_space=pl.ANY),
                  pl.BlockSpec((tc, LANES), row), pl.BlockSpec((tc, d), row),
                  pl.BlockSpec((1, 1, d), lambda i: (i // per_b, 0, 0)),
                  pl.BlockSpec((1, d), const), pl.BlockSpec((1, d), const)],
        out_specs=pl.BlockSpec((tc, d), row),
        out_shape=jax.ShapeDtypeStruct((t, d), F32),
        scratch_shapes=[pltpu.VMEM((TOP_K, tc, d), F32), pltpu.SemaphoreType.DMA((1,))],
        compiler_params=_cparams(("arbitrary",)),
        name="moe_combine_ln",
    )(dest.reshape(t // tc, 1, tc * TOP_K), y_pad, gates, x1, g2, ln_g, ln_b)


def _count_le(ends, q):
    return jnp.sum((ends[None, :] <= q[:, None]).astype(I32), axis=1)


def _lookup(table, idx):
    hit = idx[:, None] == jnp.arange(table.shape[0], dtype=I32)
    return jnp.sum(jnp.where(hit, table[None, :], 0), axis=1)


def _routing_tables(top_idx):
    t = top_idx.shape[0]
    tk = t * TOP_K
    experts = jnp.arange(N_EXPERTS, dtype=I32)
    hits = [top_idx[:, k:k + 1] == experts for k in range(TOP_K)]
    onehot = sum(h.astype(I32) for h in hits)
    csum = jnp.cumsum(onehot, axis=0)
    counts = csum[-1]
    earlier = csum - onehot
    starts = jnp.cumsum(counts) - counts
    padded = ((counts + MOE_ROW_PAD - 1) // MOE_ROW_PAD) * MOE_ROW_PAD
    pad_end = jnp.cumsum(padded)
    pad_start = pad_end - padded
    dest = jnp.stack([jnp.sum(jnp.where(h, earlier + pad_start, 0), axis=1) for h in hits], axis=1).reshape(tk)

    n_rows = tk + N_EXPERTS * MOE_ROW_PAD
    n_tab = n_rows + 2 * MOE_ITEM_ROWS
    tok_sorted = (jnp.argsort(top_idx.reshape(tk)) // TOP_K).astype(I32)
    blk = jnp.arange(n_tab // MOE_ROW_PAD, dtype=I32) * MOE_ROW_PAD
    blk_e = jnp.minimum(_count_le(pad_end, blk), N_EXPERTS - 1)
    local = (blk - _lookup(pad_start, blk_e))[:, None] + jnp.arange(MOE_ROW_PAD, dtype=I32)
    src = jnp.clip(_lookup(starts, blk_e)[:, None] + local, 0, tk - 1)
    live = jnp.logical_and(local < _lookup(counts, blk_e)[:, None], (blk < pad_end[-1])[:, None])
    tok_pad = jnp.where(live, tok_sorted[src], 0).reshape(-1, LANES)

    items_per = (padded + MOE_ITEM_ROWS - 1) // MOE_ITEM_ROWS
    item_end = jnp.cumsum(items_per)
    item_start = item_end - items_per
    n_items = N_EXPERTS + n_rows // MOE_ITEM_ROWS
    idx = jnp.arange(n_items, dtype=I32)
    valid = idx < item_end[-1]
    e_i = jnp.minimum(_count_le(item_end, idx), N_EXPERTS - 1)
    k_i = idx - _lookup(item_start, e_i)
    row0 = _lookup(pad_start, e_i) + k_i * MOE_ITEM_ROWS
    nrows = jnp.clip(_lookup(padded, e_i) - k_i * MOE_ITEM_ROWS, 0, MOE_ITEM_ROWS)
    e_last = jnp.sum(jnp.where(idx == item_end[-1] - 1, e_i, 0))
    item_e = jnp.where(valid, e_i, e_last).astype(I32)
    item_row0 = jnp.where(valid, row0, 0).astype(I32)
    item_nch = jnp.where(valid, nrows // MOE_CHUNK, 0).astype(I32)
    item_tail = jnp.stack([pad_end[-1], jnp.sum(jnp.where(idx == item_end[-1] - 1, item_nch, 0))]).astype(I32)
    return tok_pad, dest, item_e, item_row0, item_nch, valid.astype(I32), item_tail, n_rows


def kernel(x, c, w_ada, b_ada, w_in, b_in, conv_w, conv_b, m_norm_g, a_norm_g, w_out, ln1_g, ln1_b,
           router_w, router_b, w_gu, b_gu, w_dn, b_dn, ln2_g, ln2_b):
    bsz, seq, d = x.shape
    depth = w_ada.shape[0]
    t = bsz * seq
    alpha = float((2 * depth) ** 0.25)
    qk_w = 2 * M_HEADS * M_DQK
    mv_w = M_HEADS * M_DV
    aw = A_HEADS * A_DH
    gate_lo = qk_w + 2 * mv_w
    gate_hi = gate_lo + 2 * M_HEADS

    x2 = x.reshape(t, d)
    for l in range(depth):
        mod = _ada_mod(c, w_ada[l], b_ada[l]).reshape(bsz, 6, 1, d)
        sh1, sc1, g1, sh2, sc2, g2 = (mod[:, i] for i in range(6))

        w_main = jnp.concatenate([w_in[l][:, :gate_lo], w_in[l][:, gate_hi:]], axis=1).astype(BF16)
        b_main = jnp.concatenate([b_in[l][:gate_lo], b_in[l][gate_hi:]]).reshape(1, -1)
        w_gate = jnp.zeros((d, LANES), BF16).at[:, :2 * M_HEADS].set(w_in[l][:, gate_lo:gate_hi].astype(BF16))
        b_gate = jnp.zeros((1, LANES), F32).at[0, :2 * M_HEADS].set(b_in[l][gate_lo:gate_hi])
        attn_col0 = gate_lo // aw
        proj, gates_c, *qkv_dil = _in_proj(x2, sc1, sh1, w_main, b_main, w_gate, b_gate, seq, attn_col0)

        gates_r = gates_c[:, :2 * M_HEADS].reshape(bsz, seq, 2 * M_HEADS).transpose(0, 2, 1)
        y_m = _mlstm(proj, gates_c, gates_r, conv_w[l], conv_b[l].reshape(1, -1),
                     m_norm_g[l].reshape(1, -1), bsz, seq)

        outs, lses = [], []
        for dil in DILATIONS:
            if dil == 1:
                o_d, lse_d = _attn_group(proj.reshape(bsz, 1, seq, -1), dil, attn_col0)
            else:
                o_d, lse_d = _attn_group(qkv_dil[DILATIONS.index(dil) - 1], dil, 0)
            outs.append(o_d)
            lses.append(lse_d)
        y_a = _attn_merge(outs, lses, a_norm_g[l].reshape(1, -1))

        rw = jnp.zeros((d, LANES), BF16).at[:, :N_EXPERTS].set(router_w[l].astype(BF16))
        rb = jnp.zeros((1, LANES), F32).at[0, :N_EXPERTS].set(router_b[l])
        x1, h2, top_idx, gates = _out_proj(y_m, y_a, w_out[l].astype(BF16), x2, g1, sc2, sh2,
                                           ln1_g[l].reshape(1, -1), ln1_b[l].reshape(1, -1), rw, rb, alpha, seq)

        (tok_pad, dest, item_e, item_row0, item_nch, item_valid, item_tail,
         n_rows) = _routing_tables(top_idx[:, :TOP_K])
        y_pad = _experts(h2, tok_pad, item_e, item_row0, item_nch, item_valid, item_tail,
                         w_gu[l], b_gu[l], w_dn[l], b_dn[l], n_rows)
        x2 = _combine(dest, y_pad, gates, x1, g2, ln2_g[l].reshape(1, -1), ln2_b[l].reshape(1, -1), alpha, seq)
    return x2.reshape(bsz, seq, d)
```

```python
import functools

import jax
import jax.numpy as jnp
import numpy as np
from jax import lax
from jax.experimental import pallas as pl
from jax.experimental.pallas import tpu as pltpu

F32 = jnp.float32
BF16 = jnp.bfloat16
I32 = jnp.int32

M_HEADS = 4
M_DQK = 128
M_DV = 256
CONV_WIDTH = 4
A_HEADS = 16
A_DH = 64
ATTN_BLOCK = 128
DILATIONS = (1, 4, 16)
N_EXPERTS = 32
TOP_K = 4
SWIGLU_ALPHA = 1.702
SWIGLU_LIMIT = 7.0
EPS = 1e-5

LANES = 128
VMEM_LIMIT = 56 * 1024 * 1024

MLSTM_CHUNK = 256
MOE_ROW_PAD = 256
MOE_CHUNK = 256
MOE_ITEM_ROWS = 1280
MOE_TN = 512
COMBINE_TOKENS = 128
DMA_UNROLL = 8


def _cparams(sem, vmem=VMEM_LIMIT):
    return pltpu.CompilerParams(dimension_semantics=sem, vmem_limit_bytes=vmem)


def _sigmoid(x):
    return 1.0 / (1.0 + jnp.exp(-x))


def _log_sigmoid(x):
    return jnp.minimum(x, 0.0) - jnp.log(1.0 + jnp.exp(-jnp.abs(x)))


def _layer_norm(z, g, b):
    mu = jnp.mean(z, axis=-1, keepdims=True)
    zc = z - mu
    var = jnp.mean(zc * zc, axis=-1, keepdims=True)
    return zc * lax.rsqrt(var + EPS) * g + b


def _dot(a, b):
    return jnp.dot(a, b, preferred_element_type=F32)


def _dot_nt(a, b):
    return lax.dot_general(a, b, (((1,), (1,)), ((), ())), preferred_element_type=F32)


def _dot_hilo(a, sel):
    hi = a.astype(BF16)
    lo = (a - hi.astype(F32)).astype(BF16)
    return _dot(hi, sel) + _dot(lo, sel)


def _ada_kernel(c_ref, w_ref, b_ref, o_ref):
    c = c_ref[...]
    cond = c * _sigmoid(c)
    o_ref[...] = _dot(cond.astype(BF16), w_ref[...].astype(BF16)) + b_ref[...]


def _ada_mod(c, w_ada, b_ada):
    bsz, d = c.shape
    n = w_ada.shape[1]
    tn = 1024
    rows = 8
    c_pad = jnp.zeros((rows, d), F32).at[:bsz].set(c)
    out = pl.pallas_call(
        _ada_kernel,
        grid=(n // tn,),
        in_specs=[pl.BlockSpec((rows, d), lambda j: (0, 0)),
                  pl.BlockSpec((d, tn), lambda j: (0, j)),
                  pl.BlockSpec((1, tn), lambda j: (0, j))],
        out_specs=pl.BlockSpec((rows, tn), lambda j: (0, j)),
        out_shape=jax.ShapeDtypeStruct((rows, n), F32),
        compiler_params=_cparams(("arbitrary",)),
        name="ada_mod",
    )(c_pad, w_ada, b_ada.reshape(1, n))
    return out[:bsz]


def _inproj_kernel(attn_col0, x_ref, sc_ref, sh_ref, w_ref, b_ref, wg_ref, bg_ref, o_ref, g_ref, *rest):
    dil_refs, (h_ref, r_scr) = rest[:-2], rest[-2:]
    j = pl.program_id(1)

    @pl.when(j == 0)
    def _():
        h = x_ref[...] * (1.0 + sc_ref[0]) + sh_ref[0]
        hb = h.astype(BF16)
        h_ref[...] = hb
        g_ref[...] = _dot(hb, wg_ref[...]) + bg_ref[...]

    res = _dot(h_ref[...], w_ref[...]) + b_ref[...]
    o_ref[...] = res.astype(BF16)

    @pl.when(j >= attn_col0)
    def _():
        cols = res.shape[1] // LANES
        for c in range(cols):
            r_scr[c] = res[:, c * LANES:(c + 1) * LANES]
        for ref in dil_refs:
            d, n = ref.shape[1], ref.shape[2]
            for r in range(d):
                for c in range(cols):
                    ref[0, r, :, c * LANES:(c + 1) * LANES] = r_scr[c, pl.ds(r, n, stride=d), :].astype(BF16)


def _in_proj(x2, sc, sh, w_main, b_main, w_gate, b_gate, seq, attn_col0):
    t, d = x2.shape
    n = w_main.shape[1]
    tm, tn = 512, 1024
    per_b = seq // tm
    bsz = t // seq
    dils = [dl for dl in DILATIONS if dl > 1]
    aw3 = n - attn_col0 * tn
    dil_specs = [pl.BlockSpec((1, dl, tm // dl, tn),
                              lambda i, j: (i // per_b, 0, i % per_b, jnp.maximum(j - attn_col0, 0))) for dl in dils]
    dil_shapes = [jax.ShapeDtypeStruct((bsz, dl, seq // dl, aw3), BF16) for dl in dils]
    return pl.pallas_call(
        functools.partial(_inproj_kernel, attn_col0),
        grid=(t // tm, n // tn),
        in_specs=[pl.BlockSpec((tm, d), lambda i, j: (i, 0)),
                  pl.BlockSpec((1, 1, d), lambda i, j: (i // per_b, 0, 0)),
                  pl.BlockSpec((1, 1, d), lambda i, j: (i // per_b, 0, 0)),
                  pl.BlockSpec((d, tn), lambda i, j: (0, j)),
                  pl.BlockSpec((1, tn), lambda i, j: (0, j)),
                  pl.BlockSpec((d, LANES), lambda i, j: (0, 0)),
                  pl.BlockSpec((1, LANES), lambda i, j: (0, 0))],
        out_specs=[pl.BlockSpec((tm, tn), lambda i, j: (i, j)),
                   pl.BlockSpec((tm, LANES), lambda i, j: (i, 0))] + dil_specs,
        out_shape=[jax.ShapeDtypeStruct((t, n), BF16),
                   jax.ShapeDtypeStruct((t, LANES), F32)] + dil_shapes,
        scratch_shapes=[pltpu.VMEM((tm, d), BF16), pltpu.VMEM((tn // LANES, tm, LANES), F32)],
        compiler_params=_cparams(("arbitrary", "arbitrary")),
        name="in_proj",
    )(x2, sc, sh, w_main, b_main, w_gate, b_gate)


def _mlstm_kernel(qk_ref, v_ref, o_ref, gc_ref, gr_ref, cw_ref, cb_ref, ng_ref, out_ref,
                  ct_ref, n_ref, m_ref, prev_ref):
    c = pl.program_id(1)
    L = MLSTM_CHUNK

    @pl.when(c == 0)
    def _():
        ct_ref[...] = jnp.zeros_like(ct_ref)
        n_ref[...] = jnp.zeros_like(n_ref)
        m_ref[...] = jnp.zeros_like(m_ref)
        prev_ref[...] = jnp.zeros_like(prev_ref)

    x = qk_ref[...].astype(F32)
    prev = prev_ref[...]
    row = lax.broadcasted_iota(I32, (L, 1), 0)
    y = cw_ref[CONV_WIDTH - 1:CONV_WIDTH, :] * x + cb_ref[...]
    for k in range(1, CONV_WIDTH):
        xs = jnp.where(row < k, pltpu.roll(prev, k, 0), pltpu.roll(x, k, 0))
        y = y + cw_ref[CONV_WIDTH - 1 - k:CONV_WIDTH - k, :] * xs
    prev_ref[...] = x
    y = y * _sigmoid(y)

    gc = gc_ref[...]
    gr = gr_ref[0]
    ti = lax.broadcasted_iota(I32, (L, L), 0)
    si = lax.broadcasted_iota(I32, (L, L), 1)
    causal = si <= ti
    tril = causal.astype(F32)
    triu = (ti <= si).astype(F32)
    b_cols = jnp.dot(tril, _log_sigmoid(gc), precision=lax.Precision.HIGHEST, preferred_element_type=F32)
    b_rows = jnp.dot(_log_sigmoid(gr), triu, precision=lax.Precision.HIGHEST, preferred_element_type=F32)

    qk_w = M_HEADS * M_DQK
    heads = range(M_HEADS)
    qf = [y[:, h * M_DQK:(h + 1) * M_DQK] for h in heads]
    kf = [y[:, qk_w + h * M_DQK:qk_w + (h + 1) * M_DQK] * (M_DQK ** -0.5) for h in heads]
    qb = [t.astype(BF16) for t in qf]
    vb = [v_ref[:, h * M_DV:(h + 1) * M_DV] for h in heads]
    bc = [b_cols[:, M_HEADS + h:M_HEADS + h + 1] for h in heads]
    ic = [gc[:, h:h + 1] for h in heads]
    br = [b_rows[M_HEADS + h:M_HEADS + h + 1, :] for h in heads]
    ir = [gr[h:h + 1, :] for h in heads]
    m_prev = [m_ref[h][0:1, 0:1] for h in heads]
    n_prev = [n_ref[h][0:1, :] for h in heads]

    b_last = [bc[h][L - 1:L, :] for h in heads]
    g_col = [b_last[h] - bc[h] + ic[h] for h in heads]
    m_new = [jnp.maximum(b_last[h] + m_prev[h], jnp.max(g_col[h], axis=0, keepdims=True)) for h in heads]
    w_col = [jnp.exp(g_col[h] - m_new[h]) for h in heads]
    decay = [jnp.exp(b_last[h] + m_prev[h] - m_new[h]) for h in heads]

    qk = [_dot_nt(qb[h], kf[h].astype(BF16)) for h in heads]
    q_state = [_dot(qb[h], ct_ref[h].astype(BF16)) for h in heads]
    kv_new = [_dot(kf[h].T.astype(BF16), (w_col[h] * vb[h].astype(F32)).astype(BF16)) for h in heads]

    dm = [jnp.where(causal, bc[h] - br[h] + ir[h], -jnp.inf) for h in heads]
    inter = [bc[h] + m_prev[h] for h in heads]
    mt = [jnp.maximum(inter[h], jnp.max(dm[h], axis=1, keepdims=True)) for h in heads]
    a = [jnp.exp(dm[h] - mt[h]) * qk[h] for h in heads]
    e_int = [jnp.exp(inter[h] - mt[h]) for h in heads]
    av = [_dot(a[h].astype(BF16), vb[h]) for h in heads]

    for h in heads:
        num = av[h] + e_int[h] * q_state[h]
        den = (jnp.sum(a[h], axis=1, keepdims=True)
               + e_int[h] * jnp.sum(qf[h] * n_prev[h], axis=1, keepdims=True))
        hh = num / jnp.maximum(jnp.abs(den), jnp.exp(-mt[h]))

        ct_ref[h] = decay[h] * ct_ref[h] + kv_new[h]
        n_new = decay[h] * n_prev[h] + jnp.sum(w_col[h] * kf[h], axis=0, keepdims=True)
        n_ref[h] = jnp.broadcast_to(n_new, n_ref.shape[1:])
        m_ref[h] = jnp.broadcast_to(m_new[h], m_ref.shape[1:])

        ms = jnp.mean(hh * hh, axis=1, keepdims=True)
        og = o_ref[:, h * M_DV:(h + 1) * M_DV].astype(F32)
        yh = hh * lax.rsqrt(ms + EPS) * ng_ref[:, h * M_DV:(h + 1) * M_DV] * _sigmoid(og)
        out_ref[:, h * M_DV:(h + 1) * M_DV] = yh.astype(BF16)


def _mlstm(proj, gates_c, gates_r, conv_w, conv_b, norm_g, bsz, seq):
    L = MLSTM_CHUNK
    nc = seq // L
    t = bsz * seq
    w = M_HEADS * M_DV
    return pl.pallas_call(
        _mlstm_kernel,
        grid=(bsz, nc),
        in_specs=[pl.BlockSpec((L, w), lambda b, c: (b * nc + c, 0)),
                  pl.BlockSpec((L, w), lambda b, c: (b * nc + c, 1)),
                  pl.BlockSpec((L, w), lambda b, c: (b * nc + c, 2)),
                  pl.BlockSpec((L, LANES), lambda b, c: (b * nc + c, 0)),
                  pl.BlockSpec((1, 8, L), lambda b, c: (b, 0, c)),
                  pl.BlockSpec((CONV_WIDTH, w), lambda b, c: (0, 0)),
                  pl.BlockSpec((1, w), lambda b, c: (0, 0)),
                  pl.BlockSpec((1, w), lambda b, c: (0, 0))],
        out_specs=pl.BlockSpec((L, w), lambda b, c: (b * nc + c, 0)),
        out_shape=jax.ShapeDtypeStruct((t, w), BF16),
        scratch_shapes=[pltpu.VMEM((M_HEADS, M_DQK, M_DV), F32),
                        pltpu.VMEM((M_HEADS, 8, M_DQK), F32),
                        pltpu.VMEM((M_HEADS, 8, LANES), F32),
                        pltpu.VMEM((L, w), F32)],
        compiler_params=_cparams(("arbitrary", "arbitrary")),
        name="mlstm",
    )(proj, proj, proj, gates_c, gates_r, conv_w, conv_b, norm_g)


def _attn_kernel(dilation, has_prev, *refs):
    nq = ATTN_BLOCK
    if has_prev:
        q_ref, kc_ref, vc_ref, kp_ref, vp_ref, o_ref, lse_ref, k_all, v_all = refs
        k_all[0:nq, :] = kp_ref[0, 0]
        k_all[nq:2 * nq, :] = kc_ref[0, 0]
        v_all[0:nq, :] = vp_ref[0, 0]
        v_all[nq:2 * nq, :] = vc_ref[0, 0]
        nk = 2 * nq
    else:
        q_ref, k_all, v_all, o_ref, lse_ref = refs
        k_all, v_all = k_all.at[0, 0], v_all.at[0, 0]
        nk = nq
    n = pl.program_id(2)
    qi = lax.broadcasted_iota(I32, (nq, nk), 0)
    ki = lax.broadcasted_iota(I32, (nq, nk), 1)
    dist = qi - ki + (nk - nq)
    ok = jnp.logical_and(dist >= 0, dist <= nq)
    if has_prev:
        ok = jnp.logical_and(ok, jnp.logical_or(ki >= nq, n > 0))
    dist_f = jnp.where(ok, dist.astype(F32), jnp.inf)
    lane = lax.broadcasted_iota(I32, (nq, LANES), 1)
    left_q = lane < A_DH
    left_k = lax.broadcasted_iota(I32, (nk, LANES), 1) < A_DH
    n_pairs = A_HEADS // 2

    scores = []
    for p in range(n_pairs):
        cols = slice(p * LANES, (p + 1) * LANES)
        qp = q_ref[0, 0, :, cols] * (A_DH ** -0.5)
        kp = k_all[:, cols]
        zero = jnp.zeros_like(qp)
        scores.append(_dot_nt(jnp.where(left_q, qp, zero), kp))
        scores.append(_dot_nt(jnp.where(left_q, zero, qp), kp))
    probs, maxes = [], []
    for h in range(A_HEADS):
        coef = -(2.0 ** (-8.0 * (h + 1) / A_HEADS)) * dilation
        s = scores[h] + dist_f * coef
        m = jnp.max(s, axis=1, keepdims=True)
        probs.append(jnp.exp(s - m).astype(BF16))
        maxes.append(m)
    lse_tile = jnp.zeros((nq, LANES), F32)
    for p in range(n_pairs):
        cols = slice(p * LANES, (p + 1) * LANES)
        vp = v_all[:, cols]
        one = jnp.ones_like(vp)
        pv_e = _dot(probs[2 * p], jnp.where(left_k, vp, one))
        pv_o = _dot(probs[2 * p + 1], jnp.where(left_k, one, vp))
        num = jnp.where(left_q, pv_e, pv_o)
        den = pltpu.roll(jnp.where(left_q, pv_o, pv_e), A_DH, 1)
        o_ref[0, 0, :, cols] = (num / den).astype(BF16)
        lse_tile = jnp.where(lane == 2 * p, maxes[2 * p] + jnp.log(pv_e[:, A_DH:A_DH + 1]), lse_tile)
        lse_tile = jnp.where(lane == 2 * p + 1, maxes[2 * p + 1] + jnp.log(pv_o[:, 0:1]), lse_tile)
    lse_ref[0, 0] = lse_tile


def _attn_group(qkv, dilation, col0):
    bsz, d, ls, _ = qkv.shape
    aw = A_HEADS * A_DH
    nq = ATTN_BLOCK
    nb = ls // nq
    has_prev = nb > 1
    blk = (1, 1, nq, aw)
    in_specs = [pl.BlockSpec(blk, lambda b, r, n: (b, r, n, col0)),
                pl.BlockSpec(blk, lambda b, r, n: (b, r, n, col0 + 1)),
                pl.BlockSpec(blk, lambda b, r, n: (b, r, n, col0 + 2))]
    args = [qkv, qkv, qkv]
    if has_prev:
        in_specs += [pl.BlockSpec(blk, lambda b, r, n: (b, r, jnp.maximum(n - 1, 0), col0 + 1)),
                     pl.BlockSpec(blk, lambda b, r, n: (b, r, jnp.maximum(n - 1, 0), col0 + 2))]
        args += [qkv, qkv]
    return pl.pallas_call(
        functools.partial(_attn_kernel, dilation, has_prev),
        grid=(bsz, d, nb),
        in_specs=in_specs,
        out_specs=[pl.BlockSpec(blk, lambda b, r, n: (b, r, n, 0)),
                   pl.BlockSpec((1, 1, nq, LANES), lambda b, r, n: (b, r, n, 0))],
        out_shape=[jax.ShapeDtypeStruct((bsz, d, ls, aw), BF16),
                   jax.ShapeDtypeStruct((bsz, d, ls, LANES), F32)],
        scratch_shapes=[pltpu.VMEM((2 * nq, aw), BF16)] * 2 if has_prev else [],
        compiler_params=_cparams(("arbitrary", "arbitrary", "arbitrary")),
        name=f"dilated_attn_d{dilation}",
    )(*args)


def _head_maps(n_heads, dh):
    w = n_heads * dh
    e = np.zeros((LANES, w), np.float32)
    for h in range(n_heads):
        e[h, h * dh:(h + 1) * dh] = 1.0
    return jnp.asarray(e, BF16), jnp.asarray(e.T.copy(), BF16)


def _natural_rows(ref, scr):
    d, n, w = ref.shape[1:]
    if d == 1:
        return ref[0, 0].astype(F32)
    cols = w // LANES
    for r in range(d):
        blk = ref[0, r].astype(F32)
        for c in range(cols):
            scr[c, pl.ds(r, n, stride=d), :] = blk[:, c * LANES:(c + 1) * LANES]
    return jnp.concatenate([scr[c] for c in range(cols)], axis=1)


def _merged_heads(o_refs, l_refs, g_ref, e_ref, p_ref, o_scr, l_scr):
    l1, l2, l3 = (_natural_rows(ref, l_scr.at[g]) for g, ref in enumerate(l_refs))
    mx = jnp.maximum(jnp.maximum(l1, l2), l3)
    w1, w2, w3 = jnp.exp(l1 - mx), jnp.exp(l2 - mx), jnp.exp(l3 - mx)
    inv = 1.0 / (w1 + w2 + w3)
    e = e_ref[...]
    o = (_dot_hilo(w1 * inv, e) * _natural_rows(o_refs[0], o_scr.at[0])
         + _dot_hilo(w2 * inv, e) * _natural_rows(o_refs[1], o_scr.at[1])
         + _dot_hilo(w3 * inv, e) * _natural_rows(o_refs[2], o_scr.at[2]))
    ms = _dot_hilo(o * o, p_ref[...]) * (1.0 / A_DH)
    scale = _dot_hilo(lax.rsqrt(ms + EPS), e)
    return (o * scale * g_ref[...]).astype(BF16)


def _outproj_kernel(alpha, ym_ref, o1_ref, o2_ref, o3_ref, l1_ref, l2_ref, l3_ref, ag_ref, e_ref, p_ref,
                    w_ref, x_ref, g1_ref, sc_ref, sh_ref, lg_ref, lb_ref, rw_ref, rb_ref,
                    x1_ref, h2_ref, ti_ref, tg_ref, o_scr, l_scr):
    half = ym_ref.shape[1]
    y_a = _merged_heads((o1_ref, o2_ref, o3_ref), (l1_ref, l2_ref, l3_ref), ag_ref, e_ref, p_ref, o_scr, l_scr)
    y = _dot(ym_ref[...], w_ref[0:half, :]) + _dot(y_a, w_ref[half:2 * half, :])
    z = alpha * x_ref[...] + (1.0 + g1_ref[0]) * y
    x1 = _layer_norm(z, lg_ref[...], lb_ref[...])
    x1_ref[...] = x1
    h2 = x1 * (1.0 + sc_ref[0]) + sh_ref[0]
    h2_ref[...] = h2
    logits = _dot(h2.astype(BF16), rw_ref[...]) + rb_ref[...]
    lane = lax.broadcasted_iota(I32, logits.shape, 1)
    lane_f = lane.astype(F32)
    work = jnp.where(lane < N_EXPERTS, logits, -jnp.inf)
    idx_tile = jnp.zeros(logits.shape, F32)
    val_tile = jnp.zeros(logits.shape, F32)
    top = None
    denom = None
    for k in range(TOP_K):
        mk = jnp.max(work, axis=1, keepdims=True)
        ik = jnp.min(jnp.where(work == mk, lane_f, float(LANES)), axis=1, keepdims=True)
        work = jnp.where(lane_f == ik, -jnp.inf, work)
        if k == 0:
            top = mk
        ek = jnp.exp(mk - top)
        denom = ek if k == 0 else denom + ek
        idx_tile = jnp.where(lane == k, ik, idx_tile)
        val_tile = jnp.where(lane == k, ek, val_tile)
    ti_ref[...] = idx_tile.astype(I32)
    tg_ref[...] = val_tile / denom


def _out_proj(y_m, outs, lses, norm_g, w_out, x2, g1, sc2, sh2, ln_g, ln_b, rw, rb, alpha, seq):
    t, d = x2.shape
    half = y_m.shape[1]
    aw = outs[0].shape[3]
    tm = 256
    per_b = seq // tm
    row = lambda i: (i, 0)
    const = lambda i: (0, 0)
    mod = lambda i: (i // per_b, 0, 0)
    expand, pool = _head_maps(A_HEADS, A_DH)

    def grouped(arr):
        dl, w = arr.shape[1], arr.shape[3]
        return pl.BlockSpec((1, dl, tm // dl, w), lambda i: (i // per_b, 0, i % per_b, 0))

    return pl.pallas_call(
        functools.partial(_outproj_kernel, alpha),
        grid=(t // tm,),
        in_specs=[pl.BlockSpec((tm, half), row)] + [grouped(a) for a in outs] + [grouped(a) for a in lses]
        + [pl.BlockSpec((1, aw), const), pl.BlockSpec((LANES, aw), const), pl.BlockSpec((aw, LANES), const),
                  pl.BlockSpec((2 * half, d), const), pl.BlockSpec((tm, d), row),
                  pl.BlockSpec((1, 1, d), mod), pl.BlockSpec((1, 1, d), mod), pl.BlockSpec((1, 1, d), mod),
                  pl.BlockSpec((1, d), const), pl.BlockSpec((1, d), const),
                  pl.BlockSpec((d, LANES), const), pl.BlockSpec((1, LANES), const)],
        out_specs=[pl.BlockSpec((tm, d), row), pl.BlockSpec((tm, d), row),
                   pl.BlockSpec((tm, LANES), row), pl.BlockSpec((tm, LANES), row)],
        out_shape=[jax.ShapeDtypeStruct((t, d), F32), jax.ShapeDtypeStruct((t, d), F32),
                   jax.ShapeDtypeStruct((t, LANES), I32), jax.ShapeDtypeStruct((t, LANES), F32)],
        scratch_shapes=[pltpu.VMEM((len(outs), aw // LANES, tm, LANES), F32),
                        pltpu.VMEM((len(lses), 1, tm, LANES), F32)],
        compiler_params=_cparams(("arbitrary",)),
        name="out_proj_ln_router",
    )(y_m, *outs, *lses, norm_g, expand, pool, w_out, x2, g1, sc2, sh2, ln_g, ln_b, rw, rb)


def _expert_kernel(nj, ie_ref, ir_ref, ic_ref, iv_ref, tail_ref, tok_hbm, h_hbm, wgu_ref, bgu_ref, wdn_ref,
                   bdn_ref, sel_ref, y_hbm, tok_smem, ring, xb, yacc, wgu_b, wdn_b, gu_scr, gsem, sems):
    i = pl.program_id(0)
    j = pl.program_id(1)
    row0 = ir_ref[i]
    nch = ic_ref[i]
    ch = MOE_CHUNK
    lead = MOE_ITEM_ROWS
    ring_chunks = ring.shape[0] // ch
    per_body = ch // nj

    def chunk_slot(first_row):
        return lax.rem(lax.shift_right_logical(first_row, 8), ring_chunks)

    def load_tokens(first_row):
        rows = pl.ds(lax.shift_right_logical(first_row, 7), lead // LANES)
        cp = pltpu.make_async_copy(tok_hbm.at[rows, :], tok_smem, sems.at[0])
        cp.start()
        cp.wait()

    def gather_rows(first_row, first_idx, count):
        slot = chunk_slot(first_row)
        base = slot * ch + jnp.bitwise_and(first_row, ch - 1)
        for u in range(count):
            idx = first_idx + u
            t = tok_smem[lax.shift_right_logical(idx, 7), jnp.bitwise_and(idx, LANES - 1)]
            pltpu.make_async_copy(h_hbm.at[pl.ds(t, 1), :], ring.at[pl.ds(base + u, 1), :], gsem.at[slot]).start()

    def chunk_wait(first_row):
        slot = chunk_slot(first_row)
        rows = pl.ds(pl.multiple_of(slot * ch, ch), ch)
        pltpu.make_async_copy(h_hbm.at[pl.ds(0, ch), :], ring.at[rows, :], gsem.at[slot]).wait()
        return rows

    @pl.when(jnp.logical_and(i == 0, j == 0))
    def _():
        load_tokens(0)

        def issue(g, carry):
            gather_rows(g * DMA_UNROLL, g * DMA_UNROLL, DMA_UNROLL)
            return carry

        lax.fori_loop(0, lead // DMA_UNROLL, issue, 0)

    def out_wait(count):
        def drain(m, carry):
            pltpu.make_async_copy(yacc.at[pl.ds(0, ch), :], y_hbm.at[pl.ds(0, ch), :], sems.at[1]).wait()
            return carry

        lax.fori_loop(0, count, drain, 0)

    @pl.when(jnp.logical_and(j == 0, nch > 0))
    def _():
        load_tokens(row0 + lead)

        def cast_rows(m, carry):
            src = chunk_wait(row0 + m * ch)
            xb[pl.ds(pl.multiple_of(m * ch, ch), ch), :] = ring[src, :].astype(BF16)
            return carry

        lax.fori_loop(0, nch, cast_rows, 0)

        @pl.when(i > 0)
        def _():
            out_wait(ic_ref[jnp.maximum(i - 1, 0)])

        def seed(m, carry):
            yacc[pl.ds(pl.multiple_of(m * ch, ch), ch), :] = jnp.broadcast_to(bdn_ref[0], (ch, yacc.shape[1]))
            return carry

        lax.fori_loop(0, nch, seed, 0)

    @pl.when(nch > 0)
    def _():
        wgu_b[...] = wgu_ref[0].astype(BF16)
        wdn_b[...] = wdn_ref[0].astype(BF16)
        bgu = bgu_ref[0]
        last = j == nj - 1

        def prefetch(m):
            first_idx = (j * nch + m) * per_body
            gather_rows(row0 + lead + first_idx, first_idx, per_body)

        def gate_up(m):
            rows = pl.ds(pl.multiple_of(m * ch, ch), ch)
            return _dot(xb[rows, :], wgu_b[...]) + bgu

        def finish(m, gu):
            rows = pl.ds(pl.multiple_of(m * ch, ch), ch)
            glu = jnp.minimum(gu, SWIGLU_LIMIT)
            f_glu = glu * _sigmoid(SWIGLU_ALPHA * glu)
            f_lin = jnp.clip(gu, -SWIGLU_LIMIT, SWIGLU_LIMIT) + 1.0
            prod = (pltpu.roll(f_glu, 1, 1) * f_lin).astype(BF16)
            parts = [_dot(prod[:, q * 2 * LANES:(q + 1) * 2 * LANES], sel_ref[...])
                     for q in range(MOE_TN // (2 * LANES))]
            act = jnp.concatenate(parts, axis=1).astype(BF16)
            yacc[rows, :] += _dot(act, wdn_b[...])

        def step(m):
            gu = gu_scr[...]
            gu_scr[...] = gate_up(m + 1)
            finish(m, gu)
            prefetch(m)

        gu_scr[...] = gate_up(0)
        n_steps = nch - 1

        def pair(p, carry):
            step(2 * p)
            step(2 * p + 1)
            return carry

        lax.fori_loop(0, lax.shift_right_logical(n_steps, 1), pair, 0)

        @pl.when(jnp.bitwise_and(n_steps, 1) == 1)
        def _():
            step(n_steps - 1)

        finish(nch - 1, gu_scr[...])
        prefetch(nch - 1)

        @pl.when(last)
        def _():
            def issue(m, carry):
                rows = pl.ds(pl.multiple_of(m * ch, ch), ch)
                dst = pl.ds(pl.multiple_of(row0 + m * ch, ch), ch)
                pltpu.make_async_copy(yacc.at[rows, :], y_hbm.at[dst, :], sems.at[1]).start()
                return carry

            lax.fori_loop(0, nch, issue, 0)

    @pl.when(jnp.logical_and(i == pl.num_programs(0) - 1, j == nj - 1))
    def _():
        out_wait(tail_ref[1])

        def drain(m, carry):
            chunk_wait(tail_ref[0] + m * ch)
            return carry

        lax.fori_loop(0, lead // ch, drain, 0)


def _experts(h2, tok_pad, item_e, item_row0, item_nch, item_valid, item_tail, w_gu, b_gu, w_dn, b_dn, n_rows):
    ne, d, two_de = w_gu.shape
    tn = MOE_TN
    nj = two_de // tn
    assert nj > 1, "the kernel separates its first and last column-tile steps"
    assert MOE_CHUNK % nj == 0 and (MOE_CHUNK // nj) % 8 == 0, "row copies per chunk stage"
    ni = item_e.shape[0]
    ring_rows = 2 * MOE_ITEM_ROWS
    sel = np.zeros((2 * LANES, LANES), np.float32)
    sel[2 * np.arange(LANES) + 1, np.arange(LANES)] = 1.0

    def jmap(i, j, iv):
        return jnp.where(iv[i] > 0, j, nj - 1)

    grid_spec = pltpu.PrefetchScalarGridSpec(
        num_scalar_prefetch=5,
        grid=(ni, nj),
        in_specs=[pl.BlockSpec(memory_space=pl.ANY),
                  pl.BlockSpec(memory_space=pl.ANY),
                  pl.BlockSpec((1, d, tn), lambda i, j, ie, ir, ic, iv, it: (ie[i], 0, jmap(i, j, iv))),
                  pl.BlockSpec((1, 1, tn), lambda i, j, ie, ir, ic, iv, it: (ie[i], 0, jmap(i, j, iv))),
                  pl.BlockSpec((1, tn // 2, d), lambda i, j, ie, ir, ic, iv, it: (ie[i], jmap(i, j, iv), 0)),
                  pl.BlockSpec((1, 1, d), lambda i, j, ie, ir, ic, iv, it: (ie[i], 0, 0)),
                  pl.BlockSpec((2 * LANES, LANES), lambda i, j, ie, ir, ic, iv, it: (0, 0))],
        out_specs=pl.BlockSpec(memory_space=pl.ANY),
        scratch_shapes=[pltpu.SMEM((MOE_ITEM_ROWS // LANES, LANES), I32),
                        pltpu.VMEM((ring_rows, d), F32),
                        pltpu.VMEM((MOE_ITEM_ROWS, d), BF16),
                        pltpu.VMEM((MOE_ITEM_ROWS, d), F32),
                        pltpu.VMEM((d, tn), BF16),
                        pltpu.VMEM((tn // 2, d), BF16),
                        pltpu.VMEM((MOE_CHUNK, tn), F32),
                        pltpu.SemaphoreType.DMA((ring_rows // MOE_CHUNK,)),
                        pltpu.SemaphoreType.DMA((2,))],
    )
    return pl.pallas_call(
        functools.partial(_expert_kernel, nj),
        grid_spec=grid_spec,
        out_shape=jax.ShapeDtypeStruct((n_rows, d), F32),
        compiler_params=_cparams(("arbitrary", "arbitrary")),
        name="moe_experts",
    )(item_e, item_row0, item_nch, item_valid, item_tail, tok_pad, h2, w_gu, b_gu.reshape(ne, 1, two_de),
      w_dn, b_dn.reshape(ne, 1, d), jnp.asarray(sel, BF16))


def _combine_kernel(alpha, dest_ref, y_hbm, gate_ref, x1_ref, g2_ref, lg_ref, lb_ref, out_ref, buf, sem):
    tc = COMBINE_TOKENS

    group = 8

    def issue(g, carry):
        first = pl.multiple_of(g * group, group)
        for dr in range(group):
            for k in range(TOP_K):
                src = dest_ref[0, 0, (first + dr) * TOP_K + k]
                pltpu.make_async_copy(y_hbm.at[pl.ds(src, 1), :], buf.at[k, pl.ds(first + dr, 1), :],
                                      sem.at[0]).start()
        return carry

    lax.fori_loop(0, tc // group, issue, 0)
    for k in range(TOP_K):
        pltpu.make_async_copy(y_hbm.at[pl.ds(0, tc), :], buf.at[k], sem.at[0]).wait()

    gates = gate_ref[...]
    y = gates[:, 0:1] * buf[0]
    for k in range(1, TOP_K):
        y = y + gates[:, k:k + 1] * buf[k]
    z = alpha * x1_ref[...] + (1.0 + g2_ref[0]) * y
    out_ref[...] = _layer_norm(z, lg_ref[...], lb_ref[...])


def _combine(dest, y_pad, gates, x1, g2, ln_g, ln_b, alpha, seq):
    t, d = x1.shape
    tc = COMBINE_TOKENS
    per_b = seq // tc
    row = lambda i: (i, 0)
    const = lambda i: (0, 0)
    return pl.pallas_call(
        functools.partial(_combine_kernel, alpha),
        grid=(t // tc,),
        in_specs=[pl.BlockSpec((1, 1, tc * TOP_K), lambda i: (i, 0, 0), memory_space=pltpu.SMEM),
                  pl.BlockSpec(memory_space=pl.ANY),
                  pl.BlockSpec((tc, LANES), row), pl.BlockSpec((tc, d), row),
                  pl.BlockSpec((1, 1, d), lambda i: (i // per_b, 0, 0)),
                  pl.BlockSpec((1, d), const), pl.BlockSpec((1, d), const)],
        out_specs=pl.BlockSpec((tc, d), row),
        out_shape=jax.ShapeDtypeStruct((t, d), F32),
        scratch_shapes=[pltpu.VMEM((TOP_K, tc, d), F32), pltpu.SemaphoreType.DMA((1,))],
        compiler_params=_cparams(("arbitrary",)),
        name="moe_combine_ln",
    )(dest.reshape(t // tc, 1, tc * TOP_K), y_pad, gates, x1, g2, ln_g, ln_b)


def _count_le(ends, q):
    return jnp.sum((ends[None, :] <= q[:, None]).astype(I32), axis=1)


def _lookup(table, idx):
    hit = idx[:, None] == jnp.arange(table.shape[0], dtype=I32)
    return jnp.sum(jnp.where(hit, table[None, :], 0), axis=1)


def _routing_tables(top_idx):
    t = top_idx.shape[0]
    tk = t * TOP_K
    experts = jnp.arange(N_EXPERTS, dtype=I32)
    hits = [top_idx[:, k:k + 1] == experts for k in range(TOP_K)]
    onehot = sum(h.astype(I32) for h in hits)
    csum = jnp.cumsum(onehot, axis=0)
    counts = csum[-1]
    earlier = csum - onehot
    starts = jnp.cumsum(counts) - counts
    padded = ((counts + MOE_ROW_PAD - 1) // MOE_ROW_PAD) * MOE_ROW_PAD
    pad_end = jnp.cumsum(padded)
    pad_start = pad_end - padded
    dest = jnp.stack([jnp.sum(jnp.where(h, earlier + pad_start, 0), axis=1) for h in hits], axis=1).reshape(tk)

    n_rows = tk + N_EXPERTS * MOE_ROW_PAD
    n_tab = n_rows + 2 * MOE_ITEM_ROWS
    tok_sorted = (jnp.argsort(top_idx.reshape(tk)) // TOP_K).astype(I32)
    blk = jnp.arange(n_tab // MOE_ROW_PAD, dtype=I32) * MOE_ROW_PAD
    blk_e = jnp.minimum(_count_le(pad_end, blk), N_EXPERTS - 1)
    local = (blk - _lookup(pad_start, blk_e))[:, None] + jnp.arange(MOE_ROW_PAD, dtype=I32)
    src = jnp.clip(_lookup(starts, blk_e)[:, None] + local, 0, tk - 1)
    live = jnp.logical_and(local < _lookup(counts, blk_e)[:, None], (blk < pad_end[-1])[:, None])
    tok_pad = jnp.where(live, tok_sorted[src], 0).reshape(-1, LANES)

    items_per = (padded + MOE_ITEM_ROWS - 1) // MOE_ITEM_ROWS
    item_end = jnp.cumsum(items_per)
    item_start = item_end - items_per
    n_items = N_EXPERTS + n_rows // MOE_ITEM_ROWS
    idx = jnp.arange(n_items, dtype=I32)
    valid = idx < item_end[-1]
    e_i = jnp.minimum(_count_le(item_end, idx), N_EXPERTS - 1)
    k_i = idx - _lookup(item_start, e_i)
    row0 = _lookup(pad_start, e_i) + k_i * MOE_ITEM_ROWS
    nrows = jnp.clip(_lookup(padded, e_i) - k_i * MOE_ITEM_ROWS, 0, MOE_ITEM_ROWS)
    e_last = jnp.sum(jnp.where(idx == item_end[-1] - 1, e_i, 0))
    item_e = jnp.where(valid, e_i, e_last).astype(I32)
    item_row0 = jnp.where(valid, row0, 0).astype(I32)
    item_nch = jnp.where(valid, nrows // MOE_CHUNK, 0).astype(I32)
    item_tail = jnp.stack([pad_end[-1], jnp.sum(jnp.where(idx == item_end[-1] - 1, item_nch, 0))]).astype(I32)
    return tok_pad, dest, item_e, item_row0, item_nch, valid.astype(I32), item_tail, n_rows


def kernel(x, c, w_ada, b_ada, w_in, b_in, conv_w, conv_b, m_norm_g, a_norm_g, w_out, ln1_g, ln1_b,
           router_w, router_b, w_gu, b_gu, w_dn, b_dn, ln2_g, ln2_b):
    bsz, seq, d = x.shape
    depth = w_ada.shape[0]
    t = bsz * seq
    alpha = float((2 * depth) ** 0.25)
    qk_w = 2 * M_HEADS * M_DQK
    mv_w = M_HEADS * M_DV
    aw = A_HEADS * A_DH
    gate_lo = qk_w + 2 * mv_w
    gate_hi = gate_lo + 2 * M_HEADS

    x2 = x.reshape(t, d)
    for l in range(depth):
        mod = _ada_mod(c, w_ada[l], b_ada[l]).reshape(bsz, 6, 1, d)
        sh1, sc1, g1, sh2, sc2, g2 = (mod[:, i] for i in range(6))

        w_main = jnp.concatenate([w_in[l][:, :gate_lo], w_in[l][:, gate_hi:]], axis=1).astype(BF16)
        b_main = jnp.concatenate([b_in[l][:gate_lo], b_in[l][gate_hi:]]).reshape(1, -1)
        w_gate = jnp.zeros((d, LANES), BF16).at[:, :2 * M_HEADS].set(w_in[l][:, gate_lo:gate_hi].astype(BF16))
        b_gate = jnp.zeros((1, LANES), F32).at[0, :2 * M_HEADS].set(b_in[l][gate_lo:gate_hi])
        attn_col0 = gate_lo // aw
        proj, gates_c, *qkv_dil = _in_proj(x2, sc1, sh1, w_main, b_main, w_gate, b_gate, seq, attn_col0)

        gates_r = gates_c[:, :2 * M_HEADS].reshape(bsz, seq, 2 * M_HEADS).transpose(0, 2, 1)
        y_m = _mlstm(proj, gates_c, gates_r, conv_w[l], conv_b[l].reshape(1, -1),
                     m_norm_g[l].reshape(1, -1), bsz, seq)

        outs, lses = [], []
        for dil in DILATIONS:
            if dil == 1:
                o_d, lse_d = _attn_group(proj.reshape(bsz, 1, seq, -1), dil, attn_col0)
            else:
                o_d, lse_d = _attn_group(qkv_dil[DILATIONS.index(dil) - 1], dil, 0)
            outs.append(o_d)
            lses.append(lse_d)

        rw = jnp.zeros((d, LANES), BF16).at[:, :N_EXPERTS].set(router_w[l].astype(BF16))
        rb = jnp.zeros((1, LANES), F32).at[0, :N_EXPERTS].set(router_b[l])
        x1, h2, top_idx, gates = _out_proj(y_m, outs, lses, a_norm_g[l].reshape(1, -1), w_out[l].astype(BF16),
                                           x2, g1, sc2, sh2, ln1_g[l].reshape(1, -1), ln1_b[l].reshape(1, -1),
                                           rw, rb, alpha, seq)

        (tok_pad, dest, item_e, item_row0, item_nch, item_valid, item_tail,
         n_rows) = _routing_tables(top_idx[:, :TOP_K])
        y_pad = _experts(h2, tok_pad, item_e, item_row0, item_nch, item_valid, item_tail,
                         w_gu[l], b_gu[l], w_dn[l], b_dn[l], n_rows)
        x2 = _combine(dest, y_pad, gates, x1, g2, ln2_g[l].reshape(1, -1), ln2_b[l].reshape(1, -1), alpha, seq)
    return x2.reshape(bsz, seq, d)
```

```python
import functools

import jax
import jax.numpy as jnp
import numpy as np
from jax import lax
from jax.experimental import pallas as pl
from jax.experimental.pallas import tpu as pltpu

F32 = jnp.float32
BF16 = jnp.bfloat16
I32 = jnp.int32

M_HEADS = 4
M_DQK = 128
M_DV = 256
CONV_WIDTH = 4
A_HEADS = 16
A_DH = 64
ATTN_BLOCK = 128
DILATIONS = (1, 4, 16)
N_EXPERTS = 32
TOP_K = 4
SWIGLU_ALPHA = 1.702
SWIGLU_LIMIT = 7.0
EPS = 1e-5

LANES = 128
VMEM_LIMIT = 56 * 1024 * 1024

MLSTM_CHUNK = 256
MOE_ROW_PAD = 256
MOE_CHUNK = 256
MOE_ITEM_ROWS = 1280
MOE_TN = 1024
MOE_VMEM_LIMIT = 60 * 1024 * 1024
COMBINE_TOKENS = 128
DMA_UNROLL = 8


def _cparams(sem, vmem=VMEM_LIMIT):
    return pltpu.CompilerParams(dimension_semantics=sem, vmem_limit_bytes=vmem)


def _sigmoid(x):
    return 1.0 / (1.0 + jnp.exp(-x))


def _log_sigmoid(x):
    return jnp.minimum(x, 0.0) - jnp.log(1.0 + jnp.exp(-jnp.abs(x)))


def _layer_norm(z, g, b):
    mu = jnp.mean(z, axis=-1, keepdims=True)
    zc = z - mu
    var = jnp.mean(zc * zc, axis=-1, keepdims=True)
    return zc * lax.rsqrt(var + EPS) * g + b


def _dot(a, b):
    return jnp.dot(a, b, preferred_element_type=F32)


def _dot_nt(a, b):
    return lax.dot_general(a, b, (((1,), (1,)), ((), ())), preferred_element_type=F32)


HIGH_HALF = np.uint32(0xFFFF0000)


def _pack_bf16_pairs(x):
    half = x.shape[1] // 2
    xb = x.astype(BF16).astype(F32)
    lo = lax.bitcast_convert_type(xb[:, :half], jnp.uint32)
    hi = lax.bitcast_convert_type(xb[:, half:], jnp.uint32)
    return jnp.bitwise_or(lax.shift_right_logical(lo, jnp.uint32(16)), jnp.bitwise_and(hi, HIGH_HALF))


def _unpack_bf16_pairs(w):
    lo = lax.bitcast_convert_type(lax.shift_left(w, jnp.uint32(16)), F32)
    hi = lax.bitcast_convert_type(jnp.bitwise_and(w, HIGH_HALF), F32)
    return lo.astype(BF16), hi.astype(BF16)


def _dot_hilo(a, sel):
    hi = a.astype(BF16)
    lo = (a - hi.astype(F32)).astype(BF16)
    return _dot(hi, sel) + _dot(lo, sel)


def _ada_kernel(c_ref, w_ref, b_ref, o_ref):
    c = c_ref[...]
    cond = c * _sigmoid(c)
    o_ref[...] = _dot(cond.astype(BF16), w_ref[...].astype(BF16)) + b_ref[...]


def _ada_mod(c, w_ada, b_ada):
    bsz, d = c.shape
    n = w_ada.shape[1]
    tn = 1024
    rows = 8
    c_pad = jnp.zeros((rows, d), F32).at[:bsz].set(c)
    out = pl.pallas_call(
        _ada_kernel,
        grid=(n // tn,),
        in_specs=[pl.BlockSpec((rows, d), lambda j: (0, 0)),
                  pl.BlockSpec((d, tn), lambda j: (0, j)),
                  pl.BlockSpec((1, tn), lambda j: (0, j))],
        out_specs=pl.BlockSpec((rows, tn), lambda j: (0, j)),
        out_shape=jax.ShapeDtypeStruct((rows, n), F32),
        compiler_params=_cparams(("arbitrary",)),
        name="ada_mod",
    )(c_pad, w_ada, b_ada.reshape(1, n))
    return out[:bsz]


def _inproj_kernel(attn_col0, x_ref, sc_ref, sh_ref, w_ref, b_ref, wg_ref, bg_ref, o_ref, g_ref, *rest):
    dil_refs, (h_ref, r_scr) = rest[:-2], rest[-2:]
    j = pl.program_id(1)

    @pl.when(j == 0)
    def _():
        h = x_ref[...] * (1.0 + sc_ref[0]) + sh_ref[0]
        hb = h.astype(BF16)
        h_ref[...] = hb
        g_ref[...] = _dot(hb, wg_ref[...]) + bg_ref[...]

    res = _dot(h_ref[...], w_ref[...]) + b_ref[...]
    o_ref[...] = res.astype(BF16)

    @pl.when(j >= attn_col0)
    def _():
        cols = res.shape[1] // LANES
        for c in range(cols):
            r_scr[c] = res[:, c * LANES:(c + 1) * LANES]
        for ref in dil_refs:
            d, n = ref.shape[1], ref.shape[2]
            for r in range(d):
                for c in range(cols):
                    ref[0, r, :, c * LANES:(c + 1) * LANES] = r_scr[c, pl.ds(r, n, stride=d), :].astype(BF16)


def _in_proj(x2, sc, sh, w_main, b_main, w_gate, b_gate, seq, attn_col0):
    t, d = x2.shape
    n = w_main.shape[1]
    tm, tn = 512, 1024
    per_b = seq // tm
    bsz = t // seq
    dils = [dl for dl in DILATIONS if dl > 1]
    aw3 = n - attn_col0 * tn
    dil_specs = [pl.BlockSpec((1, dl, tm // dl, tn),
                              lambda i, j: (i // per_b, 0, i % per_b, jnp.maximum(j - attn_col0, 0))) for dl in dils]
    dil_shapes = [jax.ShapeDtypeStruct((bsz, dl, seq // dl, aw3), BF16) for dl in dils]
    return pl.pallas_call(
        functools.partial(_inproj_kernel, attn_col0),
        grid=(t // tm, n // tn),
        in_specs=[pl.BlockSpec((tm, d), lambda i, j: (i, 0)),
                  pl.BlockSpec((1, 1, d), lambda i, j: (i // per_b, 0, 0)),
                  pl.BlockSpec((1, 1, d), lambda i, j: (i // per_b, 0, 0)),
                  pl.BlockSpec((d, tn), lambda i, j: (0, j)),
                  pl.BlockSpec((1, tn), lambda i, j: (0, j)),
                  pl.BlockSpec((d, LANES), lambda i, j: (0, 0)),
                  pl.BlockSpec((1, LANES), lambda i, j: (0, 0))],
        out_specs=[pl.BlockSpec((tm, tn), lambda i, j: (i, j)),
                   pl.BlockSpec((tm, LANES), lambda i, j: (i, 0))] + dil_specs,
        out_shape=[jax.ShapeDtypeStruct((t, n), BF16),
                   jax.ShapeDtypeStruct((t, LANES), F32)] + dil_shapes,
        scratch_shapes=[pltpu.VMEM((tm, d), BF16), pltpu.VMEM((tn // LANES, tm, LANES), F32)],
        compiler_params=_cparams(("arbitrary", "arbitrary")),
        name="in_proj",
    )(x2, sc, sh, w_main, b_main, w_gate, b_gate)


def _mlstm_kernel(qk_ref, v_ref, o_ref, gc_ref, gr_ref, cw_ref, cb_ref, ng_ref, out_ref,
                  ct_ref, n_ref, m_ref, prev_ref):
    c = pl.program_id(1)
    L = MLSTM_CHUNK

    @pl.when(c == 0)
    def _():
        ct_ref[...] = jnp.zeros_like(ct_ref)
        n_ref[...] = jnp.zeros_like(n_ref)
        m_ref[...] = jnp.zeros_like(m_ref)
        prev_ref[...] = jnp.zeros_like(prev_ref)

    x = qk_ref[...].astype(F32)
    prev = prev_ref[...]
    row = lax.broadcasted_iota(I32, (L, 1), 0)
    y = cw_ref[CONV_WIDTH - 1:CONV_WIDTH, :] * x + cb_ref[...]
    for k in range(1, CONV_WIDTH):
        xs = jnp.where(row < k, pltpu.roll(prev, k, 0), pltpu.roll(x, k, 0))
        y = y + cw_ref[CONV_WIDTH - 1 - k:CONV_WIDTH - k, :] * xs
    prev_ref[...] = x
    y = y * _sigmoid(y)

    gc = gc_ref[...]
    gr = gr_ref[0]
    ti = lax.broadcasted_iota(I32, (L, L), 0)
    si = lax.broadcasted_iota(I32, (L, L), 1)
    causal = si <= ti
    tril = causal.astype(F32)
    triu = (ti <= si).astype(F32)
    b_cols = jnp.dot(tril, _log_sigmoid(gc), precision=lax.Precision.HIGHEST, preferred_element_type=F32)
    b_rows = jnp.dot(_log_sigmoid(gr), triu, precision=lax.Precision.HIGHEST, preferred_element_type=F32)

    qk_w = M_HEADS * M_DQK
    heads = range(M_HEADS)
    qf = [y[:, h * M_DQK:(h + 1) * M_DQK] for h in heads]
    kf = [y[:, qk_w + h * M_DQK:qk_w + (h + 1) * M_DQK] * (M_DQK ** -0.5) for h in heads]
    qb = [t.astype(BF16) for t in qf]
    vb = [v_ref[:, h * M_DV:(h + 1) * M_DV] for h in heads]
    bc = [b_cols[:, M_HEADS + h:M_HEADS + h + 1] for h in heads]
    ic = [gc[:, h:h + 1] for h in heads]
    br = [b_rows[M_HEADS + h:M_HEADS + h + 1, :] for h in heads]
    ir = [gr[h:h + 1, :] for h in heads]
    m_prev = [m_ref[h][0:1, 0:1] for h in heads]
    n_prev = [n_ref[h][0:1, :] for h in heads]

    b_last = [bc[h][L - 1:L, :] for h in heads]
    g_col = [b_last[h] - bc[h] + ic[h] for h in heads]
    m_new = [jnp.maximum(b_last[h] + m_prev[h], jnp.max(g_col[h], axis=0, keepdims=True)) for h in heads]
    w_col = [jnp.exp(g_col[h] - m_new[h]) for h in heads]
    decay = [jnp.exp(b_last[h] + m_prev[h] - m_new[h]) for h in heads]

    qk = [_dot_nt(qb[h], kf[h].astype(BF16)) for h in heads]
    q_state = [_dot(qb[h], ct_ref[h].astype(BF16)) for h in heads]
    kv_new = [_dot(kf[h].T.astype(BF16), (w_col[h] * vb[h].astype(F32)).astype(BF16)) for h in heads]

    dm = [jnp.where(causal, bc[h] - br[h] + ir[h], -jnp.inf) for h in heads]
    inter = [bc[h] + m_prev[h] for h in heads]
    mt = [jnp.maximum(inter[h], jnp.max(dm[h], axis=1, keepdims=True)) for h in heads]
    a = [jnp.exp(dm[h] - mt[h]) * qk[h] for h in heads]
    e_int = [jnp.exp(inter[h] - mt[h]) for h in heads]
    av = [_dot(a[h].astype(BF16), vb[h]) for h in heads]

    for h in heads:
        num = av[h] + e_int[h] * q_state[h]
        den = (jnp.sum(a[h], axis=1, keepdims=True)
               + e_int[h] * jnp.sum(qf[h] * n_prev[h], axis=1, keepdims=True))
        hh = num / jnp.maximum(jnp.abs(den), jnp.exp(-mt[h]))

        ct_ref[h] = decay[h] * ct_ref[h] + kv_new[h]
        n_new = decay[h] * n_prev[h] + jnp.sum(w_col[h] * kf[h], axis=0, keepdims=True)
        n_ref[h] = jnp.broadcast_to(n_new, n_ref.shape[1:])
        m_ref[h] = jnp.broadcast_to(m_new[h], m_ref.shape[1:])

        ms = jnp.mean(hh * hh, axis=1, keepdims=True)
        og = o_ref[:, h * M_DV:(h + 1) * M_DV].astype(F32)
        yh = hh * lax.rsqrt(ms + EPS) * ng_ref[:, h * M_DV:(h + 1) * M_DV] * _sigmoid(og)
        out_ref[:, h * M_DV:(h + 1) * M_DV] = yh.astype(BF16)


def _mlstm(proj, gates_c, gates_r, conv_w, conv_b, norm_g, bsz, seq):
    L = MLSTM_CHUNK
    nc = seq // L
    t = bsz * seq
    w = M_HEADS * M_DV
    return pl.pallas_call(
        _mlstm_kernel,
        grid=(bsz, nc),
        in_specs=[pl.BlockSpec((L, w), lambda b, c: (b * nc + c, 0)),
                  pl.BlockSpec((L, w), lambda b, c: (b * nc + c, 1)),
                  pl.BlockSpec((L, w), lambda b, c: (b * nc + c, 2)),
                  pl.BlockSpec((L, LANES), lambda b, c: (b * nc + c, 0)),
                  pl.BlockSpec((1, 8, L), lambda b, c: (b, 0, c)),
                  pl.BlockSpec((CONV_WIDTH, w), lambda b, c: (0, 0)),
                  pl.BlockSpec((1, w), lambda b, c: (0, 0)),
                  pl.BlockSpec((1, w), lambda b, c: (0, 0))],
        out_specs=pl.BlockSpec((L, w), lambda b, c: (b * nc + c, 0)),
        out_shape=jax.ShapeDtypeStruct((t, w), BF16),
        scratch_shapes=[pltpu.VMEM((M_HEADS, M_DQK, M_DV), F32),
                        pltpu.VMEM((M_HEADS, 8, M_DQK), F32),
                        pltpu.VMEM((M_HEADS, 8, LANES), F32),
                        pltpu.VMEM((L, w), F32)],
        compiler_params=_cparams(("arbitrary", "arbitrary")),
        name="mlstm",
    )(proj, proj, proj, gates_c, gates_r, conv_w, conv_b, norm_g)


def _attn_kernel(dilation, has_prev, *refs):
    nq = ATTN_BLOCK
    if has_prev:
        q_ref, kc_ref, vc_ref, kp_ref, vp_ref, o_ref, lse_ref, k_all, v_all = refs
        k_all[0:nq, :] = kp_ref[0, 0]
        k_all[nq:2 * nq, :] = kc_ref[0, 0]
        v_all[0:nq, :] = vp_ref[0, 0]
        v_all[nq:2 * nq, :] = vc_ref[0, 0]
        nk = 2 * nq
    else:
        q_ref, k_all, v_all, o_ref, lse_ref = refs
        k_all, v_all = k_all.at[0, 0], v_all.at[0, 0]
        nk = nq
    n = pl.program_id(2)
    qi = lax.broadcasted_iota(I32, (nq, nk), 0)
    ki = lax.broadcasted_iota(I32, (nq, nk), 1)
    dist = qi - ki + (nk - nq)
    ok = jnp.logical_and(dist >= 0, dist <= nq)
    if has_prev:
        ok = jnp.logical_and(ok, jnp.logical_or(ki >= nq, n > 0))
    dist_f = jnp.where(ok, dist.astype(F32), jnp.inf)
    lane = lax.broadcasted_iota(I32, (nq, LANES), 1)
    left_q = lane < A_DH
    left_k = lax.broadcasted_iota(I32, (nk, LANES), 1) < A_DH
    n_pairs = A_HEADS // 2

    scores = []
    for p in range(n_pairs):
        cols = slice(p * LANES, (p + 1) * LANES)
        qp = q_ref[0, 0, :, cols] * (A_DH ** -0.5)
        kp = k_all[:, cols]
        zero = jnp.zeros_like(qp)
        scores.append(_dot_nt(jnp.where(left_q, qp, zero), kp))
        scores.append(_dot_nt(jnp.where(left_q, zero, qp), kp))
    probs, maxes = [], []
    for h in range(A_HEADS):
        coef = -(2.0 ** (-8.0 * (h + 1) / A_HEADS)) * dilation
        s = scores[h] + dist_f * coef
        m = jnp.max(s, axis=1, keepdims=True)
        probs.append(jnp.exp(s - m).astype(BF16))
        maxes.append(m)
    lse_tile = jnp.zeros((nq, LANES), F32)
    for p in range(n_pairs):
        cols = slice(p * LANES, (p + 1) * LANES)
        vp = v_all[:, cols]
        one = jnp.ones_like(vp)
        pv_e = _dot(probs[2 * p], jnp.where(left_k, vp, one))
        pv_o = _dot(probs[2 * p + 1], jnp.where(left_k, one, vp))
        num = jnp.where(left_q, pv_e, pv_o)
        den = pltpu.roll(jnp.where(left_q, pv_o, pv_e), A_DH, 1)
        o_ref[0, 0, :, cols] = (num / den).astype(BF16)
        lse_tile = jnp.where(lane == 2 * p, maxes[2 * p] + jnp.log(pv_e[:, A_DH:A_DH + 1]), lse_tile)
        lse_tile = jnp.where(lane == 2 * p + 1, maxes[2 * p + 1] + jnp.log(pv_o[:, 0:1]), lse_tile)
    lse_ref[0, 0] = lse_tile


def _attn_group(qkv, dilation, col0):
    bsz, d, ls, _ = qkv.shape
    aw = A_HEADS * A_DH
    nq = ATTN_BLOCK
    nb = ls // nq
    has_prev = nb > 1
    blk = (1, 1, nq, aw)
    in_specs = [pl.BlockSpec(blk, lambda b, r, n: (b, r, n, col0)),
                pl.BlockSpec(blk, lambda b, r, n: (b, r, n, col0 + 1)),
                pl.BlockSpec(blk, lambda b, r, n: (b, r, n, col0 + 2))]
    args = [qkv, qkv, qkv]
    if has_prev:
        in_specs += [pl.BlockSpec(blk, lambda b, r, n: (b, r, jnp.maximum(n - 1, 0), col0 + 1)),
                     pl.BlockSpec(blk, lambda b, r, n: (b, r, jnp.maximum(n - 1, 0), col0 + 2))]
        args += [qkv, qkv]
    return pl.pallas_call(
        functools.partial(_attn_kernel, dilation, has_prev),
        grid=(bsz, d, nb),
        in_specs=in_specs,
        out_specs=[pl.BlockSpec(blk, lambda b, r, n: (b, r, n, 0)),
                   pl.BlockSpec((1, 1, nq, LANES), lambda b, r, n: (b, r, n, 0))],
        out_shape=[jax.ShapeDtypeStruct((bsz, d, ls, aw), BF16),
                   jax.ShapeDtypeStruct((bsz, d, ls, LANES), F32)],
        scratch_shapes=[pltpu.VMEM((2 * nq, aw), BF16)] * 2 if has_prev else [],
        compiler_params=_cparams(("arbitrary", "arbitrary", "arbitrary")),
        name=f"dilated_attn_d{dilation}",
    )(*args)


def _head_maps(n_heads, dh):
    w = n_heads * dh
    e = np.zeros((LANES, w), np.float32)
    for h in range(n_heads):
        e[h, h * dh:(h + 1) * dh] = 1.0
    return jnp.asarray(e, BF16), jnp.asarray(e.T.copy(), BF16)


def _natural_rows(ref, scr):
    d, n, w = ref.shape[1:]
    if d == 1:
        return ref[0, 0].astype(F32)
    cols = w // LANES
    for r in range(d):
        blk = ref[0, r].astype(F32)
        for c in range(cols):
            scr[c, pl.ds(r, n, stride=d), :] = blk[:, c * LANES:(c + 1) * LANES]
    return jnp.concatenate([scr[c] for c in range(cols)], axis=1)


def _merged_heads(o_refs, l_refs, g_ref, e_ref, p_ref, o_scr, l_scr):
    l1, l2, l3 = (_natural_rows(ref, l_scr.at[g]) for g, ref in enumerate(l_refs))
    mx = jnp.maximum(jnp.maximum(l1, l2), l3)
    w1, w2, w3 = jnp.exp(l1 - mx), jnp.exp(l2 - mx), jnp.exp(l3 - mx)
    inv = 1.0 / (w1 + w2 + w3)
    e = e_ref[...]
    o = (_dot_hilo(w1 * inv, e) * _natural_rows(o_refs[0], o_scr.at[0])
         + _dot_hilo(w2 * inv, e) * _natural_rows(o_refs[1], o_scr.at[1])
         + _dot_hilo(w3 * inv, e) * _natural_rows(o_refs[2], o_scr.at[2]))
    ms = _dot_hilo(o * o, p_ref[...]) * (1.0 / A_DH)
    scale = _dot_hilo(lax.rsqrt(ms + EPS), e)
    return (o * scale * g_ref[...]).astype(BF16)


def _outproj_kernel(alpha, ym_ref, o1_ref, o2_ref, o3_ref, l1_ref, l2_ref, l3_ref, ag_ref, e_ref, p_ref,
                    w_ref, x_ref, g1_ref, sc_ref, sh_ref, lg_ref, lb_ref, rw_ref, rb_ref,
                    x1_ref, h2_ref, ti_ref, tg_ref, o_scr, l_scr):
    half = ym_ref.shape[1]
    y_a = _merged_heads((o1_ref, o2_ref, o3_ref), (l1_ref, l2_ref, l3_ref), ag_ref, e_ref, p_ref, o_scr, l_scr)
    y = _dot(ym_ref[...], w_ref[0:half, :]) + _dot(y_a, w_ref[half:2 * half, :])
    z = alpha * x_ref[...] + (1.0 + g1_ref[0]) * y
    x1 = _layer_norm(z, lg_ref[...], lb_ref[...])
    x1_ref[...] = x1
    h2 = x1 * (1.0 + sc_ref[0]) + sh_ref[0]
    h2_ref[...] = _pack_bf16_pairs(h2)
    logits = _dot(h2.astype(BF16), rw_ref[...]) + rb_ref[...]
    lane = lax.broadcasted_iota(I32, logits.shape, 1)
    lane_f = lane.astype(F32)
    work = jnp.where(lane < N_EXPERTS, logits, -jnp.inf)
    idx_tile = jnp.zeros(logits.shape, F32)
    val_tile = jnp.zeros(logits.shape, F32)
    top = None
    denom = None
    for k in range(TOP_K):
        mk = jnp.max(work, axis=1, keepdims=True)
        ik = jnp.min(jnp.where(work == mk, lane_f, float(LANES)), axis=1, keepdims=True)
        work = jnp.where(lane_f == ik, -jnp.inf, work)
        if k == 0:
            top = mk
        ek = jnp.exp(mk - top)
        denom = ek if k == 0 else denom + ek
        idx_tile = jnp.where(lane == k, ik, idx_tile)
        val_tile = jnp.where(lane == k, ek, val_tile)
    ti_ref[...] = idx_tile.astype(I32)
    tg_ref[...] = val_tile / denom


def _out_proj(y_m, outs, lses, norm_g, w_out, x2, g1, sc2, sh2, ln_g, ln_b, rw, rb, alpha, seq):
    t, d = x2.shape
    half = y_m.shape[1]
    aw = outs[0].shape[3]
    tm = 256
    per_b = seq // tm
    row = lambda i: (i, 0)
    const = lambda i: (0, 0)
    mod = lambda i: (i // per_b, 0, 0)
    expand, pool = _head_maps(A_HEADS, A_DH)

    def grouped(arr):
        dl, w = arr.shape[1], arr.shape[3]
        return pl.BlockSpec((1, dl, tm // dl, w), lambda i: (i // per_b, 0, i % per_b, 0))

    return pl.pallas_call(
        functools.partial(_outproj_kernel, alpha),
        grid=(t // tm,),
        in_specs=[pl.BlockSpec((tm, half), row)] + [grouped(a) for a in outs] + [grouped(a) for a in lses]
        + [pl.BlockSpec((1, aw), const), pl.BlockSpec((LANES, aw), const), pl.BlockSpec((aw, LANES), const),
                  pl.BlockSpec((2 * half, d), const), pl.BlockSpec((tm, d), row),
                  pl.BlockSpec((1, 1, d), mod), pl.BlockSpec((1, 1, d), mod), pl.BlockSpec((1, 1, d), mod),
                  pl.BlockSpec((1, d), const), pl.BlockSpec((1, d), const),
                  pl.BlockSpec((d, LANES), const), pl.BlockSpec((1, LANES), const)],
        out_specs=[pl.BlockSpec((tm, d), row), pl.BlockSpec((tm, d // 2), row),
                   pl.BlockSpec((tm, LANES), row), pl.BlockSpec((tm, LANES), row)],
        out_shape=[jax.ShapeDtypeStruct((t, d), F32), jax.ShapeDtypeStruct((t, d // 2), jnp.uint32),
                   jax.ShapeDtypeStruct((t, LANES), I32), jax.ShapeDtypeStruct((t, LANES), F32)],
        scratch_shapes=[pltpu.VMEM((len(outs), aw // LANES, tm, LANES), F32),
                        pltpu.VMEM((len(lses), 1, tm, LANES), F32)],
        compiler_params=_cparams(("arbitrary",)),
        name="out_proj_ln_router",
    )(y_m, *outs, *lses, norm_g, expand, pool, w_out, x2, g1, sc2, sh2, ln_g, ln_b, rw, rb)


def _expert_kernel(nj, ie_ref, ir_ref, ic_ref, iv_ref, tail_ref, tok_hbm, h_hbm, wgu_ref, bgu_ref, wdn_ref,
                   bdn_ref, sel_ref, y_hbm, tok_smem, ring, xb, yacc, wgu_b, wdn_b, gu_scr, gsem, sems):
    i = pl.program_id(0)
    j = pl.program_id(1)
    row0 = ir_ref[i]
    nch = ic_ref[i]
    ch = MOE_CHUNK
    lead = MOE_ITEM_ROWS
    ring_chunks = ring.shape[0] // ch
    per_body = ch // nj

    def chunk_slot(first_row):
        return lax.rem(lax.shift_right_logical(first_row, 8), ring_chunks)

    def load_tokens(first_row):
        rows = pl.ds(lax.shift_right_logical(first_row, 7), lead // LANES)
        cp = pltpu.make_async_copy(tok_hbm.at[rows, :], tok_smem, sems.at[0])
        cp.start()
        cp.wait()

    def gather_rows(first_row, first_idx, count):
        slot = chunk_slot(first_row)
        base = slot * ch + jnp.bitwise_and(first_row, ch - 1)
        for u in range(count):
            idx = first_idx + u
            t = tok_smem[lax.shift_right_logical(idx, 7), jnp.bitwise_and(idx, LANES - 1)]
            pltpu.make_async_copy(h_hbm.at[pl.ds(t, 1), :], ring.at[pl.ds(base + u, 1), :], gsem.at[slot]).start()

    def chunk_wait(first_row):
        slot = chunk_slot(first_row)
        rows = pl.ds(pl.multiple_of(slot * ch, ch), ch)
        pltpu.make_async_copy(h_hbm.at[pl.ds(0, ch), :], ring.at[rows, :], gsem.at[slot]).wait()
        return rows

    @pl.when(jnp.logical_and(i == 0, j == 0))
    def _():
        load_tokens(0)

        def issue(g, carry):
            gather_rows(g * DMA_UNROLL, g * DMA_UNROLL, DMA_UNROLL)
            return carry

        lax.fori_loop(0, lead // DMA_UNROLL, issue, 0)

    def out_wait(count):
        def drain(m, carry):
            pltpu.make_async_copy(yacc.at[pl.ds(0, ch), :], y_hbm.at[pl.ds(0, ch), :], sems.at[1]).wait()
            return carry

        lax.fori_loop(0, count, drain, 0)

    @pl.when(jnp.logical_and(j == 0, nch > 0))
    def _():
        load_tokens(row0 + lead)

        def cast_rows(m, carry):
            src = chunk_wait(row0 + m * ch)
            rows = pl.ds(pl.multiple_of(m * ch, ch), ch)
            half = ring.shape[1]
            xb[rows, 0:half], xb[rows, half:2 * half] = _unpack_bf16_pairs(ring[src, :])
            return carry

        lax.fori_loop(0, nch, cast_rows, 0)

        @pl.when(i > 0)
        def _():
            out_wait(ic_ref[jnp.maximum(i - 1, 0)])

        def seed(m, carry):
            yacc[pl.ds(pl.multiple_of(m * ch, ch), ch), :] = jnp.broadcast_to(bdn_ref[0], (ch, yacc.shape[1]))
            return carry

        lax.fori_loop(0, nch, seed, 0)

    @pl.when(nch > 0)
    def _():
        wgu_b[...] = wgu_ref[0].astype(BF16)
        wdn_b[...] = wdn_ref[0].astype(BF16)
        bgu = bgu_ref[0]
        last = j == nj - 1

        def prefetch(m):
            first_idx = (j * nch + m) * per_body
            gather_rows(row0 + lead + first_idx, first_idx, per_body)

        def gate_up(m):
            rows = pl.ds(pl.multiple_of(m * ch, ch), ch)
            return _dot(xb[rows, :], wgu_b[...]) + bgu

        def finish(m, gu):
            rows = pl.ds(pl.multiple_of(m * ch, ch), ch)
            glu = jnp.minimum(gu, SWIGLU_LIMIT)
            f_glu = glu * _sigmoid(SWIGLU_ALPHA * glu)
            f_lin = jnp.clip(gu, -SWIGLU_LIMIT, SWIGLU_LIMIT) + 1.0
            prod = (pltpu.roll(f_glu, 1, 1) * f_lin).astype(BF16)
            parts = [_dot(prod[:, q * 2 * LANES:(q + 1) * 2 * LANES], sel_ref[...])
                     for q in range(MOE_TN // (2 * LANES))]
            act = jnp.concatenate(parts, axis=1).astype(BF16)
            yacc[rows, :] += _dot(act, wdn_b[...])

        def step(m):
            gu = gu_scr[...]
            gu_scr[...] = gate_up(m + 1)
            finish(m, gu)
            prefetch(m)

        gu_scr[...] = gate_up(0)
        n_steps = nch - 1

        def pair(p, carry):
            step(2 * p)
            step(2 * p + 1)
            return carry

        lax.fori_loop(0, lax.shift_right_logical(n_steps, 1), pair, 0)

        @pl.when(jnp.bitwise_and(n_steps, 1) == 1)
        def _():
            step(n_steps - 1)

        finish(nch - 1, gu_scr[...])
        prefetch(nch - 1)

        @pl.when(last)
        def _():
            def issue(m, carry):
                rows = pl.ds(pl.multiple_of(m * ch, ch), ch)
                dst = pl.ds(pl.multiple_of(row0 + m * ch, ch), ch)
                pltpu.make_async_copy(yacc.at[rows, :], y_hbm.at[dst, :], sems.at[1]).start()
                return carry

            lax.fori_loop(0, nch, issue, 0)

    @pl.when(jnp.logical_and(i == pl.num_programs(0) - 1, j == nj - 1))
    def _():
        out_wait(tail_ref[1])

        def drain(m, carry):
            chunk_wait(tail_ref[0] + m * ch)
            return carry

        lax.fori_loop(0, lead // ch, drain, 0)


def _experts(h2, tok_pad, item_e, item_row0, item_nch, item_valid, item_tail, w_gu, b_gu, w_dn, b_dn, n_rows):
    ne, d, two_de = w_gu.shape
    tn = MOE_TN
    nj = two_de // tn
    assert nj > 1, "the kernel separates its first and last column-tile steps"
    assert MOE_CHUNK % nj == 0 and (MOE_CHUNK // nj) % 8 == 0, "row copies per chunk stage"
    ni = item_e.shape[0]
    ring_rows = MOE_ITEM_ROWS
    sel = np.zeros((2 * LANES, LANES), np.float32)
    sel[2 * np.arange(LANES) + 1, np.arange(LANES)] = 1.0

    def jmap(i, j, iv):
        return jnp.where(iv[i] > 0, j, nj - 1)

    grid_spec = pltpu.PrefetchScalarGridSpec(
        num_scalar_prefetch=5,
        grid=(ni, nj),
        in_specs=[pl.BlockSpec(memory_space=pl.ANY),
                  pl.BlockSpec(memory_space=pl.ANY),
                  pl.BlockSpec((1, d, tn), lambda i, j, ie, ir, ic, iv, it: (ie[i], 0, jmap(i, j, iv))),
                  pl.BlockSpec((1, 1, tn), lambda i, j, ie, ir, ic, iv, it: (ie[i], 0, jmap(i, j, iv))),
                  pl.BlockSpec((1, tn // 2, d), lambda i, j, ie, ir, ic, iv, it: (ie[i], jmap(i, j, iv), 0)),
                  pl.BlockSpec((1, 1, d), lambda i, j, ie, ir, ic, iv, it: (ie[i], 0, 0)),
                  pl.BlockSpec((2 * LANES, LANES), lambda i, j, ie, ir, ic, iv, it: (0, 0))],
        out_specs=pl.BlockSpec(memory_space=pl.ANY),
        scratch_shapes=[pltpu.SMEM((MOE_ITEM_ROWS // LANES, LANES), I32),
                        pltpu.VMEM((ring_rows, d // 2), jnp.uint32),
                        pltpu.VMEM((MOE_ITEM_ROWS, d), BF16),
                        pltpu.VMEM((MOE_ITEM_ROWS, d), F32),
                        pltpu.VMEM((d, tn), BF16),
                        pltpu.VMEM((tn // 2, d), BF16),
                        pltpu.VMEM((MOE_CHUNK, tn), F32),
                        pltpu.SemaphoreType.DMA((ring_rows // MOE_CHUNK,)),
                        pltpu.SemaphoreType.DMA((2,))],
    )
    return pl.pallas_call(
        functools.partial(_expert_kernel, nj),
        grid_spec=grid_spec,
        out_shape=jax.ShapeDtypeStruct((n_rows, d), F32),
        compiler_params=_cparams(("arbitrary", "arbitrary"), MOE_VMEM_LIMIT),
        name="moe_experts",
    )(item_e, item_row0, item_nch, item_valid, item_tail, tok_pad, h2, w_gu, b_gu.reshape(ne, 1, two_de),
      w_dn, b_dn.reshape(ne, 1, d), jnp.asarray(sel, BF16))


def _combine_kernel(alpha, dest_ref, y_hbm, gate_ref, x1_ref, g2_ref, lg_ref, lb_ref, out_ref, buf, sem):
    tc = COMBINE_TOKENS

    group = 8

    def issue(g, carry):
        first = pl.multiple_of(g * group, group)
        for dr in range(group):
            for k in range(TOP_K):
                src = dest_ref[0, 0, (first + dr) * TOP_K + k]
                pltpu.make_async_copy(y_hbm.at[pl.ds(src, 1), :], buf.at[k, pl.ds(first + dr, 1), :],
                                      sem.at[0]).start()
        return carry

    lax.fori_loop(0, tc // group, issue, 0)
    for k in range(TOP_K):
        pltpu.make_async_copy(y_hbm.at[pl.ds(0, tc), :], buf.at[k], sem.at[0]).wait()

    gates = gate_ref[...]
    y = gates[:, 0:1] * buf[0]
    for k in range(1, TOP_K):
        y = y + gates[:, k:k + 1] * buf[k]
    z = alpha * x1_ref[...] + (1.0 + g2_ref[0]) * y
    out_ref[...] = _layer_norm(z, lg_ref[...], lb_ref[...])


def _combine(dest, y_pad, gates, x1, g2, ln_g, ln_b, alpha, seq):
    t, d = x1.shape
    tc = COMBINE_TOKENS
    per_b = seq // tc
    row = lambda i: (i, 0)
    const = lambda i: (0, 0)
    return pl.pallas_call(
        functools.partial(_combine_kernel, alpha),
        grid=(t // tc,),
        in_specs=[pl.BlockSpec((1, 1, tc * TOP_K), lambda i: (i, 0, 0), memory_space=pltpu.SMEM),
                  pl.BlockSpec(memory_space=pl.ANY),
                  pl.BlockSpec((tc, LANES), row), pl.BlockSpec((tc, d), row),
                  pl.BlockSpec((1, 1, d), lambda i: (i // per_b, 0, 0)),
                  pl.BlockSpec((1, d), const), pl.BlockSpec((1, d), const)],
        out_specs=pl.BlockSpec((tc, d), row),
        out_shape=jax.ShapeDtypeStruct((t, d), F32),
        scratch_shapes=[pltpu.VMEM((TOP_K, tc, d), F32), pltpu.SemaphoreType.DMA((1,))],
        compiler_params=_cparams(("arbitrary",)),
        name="moe_combine_ln",
    )(dest.reshape(t // tc, 1, tc * TOP_K), y_pad, gates, x1, g2, ln_g, ln_b)


def _count_le(ends, q):
    return jnp.sum((ends[None, :] <= q[:, None]).astype(I32), axis=1)


def _lookup(table, idx):
    hit = idx[:, None] == jnp.arange(table.shape[0], dtype=I32)
    return jnp.sum(jnp.where(hit, table[None, :], 0), axis=1)


def _routing_tables(top_idx):
    t = top_idx.shape[0]
    tk = t * TOP_K
    experts = jnp.arange(N_EXPERTS, dtype=I32)
    hits = [top_idx[:, k:k + 1] == experts for k in range(TOP_K)]
    onehot = sum(h.astype(I32) for h in hits)
    csum = jnp.cumsum(onehot, axis=0)
    counts = csum[-1]
    earlier = csum - onehot
    starts = jnp.cumsum(counts) - counts
    padded = ((counts + MOE_ROW_PAD - 1) // MOE_ROW_PAD) * MOE_ROW_PAD
    pad_end = jnp.cumsum(padded)
    pad_start = pad_end - padded
    dest = jnp.stack([jnp.sum(jnp.where(h, earlier + pad_start, 0), axis=1) for h in hits], axis=1).reshape(tk)

    n_rows = tk + N_EXPERTS * MOE_ROW_PAD
    n_tab = n_rows + 2 * MOE_ITEM_ROWS
    tok_sorted = (jnp.argsort(top_idx.reshape(tk)) // TOP_K).astype(I32)
    blk = jnp.arange(n_tab // MOE_ROW_PAD, dtype=I32) * MOE_ROW_PAD
    blk_e = jnp.minimum(_count_le(pad_end, blk), N_EXPERTS - 1)
    local = (blk - _lookup(pad_start, blk_e))[:, None] + jnp.arange(MOE_ROW_PAD, dtype=I32)
    src = jnp.clip(_lookup(starts, blk_e)[:, None] + local, 0, tk - 1)
    live = jnp.logical_and(local < _lookup(counts, blk_e)[:, None], (blk < pad_end[-1])[:, None])
    tok_pad = jnp.where(live, tok_sorted[src], 0).reshape(-1, LANES)

    items_per = (padded + MOE_ITEM_ROWS - 1) // MOE_ITEM_ROWS
    item_end = jnp.cumsum(items_per)
    item_start = item_end - items_per
    n_items = N_EXPERTS + n_rows // MOE_ITEM_ROWS
    idx = jnp.arange(n_items, dtype=I32)
    valid = idx < item_end[-1]
    e_i = jnp.minimum(_count_le(item_end, idx), N_EXPERTS - 1)
    k_i = idx - _lookup(item_start, e_i)
    row0 = _lookup(pad_start, e_i) + k_i * MOE_ITEM_ROWS
    nrows = jnp.clip(_lookup(padded, e_i) - k_i * MOE_ITEM_ROWS, 0, MOE_ITEM_ROWS)
    e_last = jnp.sum(jnp.where(idx == item_end[-1] - 1, e_i, 0))
    item_e = jnp.where(valid, e_i, e_last).astype(I32)
    item_row0 = jnp.where(valid, row0, 0).astype(I32)
    item_nch = jnp.where(valid, nrows // MOE_CHUNK, 0).astype(I32)
    item_tail = jnp.stack([pad_end[-1], jnp.sum(jnp.where(idx == item_end[-1] - 1, item_nch, 0))]).astype(I32)
    return tok_pad, dest, item_e, item_row0, item_nch, valid.astype(I32), item_tail, n_rows


def kernel(x, c, w_ada, b_ada, w_in, b_in, conv_w, conv_b, m_norm_g, a_norm_g, w_out, ln1_g, ln1_b,
           router_w, router_b, w_gu, b_gu, w_dn, b_dn, ln2_g, ln2_b):
    bsz, seq, d = x.shape
    depth = w_ada.shape[0]
    t = bsz * seq
    alpha = float((2 * depth) ** 0.25)
    qk_w = 2 * M_HEADS * M_DQK
    mv_w = M_HEADS * M_DV
    aw = A_HEADS * A_DH
    gate_lo = qk_w + 2 * mv_w
    gate_hi = gate_lo + 2 * M_HEADS

    x2 = x.reshape(t, d)
    for l in range(depth):
        mod = _ada_mod(c, w_ada[l], b_ada[l]).reshape(bsz, 6, 1, d)
        sh1, sc1, g1, sh2, sc2, g2 = (mod[:, i] for i in range(6))

        w_main = jnp.concatenate([w_in[l][:, :gate_lo], w_in[l][:, gate_hi:]], axis=1).astype(BF16)
        b_main = jnp.concatenate([b_in[l][:gate_lo], b_in[l][gate_hi:]]).reshape(1, -1)
        w_gate = jnp.zeros((d, LANES), BF16).at[:, :2 * M_HEADS].set(w_in[l][:, gate_lo:gate_hi].astype(BF16))
        b_gate = jnp.zeros((1, LANES), F32).at[0, :2 * M_HEADS].set(b_in[l][gate_lo:gate_hi])
        attn_col0 = gate_lo // aw
        proj, gates_c, *qkv_dil = _in_proj(x2, sc1, sh1, w_main, b_main, w_gate, b_gate, seq, attn_col0)

        gates_r = gates_c[:, :2 * M_HEADS].reshape(bsz, seq, 2 * M_HEADS).transpose(0, 2, 1)
        y_m = _mlstm(proj, gates_c, gates_r, conv_w[l], conv_b[l].reshape(1, -1),
                     m_norm_g[l].reshape(1, -1), bsz, seq)

        outs, lses = [], []
        for dil in DILATIONS:
            if dil == 1:
                o_d, lse_d = _attn_group(proj.reshape(bsz, 1, seq, -1), dil, attn_col0)
            else:
                o_d, lse_d = _attn_group(qkv_dil[DILATIONS.index(dil) - 1], dil, 0)
            outs.append(o_d)
            lses.append(lse_d)

        rw = jnp.zeros((d, LANES), BF16).at[:, :N_EXPERTS].set(router_w[l].astype(BF16))
        rb = jnp.zeros((1, LANES), F32).at[0, :N_EXPERTS].set(router_b[l])
        x1, h2, top_idx, gates = _out_proj(y_m, outs, lses, a_norm_g[l].reshape(1, -1), w_out[l].astype(BF16),
                                           x2, g1, sc2, sh2, ln1_g[l].reshape(1, -1), ln1_b[l].reshape(1, -1),
                                           rw, rb, alpha, seq)

        (tok_pad, dest, item_e, item_row0, item_nch, item_valid, item_tail,
         n_rows) = _routing_tables(top_idx[:, :TOP_K])
        y_pad = _experts(h2, tok_pad, item_e, item_row0, item_nch, item_valid, item_tail,
                         w_gu[l], b_gu[l], w_dn[l], b_dn[l], n_rows)
        x2 = _combine(dest, y_pad, gates, x1, g2, ln2_g[l].reshape(1, -1), ln2_b[l].reshape(1, -1), alpha, seq)
    return x2.reshape(bsz, seq, d)
```

```python
import functools

import jax
import jax.numpy as jnp
import numpy as np
from jax import lax
from jax.experimental import pallas as pl
from jax.experimental.pallas import tpu as pltpu

F32 = jnp.float32
BF16 = jnp.bfloat16
I32 = jnp.int32

M_HEADS = 4
M_DQK = 128
M_DV = 256
CONV_WIDTH = 4
A_HEADS = 16
A_DH = 64
ATTN_BLOCK = 128
DILATIONS = (1, 4, 16)
N_EXPERTS = 32
TOP_K = 4
SWIGLU_ALPHA = 1.702
SWIGLU_LIMIT = 7.0
EPS = 1e-5

LANES = 128
VMEM_LIMIT = 56 * 1024 * 1024

MLSTM_CHUNK = 256
MOE_ROW_PAD = 256
MOE_CHUNK = 256
MOE_ITEM_ROWS = 1280
MOE_TN = 1024
MOE_VMEM_LIMIT = 60 * 1024 * 1024
COMBINE_TOKENS = 512
DMA_UNROLL = 8


def _cparams(sem, vmem=VMEM_LIMIT):
    return pltpu.CompilerParams(dimension_semantics=sem, vmem_limit_bytes=vmem)


def _sigmoid(x):
    return 1.0 / (1.0 + jnp.exp(-x))


def _log_sigmoid(x):
    return jnp.minimum(x, 0.0) - jnp.log(1.0 + jnp.exp(-jnp.abs(x)))


def _layer_norm(z, g, b):
    mu = jnp.mean(z, axis=-1, keepdims=True)
    zc = z - mu
    var = jnp.mean(zc * zc, axis=-1, keepdims=True)
    return zc * lax.rsqrt(var + EPS) * g + b


def _dot(a, b):
    return jnp.dot(a, b, preferred_element_type=F32)


def _dot_nt(a, b):
    return lax.dot_general(a, b, (((1,), (1,)), ((), ())), preferred_element_type=F32)


HIGH_HALF = np.uint32(0xFFFF0000)


def _pack_bf16_pairs(x):
    half = x.shape[1] // 2
    xb = x.astype(BF16).astype(F32)
    lo = lax.bitcast_convert_type(xb[:, :half], jnp.uint32)
    hi = lax.bitcast_convert_type(xb[:, half:], jnp.uint32)
    return jnp.bitwise_or(lax.shift_right_logical(lo, jnp.uint32(16)), jnp.bitwise_and(hi, HIGH_HALF))


def _unpack_bf16_pairs(w):
    lo = lax.bitcast_convert_type(lax.shift_left(w, jnp.uint32(16)), F32)
    hi = lax.bitcast_convert_type(jnp.bitwise_and(w, HIGH_HALF), F32)
    return lo.astype(BF16), hi.astype(BF16)


def _dot_hilo(a, sel):
    hi = a.astype(BF16)
    lo = (a - hi.astype(F32)).astype(BF16)
    return _dot(hi, sel) + _dot(lo, sel)


def _ada_kernel(c_ref, w_ref, b_ref, o_ref):
    c = c_ref[...]
    cond = c * _sigmoid(c)
    o_ref[...] = _dot(cond.astype(BF16), w_ref[...].astype(BF16)) + b_ref[...]


def _ada_mod(c, w_ada, b_ada):
    bsz, d = c.shape
    n = w_ada.shape[1]
    tn = 1024
    rows = 8
    c_pad = jnp.zeros((rows, d), F32).at[:bsz].set(c)
    out = pl.pallas_call(
        _ada_kernel,
        grid=(n // tn,),
        in_specs=[pl.BlockSpec((rows, d), lambda j: (0, 0)),
                  pl.BlockSpec((d, tn), lambda j: (0, j)),
                  pl.BlockSpec((1, tn), lambda j: (0, j))],
        out_specs=pl.BlockSpec((rows, tn), lambda j: (0, j)),
        out_shape=jax.ShapeDtypeStruct((rows, n), F32),
        compiler_params=_cparams(("arbitrary",)),
        name="ada_mod",
    )(c_pad, w_ada, b_ada.reshape(1, n))
    return out[:bsz]


def _inproj_kernel(attn_col0, x_ref, sc_ref, sh_ref, w_ref, b_ref, wg_ref, bg_ref, o_ref, g_ref, *rest):
    dil_refs, (h_ref, r_scr) = rest[:-2], rest[-2:]
    j = pl.program_id(1)

    @pl.when(j == 0)
    def _():
        h = x_ref[...] * (1.0 + sc_ref[0]) + sh_ref[0]
        hb = h.astype(BF16)
        h_ref[...] = hb
        g_ref[...] = _dot(hb, wg_ref[...]) + bg_ref[...]

    res = _dot(h_ref[...], w_ref[...]) + b_ref[...]
    o_ref[...] = res.astype(BF16)

    @pl.when(j >= attn_col0)
    def _():
        cols = res.shape[1] // LANES
        for c in range(cols):
            r_scr[c] = res[:, c * LANES:(c + 1) * LANES]
        for ref in dil_refs:
            d, n = ref.shape[1], ref.shape[2]
            for r in range(d):
                for c in range(cols):
                    ref[0, r, :, c * LANES:(c + 1) * LANES] = r_scr[c, pl.ds(r, n, stride=d), :].astype(BF16)


def _in_proj(x2, sc, sh, w_main, b_main, w_gate, b_gate, seq, attn_col0):
    t, d = x2.shape
    n = w_main.shape[1]
    tm, tn = 512, 1024
    per_b = seq // tm
    bsz = t // seq
    dils = [dl for dl in DILATIONS if dl > 1]
    aw3 = n - attn_col0 * tn
    dil_specs = [pl.BlockSpec((1, dl, tm // dl, tn),
                              lambda i, j: (i // per_b, 0, i % per_b, jnp.maximum(j - attn_col0, 0))) for dl in dils]
    dil_shapes = [jax.ShapeDtypeStruct((bsz, dl, seq // dl, aw3), BF16) for dl in dils]
    return pl.pallas_call(
        functools.partial(_inproj_kernel, attn_col0),
        grid=(t // tm, n // tn),
        in_specs=[pl.BlockSpec((tm, d), lambda i, j: (i, 0)),
                  pl.BlockSpec((1, 1, d), lambda i, j: (i // per_b, 0, 0)),
                  pl.BlockSpec((1, 1, d), lambda i, j: (i // per_b, 0, 0)),
                  pl.BlockSpec((d, tn), lambda i, j: (0, j)),
                  pl.BlockSpec((1, tn), lambda i, j: (0, j)),
                  pl.BlockSpec((d, LANES), lambda i, j: (0, 0)),
                  pl.BlockSpec((1, LANES), lambda i, j: (0, 0))],
        out_specs=[pl.BlockSpec((tm, tn), lambda i, j: (i, j)),
                   pl.BlockSpec((tm, LANES), lambda i, j: (i, 0))] + dil_specs,
        out_shape=[jax.ShapeDtypeStruct((t, n), BF16),
                   jax.ShapeDtypeStruct((t, LANES), F32)] + dil_shapes,
        scratch_shapes=[pltpu.VMEM((tm, d), BF16), pltpu.VMEM((tn // LANES, tm, LANES), F32)],
        compiler_params=_cparams(("arbitrary", "arbitrary")),
        name="in_proj",
    )(x2, sc, sh, w_main, b_main, w_gate, b_gate)


def _mlstm_kernel(qk_ref, v_ref, o_ref, gc_ref, gr_ref, cw_ref, cb_ref, ng_ref, out_ref,
                  ct_ref, n_ref, m_ref, prev_ref):
    c = pl.program_id(1)
    L = MLSTM_CHUNK

    @pl.when(c == 0)
    def _():
        ct_ref[...] = jnp.zeros_like(ct_ref)
        n_ref[...] = jnp.zeros_like(n_ref)
        m_ref[...] = jnp.zeros_like(m_ref)
        prev_ref[...] = jnp.zeros_like(prev_ref)

    x = qk_ref[...].astype(F32)
    prev = prev_ref[...]
    row = lax.broadcasted_iota(I32, (L, 1), 0)
    y = cw_ref[CONV_WIDTH - 1:CONV_WIDTH, :] * x + cb_ref[...]
    for k in range(1, CONV_WIDTH):
        xs = jnp.where(row < k, pltpu.roll(prev, k, 0), pltpu.roll(x, k, 0))
        y = y + cw_ref[CONV_WIDTH - 1 - k:CONV_WIDTH - k, :] * xs
    prev_ref[...] = x
    y = y * _sigmoid(y)

    gc = gc_ref[...]
    gr = gr_ref[0]
    ti = lax.broadcasted_iota(I32, (L, L), 0)
    si = lax.broadcasted_iota(I32, (L, L), 1)
    causal = si <= ti
    tril = causal.astype(F32)
    triu = (ti <= si).astype(F32)
    b_cols = jnp.dot(tril, _log_sigmoid(gc), precision=lax.Precision.HIGHEST, preferred_element_type=F32)
    b_rows = jnp.dot(_log_sigmoid(gr), triu, precision=lax.Precision.HIGHEST, preferred_element_type=F32)

    qk_w = M_HEADS * M_DQK
    heads = range(M_HEADS)
    qf = [y[:, h * M_DQK:(h + 1) * M_DQK] for h in heads]
    kf = [y[:, qk_w + h * M_DQK:qk_w + (h + 1) * M_DQK] * (M_DQK ** -0.5) for h in heads]
    qb = [t.astype(BF16) for t in qf]
    vb = [v_ref[:, h * M_DV:(h + 1) * M_DV] for h in heads]
    bc = [b_cols[:, M_HEADS + h:M_HEADS + h + 1] for h in heads]
    ic = [gc[:, h:h + 1] for h in heads]
    br = [b_rows[M_HEADS + h:M_HEADS + h + 1, :] for h in heads]
    ir = [gr[h:h + 1, :] for h in heads]
    m_prev = [m_ref[h][0:1, 0:1] for h in heads]
    n_prev = [n_ref[h][0:1, :] for h in heads]

    b_last = [bc[h][L - 1:L, :] for h in heads]
    g_col = [b_last[h] - bc[h] + ic[h] for h in heads]
    m_new = [jnp.maximum(b_last[h] + m_prev[h], jnp.max(g_col[h], axis=0, keepdims=True)) for h in heads]
    w_col = [jnp.exp(g_col[h] - m_new[h]) for h in heads]
    decay = [jnp.exp(b_last[h] + m_prev[h] - m_new[h]) for h in heads]

    qk = [_dot_nt(qb[h], kf[h].astype(BF16)) for h in heads]
    q_state = [_dot(qb[h], ct_ref[h].astype(BF16)) for h in heads]
    kv_new = [_dot(kf[h].T.astype(BF16), (w_col[h] * vb[h].astype(F32)).astype(BF16)) for h in heads]

    dm = [jnp.where(causal, bc[h] - br[h] + ir[h], -jnp.inf) for h in heads]
    inter = [bc[h] + m_prev[h] for h in heads]
    mt = [jnp.maximum(inter[h], jnp.max(dm[h], axis=1, keepdims=True)) for h in heads]
    a = [jnp.exp(dm[h] - mt[h]) * qk[h] for h in heads]
    e_int = [jnp.exp(inter[h] - mt[h]) for h in heads]
    av = [_dot(a[h].astype(BF16), vb[h]) for h in heads]

    for h in heads:
        num = av[h] + e_int[h] * q_state[h]
        den = (jnp.sum(a[h], axis=1, keepdims=True)
               + e_int[h] * jnp.sum(qf[h] * n_prev[h], axis=1, keepdims=True))
        hh = num / jnp.maximum(jnp.abs(den), jnp.exp(-mt[h]))

        ct_ref[h] = decay[h] * ct_ref[h] + kv_new[h]
        n_new = decay[h] * n_prev[h] + jnp.sum(w_col[h] * kf[h], axis=0, keepdims=True)
        n_ref[h] = jnp.broadcast_to(n_new, n_ref.shape[1:])
        m_ref[h] = jnp.broadcast_to(m_new[h], m_ref.shape[1:])

        ms = jnp.mean(hh * hh, axis=1, keepdims=True)
        og = o_ref[:, h * M_DV:(h + 1) * M_DV].astype(F32)
        yh = hh * lax.rsqrt(ms + EPS) * ng_ref[:, h * M_DV:(h + 1) * M_DV] * _sigmoid(og)
        out_ref[:, h * M_DV:(h + 1) * M_DV] = yh.astype(BF16)


def _mlstm(proj, gates_c, gates_r, conv_w, conv_b, norm_g, bsz, seq):
    L = MLSTM_CHUNK
    nc = seq // L
    t = bsz * seq
    w = M_HEADS * M_DV
    return pl.pallas_call(
        _mlstm_kernel,
        grid=(bsz, nc),
        in_specs=[pl.BlockSpec((L, w), lambda b, c: (b * nc + c, 0)),
                  pl.BlockSpec((L, w), lambda b, c: (b * nc + c, 1)),
                  pl.BlockSpec((L, w), lambda b, c: (b * nc + c, 2)),
                  pl.BlockSpec((L, LANES), lambda b, c: (b * nc + c, 0)),
                  pl.BlockSpec((1, 8, L), lambda b, c: (b, 0, c)),
                  pl.BlockSpec((CONV_WIDTH, w), lambda b, c: (0, 0)),
                  pl.BlockSpec((1, w), lambda b, c: (0, 0)),
                  pl.BlockSpec((1, w), lambda b, c: (0, 0))],
        out_specs=pl.BlockSpec((L, w), lambda b, c: (b * nc + c, 0)),
        out_shape=jax.ShapeDtypeStruct((t, w), BF16),
        scratch_shapes=[pltpu.VMEM((M_HEADS, M_DQK, M_DV), F32),
                        pltpu.VMEM((M_HEADS, 8, M_DQK), F32),
                        pltpu.VMEM((M_HEADS, 8, LANES), F32),
                        pltpu.VMEM((L, w), F32)],
        compiler_params=_cparams(("arbitrary", "arbitrary")),
        name="mlstm",
    )(proj, proj, proj, gates_c, gates_r, conv_w, conv_b, norm_g)


def _attn_kernel(dilation, has_prev, *refs):
    nq = ATTN_BLOCK
    if has_prev:
        q_ref, kc_ref, vc_ref, kp_ref, vp_ref, o_ref, lse_ref, k_all, v_all = refs
        k_all[0:nq, :] = kp_ref[0, 0]
        k_all[nq:2 * nq, :] = kc_ref[0, 0]
        v_all[0:nq, :] = vp_ref[0, 0]
        v_all[nq:2 * nq, :] = vc_ref[0, 0]
        nk = 2 * nq
    else:
        q_ref, k_all, v_all, o_ref, lse_ref = refs
        k_all, v_all = k_all.at[0, 0], v_all.at[0, 0]
        nk = nq
    n = pl.program_id(2)
    qi = lax.broadcasted_iota(I32, (nq, nk), 0)
    ki = lax.broadcasted_iota(I32, (nq, nk), 1)
    dist = qi - ki + (nk - nq)
    ok = jnp.logical_and(dist >= 0, dist <= nq)
    if has_prev:
        ok = jnp.logical_and(ok, jnp.logical_or(ki >= nq, n > 0))
    dist_f = jnp.where(ok, dist.astype(F32), jnp.inf)
    lane = lax.broadcasted_iota(I32, (nq, LANES), 1)
    left_q = lane < A_DH
    left_k = lax.broadcasted_iota(I32, (nk, LANES), 1) < A_DH
    n_pairs = A_HEADS // 2

    scores = []
    for p in range(n_pairs):
        cols = slice(p * LANES, (p + 1) * LANES)
        qp = q_ref[0, 0, :, cols] * (A_DH ** -0.5)
        kp = k_all[:, cols]
        zero = jnp.zeros_like(qp)
        scores.append(_dot_nt(jnp.where(left_q, qp, zero), kp))
        scores.append(_dot_nt(jnp.where(left_q, zero, qp), kp))
    probs, maxes = [], []
    for h in range(A_HEADS):
        coef = -(2.0 ** (-8.0 * (h + 1) / A_HEADS)) * dilation
        s = scores[h] + dist_f * coef
        m = jnp.max(s, axis=1, keepdims=True)
        probs.append(jnp.exp(s - m).astype(BF16))
        maxes.append(m)
    lse_tile = jnp.zeros((nq, LANES), F32)
    for p in range(n_pairs):
        cols = slice(p * LANES, (p + 1) * LANES)
        vp = v_all[:, cols]
        one = jnp.ones_like(vp)
        pv_e = _dot(probs[2 * p], jnp.where(left_k, vp, one))
        pv_o = _dot(probs[2 * p + 1], jnp.where(left_k, one, vp))
        num = jnp.where(left_q, pv_e, pv_o)
        den = pltpu.roll(jnp.where(left_q, pv_o, pv_e), A_DH, 1)
        o_ref[0, 0, :, cols] = (num / den).astype(BF16)
        lse_tile = jnp.where(lane == 2 * p, maxes[2 * p] + jnp.log(pv_e[:, A_DH:A_DH + 1]), lse_tile)
        lse_tile = jnp.where(lane == 2 * p + 1, maxes[2 * p + 1] + jnp.log(pv_o[:, 0:1]), lse_tile)
    lse_ref[0, 0] = lse_tile


def _attn_group(qkv, dilation, col0):
    bsz, d, ls, _ = qkv.shape
    aw = A_HEADS * A_DH
    nq = ATTN_BLOCK
    nb = ls // nq
    has_prev = nb > 1
    blk = (1, 1, nq, aw)
    in_specs = [pl.BlockSpec(blk, lambda b, r, n: (b, r, n, col0)),
                pl.BlockSpec(blk, lambda b, r, n: (b, r, n, col0 + 1)),
                pl.BlockSpec(blk, lambda b, r, n: (b, r, n, col0 + 2))]
    args = [qkv, qkv, qkv]
    if has_prev:
        in_specs += [pl.BlockSpec(blk, lambda b, r, n: (b, r, jnp.maximum(n - 1, 0), col0 + 1)),
                     pl.BlockSpec(blk, lambda b, r, n: (b, r, jnp.maximum(n - 1, 0), col0 + 2))]
        args += [qkv, qkv]
    return pl.pallas_call(
        functools.partial(_attn_kernel, dilation, has_prev),
        grid=(bsz, d, nb),
        in_specs=in_specs,
        out_specs=[pl.BlockSpec(blk, lambda b, r, n: (b, r, n, 0)),
                   pl.BlockSpec((1, 1, nq, LANES), lambda b, r, n: (b, r, n, 0))],
        out_shape=[jax.ShapeDtypeStruct((bsz, d, ls, aw), BF16),
                   jax.ShapeDtypeStruct((bsz, d, ls, LANES), F32)],
        scratch_shapes=[pltpu.VMEM((2 * nq, aw), BF16)] * 2 if has_prev else [],
        compiler_params=_cparams(("arbitrary", "arbitrary", "arbitrary")),
        name=f"dilated_attn_d{dilation}",
    )(*args)


def _head_maps(n_heads, dh):
    w = n_heads * dh
    e = np.zeros((LANES, w), np.float32)
    for h in range(n_heads):
        e[h, h * dh:(h + 1) * dh] = 1.0
    return jnp.asarray(e, BF16), jnp.asarray(e.T.copy(), BF16)


def _natural_rows(ref, scr):
    d, n, w = ref.shape[1:]
    if d == 1:
        return ref[0, 0].astype(F32)
    cols = w // LANES
    for r in range(d):
        blk = ref[0, r].astype(F32)
        for c in range(cols):
            scr[c, pl.ds(r, n, stride=d), :] = blk[:, c * LANES:(c + 1) * LANES]
    return jnp.concatenate([scr[c] for c in range(cols)], axis=1)


def _merged_heads(o_refs, l_refs, g_ref, e_ref, p_ref, o_scr, l_scr):
    l1, l2, l3 = (_natural_rows(ref, l_scr.at[g]) for g, ref in enumerate(l_refs))
    mx = jnp.maximum(jnp.maximum(l1, l2), l3)
    w1, w2, w3 = jnp.exp(l1 - mx), jnp.exp(l2 - mx), jnp.exp(l3 - mx)
    inv = 1.0 / (w1 + w2 + w3)
    e = e_ref[...]
    o = (_dot_hilo(w1 * inv, e) * _natural_rows(o_refs[0], o_scr.at[0])
         + _dot_hilo(w2 * inv, e) * _natural_rows(o_refs[1], o_scr.at[1])
         + _dot_hilo(w3 * inv, e) * _natural_rows(o_refs[2], o_scr.at[2]))
    ms = _dot_hilo(o * o, p_ref[...]) * (1.0 / A_DH)
    scale = _dot_hilo(lax.rsqrt(ms + EPS), e)
    return (o * scale * g_ref[...]).astype(BF16)


def _outproj_kernel(alpha, ym_ref, o1_ref, o2_ref, o3_ref, l1_ref, l2_ref, l3_ref, ag_ref, e_ref, p_ref,
                    w_ref, x_ref, g1_ref, sc_ref, sh_ref, lg_ref, lb_ref, rw_ref, rb_ref,
                    x1_ref, h2_ref, ti_ref, tg_ref, o_scr, l_scr):
    half = ym_ref.shape[1]
    y_a = _merged_heads((o1_ref, o2_ref, o3_ref), (l1_ref, l2_ref, l3_ref), ag_ref, e_ref, p_ref, o_scr, l_scr)
    y = _dot(ym_ref[...], w_ref[0:half, :]) + _dot(y_a, w_ref[half:2 * half, :])
    z = alpha * x_ref[...] + (1.0 + g1_ref[0]) * y
    x1 = _layer_norm(z, lg_ref[...], lb_ref[...])
    x1_ref[...] = x1
    h2 = x1 * (1.0 + sc_ref[0]) + sh_ref[0]
    h2_ref[...] = _pack_bf16_pairs(h2)
    logits = _dot(h2.astype(BF16), rw_ref[...]) + rb_ref[...]
    lane = lax.broadcasted_iota(I32, logits.shape, 1)
    lane_f = lane.astype(F32)
    work = jnp.where(lane < N_EXPERTS, logits, -jnp.inf)
    idx_tile = jnp.zeros(logits.shape, F32)
    val_tile = jnp.zeros(logits.shape, F32)
    top = None
    denom = None
    for k in range(TOP_K):
        mk = jnp.max(work, axis=1, keepdims=True)
        ik = jnp.min(jnp.where(work == mk, lane_f, float(LANES)), axis=1, keepdims=True)
        work = jnp.where(lane_f == ik, -jnp.inf, work)
        if k == 0:
            top = mk
        ek = jnp.exp(mk - top)
        denom = ek if k == 0 else denom + ek
        idx_tile = jnp.where(lane == k, ik, idx_tile)
        val_tile = jnp.where(lane == k, ek, val_tile)
    ti_ref[...] = idx_tile.astype(I32)
    tg_ref[...] = val_tile / denom


def _out_proj(y_m, outs, lses, norm_g, w_out, x2, g1, sc2, sh2, ln_g, ln_b, rw, rb, alpha, seq):
    t, d = x2.shape
    half = y_m.shape[1]
    aw = outs[0].shape[3]
    tm = 256
    per_b = seq // tm
    row = lambda i: (i, 0)
    const = lambda i: (0, 0)
    mod = lambda i: (i // per_b, 0, 0)
    expand, pool = _head_maps(A_HEADS, A_DH)

    def grouped(arr):
        dl, w = arr.shape[1], arr.shape[3]
        return pl.BlockSpec((1, dl, tm // dl, w), lambda i: (i // per_b, 0, i % per_b, 0))

    return pl.pallas_call(
        functools.partial(_outproj_kernel, alpha),
        grid=(t // tm,),
        in_specs=[pl.BlockSpec((tm, half), row)] + [grouped(a) for a in outs] + [grouped(a) for a in lses]
        + [pl.BlockSpec((1, aw), const), pl.BlockSpec((LANES, aw), const), pl.BlockSpec((aw, LANES), const),
                  pl.BlockSpec((2 * half, d), const), pl.BlockSpec((tm, d), row),
                  pl.BlockSpec((1, 1, d), mod), pl.BlockSpec((1, 1, d), mod), pl.BlockSpec((1, 1, d), mod),
                  pl.BlockSpec((1, d), const), pl.BlockSpec((1, d), const),
                  pl.BlockSpec((d, LANES), const), pl.BlockSpec((1, LANES), const)],
        out_specs=[pl.BlockSpec((tm, d), row), pl.BlockSpec((tm, d // 2), row),
                   pl.BlockSpec((tm, LANES), row), pl.BlockSpec((tm, LANES), row)],
        out_shape=[jax.ShapeDtypeStruct((t, d), F32), jax.ShapeDtypeStruct((t, d // 2), jnp.uint32),
                   jax.ShapeDtypeStruct((t, LANES), I32), jax.ShapeDtypeStruct((t, LANES), F32)],
        scratch_shapes=[pltpu.VMEM((len(outs), aw // LANES, tm, LANES), F32),
                        pltpu.VMEM((len(lses), 1, tm, LANES), F32)],
        compiler_params=_cparams(("arbitrary",)),
        name="out_proj_ln_router",
    )(y_m, *outs, *lses, norm_g, expand, pool, w_out, x2, g1, sc2, sh2, ln_g, ln_b, rw, rb)


def _expert_kernel(nj, ie_ref, ir_ref, ic_ref, iv_ref, tail_ref, tok_hbm, h_hbm, wgu_ref, bgu_ref, wdn_ref,
                   bdn_ref, sel_ref, y_hbm, tok_smem, ring, xb, yacc, wgu_b, wdn_b, gu_scr, gsem, sems):
    i = pl.program_id(0)
    j = pl.program_id(1)
    row0 = ir_ref[i]
    nch = ic_ref[i]
    ch = MOE_CHUNK
    lead = MOE_ITEM_ROWS
    ring_chunks = ring.shape[0] // ch
    per_body = ch // nj

    def chunk_slot(first_row):
        return lax.rem(lax.shift_right_logical(first_row, 8), ring_chunks)

    def load_tokens(first_row):
        rows = pl.ds(lax.shift_right_logical(first_row, 7), lead // LANES)
        cp = pltpu.make_async_copy(tok_hbm.at[rows, :], tok_smem, sems.at[0])
        cp.start()
        cp.wait()

    def gather_rows(first_row, first_idx, count):
        slot = chunk_slot(first_row)
        base = slot * ch + jnp.bitwise_and(first_row, ch - 1)
        for u in range(count):
            idx = first_idx + u
            t = tok_smem[lax.shift_right_logical(idx, 7), jnp.bitwise_and(idx, LANES - 1)]
            pltpu.make_async_copy(h_hbm.at[pl.ds(t, 1), :], ring.at[pl.ds(base + u, 1), :], gsem.at[slot]).start()

    def chunk_wait(first_row):
        slot = chunk_slot(first_row)
        rows = pl.ds(pl.multiple_of(slot * ch, ch), ch)
        pltpu.make_async_copy(h_hbm.at[pl.ds(0, ch), :], ring.at[rows, :], gsem.at[slot]).wait()
        return rows

    @pl.when(jnp.logical_and(i == 0, j == 0))
    def _():
        load_tokens(0)

        def issue(g, carry):
            gather_rows(g * DMA_UNROLL, g * DMA_UNROLL, DMA_UNROLL)
            return carry

        lax.fori_loop(0, lead // DMA_UNROLL, issue, 0)

    def out_wait(count):
        def drain(m, carry):
            pltpu.make_async_copy(yacc.at[pl.ds(0, ch), :], y_hbm.at[pl.ds(0, ch), :], sems.at[1]).wait()
            return carry

        lax.fori_loop(0, count, drain, 0)

    @pl.when(jnp.logical_and(j == 0, nch > 0))
    def _():
        load_tokens(row0 + lead)

        def cast_rows(m, carry):
            src = chunk_wait(row0 + m * ch)
            rows = pl.ds(pl.multiple_of(m * ch, ch), ch)
            half = ring.shape[1]
            xb[rows, 0:half], xb[rows, half:2 * half] = _unpack_bf16_pairs(ring[src, :])
            return carry

        lax.fori_loop(0, nch, cast_rows, 0)

        @pl.when(i > 0)
        def _():
            out_wait(ic_ref[jnp.maximum(i - 1, 0)])

        def seed(m, carry):
            yacc[pl.ds(pl.multiple_of(m * ch, ch), ch), :] = jnp.broadcast_to(bdn_ref[0], (ch, yacc.shape[1]))
            return carry

        lax.fori_loop(0, nch, seed, 0)

    @pl.when(nch > 0)
    def _():
        wgu_b[...] = wgu_ref[0].astype(BF16)
        wdn_b[...] = wdn_ref[0].astype(BF16)
        bgu = bgu_ref[0]
        last = j == nj - 1

        def prefetch(m):
            first_idx = (j * nch + m) * per_body
            gather_rows(row0 + lead + first_idx, first_idx, per_body)

        def gate_up(m):
            rows = pl.ds(pl.multiple_of(m * ch, ch), ch)
            return _dot(xb[rows, :], wgu_b[...]) + bgu

        def finish(m, gu):
            rows = pl.ds(pl.multiple_of(m * ch, ch), ch)
            glu = jnp.minimum(gu, SWIGLU_LIMIT)
            f_glu = glu * _sigmoid(SWIGLU_ALPHA * glu)
            f_lin = jnp.clip(gu, -SWIGLU_LIMIT, SWIGLU_LIMIT) + 1.0
            prod = (pltpu.roll(f_glu, 1, 1) * f_lin).astype(BF16)
            parts = [_dot(prod[:, q * 2 * LANES:(q + 1) * 2 * LANES], sel_ref[...])
                     for q in range(MOE_TN // (2 * LANES))]
            act = jnp.concatenate(parts, axis=1).astype(BF16)
            yacc[rows, :] += _dot(act, wdn_b[...])

        def step(m):
            gu = gu_scr[...]
            gu_scr[...] = gate_up(m + 1)
            finish(m, gu)
            prefetch(m)

        gu_scr[...] = gate_up(0)
        n_steps = nch - 1

        def pair(p, carry):
            step(2 * p)
            step(2 * p + 1)
            return carry

        lax.fori_loop(0, lax.shift_right_logical(n_steps, 1), pair, 0)

        @pl.when(jnp.bitwise_and(n_steps, 1) == 1)
        def _():
            step(n_steps - 1)

        finish(nch - 1, gu_scr[...])
        prefetch(nch - 1)

        @pl.when(last)
        def _():
            def issue(m, carry):
                rows = pl.ds(pl.multiple_of(m * ch, ch), ch)
                dst = pl.ds(pl.multiple_of(row0 + m * ch, ch), ch)
                pltpu.make_async_copy(yacc.at[rows, :], y_hbm.at[dst, :], sems.at[1]).start()
                return carry

            lax.fori_loop(0, nch, issue, 0)

    @pl.when(jnp.logical_and(i == pl.num_programs(0) - 1, j == nj - 1))
    def _():
        out_wait(tail_ref[1])

        def drain(m, carry):
            chunk_wait(tail_ref[0] + m * ch)
            return carry

        lax.fori_loop(0, lead // ch, drain, 0)


def _experts(h2, tok_pad, item_e, item_row0, item_nch, item_valid, item_tail, w_gu, b_gu, w_dn, b_dn, n_rows):
    ne, d, two_de = w_gu.shape
    tn = MOE_TN
    nj = two_de // tn
    assert nj > 1, "the kernel separates its first and last column-tile steps"
    assert MOE_CHUNK % nj == 0 and (MOE_CHUNK // nj) % 8 == 0, "row copies per chunk stage"
    ni = item_e.shape[0]
    ring_rows = MOE_ITEM_ROWS
    sel = np.zeros((2 * LANES, LANES), np.float32)
    sel[2 * np.arange(LANES) + 1, np.arange(LANES)] = 1.0

    def jmap(i, j, iv):
        return jnp.where(iv[i] > 0, j, nj - 1)

    grid_spec = pltpu.PrefetchScalarGridSpec(
        num_scalar_prefetch=5,
        grid=(ni, nj),
        in_specs=[pl.BlockSpec(memory_space=pl.ANY),
                  pl.BlockSpec(memory_space=pl.ANY),
                  pl.BlockSpec((1, d, tn), lambda i, j, ie, ir, ic, iv, it: (ie[i], 0, jmap(i, j, iv))),
                  pl.BlockSpec((1, 1, tn), lambda i, j, ie, ir, ic, iv, it: (ie[i], 0, jmap(i, j, iv))),
                  pl.BlockSpec((1, tn // 2, d), lambda i, j, ie, ir, ic, iv, it: (ie[i], jmap(i, j, iv), 0)),
                  pl.BlockSpec((1, 1, d), lambda i, j, ie, ir, ic, iv, it: (ie[i], 0, 0)),
                  pl.BlockSpec((2 * LANES, LANES), lambda i, j, ie, ir, ic, iv, it: (0, 0))],
        out_specs=pl.BlockSpec(memory_space=pl.ANY),
        scratch_shapes=[pltpu.SMEM((MOE_ITEM_ROWS // LANES, LANES), I32),
                        pltpu.VMEM((ring_rows, d // 2), jnp.uint32),
                        pltpu.VMEM((MOE_ITEM_ROWS, d), BF16),
                        pltpu.VMEM((MOE_ITEM_ROWS, d), F32),
                        pltpu.VMEM((d, tn), BF16),
                        pltpu.VMEM((tn // 2, d), BF16),
                        pltpu.VMEM((MOE_CHUNK, tn), F32),
                        pltpu.SemaphoreType.DMA((ring_rows // MOE_CHUNK,)),
                        pltpu.SemaphoreType.DMA((2,))],
    )
    return pl.pallas_call(
        functools.partial(_expert_kernel, nj),
        grid_spec=grid_spec,
        out_shape=jax.ShapeDtypeStruct((n_rows, d), F32),
        compiler_params=_cparams(("arbitrary", "arbitrary"), MOE_VMEM_LIMIT),
        name="moe_experts",
    )(item_e, item_row0, item_nch, item_valid, item_tail, tok_pad, h2, w_gu, b_gu.reshape(ne, 1, two_de),
      w_dn, b_dn.reshape(ne, 1, d), jnp.asarray(sel, BF16))


def _combine_kernel(alpha, dest_ref, y_hbm, gate_ref, x1_ref, g2_ref, lg_ref, lb_ref, out_ref, buf, sem):
    tc = COMBINE_TOKENS

    group = 8

    def issue(g, carry):
        first = pl.multiple_of(g * group, group)
        for dr in range(group):
            for k in range(TOP_K):
                src = dest_ref[0, 0, (first + dr) * TOP_K + k]
                pltpu.make_async_copy(y_hbm.at[pl.ds(src, 1), :], buf.at[k, pl.ds(first + dr, 1), :],
                                      sem.at[0]).start()
        return carry

    lax.fori_loop(0, tc // group, issue, 0)
    for k in range(TOP_K):
        pltpu.make_async_copy(y_hbm.at[pl.ds(0, tc), :], buf.at[k], sem.at[0]).wait()

    gates = gate_ref[...]
    y = gates[:, 0:1] * buf[0]
    for k in range(1, TOP_K):
        y = y + gates[:, k:k + 1] * buf[k]
    z = alpha * x1_ref[...] + (1.0 + g2_ref[0]) * y
    out_ref[...] = _layer_norm(z, lg_ref[...], lb_ref[...])


def _combine(dest, y_pad, gates, x1, g2, ln_g, ln_b, alpha, seq):
    t, d = x1.shape
    tc = COMBINE_TOKENS
    per_b = seq // tc
    row = lambda i: (i, 0)
    const = lambda i: (0, 0)
    return pl.pallas_call(
        functools.partial(_combine_kernel, alpha),
        grid=(t // tc,),
        in_specs=[pl.BlockSpec((1, 1, tc * TOP_K), lambda i: (i, 0, 0), memory_space=pltpu.SMEM),
                  pl.BlockSpec(memory_space=pl.ANY),
                  pl.BlockSpec((tc, LANES), row), pl.BlockSpec((tc, d), row),
                  pl.BlockSpec((1, 1, d), lambda i: (i // per_b, 0, 0)),
                  pl.BlockSpec((1, d), const), pl.BlockSpec((1, d), const)],
        out_specs=pl.BlockSpec((tc, d), row),
        out_shape=jax.ShapeDtypeStruct((t, d), F32),
        scratch_shapes=[pltpu.VMEM((TOP_K, tc, d), F32), pltpu.SemaphoreType.DMA((1,))],
        compiler_params=_cparams(("arbitrary",)),
        name="moe_combine_ln",
    )(dest.reshape(t // tc, 1, tc * TOP_K), y_pad, gates, x1, g2, ln_g, ln_b)


def _count_le(ends, q):
    return jnp.sum((ends[None, :] <= q[:, None]).astype(I32), axis=1)


def _lookup(table, idx):
    hit = idx[:, None] == jnp.arange(table.shape[0], dtype=I32)
    return jnp.sum(jnp.where(hit, table[None, :], 0), axis=1)


def _routing_tables(top_idx):
    t = top_idx.shape[0]
    tk = t * TOP_K
    experts = jnp.arange(N_EXPERTS, dtype=I32)
    hits = [top_idx[:, k:k + 1] == experts for k in range(TOP_K)]
    onehot = sum(h.astype(I32) for h in hits)
    csum = jnp.cumsum(onehot, axis=0)
    counts = csum[-1]
    earlier = csum - onehot
    starts = jnp.cumsum(counts) - counts
    padded = ((counts + MOE_ROW_PAD - 1) // MOE_ROW_PAD) * MOE_ROW_PAD
    pad_end = jnp.cumsum(padded)
    pad_start = pad_end - padded
    dest = jnp.stack([jnp.sum(jnp.where(h, earlier + pad_start, 0), axis=1) for h in hits], axis=1).reshape(tk)

    n_rows = tk + N_EXPERTS * MOE_ROW_PAD
    n_tab = n_rows + 2 * MOE_ITEM_ROWS
    tok_sorted = (jnp.argsort(top_idx.reshape(tk)) // TOP_K).astype(I32)
    blk = jnp.arange(n_tab // MOE_ROW_PAD, dtype=I32) * MOE_ROW_PAD
    blk_e = jnp.minimum(_count_le(pad_end, blk), N_EXPERTS - 1)
    local = (blk - _lookup(pad_start, blk_e))[:, None] + jnp.arange(MOE_ROW_PAD, dtype=I32)
    src = jnp.clip(_lookup(starts, blk_e)[:, None] + local, 0, tk - 1)
    live = jnp.logical_and(local < _lookup(counts, blk_e)[:, None], (blk < pad_end[-1])[:, None])
    tok_pad = jnp.where(live, tok_sorted[src], 0).reshape(-1, LANES)

    items_per = (padded + MOE_ITEM_ROWS - 1) // MOE_ITEM_ROWS
    item_end = jnp.cumsum(items_per)
    item_start = item_end - items_per
    n_items = N_EXPERTS + n_rows // MOE_ITEM_ROWS
    idx = jnp.arange(n_items, dtype=I32)
    valid = idx < item_end[-1]
    e_i = jnp.minimum(_count_le(item_end, idx), N_EXPERTS - 1)
    k_i = idx - _lookup(item_start, e_i)
    row0 = _lookup(pad_start, e_i) + k_i * MOE_ITEM_ROWS
    nrows = jnp.clip(_lookup(padded, e_i) - k_i * MOE_ITEM_ROWS, 0, MOE_ITEM_ROWS)
    e_last = jnp.sum(jnp.where(idx == item_end[-1] - 1, e_i, 0))
    item_e = jnp.where(valid, e_i, e_last).astype(I32)
    item_row0 = jnp.where(valid, row0, 0).astype(I32)
    item_nch = jnp.where(valid, nrows // MOE_CHUNK, 0).astype(I32)
    item_tail = jnp.stack([pad_end[-1], jnp.sum(jnp.where(idx == item_end[-1] - 1, item_nch, 0))]).astype(I32)
    return tok_pad, dest, item_e, item_row0, item_nch, valid.astype(I32), item_tail, n_rows


def kernel(x, c, w_ada, b_ada, w_in, b_in, conv_w, conv_b, m_norm_g, a_norm_g, w_out, ln1_g, ln1_b,
           router_w, router_b, w_gu, b_gu, w_dn, b_dn, ln2_g, ln2_b):
    bsz, seq, d = x.shape
    depth = w_ada.shape[0]
    t = bsz * seq
    alpha = float((2 * depth) ** 0.25)
    qk_w = 2 * M_HEADS * M_DQK
    mv_w = M_HEADS * M_DV
    aw = A_HEADS * A_DH
    gate_lo = qk_w + 2 * mv_w
    gate_hi = gate_lo + 2 * M_HEADS

    x2 = x.reshape(t, d)
    for l in range(depth):
        mod = _ada_mod(c, w_ada[l], b_ada[l]).reshape(bsz, 6, 1, d)
        sh1, sc1, g1, sh2, sc2, g2 = (mod[:, i] for i in range(6))

        w_main = jnp.concatenate([w_in[l][:, :gate_lo], w_in[l][:, gate_hi:]], axis=1).astype(BF16)
        b_main = jnp.concatenate([b_in[l][:gate_lo], b_in[l][gate_hi:]]).reshape(1, -1)
        w_gate = jnp.zeros((d, LANES), BF16).at[:, :2 * M_HEADS].set(w_in[l][:, gate_lo:gate_hi].astype(BF16))
        b_gate = jnp.zeros((1, LANES), F32).at[0, :2 * M_HEADS].set(b_in[l][gate_lo:gate_hi])
        attn_col0 = gate_lo // aw
        proj, gates_c, *qkv_dil = _in_proj(x2, sc1, sh1, w_main, b_main, w_gate, b_gate, seq, attn_col0)

        gates_r = gates_c[:, :2 * M_HEADS].reshape(bsz, seq, 2 * M_HEADS).transpose(0, 2, 1)
        y_m = _mlstm(proj, gates_c, gates_r, conv_w[l], conv_b[l].reshape(1, -1),
                     m_norm_g[l].reshape(1, -1), bsz, seq)

        outs, lses = [], []
        for dil in DILATIONS:
            if dil == 1:
                o_d, lse_d = _attn_group(proj.reshape(bsz, 1, seq, -1), dil, attn_col0)
            else:
                o_d, lse_d = _attn_group(qkv_dil[DILATIONS.index(dil) - 1], dil, 0)
            outs.append(o_d)
            lses.append(lse_d)

        rw = jnp.zeros((d, LANES), BF16).at[:, :N_EXPERTS].set(router_w[l].astype(BF16))
        rb = jnp.zeros((1, LANES), F32).at[0, :N_EXPERTS].set(router_b[l])
        x1, h2, top_idx, gates = _out_proj(y_m, outs, lses, a_norm_g[l].reshape(1, -1), w_out[l].astype(BF16),
                                           x2, g1, sc2, sh2, ln1_g[l].reshape(1, -1), ln1_b[l].reshape(1, -1),
                                           rw, rb, alpha, seq)

        (tok_pad, dest, item_e, item_row0, item_nch, item_valid, item_tail,
         n_rows) = _routing_tables(top_idx[:, :TOP_K])
        y_pad = _experts(h2, tok_pad, item_e, item_row0, item_nch, item_valid, item_tail,
                         w_gu[l], b_gu[l], w_dn[l], b_dn[l], n_rows)
        x2 = _combine(dest, y_pad, gates, x1, g2, ln2_g[l].reshape(1, -1), ln2_b[l].reshape(1, -1), alpha, seq)
    return x2.reshape(bsz, seq, d)
```

```python
import functools

import jax
import jax.numpy as jnp
import numpy as np
from jax import lax
from jax.experimental import pallas as pl
from jax.experimental.pallas import tpu as pltpu

F32 = jnp.float32
BF16 = jnp.bfloat16
I32 = jnp.int32

M_HEADS = 4
M_DQK = 128
M_DV = 256
CONV_WIDTH = 4
A_HEADS = 16
A_DH = 64
ATTN_BLOCK = 128
DILATIONS = (1, 4, 16)
N_EXPERTS = 32
TOP_K = 4
SWIGLU_ALPHA = 1.702
SWIGLU_LIMIT = 7.0
EPS = 1e-5

LANES = 128
VMEM_LIMIT = 56 * 1024 * 1024

MLSTM_CHUNK = 256
MOE_ROW_PAD = 256
MOE_CHUNK = 256
MOE_ITEM_ROWS = 1280
MOE_TN = 1024
MOE_VMEM_LIMIT = 60 * 1024 * 1024
COMBINE_TOKENS = 512
DMA_UNROLL = 8


def _cparams(sem, vmem=VMEM_LIMIT):
    return pltpu.CompilerParams(dimension_semantics=sem, vmem_limit_bytes=vmem)


def _sigmoid(x):
    return 1.0 / (1.0 + jnp.exp(-x))


def _log_sigmoid(x):
    return jnp.minimum(x, 0.0) - jnp.log(1.0 + jnp.exp(-jnp.abs(x)))


def _layer_norm(z, g, b):
    mu = jnp.mean(z, axis=-1, keepdims=True)
    zc = z - mu
    var = jnp.mean(zc * zc, axis=-1, keepdims=True)
    return zc * lax.rsqrt(var + EPS) * g + b


def _dot(a, b):
    return jnp.dot(a, b, preferred_element_type=F32)


def _dot_nt(a, b):
    return lax.dot_general(a, b, (((1,), (1,)), ((), ())), preferred_element_type=F32)


HIGH_HALF = np.uint32(0xFFFF0000)


def _pack_bf16_pairs(x):
    half = x.shape[1] // 2
    xb = x.astype(BF16).astype(F32)
    lo = lax.bitcast_convert_type(xb[:, :half], jnp.uint32)
    hi = lax.bitcast_convert_type(xb[:, half:], jnp.uint32)
    return jnp.bitwise_or(lax.shift_right_logical(lo, jnp.uint32(16)), jnp.bitwise_and(hi, HIGH_HALF))


def _unpack_bf16_pairs(w):
    lo = lax.bitcast_convert_type(lax.shift_left(w, jnp.uint32(16)), F32)
    hi = lax.bitcast_convert_type(jnp.bitwise_and(w, HIGH_HALF), F32)
    return lo.astype(BF16), hi.astype(BF16)


def _dot_hilo(a, sel):
    hi = a.astype(BF16)
    lo = (a - hi.astype(F32)).astype(BF16)
    return _dot(hi, sel) + _dot(lo, sel)


def _ada_kernel(c_ref, w_ref, b_ref, o_ref):
    c = c_ref[...]
    cond = c * _sigmoid(c)
    o_ref[...] = _dot(cond.astype(BF16), w_ref[...].astype(BF16)) + b_ref[...]


def _ada_mod(c, w_ada, b_ada):
    bsz, d = c.shape
    n = w_ada.shape[1]
    tn = 1024
    rows = 8
    c_pad = jnp.zeros((rows, d), F32).at[:bsz].set(c)
    out = pl.pallas_call(
        _ada_kernel,
        grid=(n // tn,),
        in_specs=[pl.BlockSpec((rows, d), lambda j: (0, 0)),
                  pl.BlockSpec((d, tn), lambda j: (0, j)),
                  pl.BlockSpec((1, tn), lambda j: (0, j))],
        out_specs=pl.BlockSpec((rows, tn), lambda j: (0, j)),
        out_shape=jax.ShapeDtypeStruct((rows, n), F32),
        compiler_params=_cparams(("arbitrary",)),
        name="ada_mod",
    )(c_pad, w_ada, b_ada.reshape(1, n))
    return out[:bsz]


def _inproj_kernel(attn_col0, x_ref, sc_ref, sh_ref, w_ref, b_ref, wg_ref, bg_ref, o_ref, g_ref, *rest):
    dil_refs, (h_ref, r_scr) = rest[:-2], rest[-2:]
    j = pl.program_id(1)

    @pl.when(j == 0)
    def _():
        h = x_ref[...] * (1.0 + sc_ref[0]) + sh_ref[0]
        hb = h.astype(BF16)
        h_ref[...] = hb
        g_ref[...] = _dot(hb, wg_ref[...]) + bg_ref[...]

    res = _dot(h_ref[...], w_ref[...]) + b_ref[...]
    o_ref[...] = res.astype(BF16)

    @pl.when(j >= attn_col0)
    def _():
        cols = res.shape[1] // LANES
        for c in range(cols):
            r_scr[c] = res[:, c * LANES:(c + 1) * LANES]
        for ref in dil_refs:
            d, n = ref.shape[1], ref.shape[2]
            for r in range(d):
                for c in range(cols):
                    ref[0, r, :, c * LANES:(c + 1) * LANES] = r_scr[c, pl.ds(r, n, stride=d), :].astype(BF16)


def _in_proj(x2, sc, sh, w_main, b_main, w_gate, b_gate, seq, attn_col0):
    t, d = x2.shape
    n = w_main.shape[1]
    tm, tn = 1024, 1024
    per_b = seq // tm
    bsz = t // seq
    dils = [dl for dl in DILATIONS if dl > 1]
    aw3 = n - attn_col0 * tn
    dil_specs = [pl.BlockSpec((1, dl, tm // dl, tn),
                              lambda i, j: (i // per_b, 0, i % per_b, jnp.maximum(j - attn_col0, 0))) for dl in dils]
    dil_shapes = [jax.ShapeDtypeStruct((bsz, dl, seq // dl, aw3), BF16) for dl in dils]
    return pl.pallas_call(
        functools.partial(_inproj_kernel, attn_col0),
        grid=(t // tm, n // tn),
        in_specs=[pl.BlockSpec((tm, d), lambda i, j: (i, 0)),
                  pl.BlockSpec((1, 1, d), lambda i, j: (i // per_b, 0, 0)),
                  pl.BlockSpec((1, 1, d), lambda i, j: (i // per_b, 0, 0)),
                  pl.BlockSpec((d, tn), lambda i, j: (0, j)),
                  pl.BlockSpec((1, tn), lambda i, j: (0, j)),
                  pl.BlockSpec((d, LANES), lambda i, j: (0, 0)),
                  pl.BlockSpec((1, LANES), lambda i, j: (0, 0))],
        out_specs=[pl.BlockSpec((tm, tn), lambda i, j: (i, j)),
                   pl.BlockSpec((tm, LANES), lambda i, j: (i, 0))] + dil_specs,
        out_shape=[jax.ShapeDtypeStruct((t, n), BF16),
                   jax.ShapeDtypeStruct((t, LANES), F32)] + dil_shapes,
        scratch_shapes=[pltpu.VMEM((tm, d), BF16), pltpu.VMEM((tn // LANES, tm, LANES), F32)],
        compiler_params=_cparams(("arbitrary", "arbitrary")),
        name="in_proj",
    )(x2, sc, sh, w_main, b_main, w_gate, b_gate)


def _mlstm_kernel(qk_ref, v_ref, o_ref, gc_ref, gr_ref, cw_ref, cb_ref, ng_ref, out_ref,
                  ct_ref, n_ref, m_ref, prev_ref):
    c = pl.program_id(1)
    L = MLSTM_CHUNK

    @pl.when(c == 0)
    def _():
        ct_ref[...] = jnp.zeros_like(ct_ref)
        n_ref[...] = jnp.zeros_like(n_ref)
        m_ref[...] = jnp.zeros_like(m_ref)
        prev_ref[...] = jnp.zeros_like(prev_ref)

    x = qk_ref[...].astype(F32)
    prev = prev_ref[...]
    row = lax.broadcasted_iota(I32, (L, 1), 0)
    y = cw_ref[CONV_WIDTH - 1:CONV_WIDTH, :] * x + cb_ref[...]
    for k in range(1, CONV_WIDTH):
        xs = jnp.where(row < k, pltpu.roll(prev, k, 0), pltpu.roll(x, k, 0))
        y = y + cw_ref[CONV_WIDTH - 1 - k:CONV_WIDTH - k, :] * xs
    prev_ref[...] = x
    y = y * _sigmoid(y)

    gc = gc_ref[...]
    gr = gr_ref[0]
    ti = lax.broadcasted_iota(I32, (L, L), 0)
    si = lax.broadcasted_iota(I32, (L, L), 1)
    causal = si <= ti
    tril = causal.astype(F32)
    triu = (ti <= si).astype(F32)
    b_cols = jnp.dot(tril, _log_sigmoid(gc), precision=lax.Precision.HIGHEST, preferred_element_type=F32)
    b_rows = jnp.dot(_log_sigmoid(gr), triu, precision=lax.Precision.HIGHEST, preferred_element_type=F32)

    qk_w = M_HEADS * M_DQK
    heads = range(M_HEADS)
    qf = [y[:, h * M_DQK:(h + 1) * M_DQK] for h in heads]
    kf = [y[:, qk_w + h * M_DQK:qk_w + (h + 1) * M_DQK] * (M_DQK ** -0.5) for h in heads]
    qb = [t.astype(BF16) for t in qf]
    vb = [v_ref[:, h * M_DV:(h + 1) * M_DV] for h in heads]
    bc = [b_cols[:, M_HEADS + h:M_HEADS + h + 1] for h in heads]
    ic = [gc[:, h:h + 1] for h in heads]
    br = [b_rows[M_HEADS + h:M_HEADS + h + 1, :] for h in heads]
    ir = [gr[h:h + 1, :] for h in heads]
    m_prev = [m_ref[h][0:1, 0:1] for h in heads]
    n_prev = [n_ref[h][0:1, :] for h in heads]

    b_last = [bc[h][L - 1:L, :] for h in heads]
    g_col = [b_last[h] - bc[h] + ic[h] for h in heads]
    m_new = [jnp.maximum(b_last[h] + m_prev[h], jnp.max(g_col[h], axis=0, keepdims=True)) for h in heads]
    w_col = [jnp.exp(g_col[h] - m_new[h]) for h in heads]
    decay = [jnp.exp(b_last[h] + m_prev[h] - m_new[h]) for h in heads]

    qk = [_dot_nt(qb[h], kf[h].astype(BF16)) for h in heads]
    q_state = [_dot(qb[h], ct_ref[h].astype(BF16)) for h in heads]
    kv_new = [_dot(kf[h].T.astype(BF16), (w_col[h] * vb[h].astype(F32)).astype(BF16)) for h in heads]

    dm = [jnp.where(causal, bc[h] - br[h] + ir[h], -jnp.inf) for h in heads]
    inter = [bc[h] + m_prev[h] for h in heads]
    mt = [jnp.maximum(inter[h], jnp.max(dm[h], axis=1, keepdims=True)) for h in heads]
    a = [jnp.exp(dm[h] - mt[h]) * qk[h] for h in heads]
    e_int = [jnp.exp(inter[h] - mt[h]) for h in heads]
    av = [_dot(a[h].astype(BF16), vb[h]) for h in heads]

    for h in heads:
        num = av[h] + e_int[h] * q_state[h]
        den = (jnp.sum(a[h], axis=1, keepdims=True)
               + e_int[h] * jnp.sum(qf[h] * n_prev[h], axis=1, keepdims=True))
        hh = num / jnp.maximum(jnp.abs(den), jnp.exp(-mt[h]))

        ct_ref[h] = decay[h] * ct_ref[h] + kv_new[h]
        n_new = decay[h] * n_prev[h] + jnp.sum(w_col[h] * kf[h], axis=0, keepdims=True)
        n_ref[h] = jnp.broadcast_to(n_new, n_ref.shape[1:])
        m_ref[h] = jnp.broadcast_to(m_new[h], m_ref.shape[1:])

        ms = jnp.mean(hh * hh, axis=1, keepdims=True)
        og = o_ref[:, h * M_DV:(h + 1) * M_DV].astype(F32)
        yh = hh * lax.rsqrt(ms + EPS) * ng_ref[:, h * M_DV:(h + 1) * M_DV] * _sigmoid(og)
        out_ref[:, h * M_DV:(h + 1) * M_DV] = yh.astype(BF16)


def _mlstm(proj, gates_c, gates_r, conv_w, conv_b, norm_g, bsz, seq):
    L = MLSTM_CHUNK
    nc = seq // L
    t = bsz * seq
    w = M_HEADS * M_DV
    return pl.pallas_call(
        _mlstm_kernel,
        grid=(bsz, nc),
        in_specs=[pl.BlockSpec((L, w), lambda b, c: (b * nc + c, 0)),
                  pl.BlockSpec((L, w), lambda b, c: (b * nc + c, 1)),
                  pl.BlockSpec((L, w), lambda b, c: (b * nc + c, 2)),
                  pl.BlockSpec((L, LANES), lambda b, c: (b * nc + c, 0)),
                  pl.BlockSpec((1, 8, L), lambda b, c: (b, 0, c)),
                  pl.BlockSpec((CONV_WIDTH, w), lambda b, c: (0, 0)),
                  pl.BlockSpec((1, w), lambda b, c: (0, 0)),
                  pl.BlockSpec((1, w), lambda b, c: (0, 0))],
        out_specs=pl.BlockSpec((L, w), lambda b, c: (b * nc + c, 0)),
        out_shape=jax.ShapeDtypeStruct((t, w), BF16),
        scratch_shapes=[pltpu.VMEM((M_HEADS, M_DQK, M_DV), F32),
                        pltpu.VMEM((M_HEADS, 8, M_DQK), F32),
                        pltpu.VMEM((M_HEADS, 8, LANES), F32),
                        pltpu.VMEM((L, w), F32)],
        compiler_params=_cparams(("arbitrary", "arbitrary")),
        name="mlstm",
    )(proj, proj, proj, gates_c, gates_r, conv_w, conv_b, norm_g)


def _attn_kernel(dilation, has_prev, *refs):
    nq = ATTN_BLOCK
    if has_prev:
        q_ref, kc_ref, vc_ref, kp_ref, vp_ref, o_ref, lse_ref, k_all, v_all = refs
        k_all[0:nq, :] = kp_ref[0, 0]
        k_all[nq:2 * nq, :] = kc_ref[0, 0]
        v_all[0:nq, :] = vp_ref[0, 0]
        v_all[nq:2 * nq, :] = vc_ref[0, 0]
        nk = 2 * nq
    else:
        q_ref, k_all, v_all, o_ref, lse_ref = refs
        k_all, v_all = k_all.at[0, 0], v_all.at[0, 0]
        nk = nq
    n = pl.program_id(2)
    qi = lax.broadcasted_iota(I32, (nq, nk), 0)
    ki = lax.broadcasted_iota(I32, (nq, nk), 1)
    dist = qi - ki + (nk - nq)
    ok = jnp.logical_and(dist >= 0, dist <= nq)
    if has_prev:
        ok = jnp.logical_and(ok, jnp.logical_or(ki >= nq, n > 0))
    dist_f = jnp.where(ok, dist.astype(F32), jnp.inf)
    lane = lax.broadcasted_iota(I32, (nq, LANES), 1)
    left_q = lane < A_DH
    left_k = lax.broadcasted_iota(I32, (nk, LANES), 1) < A_DH
    n_pairs = A_HEADS // 2

    scores = []
    for p in range(n_pairs):
        cols = slice(p * LANES, (p + 1) * LANES)
        qp = q_ref[0, 0, :, cols] * (A_DH ** -0.5)
        kp = k_all[:, cols]
        zero = jnp.zeros_like(qp)
        scores.append(_dot_nt(jnp.where(left_q, qp, zero), kp))
        scores.append(_dot_nt(jnp.where(left_q, zero, qp), kp))
    probs, maxes = [], []
    for h in range(A_HEADS):
        coef = -(2.0 ** (-8.0 * (h + 1) / A_HEADS)) * dilation
        s = scores[h] + dist_f * coef
        m = jnp.max(s, axis=1, keepdims=True)
        probs.append(jnp.exp(s - m).astype(BF16))
        maxes.append(m)
    lse_tile = jnp.zeros((nq, LANES), F32)
    for p in range(n_pairs):
        cols = slice(p * LANES, (p + 1) * LANES)
        vp = v_all[:, cols]
        one = jnp.ones_like(vp)
        pv_e = _dot(probs[2 * p], jnp.where(left_k, vp, one))
        pv_o = _dot(probs[2 * p + 1], jnp.where(left_k, one, vp))
        num = jnp.where(left_q, pv_e, pv_o)
        den = pltpu.roll(jnp.where(left_q, pv_o, pv_e), A_DH, 1)
        o_ref[0, 0, :, cols] = (num / den).astype(BF16)
        lse_tile = jnp.where(lane == 2 * p, maxes[2 * p] + jnp.log(pv_e[:, A_DH:A_DH + 1]), lse_tile)
        lse_tile = jnp.where(lane == 2 * p + 1, maxes[2 * p + 1] + jnp.log(pv_o[:, 0:1]), lse_tile)
    lse_ref[0, 0] = lse_tile


def _attn_group(qkv, dilation, col0):
    bsz, d, ls, _ = qkv.shape
    aw = A_HEADS * A_DH
    nq = ATTN_BLOCK
    nb = ls // nq
    has_prev = nb > 1
    blk = (1, 1, nq, aw)
    in_specs = [pl.BlockSpec(blk, lambda b, r, n: (b, r, n, col0)),
                pl.BlockSpec(blk, lambda b, r, n: (b, r, n, col0 + 1)),
                pl.BlockSpec(blk, lambda b, r, n: (b, r, n, col0 + 2))]
    args = [qkv, qkv, qkv]
    if has_prev:
        in_specs += [pl.BlockSpec(blk, lambda b, r, n: (b, r, jnp.maximum(n - 1, 0), col0 + 1)),
                     pl.BlockSpec(blk, lambda b, r, n: (b, r, jnp.maximum(n - 1, 0), col0 + 2))]
        args += [qkv, qkv]
    return pl.pallas_call(
        functools.partial(_attn_kernel, dilation, has_prev),
        grid=(bsz, d, nb),
        in_specs=in_specs,
        out_specs=[pl.BlockSpec(blk, lambda b, r, n: (b, r, n, 0)),
                   pl.BlockSpec((1, 1, nq, LANES), lambda b, r, n: (b, r, n, 0))],
        out_shape=[jax.ShapeDtypeStruct((bsz, d, ls, aw), BF16),
                   jax.ShapeDtypeStruct((bsz, d, ls, LANES), F32)],
        scratch_shapes=[pltpu.VMEM((2 * nq, aw), BF16)] * 2 if has_prev else [],
        compiler_params=_cparams(("arbitrary", "arbitrary", "arbitrary")),
        name=f"dilated_attn_d{dilation}",
    )(*args)


def _head_maps(n_heads, dh):
    w = n_heads * dh
    e = np.zeros((LANES, w), np.float32)
    for h in range(n_heads):
        e[h, h * dh:(h + 1) * dh] = 1.0
    return jnp.asarray(e, BF16), jnp.asarray(e.T.copy(), BF16)


def _natural_rows(ref, scr):
    d, n, w = ref.shape[1:]
    if d == 1:
        return ref[0, 0].astype(F32)
    cols = w // LANES
    for r in range(d):
        blk = ref[0, r].astype(F32)
        for c in range(cols):
            scr[c, pl.ds(r, n, stride=d), :] = blk[:, c * LANES:(c + 1) * LANES]
    return jnp.concatenate([scr[c] for c in range(cols)], axis=1)


def _merged_heads(o_refs, l_refs, g_ref, e_ref, p_ref, o_scr, l_scr):
    l1, l2, l3 = (_natural_rows(ref, l_scr.at[g]) for g, ref in enumerate(l_refs))
    mx = jnp.maximum(jnp.maximum(l1, l2), l3)
    w1, w2, w3 = jnp.exp(l1 - mx), jnp.exp(l2 - mx), jnp.exp(l3 - mx)
    inv = 1.0 / (w1 + w2 + w3)
    e = e_ref[...]
    o = (_dot_hilo(w1 * inv, e) * _natural_rows(o_refs[0], o_scr.at[0])
         + _dot_hilo(w2 * inv, e) * _natural_rows(o_refs[1], o_scr.at[1])
         + _dot_hilo(w3 * inv, e) * _natural_rows(o_refs[2], o_scr.at[2]))
    ms = _dot_hilo(o * o, p_ref[...]) * (1.0 / A_DH)
    scale = _dot_hilo(lax.rsqrt(ms + EPS), e)
    return (o * scale * g_ref[...]).astype(BF16)


def _outproj_kernel(alpha, ym_ref, o1_ref, o2_ref, o3_ref, l1_ref, l2_ref, l3_ref, ag_ref, e_ref, p_ref,
                    w_ref, x_ref, g1_ref, sc_ref, sh_ref, lg_ref, lb_ref, rw_ref, rb_ref,
                    x1_ref, h2_ref, ti_ref, tg_ref, o_scr, l_scr):
    half = ym_ref.shape[1]
    y_a = _merged_heads((o1_ref, o2_ref, o3_ref), (l1_ref, l2_ref, l3_ref), ag_ref, e_ref, p_ref, o_scr, l_scr)
    y = _dot(ym_ref[...], w_ref[0:half, :]) + _dot(y_a, w_ref[half:2 * half, :])
    z = alpha * x_ref[...] + (1.0 + g1_ref[0]) * y
    x1 = _layer_norm(z, lg_ref[...], lb_ref[...])
    x1_ref[...] = x1
    h2 = x1 * (1.0 + sc_ref[0]) + sh_ref[0]
    h2_ref[...] = _pack_bf16_pairs(h2)
    logits = _dot(h2.astype(BF16), rw_ref[...]) + rb_ref[...]
    lane = lax.broadcasted_iota(I32, logits.shape, 1)
    lane_f = lane.astype(F32)
    work = jnp.where(lane < N_EXPERTS, logits, -jnp.inf)
    idx_tile = jnp.zeros(logits.shape, F32)
    val_tile = jnp.zeros(logits.shape, F32)
    top = None
    denom = None
    for k in range(TOP_K):
        mk = jnp.max(work, axis=1, keepdims=True)
        ik = jnp.min(jnp.where(work == mk, lane_f, float(LANES)), axis=1, keepdims=True)
        work = jnp.where(lane_f == ik, -jnp.inf, work)
        if k == 0:
            top = mk
        ek = jnp.exp(mk - top)
        denom = ek if k == 0 else denom + ek
        idx_tile = jnp.where(lane == k, ik, idx_tile)
        val_tile = jnp.where(lane == k, ek, val_tile)
    ti_ref[...] = idx_tile.astype(I32)
    tg_ref[...] = val_tile / denom


def _out_proj(y_m, outs, lses, norm_g, w_out, x2, g1, sc2, sh2, ln_g, ln_b, rw, rb, alpha, seq):
    t, d = x2.shape
    half = y_m.shape[1]
    aw = outs[0].shape[3]
    tm = 256
    per_b = seq // tm
    row = lambda i: (i, 0)
    const = lambda i: (0, 0)
    mod = lambda i: (i // per_b, 0, 0)
    expand, pool = _head_maps(A_HEADS, A_DH)

    def grouped(arr):
        dl, w = arr.shape[1], arr.shape[3]
        return pl.BlockSpec((1, dl, tm // dl, w), lambda i: (i // per_b, 0, i % per_b, 0))

    return pl.pallas_call(
        functools.partial(_outproj_kernel, alpha),
        grid=(t // tm,),
        in_specs=[pl.BlockSpec((tm, half), row)] + [grouped(a) for a in outs] + [grouped(a) for a in lses]
        + [pl.BlockSpec((1, aw), const), pl.BlockSpec((LANES, aw), const), pl.BlockSpec((aw, LANES), const),
                  pl.BlockSpec((2 * half, d), const), pl.BlockSpec((tm, d), row),
                  pl.BlockSpec((1, 1, d), mod), pl.BlockSpec((1, 1, d), mod), pl.BlockSpec((1, 1, d), mod),
                  pl.BlockSpec((1, d), const), pl.BlockSpec((1, d), const),
                  pl.BlockSpec((d, LANES), const), pl.BlockSpec((1, LANES), const)],
        out_specs=[pl.BlockSpec((tm, d), row), pl.BlockSpec((tm, d // 2), row),
                   pl.BlockSpec((tm, LANES), row), pl.BlockSpec((tm, LANES), row)],
        out_shape=[jax.ShapeDtypeStruct((t, d), F32), jax.ShapeDtypeStruct((t, d // 2), jnp.uint32),
                   jax.ShapeDtypeStruct((t, LANES), I32), jax.ShapeDtypeStruct((t, LANES), F32)],
        scratch_shapes=[pltpu.VMEM((len(outs), aw // LANES, tm, LANES), F32),
                        pltpu.VMEM((len(lses), 1, tm, LANES), F32)],
        compiler_params=_cparams(("arbitrary",)),
        name="out_proj_ln_router",
    )(y_m, *outs, *lses, norm_g, expand, pool, w_out, x2, g1, sc2, sh2, ln_g, ln_b, rw, rb)


def _expert_kernel(nj, ie_ref, ir_ref, ic_ref, iv_ref, tail_ref, tok_hbm, h_hbm, wgu_ref, bgu_ref, wdn_ref,
                   bdn_ref, sel_ref, y_hbm, tok_smem, ring, xb, yacc, wgu_b, wdn_b, gu_scr, gsem, sems):
    i = pl.program_id(0)
    j = pl.program_id(1)
    row0 = ir_ref[i]
    nch = ic_ref[i]
    ch = MOE_CHUNK
    lead = MOE_ITEM_ROWS
    ring_chunks = ring.shape[0] // ch
    per_body = ch // nj

    def chunk_slot(first_row):
        return lax.rem(lax.shift_right_logical(first_row, 8), ring_chunks)

    def load_tokens(first_row):
        rows = pl.ds(lax.shift_right_logical(first_row, 7), lead // LANES)
        cp = pltpu.make_async_copy(tok_hbm.at[rows, :], tok_smem, sems.at[0])
        cp.start()
        cp.wait()

    def gather_rows(first_row, first_idx, count):
        slot = chunk_slot(first_row)
        base = slot * ch + jnp.bitwise_and(first_row, ch - 1)
        for u in range(count):
            idx = first_idx + u
            t = tok_smem[lax.shift_right_logical(idx, 7), jnp.bitwise_and(idx, LANES - 1)]
            pltpu.make_async_copy(h_hbm.at[pl.ds(t, 1), :], ring.at[pl.ds(base + u, 1), :], gsem.at[slot]).start()

    def chunk_wait(first_row):
        slot = chunk_slot(first_row)
        rows = pl.ds(pl.multiple_of(slot * ch, ch), ch)
        pltpu.make_async_copy(h_hbm.at[pl.ds(0, ch), :], ring.at[rows, :], gsem.at[slot]).wait()
        return rows

    @pl.when(jnp.logical_and(i == 0, j == 0))
    def _():
        load_tokens(0)

        def issue(g, carry):
            gather_rows(g * DMA_UNROLL, g * DMA_UNROLL, DMA_UNROLL)
            return carry

        lax.fori_loop(0, lead // DMA_UNROLL, issue, 0)

    def out_wait(count):
        def drain(m, carry):
            pltpu.make_async_copy(yacc.at[pl.ds(0, ch), :], y_hbm.at[pl.ds(0, ch), :], sems.at[1]).wait()
            return carry

        lax.fori_loop(0, count, drain, 0)

    @pl.when(jnp.logical_and(j == 0, nch > 0))
    def _():
        load_tokens(row0 + lead)

        def cast_rows(m, carry):
            src = chunk_wait(row0 + m * ch)
            rows = pl.ds(pl.multiple_of(m * ch, ch), ch)
            half = ring.shape[1]
            xb[rows, 0:half], xb[rows, half:2 * half] = _unpack_bf16_pairs(ring[src, :])
            return carry

        lax.fori_loop(0, nch, cast_rows, 0)

        @pl.when(i > 0)
        def _():
            out_wait(ic_ref[jnp.maximum(i - 1, 0)])

        def seed(m, carry):
            yacc[pl.ds(pl.multiple_of(m * ch, ch), ch), :] = jnp.broadcast_to(bdn_ref[0], (ch, yacc.shape[1]))
            return carry

        lax.fori_loop(0, nch, seed, 0)

    @pl.when(nch > 0)
    def _():
        wgu_b[...] = wgu_ref[0].astype(BF16)
        wdn_b[...] = wdn_ref[0].astype(BF16)
        bgu = bgu_ref[0]
        last = j == nj - 1

        def prefetch(m):
            first_idx = (j * nch + m) * per_body
            gather_rows(row0 + lead + first_idx, first_idx, per_body)

        def gate_up(m):
            rows = pl.ds(pl.multiple_of(m * ch, ch), ch)
            return _dot(xb[rows, :], wgu_b[...]) + bgu

        def finish(m, gu):
            rows = pl.ds(pl.multiple_of(m * ch, ch), ch)
            glu = jnp.minimum(gu, SWIGLU_LIMIT)
            f_glu = glu * _sigmoid(SWIGLU_ALPHA * glu)
            f_lin = jnp.clip(gu, -SWIGLU_LIMIT, SWIGLU_LIMIT) + 1.0
            prod = (pltpu.roll(f_glu, 1, 1) * f_lin).astype(BF16)
            parts = [_dot(prod[:, q * 2 * LANES:(q + 1) * 2 * LANES], sel_ref[...])
                     for q in range(MOE_TN // (2 * LANES))]
            act = jnp.concatenate(parts, axis=1).astype(BF16)
            yacc[rows, :] += _dot(act, wdn_b[...])

        def step(m):
            gu = gu_scr[...]
            gu_scr[...] = gate_up(m + 1)
            finish(m, gu)
            prefetch(m)

        gu_scr[...] = gate_up(0)
        n_steps = nch - 1

        def pair(p, carry):
            step(2 * p)
            step(2 * p + 1)
            return carry

        lax.fori_loop(0, lax.shift_right_logical(n_steps, 1), pair, 0)

        @pl.when(jnp.bitwise_and(n_steps, 1) == 1)
        def _():
            step(n_steps - 1)

        finish(nch - 1, gu_scr[...])
        prefetch(nch - 1)

        @pl.when(last)
        def _():
            def issue(m, carry):
                rows = pl.ds(pl.multiple_of(m * ch, ch), ch)
                dst = pl.ds(pl.multiple_of(row0 + m * ch, ch), ch)
                pltpu.make_async_copy(yacc.at[rows, :], y_hbm.at[dst, :], sems.at[1]).start()
                return carry

            lax.fori_loop(0, nch, issue, 0)

    @pl.when(jnp.logical_and(i == pl.num_programs(0) - 1, j == nj - 1))
    def _():
        out_wait(tail_ref[1])

        def drain(m, carry):
            chunk_wait(tail_ref[0] + m * ch)
            return carry

        lax.fori_loop(0, lead // ch, drain, 0)


def _experts(h2, tok_pad, item_e, item_row0, item_nch, item_valid, item_tail, w_gu, b_gu, w_dn, b_dn, n_rows):
    ne, d, two_de = w_gu.shape
    tn = MOE_TN
    nj = two_de // tn
    assert nj > 1, "the kernel separates its first and last column-tile steps"
    assert MOE_CHUNK % nj == 0 and (MOE_CHUNK // nj) % 8 == 0, "row copies per chunk stage"
    ni = item_e.shape[0]
    ring_rows = MOE_ITEM_ROWS
    sel = np.zeros((2 * LANES, LANES), np.float32)
    sel[2 * np.arange(LANES) + 1, np.arange(LANES)] = 1.0

    def jmap(i, j, iv):
        return jnp.where(iv[i] > 0, j, nj - 1)

    grid_spec = pltpu.PrefetchScalarGridSpec(
        num_scalar_prefetch=5,
        grid=(ni, nj),
        in_specs=[pl.BlockSpec(memory_space=pl.ANY),
                  pl.BlockSpec(memory_space=pl.ANY),
                  pl.BlockSpec((1, d, tn), lambda i, j, ie, ir, ic, iv, it: (ie[i], 0, jmap(i, j, iv))),
                  pl.BlockSpec((1, 1, tn), lambda i, j, ie, ir, ic, iv, it: (ie[i], 0, jmap(i, j, iv))),
                  pl.BlockSpec((1, tn // 2, d), lambda i, j, ie, ir, ic, iv, it: (ie[i], jmap(i, j, iv), 0)),
                  pl.BlockSpec((1, 1, d), lambda i, j, ie, ir, ic, iv, it: (ie[i], 0, 0)),
                  pl.BlockSpec((2 * LANES, LANES), lambda i, j, ie, ir, ic, iv, it: (0, 0))],
        out_specs=pl.BlockSpec(memory_space=pl.ANY),
        scratch_shapes=[pltpu.SMEM((MOE_ITEM_ROWS // LANES, LANES), I32),
                        pltpu.VMEM((ring_rows, d // 2), jnp.uint32),
                        pltpu.VMEM((MOE_ITEM_ROWS, d), BF16),
                        pltpu.VMEM((MOE_ITEM_ROWS, d), F32),
                        pltpu.VMEM((d, tn), BF16),
                        pltpu.VMEM((tn // 2, d), BF16),
                        pltpu.VMEM((MOE_CHUNK, tn), F32),
                        pltpu.SemaphoreType.DMA((ring_rows // MOE_CHUNK,)),
                        pltpu.SemaphoreType.DMA((2,))],
    )
    return pl.pallas_call(
        functools.partial(_expert_kernel, nj),
        grid_spec=grid_spec,
        out_shape=jax.ShapeDtypeStruct((n_rows, d), F32),
        compiler_params=_cparams(("arbitrary", "arbitrary"), MOE_VMEM_LIMIT),
        name="moe_experts",
    )(item_e, item_row0, item_nch, item_valid, item_tail, tok_pad, h2, w_gu, b_gu.reshape(ne, 1, two_de),
      w_dn, b_dn.reshape(ne, 1, d), jnp.asarray(sel, BF16))


def _combine_kernel(alpha, dest_ref, y_hbm, gate_ref, x1_ref, g2_ref, lg_ref, lb_ref, out_ref, buf, sem):
    tc = COMBINE_TOKENS

    group = 8

    def issue(g, carry):
        first = pl.multiple_of(g * group, group)
        for dr in range(group):
            for k in range(TOP_K):
                src = dest_ref[0, 0, (first + dr) * TOP_K + k]
                pltpu.make_async_copy(y_hbm.at[pl.ds(src, 1), :], buf.at[k, pl.ds(first + dr, 1), :],
                                      sem.at[0]).start()
        return carry

    lax.fori_loop(0, tc // group, issue, 0)
    for k in range(TOP_K):
        pltpu.make_async_copy(y_hbm.at[pl.ds(0, tc), :], buf.at[k], sem.at[0]).wait()

    gates = gate_ref[...]
    y = gates[:, 0:1] * buf[0]
    for k in range(1, TOP_K):
        y = y + gates[:, k:k + 1] * buf[k]
    z = alpha * x1_ref[...] + (1.0 + g2_ref[0]) * y
    out_ref[...] = _layer_norm(z, lg_ref[...], lb_ref[...])


def _combine(dest, y_pad, gates, x1, g2, ln_g, ln_b, alpha, seq):
    t, d = x1.shape
    tc = COMBINE_TOKENS
    per_b = seq // tc
    row = lambda i: (i, 0)
    const = lambda i: (0, 0)
    return pl.pallas_call(
        functools.partial(_combine_kernel, alpha),
        grid=(t // tc,),
        in_specs=[pl.BlockSpec((1, 1, tc * TOP_K), lambda i: (i, 0, 0), memory_space=pltpu.SMEM),
                  pl.BlockSpec(memory_space=pl.ANY),
                  pl.BlockSpec((tc, LANES), row), pl.BlockSpec((tc, d), row),
                  pl.BlockSpec((1, 1, d), lambda i: (i // per_b, 0, 0)),
                  pl.BlockSpec((1, d), const), pl.BlockSpec((1, d), const)],
        out_specs=pl.BlockSpec((tc, d), row),
        out_shape=jax.ShapeDtypeStruct((t, d), F32),
        scratch_shapes=[pltpu.VMEM((TOP_K, tc, d), F32), pltpu.SemaphoreType.DMA((1,))],
        compiler_params=_cparams(("arbitrary",)),
        name="moe_combine_ln",
    )(dest.reshape(t // tc, 1, tc * TOP_K), y_pad, gates, x1, g2, ln_g, ln_b)


def _count_le(ends, q):
    return jnp.sum((ends[None, :] <= q[:, None]).astype(I32), axis=1)


def _lookup(table, idx):
    hit = idx[:, None] == jnp.arange(table.shape[0], dtype=I32)
    return jnp.sum(jnp.where(hit, table[None, :], 0), axis=1)


def _routing_tables(top_idx):
    t = top_idx.shape[0]
    tk = t * TOP_K
    experts = jnp.arange(N_EXPERTS, dtype=I32)
    hits = [top_idx[:, k:k + 1] == experts for k in range(TOP_K)]
    onehot = sum(h.astype(I32) for h in hits)
    csum = jnp.cumsum(onehot, axis=0)
    counts = csum[-1]
    earlier = csum - onehot
    starts = jnp.cumsum(counts) - counts
    padded = ((counts + MOE_ROW_PAD - 1) // MOE_ROW_PAD) * MOE_ROW_PAD
    pad_end = jnp.cumsum(padded)
    pad_start = pad_end - padded
    dest = jnp.stack([jnp.sum(jnp.where(h, earlier + pad_start, 0), axis=1) for h in hits], axis=1).reshape(tk)

    n_rows = tk + N_EXPERTS * MOE_ROW_PAD
    n_tab = n_rows + 2 * MOE_ITEM_ROWS
    tok_sorted = (jnp.argsort(top_idx.reshape(tk)) // TOP_K).astype(I32)
    blk = jnp.arange(n_tab // MOE_ROW_PAD, dtype=I32) * MOE_ROW_PAD
    blk_e = jnp.minimum(_count_le(pad_end, blk), N_EXPERTS - 1)
    local = (blk - _lookup(pad_start, blk_e))[:, None] + jnp.arange(MOE_ROW_PAD, dtype=I32)
    src = jnp.clip(_lookup(starts, blk_e)[:, None] + local, 0, tk - 1)
    live = jnp.logical_and(local < _lookup(counts, blk_e)[:, None], (blk < pad_end[-1])[:, None])
    tok_pad = jnp.where(live, tok_sorted[src], 0).reshape(-1, LANES)

    items_per = (padded + MOE_ITEM_ROWS - 1) // MOE_ITEM_ROWS
    item_end = jnp.cumsum(items_per)
    item_start = item_end - items_per
    n_items = N_EXPERTS + n_rows // MOE_ITEM_ROWS
    idx = jnp.arange(n_items, dtype=I32)
    valid = idx < item_end[-1]
    e_i = jnp.minimum(_count_le(item_end, idx), N_EXPERTS - 1)
    k_i = idx - _lookup(item_start, e_i)
    row0 = _lookup(pad_start, e_i) + k_i * MOE_ITEM_ROWS
    nrows = jnp.clip(_lookup(padded, e_i) - k_i * MOE_ITEM_ROWS, 0, MOE_ITEM_ROWS)
    e_last = jnp.sum(jnp.where(idx == item_end[-1] - 1, e_i, 0))
    item_e = jnp.where(valid, e_i, e_last).astype(I32)
    item_row0 = jnp.where(valid, row0, 0).astype(I32)
    item_nch = jnp.where(valid, nrows // MOE_CHUNK, 0).astype(I32)
    item_tail = jnp.stack([pad_end[-1], jnp.sum(jnp.where(idx == item_end[-1] - 1, item_nch, 0))]).astype(I32)
    return tok_pad, dest, item_e, item_row0, item_nch, valid.astype(I32), item_tail, n_rows


def kernel(x, c, w_ada, b_ada, w_in, b_in, conv_w, conv_b, m_norm_g, a_norm_g, w_out, ln1_g, ln1_b,
           router_w, router_b, w_gu, b_gu, w_dn, b_dn, ln2_g, ln2_b):
    bsz, seq, d = x.shape
    depth = w_ada.shape[0]
    t = bsz * seq
    alpha = float((2 * depth) ** 0.25)
    qk_w = 2 * M_HEADS * M_DQK
    mv_w = M_HEADS * M_DV
    aw = A_HEADS * A_DH
    gate_lo = qk_w + 2 * mv_w
    gate_hi = gate_lo + 2 * M_HEADS

    x2 = x.reshape(t, d)
    for l in range(depth):
        mod = _ada_mod(c, w_ada[l], b_ada[l]).reshape(bsz, 6, 1, d)
        sh1, sc1, g1, sh2, sc2, g2 = (mod[:, i] for i in range(6))

        w_main = jnp.concatenate([w_in[l][:, :gate_lo], w_in[l][:, gate_hi:]], axis=1).astype(BF16)
        b_main = jnp.concatenate([b_in[l][:gate_lo], b_in[l][gate_hi:]]).reshape(1, -1)
        w_gate = jnp.zeros((d, LANES), BF16).at[:, :2 * M_HEADS].set(w_in[l][:, gate_lo:gate_hi].astype(BF16))
        b_gate = jnp.zeros((1, LANES), F32).at[0, :2 * M_HEADS].set(b_in[l][gate_lo:gate_hi])
        attn_col0 = gate_lo // aw
        proj, gates_c, *qkv_dil = _in_proj(x2, sc1, sh1, w_main, b_main, w_gate, b_gate, seq, attn_col0)

        gates_r = gates_c[:, :2 * M_HEADS].reshape(bsz, seq, 2 * M_HEADS).transpose(0, 2, 1)
        y_m = _mlstm(proj, gates_c, gates_r, conv_w[l], conv_b[l].reshape(1, -1),
                     m_norm_g[l].reshape(1, -1), bsz, seq)

        outs, lses = [], []
        for dil in DILATIONS:
            if dil == 1:
                o_d, lse_d = _attn_group(proj.reshape(bsz, 1, seq, -1), dil, attn_col0)
            else:
                o_d, lse_d = _attn_group(qkv_dil[DILATIONS.index(dil) - 1], dil, 0)
            outs.append(o_d)
            lses.append(lse_d)

        rw = jnp.zeros((d, LANES), BF16).at[:, :N_EXPERTS].set(router_w[l].astype(BF16))
        rb = jnp.zeros((1, LANES), F32).at[0, :N_EXPERTS].set(router_b[l])
        x1, h2, top_idx, gates = _out_proj(y_m, outs, lses, a_norm_g[l].reshape(1, -1), w_out[l].astype(BF16),
                                           x2, g1, sc2, sh2, ln1_g[l].reshape(1, -1), ln1_b[l].reshape(1, -1),
                                           rw, rb, alpha, seq)

        (tok_pad, dest, item_e, item_row0, item_nch, item_valid, item_tail,
         n_rows) = _routing_tables(top_idx[:, :TOP_K])
        y_pad = _experts(h2, tok_pad, item_e, item_row0, item_nch, item_valid, item_tail,
                         w_gu[l], b_gu[l], w_dn[l], b_dn[l], n_rows)
        x2 = _combine(dest, y_pad, gates, x1, g2, ln2_g[l].reshape(1, -1), ln2_b[l].reshape(1, -1), alpha, seq)
    return x2.reshape(bsz, seq, d)
```

```python
import functools

import jax
import jax.numpy as jnp
import numpy as np
from jax import lax
from jax.experimental import pallas as pl
from jax.experimental.pallas import tpu as pltpu

F32 = jnp.float32
BF16 = jnp.bfloat16
I32 = jnp.int32

M_HEADS = 4
M_DQK = 128
M_DV = 256
CONV_WIDTH = 4
A_HEADS = 16
A_DH = 64
ATTN_BLOCK = 128
DILATIONS = (1, 4, 16)
N_EXPERTS = 32
TOP_K = 4
SWIGLU_ALPHA = 1.702
SWIGLU_LIMIT = 7.0
EPS = 1e-5

LANES = 128
VMEM_LIMIT = 56 * 1024 * 1024

MLSTM_CHUNK = 256
MOE_ROW_PAD = 256
MOE_CHUNK = 256
MOE_ITEM_ROWS = 1280
MOE_TN = 1024
BIG_VMEM_LIMIT = 60 * 1024 * 1024
COMBINE_TOKENS = 512
DMA_UNROLL = 8


def _cparams(sem, vmem=VMEM_LIMIT):
    return pltpu.CompilerParams(dimension_semantics=sem, vmem_limit_bytes=vmem)


def _sigmoid(x):
    return 1.0 / (1.0 + jnp.exp(-x))


def _log_sigmoid(x):
    return jnp.minimum(x, 0.0) - jnp.log(1.0 + jnp.exp(-jnp.abs(x)))


def _layer_norm(z, g, b):
    mu = jnp.mean(z, axis=-1, keepdims=True)
    zc = z - mu
    var = jnp.mean(zc * zc, axis=-1, keepdims=True)
    return zc * lax.rsqrt(var + EPS) * g + b


def _dot(a, b):
    return jnp.dot(a, b, preferred_element_type=F32)


def _dot_nt(a, b):
    return lax.dot_general(a, b, (((1,), (1,)), ((), ())), preferred_element_type=F32)


HIGH_HALF = np.uint32(0xFFFF0000)


def _pack_bf16_pairs(x):
    half = x.shape[1] // 2
    xb = x.astype(BF16).astype(F32)
    lo = lax.bitcast_convert_type(xb[:, :half], jnp.uint32)
    hi = lax.bitcast_convert_type(xb[:, half:], jnp.uint32)
    return jnp.bitwise_or(lax.shift_right_logical(lo, jnp.uint32(16)), jnp.bitwise_and(hi, HIGH_HALF))


def _unpack_bf16_pairs(w):
    lo = lax.bitcast_convert_type(lax.shift_left(w, jnp.uint32(16)), F32)
    hi = lax.bitcast_convert_type(jnp.bitwise_and(w, HIGH_HALF), F32)
    return lo.astype(BF16), hi.astype(BF16)


def _dot_hilo(a, sel):
    hi = a.astype(BF16)
    lo = (a - hi.astype(F32)).astype(BF16)
    return _dot(hi, sel) + _dot(lo, sel)


def _ada_kernel(c_ref, w_ref, b_ref, o_ref):
    c = c_ref[...]
    cond = c * _sigmoid(c)
    o_ref[...] = _dot(cond.astype(BF16), w_ref[...].astype(BF16)) + b_ref[...]


def _ada_mod(c, w_ada, b_ada):
    bsz, d = c.shape
    n = w_ada.shape[1]
    tn = 1024
    rows = 8
    c_pad = jnp.zeros((rows, d), F32).at[:bsz].set(c)
    out = pl.pallas_call(
        _ada_kernel,
        grid=(n // tn,),
        in_specs=[pl.BlockSpec((rows, d), lambda j: (0, 0)),
                  pl.BlockSpec((d, tn), lambda j: (0, j)),
                  pl.BlockSpec((1, tn), lambda j: (0, j))],
        out_specs=pl.BlockSpec((rows, tn), lambda j: (0, j)),
        out_shape=jax.ShapeDtypeStruct((rows, n), F32),
        compiler_params=_cparams(("arbitrary",)),
        name="ada_mod",
    )(c_pad, w_ada, b_ada.reshape(1, n))
    return out[:bsz]


def _inproj_kernel(attn_col0, x_ref, sc_ref, sh_ref, wm_ref, wa_ref, b_ref, wg_ref, bg_ref, o_ref, g_ref, *rest):
    dil_refs, (h_ref, r_scr) = rest[:-2], rest[-2:]
    j = pl.program_id(1)

    @pl.when(j == 0)
    def _():
        h = x_ref[...] * (1.0 + sc_ref[0]) + sh_ref[0]
        hb = h.astype(BF16)
        h_ref[...] = hb
        g_ref[...] = _dot(hb, wg_ref[...]) + bg_ref[...]

    @pl.when(j < attn_col0)
    def _():
        o_ref[...] = (_dot(h_ref[...], wm_ref[...]) + b_ref[...]).astype(BF16)

    @pl.when(j >= attn_col0)
    def _():
        res = _dot(h_ref[...], wa_ref[...]) + b_ref[...]
        o_ref[...] = res.astype(BF16)
        cols = res.shape[1] // LANES
        for c in range(cols):
            r_scr[c] = res[:, c * LANES:(c + 1) * LANES]
        for ref in dil_refs:
            d, n = ref.shape[1], ref.shape[2]
            for r in range(d):
                for c in range(cols):
                    ref[0, r, :, c * LANES:(c + 1) * LANES] = r_scr[c, pl.ds(r, n, stride=d), :].astype(BF16)


def _in_proj(x2, sc, sh, w_mlstm, w_attn, b_main, w_gate, b_gate, seq):
    t, d = x2.shape
    n = w_mlstm.shape[1] + w_attn.shape[1]
    tm, tn = 1024, 1024
    attn_col0 = w_mlstm.shape[1] // tn
    per_b = seq // tm
    bsz = t // seq
    dils = [dl for dl in DILATIONS if dl > 1]
    aw3 = n - attn_col0 * tn
    dil_specs = [pl.BlockSpec((1, dl, tm // dl, tn),
                              lambda i, j: (i // per_b, 0, i % per_b, jnp.maximum(j - attn_col0, 0))) for dl in dils]
    dil_shapes = [jax.ShapeDtypeStruct((bsz, dl, seq // dl, aw3), BF16) for dl in dils]
    return pl.pallas_call(
        functools.partial(_inproj_kernel, attn_col0),
        grid=(t // tm, n // tn),
        in_specs=[pl.BlockSpec((tm, d), lambda i, j: (i, 0)),
                  pl.BlockSpec((1, 1, d), lambda i, j: (i // per_b, 0, 0)),
                  pl.BlockSpec((1, 1, d), lambda i, j: (i // per_b, 0, 0)),
                  pl.BlockSpec((d, tn), lambda i, j: (0, jnp.minimum(j, attn_col0 - 1))),
                  pl.BlockSpec((d, tn), lambda i, j: (0, jnp.maximum(j - attn_col0, 0))),
                  pl.BlockSpec((1, tn), lambda i, j: (0, j)),
                  pl.BlockSpec((d, LANES), lambda i, j: (0, 0)),
                  pl.BlockSpec((1, LANES), lambda i, j: (0, 0))],
        out_specs=[pl.BlockSpec((tm, tn), lambda i, j: (i, j)),
                   pl.BlockSpec((tm, LANES), lambda i, j: (i, 0))] + dil_specs,
        out_shape=[jax.ShapeDtypeStruct((t, n), BF16),
                   jax.ShapeDtypeStruct((t, LANES), F32)] + dil_shapes,
        scratch_shapes=[pltpu.VMEM((tm, d), BF16), pltpu.VMEM((tn // LANES, tm, LANES), F32)],
        compiler_params=_cparams(("arbitrary", "arbitrary"), BIG_VMEM_LIMIT),
        name="in_proj",
    )(x2, sc, sh, w_mlstm, w_attn, b_main, w_gate, b_gate)


def _mlstm_kernel(qk_ref, v_ref, o_ref, gc_ref, gr_ref, cw_ref, cb_ref, ng_ref, out_ref,
                  ct_ref, n_ref, m_ref, prev_ref):
    c = pl.program_id(1)
    L = MLSTM_CHUNK

    @pl.when(c == 0)
    def _():
        ct_ref[...] = jnp.zeros_like(ct_ref)
        n_ref[...] = jnp.zeros_like(n_ref)
        m_ref[...] = jnp.zeros_like(m_ref)
        prev_ref[...] = jnp.zeros_like(prev_ref)

    x = qk_ref[...].astype(F32)
    prev = prev_ref[...]
    row = lax.broadcasted_iota(I32, (L, 1), 0)
    y = cw_ref[CONV_WIDTH - 1:CONV_WIDTH, :] * x + cb_ref[...]
    for k in range(1, CONV_WIDTH):
        xs = jnp.where(row < k, pltpu.roll(prev, k, 0), pltpu.roll(x, k, 0))
        y = y + cw_ref[CONV_WIDTH - 1 - k:CONV_WIDTH - k, :] * xs
    prev_ref[...] = x
    y = y * _sigmoid(y)

    gc = gc_ref[...]
    gr = gr_ref[0]
    ti = lax.broadcasted_iota(I32, (L, L), 0)
    si = lax.broadcasted_iota(I32, (L, L), 1)
    causal = si <= ti
    tril = causal.astype(F32)
    triu = (ti <= si).astype(F32)
    b_cols = jnp.dot(tril, _log_sigmoid(gc), precision=lax.Precision.HIGHEST, preferred_element_type=F32)
    b_rows = jnp.dot(_log_sigmoid(gr), triu, precision=lax.Precision.HIGHEST, preferred_element_type=F32)

    qk_w = M_HEADS * M_DQK
    heads = range(M_HEADS)
    qf = [y[:, h * M_DQK:(h + 1) * M_DQK] for h in heads]
    kf = [y[:, qk_w + h * M_DQK:qk_w + (h + 1) * M_DQK] * (M_DQK ** -0.5) for h in heads]
    qb = [t.astype(BF16) for t in qf]
    vb = [v_ref[:, h * M_DV:(h + 1) * M_DV] for h in heads]
    bc = [b_cols[:, M_HEADS + h:M_HEADS + h + 1] for h in heads]
    ic = [gc[:, h:h + 1] for h in heads]
    br = [b_rows[M_HEADS + h:M_HEADS + h + 1, :] for h in heads]
    ir = [gr[h:h + 1, :] for h in heads]
    m_prev = [m_ref[h][0:1, 0:1] for h in heads]
    n_prev = [n_ref[h][0:1, :] for h in heads]

    b_last = [bc[h][L - 1:L, :] for h in heads]
    g_col = [b_last[h] - bc[h] + ic[h] for h in heads]
    m_new = [jnp.maximum(b_last[h] + m_prev[h], jnp.max(g_col[h], axis=0, keepdims=True)) for h in heads]
    w_col = [jnp.exp(g_col[h] - m_new[h]) for h in heads]
    decay = [jnp.exp(b_last[h] + m_prev[h] - m_new[h]) for h in heads]

    qk = [_dot_nt(qb[h], kf[h].astype(BF16)) for h in heads]
    q_state = [_dot(qb[h], ct_ref[h].astype(BF16)) for h in heads]
    kv_new = [_dot(kf[h].T.astype(BF16), (w_col[h] * vb[h].astype(F32)).astype(BF16)) for h in heads]

    dm = [jnp.where(causal, bc[h] - br[h] + ir[h], -jnp.inf) for h in heads]
    inter = [bc[h] + m_prev[h] for h in heads]
    mt = [jnp.maximum(inter[h], jnp.max(dm[h], axis=1, keepdims=True)) for h in heads]
    a = [jnp.exp(dm[h] - mt[h]) * qk[h] for h in heads]
    e_int = [jnp.exp(inter[h] - mt[h]) for h in heads]
    av = [_dot(a[h].astype(BF16), vb[h]) for h in heads]

    for h in heads:
        num = av[h] + e_int[h] * q_state[h]
        den = (jnp.sum(a[h], axis=1, keepdims=True)
               + e_int[h] * jnp.sum(qf[h] * n_prev[h], axis=1, keepdims=True))
        hh = num / jnp.maximum(jnp.abs(den), jnp.exp(-mt[h]))

        ct_ref[h] = decay[h] * ct_ref[h] + kv_new[h]
        n_new = decay[h] * n_prev[h] + jnp.sum(w_col[h] * kf[h], axis=0, keepdims=True)
        n_ref[h] = jnp.broadcast_to(n_new, n_ref.shape[1:])
        m_ref[h] = jnp.broadcast_to(m_new[h], m_ref.shape[1:])

        ms = jnp.mean(hh * hh, axis=1, keepdims=True)
        og = o_ref[:, h * M_DV:(h + 1) * M_DV].astype(F32)
        yh = hh * lax.rsqrt(ms + EPS) * ng_ref[:, h * M_DV:(h + 1) * M_DV] * _sigmoid(og)
        out_ref[:, h * M_DV:(h + 1) * M_DV] = yh.astype(BF16)


def _mlstm(proj, gates_c, gates_r, conv_w, conv_b, norm_g, bsz, seq):
    L = MLSTM_CHUNK
    nc = seq // L
    t = bsz * seq
    w = M_HEADS * M_DV
    return pl.pallas_call(
        _mlstm_kernel,
        grid=(bsz, nc),
        in_specs=[pl.BlockSpec((L, w), lambda b, c: (b * nc + c, 0)),
                  pl.BlockSpec((L, w), lambda b, c: (b * nc + c, 1)),
                  pl.BlockSpec((L, w), lambda b, c: (b * nc + c, 2)),
                  pl.BlockSpec((L, LANES), lambda b, c: (b * nc + c, 0)),
                  pl.BlockSpec((1, 8, L), lambda b, c: (b, 0, c)),
                  pl.BlockSpec((CONV_WIDTH, w), lambda b, c: (0, 0)),
                  pl.BlockSpec((1, w), lambda b, c: (0, 0)),
                  pl.BlockSpec((1, w), lambda b, c: (0, 0))],
        out_specs=pl.BlockSpec((L, w), lambda b, c: (b * nc + c, 0)),
        out_shape=jax.ShapeDtypeStruct((t, w), BF16),
        scratch_shapes=[pltpu.VMEM((M_HEADS, M_DQK, M_DV), F32),
                        pltpu.VMEM((M_HEADS, 8, M_DQK), F32),
                        pltpu.VMEM((M_HEADS, 8, LANES), F32),
                        pltpu.VMEM((L, w), F32)],
        compiler_params=_cparams(("arbitrary", "arbitrary")),
        name="mlstm",
    )(proj, proj, proj, gates_c, gates_r, conv_w, conv_b, norm_g)


def _attn_kernel(dilation, has_prev, *refs):
    nq = ATTN_BLOCK
    if has_prev:
        q_ref, kc_ref, vc_ref, kp_ref, vp_ref, o_ref, lse_ref, k_all, v_all = refs
        k_all[0:nq, :] = kp_ref[0, 0]
        k_all[nq:2 * nq, :] = kc_ref[0, 0]
        v_all[0:nq, :] = vp_ref[0, 0]
        v_all[nq:2 * nq, :] = vc_ref[0, 0]
        nk = 2 * nq
    else:
        q_ref, k_all, v_all, o_ref, lse_ref = refs
        k_all, v_all = k_all.at[0, 0], v_all.at[0, 0]
        nk = nq
    n = pl.program_id(2)
    qi = lax.broadcasted_iota(I32, (nq, nk), 0)
    ki = lax.broadcasted_iota(I32, (nq, nk), 1)
    dist = qi - ki + (nk - nq)
    ok = jnp.logical_and(dist >= 0, dist <= nq)
    if has_prev:
        ok = jnp.logical_and(ok, jnp.logical_or(ki >= nq, n > 0))
    dist_f = jnp.where(ok, dist.astype(F32), jnp.inf)
    lane = lax.broadcasted_iota(I32, (nq, LANES), 1)
    left_q = lane < A_DH
    left_k = lax.broadcasted_iota(I32, (nk, LANES), 1) < A_DH
    n_pairs = A_HEADS // 2

    scores = []
    for p in range(n_pairs):
        cols = slice(p * LANES, (p + 1) * LANES)
        qp = q_ref[0, 0, :, cols] * (A_DH ** -0.5)
        kp = k_all[:, cols]
        zero = jnp.zeros_like(qp)
        scores.append(_dot_nt(jnp.where(left_q, qp, zero), kp))
        scores.append(_dot_nt(jnp.where(left_q, zero, qp), kp))
    probs, maxes = [], []
    for h in range(A_HEADS):
        coef = -(2.0 ** (-8.0 * (h + 1) / A_HEADS)) * dilation
        s = scores[h] + dist_f * coef
        m = jnp.max(s, axis=1, keepdims=True)
        probs.append(jnp.exp(s - m).astype(BF16))
        maxes.append(m)
    lse_tile = jnp.zeros((nq, LANES), F32)
    for p in range(n_pairs):
        cols = slice(p * LANES, (p + 1) * LANES)
        vp = v_all[:, cols]
        one = jnp.ones_like(vp)
        pv_e = _dot(probs[2 * p], jnp.where(left_k, vp, one))
        pv_o = _dot(probs[2 * p + 1], jnp.where(left_k, one, vp))
        num = jnp.where(left_q, pv_e, pv_o)
        den = pltpu.roll(jnp.where(left_q, pv_o, pv_e), A_DH, 1)
        o_ref[0, 0, :, cols] = (num / den).astype(BF16)
        lse_tile = jnp.where(lane == 2 * p, maxes[2 * p] + jnp.log(pv_e[:, A_DH:A_DH + 1]), lse_tile)
        lse_tile = jnp.where(lane == 2 * p + 1, maxes[2 * p + 1] + jnp.log(pv_o[:, 0:1]), lse_tile)
    lse_ref[0, 0] = lse_tile


def _attn_group(qkv, dilation, col0):
    bsz, d, ls, _ = qkv.shape
    aw = A_HEADS * A_DH
    nq = ATTN_BLOCK
    nb = ls // nq
    has_prev = nb > 1
    blk = (1, 1, nq, aw)
    in_specs = [pl.BlockSpec(blk, lambda b, r, n: (b, r, n, col0)),
                pl.BlockSpec(blk, lambda b, r, n: (b, r, n, col0 + 1)),
                pl.BlockSpec(blk, lambda b, r, n: (b, r, n, col0 + 2))]
    args = [qkv, qkv, qkv]
    if has_prev:
        in_specs += [pl.BlockSpec(blk, lambda b, r, n: (b, r, jnp.maximum(n - 1, 0), col0 + 1)),
                     pl.BlockSpec(blk, lambda b, r, n: (b, r, jnp.maximum(n - 1, 0), col0 + 2))]
        args += [qkv, qkv]
    return pl.pallas_call(
        functools.partial(_attn_kernel, dilation, has_prev),
        grid=(bsz, d, nb),
        in_specs=in_specs,
        out_specs=[pl.BlockSpec(blk, lambda b, r, n: (b, r, n, 0)),
                   pl.BlockSpec((1, 1, nq, LANES), lambda b, r, n: (b, r, n, 0))],
        out_shape=[jax.ShapeDtypeStruct((bsz, d, ls, aw), BF16),
                   jax.ShapeDtypeStruct((bsz, d, ls, LANES), F32)],
        scratch_shapes=[pltpu.VMEM((2 * nq, aw), BF16)] * 2 if has_prev else [],
        compiler_params=_cparams(("arbitrary", "arbitrary", "arbitrary")),
        name=f"dilated_attn_d{dilation}",
    )(*args)


def _head_maps(n_heads, dh):
    w = n_heads * dh
    e = np.zeros((LANES, w), np.float32)
    for h in range(n_heads):
        e[h, h * dh:(h + 1) * dh] = 1.0
    return jnp.asarray(e, BF16), jnp.asarray(e.T.copy(), BF16)


def _natural_rows(ref, scr):
    d, n, w = ref.shape[1:]
    if d == 1:
        return ref[0, 0].astype(F32)
    cols = w // LANES
    for r in range(d):
        blk = ref[0, r].astype(F32)
        for c in range(cols):
            scr[c, pl.ds(r, n, stride=d), :] = blk[:, c * LANES:(c + 1) * LANES]
    return jnp.concatenate([scr[c] for c in range(cols)], axis=1)


def _merged_heads(o_refs, l_refs, g_ref, e_ref, p_ref, o_scr, l_scr):
    l1, l2, l3 = (_natural_rows(ref, l_scr.at[g]) for g, ref in enumerate(l_refs))
    mx = jnp.maximum(jnp.maximum(l1, l2), l3)
    w1, w2, w3 = jnp.exp(l1 - mx), jnp.exp(l2 - mx), jnp.exp(l3 - mx)
    inv = 1.0 / (w1 + w2 + w3)
    e = e_ref[...]
    o = (_dot_hilo(w1 * inv, e) * _natural_rows(o_refs[0], o_scr.at[0])
         + _dot_hilo(w2 * inv, e) * _natural_rows(o_refs[1], o_scr.at[1])
         + _dot_hilo(w3 * inv, e) * _natural_rows(o_refs[2], o_scr.at[2]))
    ms = _dot_hilo(o * o, p_ref[...]) * (1.0 / A_DH)
    scale = _dot_hilo(lax.rsqrt(ms + EPS), e)
    return (o * scale * g_ref[...]).astype(BF16)


def _outproj_kernel(alpha, ym_ref, o1_ref, o2_ref, o3_ref, l1_ref, l2_ref, l3_ref, ag_ref, e_ref, p_ref,
                    w_ref, x_ref, g1_ref, sc_ref, sh_ref, lg_ref, lb_ref, rw_ref, rb_ref,
                    x1_ref, h2_ref, ti_ref, tg_ref, o_scr, l_scr):
    half = ym_ref.shape[1]
    y_a = _merged_heads((o1_ref, o2_ref, o3_ref), (l1_ref, l2_ref, l3_ref), ag_ref, e_ref, p_ref, o_scr, l_scr)
    y = _dot(ym_ref[...], w_ref[0:half, :]) + _dot(y_a, w_ref[half:2 * half, :])
    z = alpha * x_ref[...] + (1.0 + g1_ref[0]) * y
    x1 = _layer_norm(z, lg_ref[...], lb_ref[...])
    x1_ref[...] = x1
    h2 = x1 * (1.0 + sc_ref[0]) + sh_ref[0]
    h2_ref[...] = _pack_bf16_pairs(h2)
    logits = _dot(h2.astype(BF16), rw_ref[...]) + rb_ref[...]
    lane = lax.broadcasted_iota(I32, logits.shape, 1)
    lane_f = lane.astype(F32)
    work = jnp.where(lane < N_EXPERTS, logits, -jnp.inf)
    idx_tile = jnp.zeros(logits.shape, F32)
    val_tile = jnp.zeros(logits.shape, F32)
    top = None
    denom = None
    for k in range(TOP_K):
        mk = jnp.max(work, axis=1, keepdims=True)
        ik = jnp.min(jnp.where(work == mk, lane_f, float(LANES)), axis=1, keepdims=True)
        work = jnp.where(lane_f == ik, -jnp.inf, work)
        if k == 0:
            top = mk
        ek = jnp.exp(mk - top)
        denom = ek if k == 0 else denom + ek
        idx_tile = jnp.where(lane == k, ik, idx_tile)
        val_tile = jnp.where(lane == k, ek, val_tile)
    ti_ref[...] = idx_tile.astype(I32)
    tg_ref[...] = val_tile / denom


def _out_proj(y_m, outs, lses, norm_g, w_out, x2, g1, sc2, sh2, ln_g, ln_b, rw, rb, alpha, seq):
    t, d = x2.shape
    half = y_m.shape[1]
    aw = outs[0].shape[3]
    tm = 256
    per_b = seq // tm
    row = lambda i: (i, 0)
    const = lambda i: (0, 0)
    mod = lambda i: (i // per_b, 0, 0)
    expand, pool = _head_maps(A_HEADS, A_DH)

    def grouped(arr):
        dl, w = arr.shape[1], arr.shape[3]
        return pl.BlockSpec((1, dl, tm // dl, w), lambda i: (i // per_b, 0, i % per_b, 0))

    return pl.pallas_call(
        functools.partial(_outproj_kernel, alpha),
        grid=(t // tm,),
        in_specs=[pl.BlockSpec((tm, half), row)] + [grouped(a) for a in outs] + [grouped(a) for a in lses]
        + [pl.BlockSpec((1, aw), const), pl.BlockSpec((LANES, aw), const), pl.BlockSpec((aw, LANES), const),
                  pl.BlockSpec((2 * half, d), const), pl.BlockSpec((tm, d), row),
                  pl.BlockSpec((1, 1, d), mod), pl.BlockSpec((1, 1, d), mod), pl.BlockSpec((1, 1, d), mod),
                  pl.BlockSpec((1, d), const), pl.BlockSpec((1, d), const),
                  pl.BlockSpec((d, LANES), const), pl.BlockSpec((1, LANES), const)],
        out_specs=[pl.BlockSpec((tm, d), row), pl.BlockSpec((tm, d // 2), row),
                   pl.BlockSpec((tm, LANES), row), pl.BlockSpec((tm, LANES), row)],
        out_shape=[jax.ShapeDtypeStruct((t, d), F32), jax.ShapeDtypeStruct((t, d // 2), jnp.uint32),
                   jax.ShapeDtypeStruct((t, LANES), I32), jax.ShapeDtypeStruct((t, LANES), F32)],
        scratch_shapes=[pltpu.VMEM((len(outs), aw // LANES, tm, LANES), F32),
                        pltpu.VMEM((len(lses), 1, tm, LANES), F32)],
        compiler_params=_cparams(("arbitrary",)),
        name="out_proj_ln_router",
    )(y_m, *outs, *lses, norm_g, expand, pool, w_out, x2, g1, sc2, sh2, ln_g, ln_b, rw, rb)


def _expert_kernel(nj, ie_ref, ir_ref, ic_ref, iv_ref, tail_ref, tok_hbm, h_hbm, wgu_ref, bgu_ref, wdn_ref,
                   bdn_ref, sel_ref, y_hbm, tok_smem, ring, xb, yacc, wgu_b, wdn_b, gu_scr, gsem, sems):
    i = pl.program_id(0)
    j = pl.program_id(1)
    row0 = ir_ref[i]
    nch = ic_ref[i]
    ch = MOE_CHUNK
    lead = MOE_ITEM_ROWS
    ring_chunks = ring.shape[0] // ch
    per_body = ch // nj

    def chunk_slot(first_row):
        return lax.rem(lax.shift_right_logical(first_row, 8), ring_chunks)

    def load_tokens(first_row):
        rows = pl.ds(lax.shift_right_logical(first_row, 7), lead // LANES)
        cp = pltpu.make_async_copy(tok_hbm.at[rows, :], tok_smem, sems.at[0])
        cp.start()
        cp.wait()

    def gather_rows(first_row, first_idx, count):
        slot = chunk_slot(first_row)
        base = slot * ch + jnp.bitwise_and(first_row, ch - 1)
        for u in range(count):
            idx = first_idx + u
            t = tok_smem[lax.shift_right_logical(idx, 7), jnp.bitwise_and(idx, LANES - 1)]
            pltpu.make_async_copy(h_hbm.at[pl.ds(t, 1), :], ring.at[pl.ds(base + u, 1), :], gsem.at[slot]).start()

    def chunk_wait(first_row):
        slot = chunk_slot(first_row)
        rows = pl.ds(pl.multiple_of(slot * ch, ch), ch)
        pltpu.make_async_copy(h_hbm.at[pl.ds(0, ch), :], ring.at[rows, :], gsem.at[slot]).wait()
        return rows

    @pl.when(jnp.logical_and(i == 0, j == 0))
    def _():
        load_tokens(0)

        def issue(g, carry):
            gather_rows(g * DMA_UNROLL, g * DMA_UNROLL, DMA_UNROLL)
            return carry

        lax.fori_loop(0, lead // DMA_UNROLL, issue, 0)

    def out_wait(count):
        def drain(m, carry):
            pltpu.make_async_copy(yacc.at[pl.ds(0, ch), :], y_hbm.at[pl.ds(0, ch), :], sems.at[1]).wait()
            return carry

        lax.fori_loop(0, count, drain, 0)

    @pl.when(jnp.logical_and(j == 0, nch > 0))
    def _():
        load_tokens(row0 + lead)

        def cast_rows(m, carry):
            src = chunk_wait(row0 + m * ch)
            rows = pl.ds(pl.multiple_of(m * ch, ch), ch)
            half = ring.shape[1]
            xb[rows, 0:half], xb[rows, half:2 * half] = _unpack_bf16_pairs(ring[src, :])
            return carry

        lax.fori_loop(0, nch, cast_rows, 0)

        @pl.when(i > 0)
        def _():
            out_wait(ic_ref[jnp.maximum(i - 1, 0)])

        def seed(m, carry):
            yacc[pl.ds(pl.multiple_of(m * ch, ch), ch), :] = jnp.broadcast_to(bdn_ref[0], (ch, yacc.shape[1]))
            return carry

        lax.fori_loop(0, nch, seed, 0)

    @pl.when(nch > 0)
    def _():
        wgu_b[...] = wgu_ref[0].astype(BF16)
        wdn_b[...] = wdn_ref[0].astype(BF16)
        bgu = bgu_ref[0]
        last = j == nj - 1

        def prefetch(m):
            first_idx = (j * nch + m) * per_body
            gather_rows(row0 + lead + first_idx, first_idx, per_body)

        def gate_up(m):
            rows = pl.ds(pl.multiple_of(m * ch, ch), ch)
            return _dot(xb[rows, :], wgu_b[...]) + bgu

        def finish(m, gu):
            rows = pl.ds(pl.multiple_of(m * ch, ch), ch)
            glu = jnp.minimum(gu, SWIGLU_LIMIT)
            f_glu = glu * _sigmoid(SWIGLU_ALPHA * glu)
            f_lin = jnp.clip(gu, -SWIGLU_LIMIT, SWIGLU_LIMIT) + 1.0
            prod = (pltpu.roll(f_glu, 1, 1) * f_lin).astype(BF16)
            parts = [_dot(prod[:, q * 2 * LANES:(q + 1) * 2 * LANES], sel_ref[...])
                     for q in range(MOE_TN // (2 * LANES))]
            act = jnp.concatenate(parts, axis=1).astype(BF16)
            yacc[rows, :] += _dot(act, wdn_b[...])

        def step(m):
            gu = gu_scr[...]
            gu_scr[...] = gate_up(m + 1)
            finish(m, gu)
            prefetch(m)

        gu_scr[...] = gate_up(0)
        n_steps = nch - 1

        def pair(p, carry):
            step(2 * p)
            step(2 * p + 1)
            return carry

        lax.fori_loop(0, lax.shift_right_logical(n_steps, 1), pair, 0)

        @pl.when(jnp.bitwise_and(n_steps, 1) == 1)
        def _():
            step(n_steps - 1)

        finish(nch - 1, gu_scr[...])
        prefetch(nch - 1)

        @pl.when(last)
        def _():
            def issue(m, carry):
                rows = pl.ds(pl.multiple_of(m * ch, ch), ch)
                dst = pl.ds(pl.multiple_of(row0 + m * ch, ch), ch)
                pltpu.make_async_copy(yacc.at[rows, :], y_hbm.at[dst, :], sems.at[1]).start()
                return carry

            lax.fori_loop(0, nch, issue, 0)

    @pl.when(jnp.logical_and(i == pl.num_programs(0) - 1, j == nj - 1))
    def _():
        out_wait(tail_ref[1])

        def drain(m, carry):
            chunk_wait(tail_ref[0] + m * ch)
            return carry

        lax.fori_loop(0, lead // ch, drain, 0)


def _experts(h2, tok_pad, item_e, item_row0, item_nch, item_valid, item_tail, w_gu, b_gu, w_dn, b_dn, n_rows):
    ne, d, two_de = w_gu.shape
    tn = MOE_TN
    nj = two_de // tn
    assert nj > 1, "the kernel separates its first and last column-tile steps"
    assert MOE_CHUNK % nj == 0 and (MOE_CHUNK // nj) % 8 == 0, "row copies per chunk stage"
    ni = item_e.shape[0]
    ring_rows = MOE_ITEM_ROWS
    sel = np.zeros((2 * LANES, LANES), np.float32)
    sel[2 * np.arange(LANES) + 1, np.arange(LANES)] = 1.0

    def jmap(i, j, iv):
        return jnp.where(iv[i] > 0, j, nj - 1)

    grid_spec = pltpu.PrefetchScalarGridSpec(
        num_scalar_prefetch=5,
        grid=(ni, nj),
        in_specs=[pl.BlockSpec(memory_space=pl.ANY),
                  pl.BlockSpec(memory_space=pl.ANY),
                  pl.BlockSpec((1, d, tn), lambda i, j, ie, ir, ic, iv, it: (ie[i], 0, jmap(i, j, iv))),
                  pl.BlockSpec((1, 1, tn), lambda i, j, ie, ir, ic, iv, it: (ie[i], 0, jmap(i, j, iv))),
                  pl.BlockSpec((1, tn // 2, d), lambda i, j, ie, ir, ic, iv, it: (ie[i], jmap(i, j, iv), 0)),
                  pl.BlockSpec((1, 1, d), lambda i, j, ie, ir, ic, iv, it: (ie[i], 0, 0)),
                  pl.BlockSpec((2 * LANES, LANES), lambda i, j, ie, ir, ic, iv, it: (0, 0))],
        out_specs=pl.BlockSpec(memory_space=pl.ANY),
        scratch_shapes=[pltpu.SMEM((MOE_ITEM_ROWS // LANES, LANES), I32),
                        pltpu.VMEM((ring_rows, d // 2), jnp.uint32),
                        pltpu.VMEM((MOE_ITEM_ROWS, d), BF16),
                        pltpu.VMEM((MOE_ITEM_ROWS, d), F32),
                        pltpu.VMEM((d, tn), BF16),
                        pltpu.VMEM((tn // 2, d), BF16),
                        pltpu.VMEM((MOE_CHUNK, tn), F32),
                        pltpu.SemaphoreType.DMA((ring_rows // MOE_CHUNK,)),
                        pltpu.SemaphoreType.DMA((2,))],
    )
    return pl.pallas_call(
        functools.partial(_expert_kernel, nj),
        grid_spec=grid_spec,
        out_shape=jax.ShapeDtypeStruct((n_rows, d), F32),
        compiler_params=_cparams(("arbitrary", "arbitrary"), BIG_VMEM_LIMIT),
        name="moe_experts",
    )(item_e, item_row0, item_nch, item_valid, item_tail, tok_pad, h2, w_gu, b_gu.reshape(ne, 1, two_de),
      w_dn, b_dn.reshape(ne, 1, d), jnp.asarray(sel, BF16))


def _combine_kernel(alpha, dest_ref, y_hbm, gate_ref, x1_ref, g2_ref, lg_ref, lb_ref, out_ref, buf, sem):
    tc = COMBINE_TOKENS

    group = 8

    def issue(g, carry):
        first = pl.multiple_of(g * group, group)
        for dr in range(group):
            for k in range(TOP_K):
                src = dest_ref[0, 0, (first + dr) * TOP_K + k]
                pltpu.make_async_copy(y_hbm.at[pl.ds(src, 1), :], buf.at[k, pl.ds(first + dr, 1), :],
                                      sem.at[0]).start()
        return carry

    lax.fori_loop(0, tc // group, issue, 0)
    for k in range(TOP_K):
        pltpu.make_async_copy(y_hbm.at[pl.ds(0, tc), :], buf.at[k], sem.at[0]).wait()

    gates = gate_ref[...]
    y = gates[:, 0:1] * buf[0]
    for k in range(1, TOP_K):
        y = y + gates[:, k:k + 1] * buf[k]
    z = alpha * x1_ref[...] + (1.0 + g2_ref[0]) * y
    out_ref[...] = _layer_norm(z, lg_ref[...], lb_ref[...])


def _combine(dest, y_pad, gates, x1, g2, ln_g, ln_b, alpha, seq):
    t, d = x1.shape
    tc = COMBINE_TOKENS
    per_b = seq // tc
    row = lambda i: (i, 0)
    const = lambda i: (0, 0)
    return pl.pallas_call(
        functools.partial(_combine_kernel, alpha),
        grid=(t // tc,),
        in_specs=[pl.BlockSpec((1, 1, tc * TOP_K), lambda i: (i, 0, 0), memory_space=pltpu.SMEM),
                  pl.BlockSpec(memory_space=pl.ANY),
                  pl.BlockSpec((tc, LANES), row), pl.BlockSpec((tc, d), row),
                  pl.BlockSpec((1, 1, d), lambda i: (i // per_b, 0, 0)),
                  pl.BlockSpec((1, d), const), pl.BlockSpec((1, d), const)],
        out_specs=pl.BlockSpec((tc, d), row),
        out_shape=jax.ShapeDtypeStruct((t, d), F32),
        scratch_shapes=[pltpu.VMEM((TOP_K, tc, d), F32), pltpu.SemaphoreType.DMA((1,))],
        compiler_params=_cparams(("arbitrary",)),
        name="moe_combine_ln",
    )(dest.reshape(t // tc, 1, tc * TOP_K), y_pad, gates, x1, g2, ln_g, ln_b)


def _count_le(ends, q):
    return jnp.sum((ends[None, :] <= q[:, None]).astype(I32), axis=1)


def _lookup(table, idx):
    hit = idx[:, None] == jnp.arange(table.shape[0], dtype=I32)
    return jnp.sum(jnp.where(hit, table[None, :], 0), axis=1)


def _routing_tables(top_idx):
    t = top_idx.shape[0]
    tk = t * TOP_K
    experts = jnp.arange(N_EXPERTS, dtype=I32)
    hits = [top_idx[:, k:k + 1] == experts for k in range(TOP_K)]
    onehot = sum(h.astype(I32) for h in hits)
    csum = jnp.cumsum(onehot, axis=0)
    counts = csum[-1]
    earlier = csum - onehot
    starts = jnp.cumsum(counts) - counts
    padded = ((counts + MOE_ROW_PAD - 1) // MOE_ROW_PAD) * MOE_ROW_PAD
    pad_end = jnp.cumsum(padded)
    pad_start = pad_end - padded
    dest = jnp.stack([jnp.sum(jnp.where(h, earlier + pad_start, 0), axis=1) for h in hits], axis=1).reshape(tk)

    n_rows = tk + N_EXPERTS * MOE_ROW_PAD
    n_tab = n_rows + 2 * MOE_ITEM_ROWS
    tok_sorted = (jnp.argsort(top_idx.reshape(tk)) // TOP_K).astype(I32)
    blk = jnp.arange(n_tab // MOE_ROW_PAD, dtype=I32) * MOE_ROW_PAD
    blk_e = jnp.minimum(_count_le(pad_end, blk), N_EXPERTS - 1)
    local = (blk - _lookup(pad_start, blk_e))[:, None] + jnp.arange(MOE_ROW_PAD, dtype=I32)
    src = jnp.clip(_lookup(starts, blk_e)[:, None] + local, 0, tk - 1)
    live = jnp.logical_and(local < _lookup(counts, blk_e)[:, None], (blk < pad_end[-1])[:, None])
    tok_pad = jnp.where(live, tok_sorted[src], 0).reshape(-1, LANES)

    items_per = (padded + MOE_ITEM_ROWS - 1) // MOE_ITEM_ROWS
    item_end = jnp.cumsum(items_per)
    item_start = item_end - items_per
    n_items = N_EXPERTS + n_rows // MOE_ITEM_ROWS
    idx = jnp.arange(n_items, dtype=I32)
    valid = idx < item_end[-1]
    e_i = jnp.minimum(_count_le(item_end, idx), N_EXPERTS - 1)
    k_i = idx - _lookup(item_start, e_i)
    row0 = _lookup(pad_start, e_i) + k_i * MOE_ITEM_ROWS
    nrows = jnp.clip(_lookup(padded, e_i) - k_i * MOE_ITEM_ROWS, 0, MOE_ITEM_ROWS)
    e_last = jnp.sum(jnp.where(idx == item_end[-1] - 1, e_i, 0))
    item_e = jnp.where(valid, e_i, e_last).astype(I32)
    item_row0 = jnp.where(valid, row0, 0).astype(I32)
    item_nch = jnp.where(valid, nrows // MOE_CHUNK, 0).astype(I32)
    item_tail = jnp.stack([pad_end[-1], jnp.sum(jnp.where(idx == item_end[-1] - 1, item_nch, 0))]).astype(I32)
    return tok_pad, dest, item_e, item_row0, item_nch, valid.astype(I32), item_tail, n_rows


def kernel(x, c, w_ada, b_ada, w_in, b_in, conv_w, conv_b, m_norm_g, a_norm_g, w_out, ln1_g, ln1_b,
           router_w, router_b, w_gu, b_gu, w_dn, b_dn, ln2_g, ln2_b):
    bsz, seq, d = x.shape
    depth = w_ada.shape[0]
    t = bsz * seq
    alpha = float((2 * depth) ** 0.25)
    qk_w = 2 * M_HEADS * M_DQK
    mv_w = M_HEADS * M_DV
    aw = A_HEADS * A_DH
    gate_lo = qk_w + 2 * mv_w
    gate_hi = gate_lo + 2 * M_HEADS

    x2 = x.reshape(t, d)
    for l in range(depth):
        mod = _ada_mod(c, w_ada[l], b_ada[l]).reshape(bsz, 6, 1, d)
        sh1, sc1, g1, sh2, sc2, g2 = (mod[:, i] for i in range(6))

        w_mlstm = w_in[l][:, :gate_lo].astype(BF16)
        w_attn = w_in[l][:, gate_hi:].astype(BF16)
        b_main = jnp.concatenate([b_in[l][:gate_lo], b_in[l][gate_hi:]]).reshape(1, -1)
        w_gate = jnp.zeros((d, LANES), BF16).at[:, :2 * M_HEADS].set(w_in[l][:, gate_lo:gate_hi].astype(BF16))
        b_gate = jnp.zeros((1, LANES), F32).at[0, :2 * M_HEADS].set(b_in[l][gate_lo:gate_hi])
        attn_col0 = gate_lo // aw
        proj, gates_c, *qkv_dil = _in_proj(x2, sc1, sh1, w_mlstm, w_attn, b_main, w_gate, b_gate, seq)

        gates_r = gates_c[:, :2 * M_HEADS].reshape(bsz, seq, 2 * M_HEADS).transpose(0, 2, 1)
        y_m = _mlstm(proj, gates_c, gates_r, conv_w[l], conv_b[l].reshape(1, -1),
                     m_norm_g[l].reshape(1, -1), bsz, seq)

        outs, lses = [], []
        for dil in DILATIONS:
            if dil == 1:
                o_d, lse_d = _attn_group(proj.reshape(bsz, 1, seq, -1), dil, attn_col0)
            else:
                o_d, lse_d = _attn_group(qkv_dil[DILATIONS.index(dil) - 1], dil, 0)
            outs.append(o_d)
            lses.append(lse_d)

        rw = jnp.zeros((d, LANES), BF16).at[:, :N_EXPERTS].set(router_w[l].astype(BF16))
        rb = jnp.zeros((1, LANES), F32).at[0, :N_EXPERTS].set(router_b[l])
        x1, h2, top_idx, gates = _out_proj(y_m, outs, lses, a_norm_g[l].reshape(1, -1), w_out[l].astype(BF16),
                                           x2, g1, sc2, sh2, ln1_g[l].reshape(1, -1), ln1_b[l].reshape(1, -1),
                                           rw, rb, alpha, seq)

        (tok_pad, dest, item_e, item_row0, item_nch, item_valid, item_tail,
         n_rows) = _routing_tables(top_idx[:, :TOP_K])
        y_pad = _experts(h2, tok_pad, item_e, item_row0, item_nch, item_valid, item_tail,
                         w_gu[l], b_gu[l], w_dn[l], b_dn[l], n_rows)
        x2 = _combine(dest, y_pad, gates, x1, g2, ln2_g[l].reshape(1, -1), ln2_b[l].reshape(1, -1), alpha, seq)
    return x2.reshape(bsz, seq, d)
```

```python
import functools

import jax
import jax.numpy as jnp
import numpy as np
from jax import lax
from jax.experimental import pallas as pl
from jax.experimental.pallas import tpu as pltpu

F32 = jnp.float32
BF16 = jnp.bfloat16
I32 = jnp.int32

M_HEADS = 4
M_DQK = 128
M_DV = 256
CONV_WIDTH = 4
A_HEADS = 16
A_DH = 64
ATTN_BLOCK = 128
DILATIONS = (1, 4, 16)
N_EXPERTS = 32
TOP_K = 4
SWIGLU_ALPHA = 1.702
SWIGLU_LIMIT = 7.0
EPS = 1e-5

LANES = 128
VMEM_LIMIT = 56 * 1024 * 1024

MLSTM_CHUNK = 256
MOE_ROW_PAD = 128
MOE_CHUNK = 128
MOE_ITEM_ROWS = 1280
MOE_TN = 1024
BIG_VMEM_LIMIT = 60 * 1024 * 1024
COMBINE_TOKENS = 512
DMA_UNROLL = 8


def _cparams(sem, vmem=VMEM_LIMIT):
    return pltpu.CompilerParams(dimension_semantics=sem, vmem_limit_bytes=vmem)


def _sigmoid(x):
    return 1.0 / (1.0 + jnp.exp(-x))


def _log_sigmoid(x):
    return jnp.minimum(x, 0.0) - jnp.log(1.0 + jnp.exp(-jnp.abs(x)))


def _layer_norm(z, g, b):
    mu = jnp.mean(z, axis=-1, keepdims=True)
    zc = z - mu
    var = jnp.mean(zc * zc, axis=-1, keepdims=True)
    return zc * lax.rsqrt(var + EPS) * g + b


def _dot(a, b):
    return jnp.dot(a, b, preferred_element_type=F32)


def _dot_nt(a, b):
    return lax.dot_general(a, b, (((1,), (1,)), ((), ())), preferred_element_type=F32)


HIGH_HALF = np.uint32(0xFFFF0000)


def _pack_bf16_pairs(x):
    half = x.shape[1] // 2
    xb = x.astype(BF16).astype(F32)
    lo = lax.bitcast_convert_type(xb[:, :half], jnp.uint32)
    hi = lax.bitcast_convert_type(xb[:, half:], jnp.uint32)
    return jnp.bitwise_or(lax.shift_right_logical(lo, jnp.uint32(16)), jnp.bitwise_and(hi, HIGH_HALF))


def _unpack_bf16_pairs(w):
    lo = lax.bitcast_convert_type(lax.shift_left(w, jnp.uint32(16)), F32)
    hi = lax.bitcast_convert_type(jnp.bitwise_and(w, HIGH_HALF), F32)
    return lo.astype(BF16), hi.astype(BF16)


def _dot_hilo(a, sel):
    hi = a.astype(BF16)
    lo = (a - hi.astype(F32)).astype(BF16)
    return _dot(hi, sel) + _dot(lo, sel)


def _ada_kernel(c_ref, w_ref, b_ref, o_ref):
    c = c_ref[...]
    cond = c * _sigmoid(c)
    o_ref[...] = _dot(cond.astype(BF16), w_ref[...].astype(BF16)) + b_ref[...]


def _ada_mod(c, w_ada, b_ada):
    bsz, d = c.shape
    n = w_ada.shape[1]
    tn = 1024
    rows = 8
    c_pad = jnp.zeros((rows, d), F32).at[:bsz].set(c)
    out = pl.pallas_call(
        _ada_kernel,
        grid=(n // tn,),
        in_specs=[pl.BlockSpec((rows, d), lambda j: (0, 0)),
                  pl.BlockSpec((d, tn), lambda j: (0, j)),
                  pl.BlockSpec((1, tn), lambda j: (0, j))],
        out_specs=pl.BlockSpec((rows, tn), lambda j: (0, j)),
        out_shape=jax.ShapeDtypeStruct((rows, n), F32),
        compiler_params=_cparams(("arbitrary",)),
        name="ada_mod",
    )(c_pad, w_ada, b_ada.reshape(1, n))
    return out[:bsz]


def _inproj_kernel(attn_col0, x_ref, sc_ref, sh_ref, wm_ref, wa_ref, b_ref, wg_ref, bg_ref, o_ref, g_ref, *rest):
    dil_refs, (h_ref, r_scr) = rest[:-2], rest[-2:]
    j = pl.program_id(1)

    @pl.when(j == 0)
    def _():
        h = x_ref[...] * (1.0 + sc_ref[0]) + sh_ref[0]
        hb = h.astype(BF16)
        h_ref[...] = hb
        g_ref[...] = _dot(hb, wg_ref[...]) + bg_ref[...]

    @pl.when(j < attn_col0)
    def _():
        o_ref[...] = (_dot(h_ref[...], wm_ref[...]) + b_ref[...]).astype(BF16)

    @pl.when(j >= attn_col0)
    def _():
        res = _dot(h_ref[...], wa_ref[...]) + b_ref[...]
        o_ref[...] = res.astype(BF16)
        cols = res.shape[1] // LANES
        for c in range(cols):
            r_scr[c] = res[:, c * LANES:(c + 1) * LANES]
        for ref in dil_refs:
            d, n = ref.shape[1], ref.shape[2]
            for r in range(d):
                for c in range(cols):
                    ref[0, r, :, c * LANES:(c + 1) * LANES] = r_scr[c, pl.ds(r, n, stride=d), :].astype(BF16)


def _in_proj(x2, sc, sh, w_mlstm, w_attn, b_main, w_gate, b_gate, seq):
    t, d = x2.shape
    n = w_mlstm.shape[1] + w_attn.shape[1]
    tm, tn = 1024, 1024
    attn_col0 = w_mlstm.shape[1] // tn
    per_b = seq // tm
    bsz = t // seq
    dils = [dl for dl in DILATIONS if dl > 1]
    aw3 = n - attn_col0 * tn
    dil_specs = [pl.BlockSpec((1, dl, tm // dl, tn),
                              lambda i, j: (i // per_b, 0, i % per_b, jnp.maximum(j - attn_col0, 0))) for dl in dils]
    dil_shapes = [jax.ShapeDtypeStruct((bsz, dl, seq // dl, aw3), BF16) for dl in dils]
    return pl.pallas_call(
        functools.partial(_inproj_kernel, attn_col0),
        grid=(t // tm, n // tn),
        in_specs=[pl.BlockSpec((tm, d), lambda i, j: (i, 0)),
                  pl.BlockSpec((1, 1, d), lambda i, j: (i // per_b, 0, 0)),
                  pl.BlockSpec((1, 1, d), lambda i, j: (i // per_b, 0, 0)),
                  pl.BlockSpec((d, tn), lambda i, j: (0, jnp.minimum(j, attn_col0 - 1))),
                  pl.BlockSpec((d, tn), lambda i, j: (0, jnp.maximum(j - attn_col0, 0))),
                  pl.BlockSpec((1, tn), lambda i, j: (0, j)),
                  pl.BlockSpec((d, LANES), lambda i, j: (0, 0)),
                  pl.BlockSpec((1, LANES), lambda i, j: (0, 0))],
        out_specs=[pl.BlockSpec((tm, tn), lambda i, j: (i, j)),
                   pl.BlockSpec((tm, LANES), lambda i, j: (i, 0))] + dil_specs,
        out_shape=[jax.ShapeDtypeStruct((t, n), BF16),
                   jax.ShapeDtypeStruct((t, LANES), F32)] + dil_shapes,
        scratch_shapes=[pltpu.VMEM((tm, d), BF16), pltpu.VMEM((tn // LANES, tm, LANES), F32)],
        compiler_params=_cparams(("arbitrary", "arbitrary"), BIG_VMEM_LIMIT),
        name="in_proj",
    )(x2, sc, sh, w_mlstm, w_attn, b_main, w_gate, b_gate)


def _mlstm_kernel(qk_ref, v_ref, o_ref, gc_ref, gr_ref, cw_ref, cb_ref, ng_ref, out_ref,
                  ct_ref, n_ref, m_ref, prev_ref):
    c = pl.program_id(1)
    L = MLSTM_CHUNK

    @pl.when(c == 0)
    def _():
        ct_ref[...] = jnp.zeros_like(ct_ref)
        n_ref[...] = jnp.zeros_like(n_ref)
        m_ref[...] = jnp.zeros_like(m_ref)
        prev_ref[...] = jnp.zeros_like(prev_ref)

    x = qk_ref[...].astype(F32)
    prev = prev_ref[...]
    row = lax.broadcasted_iota(I32, (L, 1), 0)
    y = cw_ref[CONV_WIDTH - 1:CONV_WIDTH, :] * x + cb_ref[...]
    for k in range(1, CONV_WIDTH):
        xs = jnp.where(row < k, pltpu.roll(prev, k, 0), pltpu.roll(x, k, 0))
        y = y + cw_ref[CONV_WIDTH - 1 - k:CONV_WIDTH - k, :] * xs
    prev_ref[...] = x
    y = y * _sigmoid(y)

    gc = gc_ref[...]
    gr = gr_ref[0]
    ti = lax.broadcasted_iota(I32, (L, L), 0)
    si = lax.broadcasted_iota(I32, (L, L), 1)
    causal = si <= ti
    tril = causal.astype(F32)
    triu = (ti <= si).astype(F32)
    b_cols = jnp.dot(tril, _log_sigmoid(gc), precision=lax.Precision.HIGHEST, preferred_element_type=F32)
    b_rows = jnp.dot(_log_sigmoid(gr), triu, precision=lax.Precision.HIGHEST, preferred_element_type=F32)

    qk_w = M_HEADS * M_DQK
    heads = range(M_HEADS)
    qf = [y[:, h * M_DQK:(h + 1) * M_DQK] for h in heads]
    kf = [y[:, qk_w + h * M_DQK:qk_w + (h + 1) * M_DQK] * (M_DQK ** -0.5) for h in heads]
    qb = [t.astype(BF16) for t in qf]
    vb = [v_ref[:, h * M_DV:(h + 1) * M_DV] for h in heads]
    bc = [b_cols[:, M_HEADS + h:M_HEADS + h + 1] for h in heads]
    ic = [gc[:, h:h + 1] for h in heads]
    br = [b_rows[M_HEADS + h:M_HEADS + h + 1, :] for h in heads]
    ir = [gr[h:h + 1, :] for h in heads]
    m_prev = [m_ref[h][0:1, 0:1] for h in heads]
    n_prev = [n_ref[h][0:1, :] for h in heads]

    b_last = [bc[h][L - 1:L, :] for h in heads]
    g_col = [b_last[h] - bc[h] + ic[h] for h in heads]
    m_new = [jnp.maximum(b_last[h] + m_prev[h], jnp.max(g_col[h], axis=0, keepdims=True)) for h in heads]
    w_col = [jnp.exp(g_col[h] - m_new[h]) for h in heads]
    decay = [jnp.exp(b_last[h] + m_prev[h] - m_new[h]) for h in heads]

    qk = [_dot_nt(qb[h], kf[h].astype(BF16)) for h in heads]
    q_state = [_dot(qb[h], ct_ref[h].astype(BF16)) for h in heads]
    kv_new = [_dot(kf[h].T.astype(BF16), (w_col[h] * vb[h].astype(F32)).astype(BF16)) for h in heads]

    dm = [jnp.where(causal, bc[h] - br[h] + ir[h], -jnp.inf) for h in heads]
    inter = [bc[h] + m_prev[h] for h in heads]
    mt = [jnp.maximum(inter[h], jnp.max(dm[h], axis=1, keepdims=True)) for h in heads]
    a = [jnp.exp(dm[h] - mt[h]) * qk[h] for h in heads]
    e_int = [jnp.exp(inter[h] - mt[h]) for h in heads]
    av = [_dot(a[h].astype(BF16), vb[h]) for h in heads]

    for h in heads:
        num = av[h] + e_int[h] * q_state[h]
        den = (jnp.sum(a[h], axis=1, keepdims=True)
               + e_int[h] * jnp.sum(qf[h] * n_prev[h], axis=1, keepdims=True))
        hh = num / jnp.maximum(jnp.abs(den), jnp.exp(-mt[h]))

        ct_ref[h] = decay[h] * ct_ref[h] + kv_new[h]
        n_new = decay[h] * n_prev[h] + jnp.sum(w_col[h] * kf[h], axis=0, keepdims=True)
        n_ref[h] = jnp.broadcast_to(n_new, n_ref.shape[1:])
        m_ref[h] = jnp.broadcast_to(m_new[h], m_ref.shape[1:])

        ms = jnp.mean(hh * hh, axis=1, keepdims=True)
        og = o_ref[:, h * M_DV:(h + 1) * M_DV].astype(F32)
        yh = hh * lax.rsqrt(ms + EPS) * ng_ref[:, h * M_DV:(h + 1) * M_DV] * _sigmoid(og)
        out_ref[:, h * M_DV:(h + 1) * M_DV] = yh.astype(BF16)


def _mlstm(proj, gates_c, gates_r, conv_w, conv_b, norm_g, bsz, seq):
    L = MLSTM_CHUNK
    nc = seq // L
    t = bsz * seq
    w = M_HEADS * M_DV
    return pl.pallas_call(
        _mlstm_kernel,
        grid=(bsz, nc),
        in_specs=[pl.BlockSpec((L, w), lambda b, c: (b * nc + c, 0)),
                  pl.BlockSpec((L, w), lambda b, c: (b * nc + c, 1)),
                  pl.BlockSpec((L, w), lambda b, c: (b * nc + c, 2)),
                  pl.BlockSpec((L, LANES), lambda b, c: (b * nc + c, 0)),
                  pl.BlockSpec((1, 8, L), lambda b, c: (b, 0, c)),
                  pl.BlockSpec((CONV_WIDTH, w), lambda b, c: (0, 0)),
                  pl.BlockSpec((1, w), lambda b, c: (0, 0)),
                  pl.BlockSpec((1, w), lambda b, c: (0, 0))],
        out_specs=pl.BlockSpec((L, w), lambda b, c: (b * nc + c, 0)),
        out_shape=jax.ShapeDtypeStruct((t, w), BF16),
        scratch_shapes=[pltpu.VMEM((M_HEADS, M_DQK, M_DV), F32),
                        pltpu.VMEM((M_HEADS, 8, M_DQK), F32),
                        pltpu.VMEM((M_HEADS, 8, LANES), F32),
                        pltpu.VMEM((L, w), F32)],
        compiler_params=_cparams(("arbitrary", "arbitrary")),
        name="mlstm",
    )(proj, proj, proj, gates_c, gates_r, conv_w, conv_b, norm_g)


def _attn_kernel(dilation, has_prev, *refs):
    nq = ATTN_BLOCK
    if has_prev:
        q_ref, kc_ref, vc_ref, kp_ref, vp_ref, o_ref, lse_ref, k_all, v_all = refs
        k_all[0:nq, :] = kp_ref[0, 0]
        k_all[nq:2 * nq, :] = kc_ref[0, 0]
        v_all[0:nq, :] = vp_ref[0, 0]
        v_all[nq:2 * nq, :] = vc_ref[0, 0]
        nk = 2 * nq
    else:
        q_ref, k_all, v_all, o_ref, lse_ref = refs
        k_all, v_all = k_all.at[0, 0], v_all.at[0, 0]
        nk = nq
    n = pl.program_id(2)
    qi = lax.broadcasted_iota(I32, (nq, nk), 0)
    ki = lax.broadcasted_iota(I32, (nq, nk), 1)
    dist = qi - ki + (nk - nq)
    ok = jnp.logical_and(dist >= 0, dist <= nq)
    if has_prev:
        ok = jnp.logical_and(ok, jnp.logical_or(ki >= nq, n > 0))
    dist_f = jnp.where(ok, dist.astype(F32), jnp.inf)
    lane = lax.broadcasted_iota(I32, (nq, LANES), 1)
    left_q = lane < A_DH
    left_k = lax.broadcasted_iota(I32, (nk, LANES), 1) < A_DH
    n_pairs = A_HEADS // 2

    scores = []
    for p in range(n_pairs):
        cols = slice(p * LANES, (p + 1) * LANES)
        qp = q_ref[0, 0, :, cols] * (A_DH ** -0.5)
        kp = k_all[:, cols]
        zero = jnp.zeros_like(qp)
        scores.append(_dot_nt(jnp.where(left_q, qp, zero), kp))
        scores.append(_dot_nt(jnp.where(left_q, zero, qp), kp))
    probs, maxes = [], []
    for h in range(A_HEADS):
        coef = -(2.0 ** (-8.0 * (h + 1) / A_HEADS)) * dilation
        s = scores[h] + dist_f * coef
        m = jnp.max(s, axis=1, keepdims=True)
        probs.append(jnp.exp(s - m).astype(BF16))
        maxes.append(m)
    lse_tile = jnp.zeros((nq, LANES), F32)
    for p in range(n_pairs):
        cols = slice(p * LANES, (p + 1) * LANES)
        vp = v_all[:, cols]
        one = jnp.ones_like(vp)
        pv_e = _dot(probs[2 * p], jnp.where(left_k, vp, one))
        pv_o = _dot(probs[2 * p + 1], jnp.where(left_k, one, vp))
        num = jnp.where(left_q, pv_e, pv_o)
        den = pltpu.roll(jnp.where(left_q, pv_o, pv_e), A_DH, 1)
        o_ref[0, 0, :, cols] = (num / den).astype(BF16)
        lse_tile = jnp.where(lane == 2 * p, maxes[2 * p] + jnp.log(pv_e[:, A_DH:A_DH + 1]), lse_tile)
        lse_tile = jnp.where(lane == 2 * p + 1, maxes[2 * p + 1] + jnp.log(pv_o[:, 0:1]), lse_tile)
    lse_ref[0, 0] = lse_tile


def _attn_group(qkv, dilation, col0):
    bsz, d, ls, _ = qkv.shape
    aw = A_HEADS * A_DH
    nq = ATTN_BLOCK
    nb = ls // nq
    has_prev = nb > 1
    blk = (1, 1, nq, aw)
    in_specs = [pl.BlockSpec(blk, lambda b, r, n: (b, r, n, col0)),
                pl.BlockSpec(blk, lambda b, r, n: (b, r, n, col0 + 1)),
                pl.BlockSpec(blk, lambda b, r, n: (b, r, n, col0 + 2))]
    args = [qkv, qkv, qkv]
    if has_prev:
        in_specs += [pl.BlockSpec(blk, lambda b, r, n: (b, r, jnp.maximum(n - 1, 0), col0 + 1)),
                     pl.BlockSpec(blk, lambda b, r, n: (b, r, jnp.maximum(n - 1, 0), col0 + 2))]
        args += [qkv, qkv]
    return pl.pallas_call(
        functools.partial(_attn_kernel, dilation, has_prev),
        grid=(bsz, d, nb),
        in_specs=in_specs,
        out_specs=[pl.BlockSpec(blk, lambda b, r, n: (b, r, n, 0)),
                   pl.BlockSpec((1, 1, nq, LANES), lambda b, r, n: (b, r, n, 0))],
        out_shape=[jax.ShapeDtypeStruct((bsz, d, ls, aw), BF16),
                   jax.ShapeDtypeStruct((bsz, d, ls, LANES), F32)],
        scratch_shapes=[pltpu.VMEM((2 * nq, aw), BF16)] * 2 if has_prev else [],
        compiler_params=_cparams(("arbitrary", "arbitrary", "arbitrary")),
        name=f"dilated_attn_d{dilation}",
    )(*args)


def _head_maps(n_heads, dh):
    w = n_heads * dh
    e = np.zeros((LANES, w), np.float32)
    for h in range(n_heads):
        e[h, h * dh:(h + 1) * dh] = 1.0
    return jnp.asarray(e, BF16), jnp.asarray(e.T.copy(), BF16)


def _natural_rows(ref, scr):
    d, n, w = ref.shape[1:]
    if d == 1:
        return ref[0, 0].astype(F32)
    cols = w // LANES
    for r in range(d):
        blk = ref[0, r].astype(F32)
        for c in range(cols):
            scr[c, pl.ds(r, n, stride=d), :] = blk[:, c * LANES:(c + 1) * LANES]
    return jnp.concatenate([scr[c] for c in range(cols)], axis=1)


def _merged_heads(o_refs, l_refs, g_ref, e_ref, p_ref, o_scr, l_scr):
    l1, l2, l3 = (_natural_rows(ref, l_scr.at[g]) for g, ref in enumerate(l_refs))
    mx = jnp.maximum(jnp.maximum(l1, l2), l3)
    w1, w2, w3 = jnp.exp(l1 - mx), jnp.exp(l2 - mx), jnp.exp(l3 - mx)
    inv = 1.0 / (w1 + w2 + w3)
    e = e_ref[...]
    o = (_dot_hilo(w1 * inv, e) * _natural_rows(o_refs[0], o_scr.at[0])
         + _dot_hilo(w2 * inv, e) * _natural_rows(o_refs[1], o_scr.at[1])
         + _dot_hilo(w3 * inv, e) * _natural_rows(o_refs[2], o_scr.at[2]))
    ms = _dot_hilo(o * o, p_ref[...]) * (1.0 / A_DH)
    scale = _dot_hilo(lax.rsqrt(ms + EPS), e)
    return (o * scale * g_ref[...]).astype(BF16)


def _outproj_kernel(alpha, ym_ref, o1_ref, o2_ref, o3_ref, l1_ref, l2_ref, l3_ref, ag_ref, e_ref, p_ref,
                    w_ref, x_ref, g1_ref, sc_ref, sh_ref, lg_ref, lb_ref, rw_ref, rb_ref,
                    x1_ref, h2_ref, ti_ref, tg_ref, o_scr, l_scr):
    half = ym_ref.shape[1]
    y_a = _merged_heads((o1_ref, o2_ref, o3_ref), (l1_ref, l2_ref, l3_ref), ag_ref, e_ref, p_ref, o_scr, l_scr)
    y = _dot(ym_ref[...], w_ref[0:half, :]) + _dot(y_a, w_ref[half:2 * half, :])
    z = alpha * x_ref[...] + (1.0 + g1_ref[0]) * y
    x1 = _layer_norm(z, lg_ref[...], lb_ref[...])
    x1_ref[...] = x1
    h2 = x1 * (1.0 + sc_ref[0]) + sh_ref[0]
    h2_ref[...] = _pack_bf16_pairs(h2)
    logits = _dot(h2.astype(BF16), rw_ref[...]) + rb_ref[...]
    lane = lax.broadcasted_iota(I32, logits.shape, 1)
    lane_f = lane.astype(F32)
    work = jnp.where(lane < N_EXPERTS, logits, -jnp.inf)
    idx_tile = jnp.zeros(logits.shape, F32)
    val_tile = jnp.zeros(logits.shape, F32)
    top = None
    denom = None
    for k in range(TOP_K):
        mk = jnp.max(work, axis=1, keepdims=True)
        ik = jnp.min(jnp.where(work == mk, lane_f, float(LANES)), axis=1, keepdims=True)
        work = jnp.where(lane_f == ik, -jnp.inf, work)
        if k == 0:
            top = mk
        ek = jnp.exp(mk - top)
        denom = ek if k == 0 else denom + ek
        idx_tile = jnp.where(lane == k, ik, idx_tile)
        val_tile = jnp.where(lane == k, ek, val_tile)
    ti_ref[...] = idx_tile.astype(I32)
    tg_ref[...] = val_tile / denom


def _out_proj(y_m, outs, lses, norm_g, w_out, x2, g1, sc2, sh2, ln_g, ln_b, rw, rb, alpha, seq):
    t, d = x2.shape
    half = y_m.shape[1]
    aw = outs[0].shape[3]
    tm = 256
    per_b = seq // tm
    row = lambda i: (i, 0)
    const = lambda i: (0, 0)
    mod = lambda i: (i // per_b, 0, 0)
    expand, pool = _head_maps(A_HEADS, A_DH)

    def grouped(arr):
        dl, w = arr.shape[1], arr.shape[3]
        return pl.BlockSpec((1, dl, tm // dl, w), lambda i: (i // per_b, 0, i % per_b, 0))

    return pl.pallas_call(
        functools.partial(_outproj_kernel, alpha),
        grid=(t // tm,),
        in_specs=[pl.BlockSpec((tm, half), row)] + [grouped(a) for a in outs] + [grouped(a) for a in lses]
        + [pl.BlockSpec((1, aw), const), pl.BlockSpec((LANES, aw), const), pl.BlockSpec((aw, LANES), const),
                  pl.BlockSpec((2 * half, d), const), pl.BlockSpec((tm, d), row),
                  pl.BlockSpec((1, 1, d), mod), pl.BlockSpec((1, 1, d), mod), pl.BlockSpec((1, 1, d), mod),
                  pl.BlockSpec((1, d), const), pl.BlockSpec((1, d), const),
                  pl.BlockSpec((d, LANES), const), pl.BlockSpec((1, LANES), const)],
        out_specs=[pl.BlockSpec((tm, d), row), pl.BlockSpec((tm, d // 2), row),
                   pl.BlockSpec((tm, LANES), row), pl.BlockSpec((tm, LANES), row)],
        out_shape=[jax.ShapeDtypeStruct((t, d), F32), jax.ShapeDtypeStruct((t, d // 2), jnp.uint32),
                   jax.ShapeDtypeStruct((t, LANES), I32), jax.ShapeDtypeStruct((t, LANES), F32)],
        scratch_shapes=[pltpu.VMEM((len(outs), aw // LANES, tm, LANES), F32),
                        pltpu.VMEM((len(lses), 1, tm, LANES), F32)],
        compiler_params=_cparams(("arbitrary",)),
        name="out_proj_ln_router",
    )(y_m, *outs, *lses, norm_g, expand, pool, w_out, x2, g1, sc2, sh2, ln_g, ln_b, rw, rb)


def _expert_kernel(nj, ie_ref, ir_ref, ic_ref, iv_ref, tail_ref, tok_hbm, h_hbm, wgu_ref, bgu_ref, wdn_ref,
                   bdn_ref, sel_ref, y_hbm, tok_smem, ring, xb, yacc, wgu_b, wdn_b, gu_scr, gsem, sems):
    i = pl.program_id(0)
    j = pl.program_id(1)
    row0 = ir_ref[i]
    nch = ic_ref[i]
    ch = MOE_CHUNK
    lead = MOE_ITEM_ROWS
    ring_chunks = ring.shape[0] // ch
    per_body = ch // nj

    def chunk_slot(first_row):
        return lax.rem(lax.shift_right_logical(first_row, ch.bit_length() - 1), ring_chunks)

    def load_tokens(first_row):
        rows = pl.ds(lax.shift_right_logical(first_row, 7), lead // LANES)
        cp = pltpu.make_async_copy(tok_hbm.at[rows, :], tok_smem, sems.at[0])
        cp.start()
        cp.wait()

    def gather_rows(first_row, first_idx, count):
        slot = chunk_slot(first_row)
        base = slot * ch + jnp.bitwise_and(first_row, ch - 1)
        for u in range(count):
            idx = first_idx + u
            t = tok_smem[lax.shift_right_logical(idx, 7), jnp.bitwise_and(idx, LANES - 1)]
            pltpu.make_async_copy(h_hbm.at[pl.ds(t, 1), :], ring.at[pl.ds(base + u, 1), :], gsem.at[slot]).start()

    def chunk_wait(first_row):
        slot = chunk_slot(first_row)
        rows = pl.ds(pl.multiple_of(slot * ch, ch), ch)
        pltpu.make_async_copy(h_hbm.at[pl.ds(0, ch), :], ring.at[rows, :], gsem.at[slot]).wait()
        return rows

    @pl.when(jnp.logical_and(i == 0, j == 0))
    def _():
        load_tokens(0)

        def issue(g, carry):
            gather_rows(g * DMA_UNROLL, g * DMA_UNROLL, DMA_UNROLL)
            return carry

        lax.fori_loop(0, lead // DMA_UNROLL, issue, 0)

    def out_wait(count):
        def drain(m, carry):
            pltpu.make_async_copy(yacc.at[pl.ds(0, ch), :], y_hbm.at[pl.ds(0, ch), :], sems.at[1]).wait()
            return carry

        lax.fori_loop(0, count, drain, 0)

    @pl.when(jnp.logical_and(j == 0, nch > 0))
    def _():
        load_tokens(row0 + lead)

        def cast_rows(m, carry):
            src = chunk_wait(row0 + m * ch)
            rows = pl.ds(pl.multiple_of(m * ch, ch), ch)
            half = ring.shape[1]
            xb[rows, 0:half], xb[rows, half:2 * half] = _unpack_bf16_pairs(ring[src, :])
            return carry

        lax.fori_loop(0, nch, cast_rows, 0)

        @pl.when(i > 0)
        def _():
            out_wait(ic_ref[jnp.maximum(i - 1, 0)])

        def seed(m, carry):
            yacc[pl.ds(pl.multiple_of(m * ch, ch), ch), :] = jnp.broadcast_to(bdn_ref[0], (ch, yacc.shape[1]))
            return carry

        lax.fori_loop(0, nch, seed, 0)

    @pl.when(nch > 0)
    def _():
        wgu_b[...] = wgu_ref[0].astype(BF16)
        wdn_b[...] = wdn_ref[0].astype(BF16)
        bgu = bgu_ref[0]
        last = j == nj - 1

        def prefetch(m):
            first_idx = (j * nch + m) * per_body
            gather_rows(row0 + lead + first_idx, first_idx, per_body)

        def gate_up(m):
            rows = pl.ds(pl.multiple_of(m * ch, ch), ch)
            return _dot(xb[rows, :], wgu_b[...]) + bgu

        def finish(m, gu):
            rows = pl.ds(pl.multiple_of(m * ch, ch), ch)
            glu = jnp.minimum(gu, SWIGLU_LIMIT)
            f_glu = glu * _sigmoid(SWIGLU_ALPHA * glu)
            f_lin = jnp.clip(gu, -SWIGLU_LIMIT, SWIGLU_LIMIT) + 1.0
            prod = (pltpu.roll(f_glu, 1, 1) * f_lin).astype(BF16)
            parts = [_dot(prod[:, q * 2 * LANES:(q + 1) * 2 * LANES], sel_ref[...])
                     for q in range(MOE_TN // (2 * LANES))]
            act = jnp.concatenate(parts, axis=1).astype(BF16)
            yacc[rows, :] += _dot(act, wdn_b[...])

        def step(m):
            gu = gu_scr[...]
            gu_scr[...] = gate_up(m + 1)
            finish(m, gu)
            prefetch(m)

        gu_scr[...] = gate_up(0)
        n_steps = nch - 1

        def pair(p, carry):
            step(2 * p)
            step(2 * p + 1)
            return carry

        lax.fori_loop(0, lax.shift_right_logical(n_steps, 1), pair, 0)

        @pl.when(jnp.bitwise_and(n_steps, 1) == 1)
        def _():
            step(n_steps - 1)

        finish(nch - 1, gu_scr[...])
        prefetch(nch - 1)

        @pl.when(last)
        def _():
            def issue(m, carry):
                rows = pl.ds(pl.multiple_of(m * ch, ch), ch)
                dst = pl.ds(pl.multiple_of(row0 + m * ch, ch), ch)
                pltpu.make_async_copy(yacc.at[rows, :], y_hbm.at[dst, :], sems.at[1]).start()
                return carry

            lax.fori_loop(0, nch, issue, 0)

    @pl.when(jnp.logical_and(i == pl.num_programs(0) - 1, j == nj - 1))
    def _():
        out_wait(tail_ref[1])

        def drain(m, carry):
            chunk_wait(tail_ref[0] + m * ch)
            return carry

        lax.fori_loop(0, lead // ch, drain, 0)


def _experts(h2, tok_pad, item_e, item_row0, item_nch, item_valid, item_tail, w_gu, b_gu, w_dn, b_dn, n_rows):
    ne, d, two_de = w_gu.shape
    tn = MOE_TN
    nj = two_de // tn
    assert nj > 1, "the kernel separates its first and last column-tile steps"
    assert MOE_CHUNK % nj == 0 and (MOE_CHUNK // nj) % 8 == 0, "row copies per chunk stage"
    ni = item_e.shape[0]
    ring_rows = MOE_ITEM_ROWS
    sel = np.zeros((2 * LANES, LANES), np.float32)
    sel[2 * np.arange(LANES) + 1, np.arange(LANES)] = 1.0

    def jmap(i, j, iv):
        return jnp.where(iv[i] > 0, j, nj - 1)

    grid_spec = pltpu.PrefetchScalarGridSpec(
        num_scalar_prefetch=5,
        grid=(ni, nj),
        in_specs=[pl.BlockSpec(memory_space=pl.ANY),
                  pl.BlockSpec(memory_space=pl.ANY),
                  pl.BlockSpec((1, d, tn), lambda i, j, ie, ir, ic, iv, it: (ie[i], 0, jmap(i, j, iv))),
                  pl.BlockSpec((1, 1, tn), lambda i, j, ie, ir, ic, iv, it: (ie[i], 0, jmap(i, j, iv))),
                  pl.BlockSpec((1, tn // 2, d), lambda i, j, ie, ir, ic, iv, it: (ie[i], jmap(i, j, iv), 0)),
                  pl.BlockSpec((1, 1, d), lambda i, j, ie, ir, ic, iv, it: (ie[i], 0, 0)),
                  pl.BlockSpec((2 * LANES, LANES), lambda i, j, ie, ir, ic, iv, it: (0, 0))],
        out_specs=pl.BlockSpec(memory_space=pl.ANY),
        scratch_shapes=[pltpu.SMEM((MOE_ITEM_ROWS // LANES, LANES), I32),
                        pltpu.VMEM((ring_rows, d // 2), jnp.uint32),
                        pltpu.VMEM((MOE_ITEM_ROWS, d), BF16),
                        pltpu.VMEM((MOE_ITEM_ROWS, d), F32),
                        pltpu.VMEM((d, tn), BF16),
                        pltpu.VMEM((tn // 2, d), BF16),
                        pltpu.VMEM((MOE_CHUNK, tn), F32),
                        pltpu.SemaphoreType.DMA((ring_rows // MOE_CHUNK,)),
                        pltpu.SemaphoreType.DMA((2,))],
    )
    return pl.pallas_call(
        functools.partial(_expert_kernel, nj),
        grid_spec=grid_spec,
        out_shape=jax.ShapeDtypeStruct((n_rows, d), F32),
        compiler_params=_cparams(("arbitrary", "arbitrary"), BIG_VMEM_LIMIT),
        name="moe_experts",
    )(item_e, item_row0, item_nch, item_valid, item_tail, tok_pad, h2, w_gu, b_gu.reshape(ne, 1, two_de),
      w_dn, b_dn.reshape(ne, 1, d), jnp.asarray(sel, BF16))


def _combine_kernel(alpha, dest_ref, y_hbm, gate_ref, x1_ref, g2_ref, lg_ref, lb_ref, out_ref, buf, sem):
    tc = COMBINE_TOKENS

    group = 8

    def issue(g, carry):
        first = pl.multiple_of(g * group, group)
        for dr in range(group):
            for k in range(TOP_K):
                src = dest_ref[0, 0, (first + dr) * TOP_K + k]
                pltpu.make_async_copy(y_hbm.at[pl.ds(src, 1), :], buf.at[k, pl.ds(first + dr, 1), :],
                                      sem.at[0]).start()
        return carry

    lax.fori_loop(0, tc // group, issue, 0)
    for k in range(TOP_K):
        pltpu.make_async_copy(y_hbm.at[pl.ds(0, tc), :], buf.at[k], sem.at[0]).wait()

    gates = gate_ref[...]
    y = gates[:, 0:1] * buf[0]
    for k in range(1, TOP_K):
        y = y + gates[:, k:k + 1] * buf[k]
    z = alpha * x1_ref[...] + (1.0 + g2_ref[0]) * y
    out_ref[...] = _layer_norm(z, lg_ref[...], lb_ref[...])


def _combine(dest, y_pad, gates, x1, g2, ln_g, ln_b, alpha, seq):
    t, d = x1.shape
    tc = COMBINE_TOKENS
    per_b = seq // tc
    row = lambda i: (i, 0)
    const = lambda i: (0, 0)
    return pl.pallas_call(
        functools.partial(_combine_kernel, alpha),
        grid=(t // tc,),
        in_specs=[pl.BlockSpec((1, 1, tc * TOP_K), lambda i: (i, 0, 0), memory_space=pltpu.SMEM),
                  pl.BlockSpec(memory_space=pl.ANY),
                  pl.BlockSpec((tc, LANES), row), pl.BlockSpec((tc, d), row),
                  pl.BlockSpec((1, 1, d), lambda i: (i // per_b, 0, 0)),
                  pl.BlockSpec((1, d), const), pl.BlockSpec((1, d), const)],
        out_specs=pl.BlockSpec((tc, d), row),
        out_shape=jax.ShapeDtypeStruct((t, d), F32),
        scratch_shapes=[pltpu.VMEM((TOP_K, tc, d), F32), pltpu.SemaphoreType.DMA((1,))],
        compiler_params=_cparams(("arbitrary",)),
        name="moe_combine_ln",
    )(dest.reshape(t // tc, 1, tc * TOP_K), y_pad, gates, x1, g2, ln_g, ln_b)


def _count_le(ends, q):
    return jnp.sum((ends[None, :] <= q[:, None]).astype(I32), axis=1)


def _lookup(table, idx):
    hit = idx[:, None] == jnp.arange(table.shape[0], dtype=I32)
    return jnp.sum(jnp.where(hit, table[None, :], 0), axis=1)


def _routing_tables(top_idx):
    t = top_idx.shape[0]
    tk = t * TOP_K
    experts = jnp.arange(N_EXPERTS, dtype=I32)
    hits = [top_idx[:, k:k + 1] == experts for k in range(TOP_K)]
    onehot = sum(h.astype(I32) for h in hits)
    csum = jnp.cumsum(onehot, axis=0)
    counts = csum[-1]
    earlier = csum - onehot
    starts = jnp.cumsum(counts) - counts
    padded = ((counts + MOE_ROW_PAD - 1) // MOE_ROW_PAD) * MOE_ROW_PAD
    pad_end = jnp.cumsum(padded)
    pad_start = pad_end - padded
    dest = jnp.stack([jnp.sum(jnp.where(h, earlier + pad_start, 0), axis=1) for h in hits], axis=1).reshape(tk)

    n_rows = tk + N_EXPERTS * MOE_ROW_PAD
    n_tab = n_rows + 2 * MOE_ITEM_ROWS
    tok_sorted = (jnp.argsort(top_idx.reshape(tk)) // TOP_K).astype(I32)
    blk = jnp.arange(n_tab // MOE_ROW_PAD, dtype=I32) * MOE_ROW_PAD
    blk_e = jnp.minimum(_count_le(pad_end, blk), N_EXPERTS - 1)
    local = (blk - _lookup(pad_start, blk_e))[:, None] + jnp.arange(MOE_ROW_PAD, dtype=I32)
    src = jnp.clip(_lookup(starts, blk_e)[:, None] + local, 0, tk - 1)
    live = jnp.logical_and(local < _lookup(counts, blk_e)[:, None], (blk < pad_end[-1])[:, None])
    tok_pad = jnp.where(live, tok_sorted[src], 0).reshape(-1, LANES)

    items_per = (padded + MOE_ITEM_ROWS - 1) // MOE_ITEM_ROWS
    item_end = jnp.cumsum(items_per)
    item_start = item_end - items_per
    n_items = N_EXPERTS + n_rows // MOE_ITEM_ROWS
    idx = jnp.arange(n_items, dtype=I32)
    valid = idx < item_end[-1]
    e_i = jnp.minimum(_count_le(item_end, idx), N_EXPERTS - 1)
    k_i = idx - _lookup(item_start, e_i)
    row0 = _lookup(pad_start, e_i) + k_i * MOE_ITEM_ROWS
    nrows = jnp.clip(_lookup(padded, e_i) - k_i * MOE_ITEM_ROWS, 0, MOE_ITEM_ROWS)
    e_last = jnp.sum(jnp.where(idx == item_end[-1] - 1, e_i, 0))
    item_e = jnp.where(valid, e_i, e_last).astype(I32)
    item_row0 = jnp.where(valid, row0, 0).astype(I32)
    item_nch = jnp.where(valid, nrows // MOE_CHUNK, 0).astype(I32)
    item_tail = jnp.stack([pad_end[-1], jnp.sum(jnp.where(idx == item_end[-1] - 1, item_nch, 0))]).astype(I32)
    return tok_pad, dest, item_e, item_row0, item_nch, valid.astype(I32), item_tail, n_rows


def kernel(x, c, w_ada, b_ada, w_in, b_in, conv_w, conv_b, m_norm_g, a_norm_g, w_out, ln1_g, ln1_b,
           router_w, router_b, w_gu, b_gu, w_dn, b_dn, ln2_g, ln2_b):
    bsz, seq, d = x.shape
    depth = w_ada.shape[0]
    t = bsz * seq
    alpha = float((2 * depth) ** 0.25)
    qk_w = 2 * M_HEADS * M_DQK
    mv_w = M_HEADS * M_DV
    aw = A_HEADS * A_DH
    gate_lo = qk_w + 2 * mv_w
    gate_hi = gate_lo + 2 * M_HEADS

    x2 = x.reshape(t, d)
    for l in range(depth):
        mod = _ada_mod(c, w_ada[l], b_ada[l]).reshape(bsz, 6, 1, d)
        sh1, sc1, g1, sh2, sc2, g2 = (mod[:, i] for i in range(6))

        w_mlstm = w_in[l][:, :gate_lo].astype(BF16)
        w_attn = w_in[l][:, gate_hi:].astype(BF16)
        b_main = jnp.concatenate([b_in[l][:gate_lo], b_in[l][gate_hi:]]).reshape(1, -1)
        w_gate = jnp.zeros((d, LANES), BF16).at[:, :2 * M_HEADS].set(w_in[l][:, gate_lo:gate_hi].astype(BF16))
        b_gate = jnp.zeros((1, LANES), F32).at[0, :2 * M_HEADS].set(b_in[l][gate_lo:gate_hi])
        attn_col0 = gate_lo // aw
        proj, gates_c, *qkv_dil = _in_proj(x2, sc1, sh1, w_mlstm, w_attn, b_main, w_gate, b_gate, seq)

        gates_r = gates_c[:, :2 * M_HEADS].reshape(bsz, seq, 2 * M_HEADS).transpose(0, 2, 1)
        y_m = _mlstm(proj, gates_c, gates_r, conv_w[l], conv_b[l].reshape(1, -1),
                     m_norm_g[l].reshape(1, -1), bsz, seq)

        outs, lses = [], []
        for dil in DILATIONS:
            if dil == 1:
                o_d, lse_d = _attn_group(proj.reshape(bsz, 1, seq, -1), dil, attn_col0)
            else:
                o_d, lse_d = _attn_group(qkv_dil[DILATIONS.index(dil) - 1], dil, 0)
            outs.append(o_d)
            lses.append(lse_d)

        rw = jnp.zeros((d, LANES), BF16).at[:, :N_EXPERTS].set(router_w[l].astype(BF16))
        rb = jnp.zeros((1, LANES), F32).at[0, :N_EXPERTS].set(router_b[l])
        x1, h2, top_idx, gates = _out_proj(y_m, outs, lses, a_norm_g[l].reshape(1, -1), w_out[l].astype(BF16),
                                           x2, g1, sc2, sh2, ln1_g[l].reshape(1, -1), ln1_b[l].reshape(1, -1),
                                           rw, rb, alpha, seq)

        (tok_pad, dest, item_e, item_row0, item_nch, item_valid, item_tail,
         n_rows) = _routing_tables(top_idx[:, :TOP_K])
        y_pad = _experts(h2, tok_pad, item_e, item_row0, item_nch, item_valid, item_tail,
                         w_gu[l], b_gu[l], w_dn[l], b_dn[l], n_rows)
        x2 = _combine(dest, y_pad, gates, x1, g2, ln2_g[l].reshape(1, -1), ln2_b[l].reshape(1, -1), alpha, seq)
    return x2.reshape(bsz, seq, d)
```

```python
import functools

import jax
import jax.numpy as jnp
import numpy as np
from jax import lax
from jax.experimental import pallas as pl
from jax.experimental.pallas import tpu as pltpu

F32 = jnp.float32
BF16 = jnp.bfloat16
I32 = jnp.int32

M_HEADS = 4
M_DQK = 128
M_DV = 256
CONV_WIDTH = 4
A_HEADS = 16
A_DH = 64
ATTN_BLOCK = 128
DILATIONS = (1, 4, 16)
N_EXPERTS = 32
TOP_K = 4
SWIGLU_ALPHA = 1.702
SWIGLU_LIMIT = 7.0
EPS = 1e-5

LANES = 128
VMEM_LIMIT = 56 * 1024 * 1024

MLSTM_CHUNK = 256
MOE_ROW_PAD = 256
MOE_CHUNK = 256
MOE_ITEM_ROWS = 1280
MOE_TN = 1024
BIG_VMEM_LIMIT = 60 * 1024 * 1024
COMBINE_TOKENS = 256
DMA_UNROLL = 8


def _cparams(sem, vmem=VMEM_LIMIT):
    return pltpu.CompilerParams(dimension_semantics=sem, vmem_limit_bytes=vmem)


def _sigmoid(x):
    return 1.0 / (1.0 + jnp.exp(-x))


def _log_sigmoid(x):
    return jnp.minimum(x, 0.0) - jnp.log(1.0 + jnp.exp(-jnp.abs(x)))


def _layer_norm(z, g, b):
    mu = jnp.mean(z, axis=-1, keepdims=True)
    zc = z - mu
    var = jnp.mean(zc * zc, axis=-1, keepdims=True)
    return zc * lax.rsqrt(var + EPS) * g + b


def _dot(a, b):
    return jnp.dot(a, b, preferred_element_type=F32)


def _dot_nt(a, b):
    return lax.dot_general(a, b, (((1,), (1,)), ((), ())), preferred_element_type=F32)


HIGH_HALF = np.uint32(0xFFFF0000)


def _pack_bf16_pairs(x):
    half = x.shape[1] // 2
    xb = x.astype(BF16).astype(F32)
    lo = lax.bitcast_convert_type(xb[:, :half], jnp.uint32)
    hi = lax.bitcast_convert_type(xb[:, half:], jnp.uint32)
    return jnp.bitwise_or(lax.shift_right_logical(lo, jnp.uint32(16)), jnp.bitwise_and(hi, HIGH_HALF))


def _unpack_bf16_pairs(w):
    lo = lax.bitcast_convert_type(lax.shift_left(w, jnp.uint32(16)), F32)
    hi = lax.bitcast_convert_type(jnp.bitwise_and(w, HIGH_HALF), F32)
    return lo.astype(BF16), hi.astype(BF16)


def _dot_hilo(a, sel):
    hi = a.astype(BF16)
    lo = (a - hi.astype(F32)).astype(BF16)
    return _dot(hi, sel) + _dot(lo, sel)


def _ada_kernel(c_ref, w_ref, b_ref, o_ref):
    c = c_ref[...]
    cond = c * _sigmoid(c)
    o_ref[...] = _dot(cond.astype(BF16), w_ref[...].astype(BF16)) + b_ref[...]


def _ada_mod(c, w_ada, b_ada):
    bsz, d = c.shape
    n = w_ada.shape[1]
    tn = 1024
    rows = 8
    c_pad = jnp.zeros((rows, d), F32).at[:bsz].set(c)
    out = pl.pallas_call(
        _ada_kernel,
        grid=(n // tn,),
        in_specs=[pl.BlockSpec((rows, d), lambda j: (0, 0)),
                  pl.BlockSpec((d, tn), lambda j: (0, j)),
                  pl.BlockSpec((1, tn), lambda j: (0, j))],
        out_specs=pl.BlockSpec((rows, tn), lambda j: (0, j)),
        out_shape=jax.ShapeDtypeStruct((rows, n), F32),
        compiler_params=_cparams(("arbitrary",)),
        name="ada_mod",
    )(c_pad, w_ada, b_ada.reshape(1, n))
    return out[:bsz]


def _inproj_kernel(attn_col0, x_ref, sc_ref, sh_ref, wm_ref, wa_ref, b_ref, wg_ref, bg_ref, o_ref, g_ref, *rest):
    dil_refs, (h_ref, r_scr) = rest[:-2], rest[-2:]
    j = pl.program_id(1)

    @pl.when(j == 0)
    def _():
        h = x_ref[...] * (1.0 + sc_ref[0]) + sh_ref[0]
        hb = h.astype(BF16)
        h_ref[...] = hb
        g_ref[...] = _dot(hb, wg_ref[...]) + bg_ref[...]

    @pl.when(j < attn_col0)
    def _():
        o_ref[...] = (_dot(h_ref[...], wm_ref[...]) + b_ref[...]).astype(BF16)

    @pl.when(j >= attn_col0)
    def _():
        res = _dot(h_ref[...], wa_ref[...]) + b_ref[...]
        o_ref[...] = res.astype(BF16)
        cols = res.shape[1] // LANES
        for c in range(cols):
            r_scr[c] = res[:, c * LANES:(c + 1) * LANES]
        for ref in dil_refs:
            d, n = ref.shape[1], ref.shape[2]
            for r in range(d):
                for c in range(cols):
                    ref[0, r, :, c * LANES:(c + 1) * LANES] = r_scr[c, pl.ds(r, n, stride=d), :].astype(BF16)


def _in_proj(x2, sc, sh, w_mlstm, w_attn, b_main, w_gate, b_gate, seq):
    t, d = x2.shape
    n = w_mlstm.shape[1] + w_attn.shape[1]
    tm, tn = 1024, 1024
    attn_col0 = w_mlstm.shape[1] // tn
    per_b = seq // tm
    bsz = t // seq
    dils = [dl for dl in DILATIONS if dl > 1]
    aw3 = n - attn_col0 * tn
    dil_specs = [pl.BlockSpec((1, dl, tm // dl, tn),
                              lambda i, j: (i // per_b, 0, i % per_b, jnp.maximum(j - attn_col0, 0))) for dl in dils]
    dil_shapes = [jax.ShapeDtypeStruct((bsz, dl, seq // dl, aw3), BF16) for dl in dils]
    return pl.pallas_call(
        functools.partial(_inproj_kernel, attn_col0),
        grid=(t // tm, n // tn),
        in_specs=[pl.BlockSpec((tm, d), lambda i, j: (i, 0)),
                  pl.BlockSpec((1, 1, d), lambda i, j: (i // per_b, 0, 0)),
                  pl.BlockSpec((1, 1, d), lambda i, j: (i // per_b, 0, 0)),
                  pl.BlockSpec((d, tn), lambda i, j: (0, jnp.minimum(j, attn_col0 - 1))),
                  pl.BlockSpec((d, tn), lambda i, j: (0, jnp.maximum(j - attn_col0, 0))),
                  pl.BlockSpec((1, tn), lambda i, j: (0, j)),
                  pl.BlockSpec((d, LANES), lambda i, j: (0, 0)),
                  pl.BlockSpec((1, LANES), lambda i, j: (0, 0))],
        out_specs=[pl.BlockSpec((tm, tn), lambda i, j: (i, j)),
                   pl.BlockSpec((tm, LANES), lambda i, j: (i, 0))] + dil_specs,
        out_shape=[jax.ShapeDtypeStruct((t, n), BF16),
                   jax.ShapeDtypeStruct((t, LANES), F32)] + dil_shapes,
        scratch_shapes=[pltpu.VMEM((tm, d), BF16), pltpu.VMEM((tn // LANES, tm, LANES), F32)],
        compiler_params=_cparams(("arbitrary", "arbitrary"), BIG_VMEM_LIMIT),
        name="in_proj",
    )(x2, sc, sh, w_mlstm, w_attn, b_main, w_gate, b_gate)


def _mlstm_kernel(qk_ref, v_ref, o_ref, gc_ref, gr_ref, cw_ref, cb_ref, ng_ref, out_ref,
                  ct_ref, n_ref, m_ref, prev_ref):
    c = pl.program_id(1)
    L = MLSTM_CHUNK

    @pl.when(c == 0)
    def _():
        ct_ref[...] = jnp.zeros_like(ct_ref)
        n_ref[...] = jnp.zeros_like(n_ref)
        m_ref[...] = jnp.zeros_like(m_ref)
        prev_ref[...] = jnp.zeros_like(prev_ref)

    x = qk_ref[...].astype(F32)
    prev = prev_ref[...]
    row = lax.broadcasted_iota(I32, (L, 1), 0)
    y = cw_ref[CONV_WIDTH - 1:CONV_WIDTH, :] * x + cb_ref[...]
    for k in range(1, CONV_WIDTH):
        xs = jnp.where(row < k, pltpu.roll(prev, k, 0), pltpu.roll(x, k, 0))
        y = y + cw_ref[CONV_WIDTH - 1 - k:CONV_WIDTH - k, :] * xs
    prev_ref[...] = x
    y = y * _sigmoid(y)

    gc = gc_ref[...]
    gr = gr_ref[0]
    ti = lax.broadcasted_iota(I32, (L, L), 0)
    si = lax.broadcasted_iota(I32, (L, L), 1)
    causal = si <= ti
    tril = causal.astype(F32)
    triu = (ti <= si).astype(F32)
    b_cols = jnp.dot(tril, _log_sigmoid(gc), precision=lax.Precision.HIGHEST, preferred_element_type=F32)
    b_rows = jnp.dot(_log_sigmoid(gr), triu, precision=lax.Precision.HIGHEST, preferred_element_type=F32)

    qk_w = M_HEADS * M_DQK
    heads = range(M_HEADS)
    qf = [y[:, h * M_DQK:(h + 1) * M_DQK] for h in heads]
    kf = [y[:, qk_w + h * M_DQK:qk_w + (h + 1) * M_DQK] * (M_DQK ** -0.5) for h in heads]
    qb = [t.astype(BF16) for t in qf]
    vb = [v_ref[:, h * M_DV:(h + 1) * M_DV] for h in heads]
    bc = [b_cols[:, M_HEADS + h:M_HEADS + h + 1] for h in heads]
    ic = [gc[:, h:h + 1] for h in heads]
    br = [b_rows[M_HEADS + h:M_HEADS + h + 1, :] for h in heads]
    ir = [gr[h:h + 1, :] for h in heads]
    m_prev = [m_ref[h][0:1, 0:1] for h in heads]
    n_prev = [n_ref[h][0:1, :] for h in heads]

    b_last = [bc[h][L - 1:L, :] for h in heads]
    g_col = [b_last[h] - bc[h] + ic[h] for h in heads]
    m_new = [jnp.maximum(b_last[h] + m_prev[h], jnp.max(g_col[h], axis=0, keepdims=True)) for h in heads]
    w_col = [jnp.exp(g_col[h] - m_new[h]) for h in heads]
    decay = [jnp.exp(b_last[h] + m_prev[h] - m_new[h]) for h in heads]

    qk = [_dot_nt(qb[h], kf[h].astype(BF16)) for h in heads]
    q_state = [_dot(qb[h], ct_ref[h].astype(BF16)) for h in heads]
    kv_new = [_dot(kf[h].T.astype(BF16), (w_col[h] * vb[h].astype(F32)).astype(BF16)) for h in heads]

    dm = [jnp.where(causal, bc[h] - br[h] + ir[h], -jnp.inf) for h in heads]
    inter = [bc[h] + m_prev[h] for h in heads]
    mt = [jnp.maximum(inter[h], jnp.max(dm[h], axis=1, keepdims=True)) for h in heads]
    a = [jnp.exp(dm[h] - mt[h]) * qk[h] for h in heads]
    e_int = [jnp.exp(inter[h] - mt[h]) for h in heads]
    av = [_dot(a[h].astype(BF16), vb[h]) for h in heads]

    for h in heads:
        num = av[h] + e_int[h] * q_state[h]
        den = (jnp.sum(a[h], axis=1, keepdims=True)
               + e_int[h] * jnp.sum(qf[h] * n_prev[h], axis=1, keepdims=True))
        hh = num / jnp.maximum(jnp.abs(den), jnp.exp(-mt[h]))

        ct_ref[h] = decay[h] * ct_ref[h] + kv_new[h]
        n_new = decay[h] * n_prev[h] + jnp.sum(w_col[h] * kf[h], axis=0, keepdims=True)
        n_ref[h] = jnp.broadcast_to(n_new, n_ref.shape[1:])
        m_ref[h] = jnp.broadcast_to(m_new[h], m_ref.shape[1:])

        ms = jnp.mean(hh * hh, axis=1, keepdims=True)
        og = o_ref[:, h * M_DV:(h + 1) * M_DV].astype(F32)
        yh = hh * lax.rsqrt(ms + EPS) * ng_ref[:, h * M_DV:(h + 1) * M_DV] * _sigmoid(og)
        out_ref[:, h * M_DV:(h + 1) * M_DV] = yh.astype(BF16)


def _mlstm(proj, gates_c, gates_r, conv_w, conv_b, norm_g, bsz, seq):
    L = MLSTM_CHUNK
    nc = seq // L
    t = bsz * seq
    w = M_HEADS * M_DV
    return pl.pallas_call(
        _mlstm_kernel,
        grid=(bsz, nc),
        in_specs=[pl.BlockSpec((L, w), lambda b, c: (b * nc + c, 0)),
                  pl.BlockSpec((L, w), lambda b, c: (b * nc + c, 1)),
                  pl.BlockSpec((L, w), lambda b, c: (b * nc + c, 2)),
                  pl.BlockSpec((L, LANES), lambda b, c: (b * nc + c, 0)),
                  pl.BlockSpec((1, 8, L), lambda b, c: (b, 0, c)),
                  pl.BlockSpec((CONV_WIDTH, w), lambda b, c: (0, 0)),
                  pl.BlockSpec((1, w), lambda b, c: (0, 0)),
                  pl.BlockSpec((1, w), lambda b, c: (0, 0))],
        out_specs=pl.BlockSpec((L, w), lambda b, c: (b * nc + c, 0)),
        out_shape=jax.ShapeDtypeStruct((t, w), BF16),
        scratch_shapes=[pltpu.VMEM((M_HEADS, M_DQK, M_DV), F32),
                        pltpu.VMEM((M_HEADS, 8, M_DQK), F32),
                        pltpu.VMEM((M_HEADS, 8, LANES), F32),
                        pltpu.VMEM((L, w), F32)],
        compiler_params=_cparams(("arbitrary", "arbitrary")),
        name="mlstm",
    )(proj, proj, proj, gates_c, gates_r, conv_w, conv_b, norm_g)


def _attn_kernel(dilation, has_prev, *refs):
    nq = ATTN_BLOCK
    if has_prev:
        q_ref, kc_ref, vc_ref, kp_ref, vp_ref, o_ref, lse_ref, k_all, v_all = refs
        k_all[0:nq, :] = kp_ref[0, 0]
        k_all[nq:2 * nq, :] = kc_ref[0, 0]
        v_all[0:nq, :] = vp_ref[0, 0]
        v_all[nq:2 * nq, :] = vc_ref[0, 0]
        nk = 2 * nq
    else:
        q_ref, k_all, v_all, o_ref, lse_ref = refs
        k_all, v_all = k_all.at[0, 0], v_all.at[0, 0]
        nk = nq
    n = pl.program_id(2)
    qi = lax.broadcasted_iota(I32, (nq, nk), 0)
    ki = lax.broadcasted_iota(I32, (nq, nk), 1)
    dist = qi - ki + (nk - nq)
    ok = jnp.logical_and(dist >= 0, dist <= nq)
    if has_prev:
        ok = jnp.logical_and(ok, jnp.logical_or(ki >= nq, n > 0))
    dist_f = jnp.where(ok, dist.astype(F32), jnp.inf)
    lane = lax.broadcasted_iota(I32, (nq, LANES), 1)
    left_q = lane < A_DH
    left_k = lax.broadcasted_iota(I32, (nk, LANES), 1) < A_DH
    n_pairs = A_HEADS // 2

    scores = []
    for p in range(n_pairs):
        cols = slice(p * LANES, (p + 1) * LANES)
        qp = q_ref[0, 0, :, cols] * (A_DH ** -0.5)
        kp = k_all[:, cols]
        zero = jnp.zeros_like(qp)
        scores.append(_dot_nt(jnp.where(left_q, qp, zero), kp))
        scores.append(_dot_nt(jnp.where(left_q, zero, qp), kp))
    probs, maxes = [], []
    for h in range(A_HEADS):
        coef = -(2.0 ** (-8.0 * (h + 1) / A_HEADS)) * dilation
        s = scores[h] + dist_f * coef
        m = jnp.max(s, axis=1, keepdims=True)
        probs.append(jnp.exp(s - m).astype(BF16))
        maxes.append(m)
    lse_tile = jnp.zeros((nq, LANES), F32)
    for p in range(n_pairs):
        cols = slice(p * LANES, (p + 1) * LANES)
        vp = v_all[:, cols]
        one = jnp.ones_like(vp)
        pv_e = _dot(probs[2 * p], jnp.where(left_k, vp, one))
        pv_o = _dot(probs[2 * p + 1], jnp.where(left_k, one, vp))
        num = jnp.where(left_q, pv_e, pv_o)
        den = pltpu.roll(jnp.where(left_q, pv_o, pv_e), A_DH, 1)
        o_ref[0, 0, :, cols] = (num / den).astype(BF16)
        lse_tile = jnp.where(lane == 2 * p, maxes[2 * p] + jnp.log(pv_e[:, A_DH:A_DH + 1]), lse_tile)
        lse_tile = jnp.where(lane == 2 * p + 1, maxes[2 * p + 1] + jnp.log(pv_o[:, 0:1]), lse_tile)
    lse_ref[0, 0] = lse_tile


def _attn_group(qkv, dilation, col0):
    bsz, d, ls, _ = qkv.shape
    aw = A_HEADS * A_DH
    nq = ATTN_BLOCK
    nb = ls // nq
    has_prev = nb > 1
    blk = (1, 1, nq, aw)
    in_specs = [pl.BlockSpec(blk, lambda b, r, n: (b, r, n, col0)),
                pl.BlockSpec(blk, lambda b, r, n: (b, r, n, col0 + 1)),
                pl.BlockSpec(blk, lambda b, r, n: (b, r, n, col0 + 2))]
    args = [qkv, qkv, qkv]
    if has_prev:
        in_specs += [pl.BlockSpec(blk, lambda b, r, n: (b, r, jnp.maximum(n - 1, 0), col0 + 1)),
                     pl.BlockSpec(blk, lambda b, r, n: (b, r, jnp.maximum(n - 1, 0), col0 + 2))]
        args += [qkv, qkv]
    return pl.pallas_call(
        functools.partial(_attn_kernel, dilation, has_prev),
        grid=(bsz, d, nb),
        in_specs=in_specs,
        out_specs=[pl.BlockSpec(blk, lambda b, r, n: (b, r, n, 0)),
                   pl.BlockSpec((1, 1, nq, LANES), lambda b, r, n: (b, r, n, 0))],
        out_shape=[jax.ShapeDtypeStruct((bsz, d, ls, aw), BF16),
                   jax.ShapeDtypeStruct((bsz, d, ls, LANES), F32)],
        scratch_shapes=[pltpu.VMEM((2 * nq, aw), BF16)] * 2 if has_prev else [],
        compiler_params=_cparams(("arbitrary", "arbitrary", "arbitrary")),
        name=f"dilated_attn_d{dilation}",
    )(*args)


def _head_maps(n_heads, dh):
    w = n_heads * dh
    e = np.zeros((LANES, w), np.float32)
    for h in range(n_heads):
        e[h, h * dh:(h + 1) * dh] = 1.0
    return jnp.asarray(e, BF16), jnp.asarray(e.T.copy(), BF16)


def _natural_rows(ref, scr):
    d, n, w = ref.shape[1:]
    if d == 1:
        return ref[0, 0].astype(F32)
    cols = w // LANES
    for r in range(d):
        blk = ref[0, r].astype(F32)
        for c in range(cols):
            scr[c, pl.ds(r, n, stride=d), :] = blk[:, c * LANES:(c + 1) * LANES]
    return jnp.concatenate([scr[c] for c in range(cols)], axis=1)


def _merged_heads(o_refs, l_refs, g_ref, e_ref, p_ref, o_scr, l_scr):
    l1, l2, l3 = (_natural_rows(ref, l_scr.at[g]) for g, ref in enumerate(l_refs))
    mx = jnp.maximum(jnp.maximum(l1, l2), l3)
    w1, w2, w3 = jnp.exp(l1 - mx), jnp.exp(l2 - mx), jnp.exp(l3 - mx)
    inv = 1.0 / (w1 + w2 + w3)
    e = e_ref[...]
    o = (_dot_hilo(w1 * inv, e) * _natural_rows(o_refs[0], o_scr.at[0])
         + _dot_hilo(w2 * inv, e) * _natural_rows(o_refs[1], o_scr.at[1])
         + _dot_hilo(w3 * inv, e) * _natural_rows(o_refs[2], o_scr.at[2]))
    ms = _dot_hilo(o * o, p_ref[...]) * (1.0 / A_DH)
    scale = _dot_hilo(lax.rsqrt(ms + EPS), e)
    return (o * scale * g_ref[...]).astype(BF16)


def _outproj_kernel(alpha, ym_ref, o1_ref, o2_ref, o3_ref, l1_ref, l2_ref, l3_ref, ag_ref, e_ref, p_ref,
                    w_ref, x_ref, g1_ref, sc_ref, sh_ref, lg_ref, lb_ref, rw_ref, rb_ref,
                    x1_ref, h2_ref, ti_ref, tg_ref, o_scr, l_scr):
    half = ym_ref.shape[1]
    y_a = _merged_heads((o1_ref, o2_ref, o3_ref), (l1_ref, l2_ref, l3_ref), ag_ref, e_ref, p_ref, o_scr, l_scr)
    y = _dot(ym_ref[...], w_ref[0:half, :]) + _dot(y_a, w_ref[half:2 * half, :])
    z = alpha * x_ref[...] + (1.0 + g1_ref[0]) * y
    x1 = _layer_norm(z, lg_ref[...], lb_ref[...])
    x1_ref[...] = x1
    h2 = x1 * (1.0 + sc_ref[0]) + sh_ref[0]
    h2_ref[...] = _pack_bf16_pairs(h2)
    logits = _dot(h2.astype(BF16), rw_ref[...]) + rb_ref[...]
    lane = lax.broadcasted_iota(I32, logits.shape, 1)
    lane_f = lane.astype(F32)
    work = jnp.where(lane < N_EXPERTS, logits, -jnp.inf)
    idx_tile = jnp.zeros(logits.shape, F32)
    val_tile = jnp.zeros(logits.shape, F32)
    top = None
    denom = None
    for k in range(TOP_K):
        mk = jnp.max(work, axis=1, keepdims=True)
        ik = jnp.min(jnp.where(work == mk, lane_f, float(LANES)), axis=1, keepdims=True)
        work = jnp.where(lane_f == ik, -jnp.inf, work)
        if k == 0:
            top = mk
        ek = jnp.exp(mk - top)
        denom = ek if k == 0 else denom + ek
        idx_tile = jnp.where(lane == k, ik, idx_tile)
        val_tile = jnp.where(lane == k, ek, val_tile)
    ti_ref[...] = idx_tile.astype(I32)
    tg_ref[...] = val_tile / denom


def _out_proj(y_m, outs, lses, norm_g, w_out, x2, g1, sc2, sh2, ln_g, ln_b, rw, rb, alpha, seq):
    t, d = x2.shape
    half = y_m.shape[1]
    aw = outs[0].shape[3]
    tm = 256
    per_b = seq // tm
    row = lambda i: (i, 0)
    const = lambda i: (0, 0)
    mod = lambda i: (i // per_b, 0, 0)
    expand, pool = _head_maps(A_HEADS, A_DH)

    def grouped(arr):
        dl, w = arr.shape[1], arr.shape[3]
        return pl.BlockSpec((1, dl, tm // dl, w), lambda i: (i // per_b, 0, i % per_b, 0))

    return pl.pallas_call(
        functools.partial(_outproj_kernel, alpha),
        grid=(t // tm,),
        in_specs=[pl.BlockSpec((tm, half), row)] + [grouped(a) for a in outs] + [grouped(a) for a in lses]
        + [pl.BlockSpec((1, aw), const), pl.BlockSpec((LANES, aw), const), pl.BlockSpec((aw, LANES), const),
                  pl.BlockSpec((2 * half, d), const), pl.BlockSpec((tm, d), row),
                  pl.BlockSpec((1, 1, d), mod), pl.BlockSpec((1, 1, d), mod), pl.BlockSpec((1, 1, d), mod),
                  pl.BlockSpec((1, d), const), pl.BlockSpec((1, d), const),
                  pl.BlockSpec((d, LANES), const), pl.BlockSpec((1, LANES), const)],
        out_specs=[pl.BlockSpec((tm, d), row), pl.BlockSpec((tm, d // 2), row),
                   pl.BlockSpec((tm, LANES), row), pl.BlockSpec((tm, LANES), row)],
        out_shape=[jax.ShapeDtypeStruct((t, d), F32), jax.ShapeDtypeStruct((t, d // 2), jnp.uint32),
                   jax.ShapeDtypeStruct((t, LANES), I32), jax.ShapeDtypeStruct((t, LANES), F32)],
        scratch_shapes=[pltpu.VMEM((len(outs), aw // LANES, tm, LANES), F32),
                        pltpu.VMEM((len(lses), 1, tm, LANES), F32)],
        compiler_params=_cparams(("arbitrary",)),
        name="out_proj_ln_router",
    )(y_m, *outs, *lses, norm_g, expand, pool, w_out, x2, g1, sc2, sh2, ln_g, ln_b, rw, rb)


def _expert_kernel(nj, ie_ref, ir_ref, ic_ref, iv_ref, tail_ref, tok_hbm, h_hbm, wgu_ref, bgu_ref, wdn_ref,
                   bdn_ref, sel_ref, y_hbm, tok_smem, ring, xb, yacc, wgu_b, wdn_b, gu_scr, gsem, sems):
    i = pl.program_id(0)
    j = pl.program_id(1)
    row0 = ir_ref[i]
    nch = ic_ref[i]
    ch = MOE_CHUNK
    lead = MOE_ITEM_ROWS
    ring_chunks = ring.shape[0] // ch
    per_body = ch // nj

    def chunk_slot(first_row):
        return lax.rem(lax.shift_right_logical(first_row, 8), ring_chunks)

    def load_tokens(first_row):
        rows = pl.ds(lax.shift_right_logical(first_row, 7), lead // LANES)
        cp = pltpu.make_async_copy(tok_hbm.at[rows, :], tok_smem, sems.at[0])
        cp.start()
        cp.wait()

    def gather_rows(first_row, first_idx, count):
        slot = chunk_slot(first_row)
        base = slot * ch + jnp.bitwise_and(first_row, ch - 1)
        for u in range(count):
            idx = first_idx + u
            t = tok_smem[lax.shift_right_logical(idx, 7), jnp.bitwise_and(idx, LANES - 1)]
            pltpu.make_async_copy(h_hbm.at[pl.ds(t, 1), :], ring.at[pl.ds(base + u, 1), :], gsem.at[slot]).start()

    def chunk_wait(first_row):
        slot = chunk_slot(first_row)
        rows = pl.ds(pl.multiple_of(slot * ch, ch), ch)
        pltpu.make_async_copy(h_hbm.at[pl.ds(0, ch), :], ring.at[rows, :], gsem.at[slot]).wait()
        return rows

    @pl.when(jnp.logical_and(i == 0, j == 0))
    def _():
        load_tokens(0)

        def issue(g, carry):
            gather_rows(g * DMA_UNROLL, g * DMA_UNROLL, DMA_UNROLL)
            return carry

        lax.fori_loop(0, lead // DMA_UNROLL, issue, 0)

    def out_wait(count):
        def drain(m, carry):
            pltpu.make_async_copy(yacc.at[pl.ds(0, ch), :], y_hbm.at[pl.ds(0, ch), :], sems.at[1]).wait()
            return carry

        lax.fori_loop(0, count, drain, 0)

    @pl.when(jnp.logical_and(j == 0, nch > 0))
    def _():
        load_tokens(row0 + lead)

        def cast_rows(m, carry):
            src = chunk_wait(row0 + m * ch)
            rows = pl.ds(pl.multiple_of(m * ch, ch), ch)
            half = ring.shape[1]
            xb[rows, 0:half], xb[rows, half:2 * half] = _unpack_bf16_pairs(ring[src, :])
            return carry

        lax.fori_loop(0, nch, cast_rows, 0)

        @pl.when(i > 0)
        def _():
            out_wait(ic_ref[jnp.maximum(i - 1, 0)])

        def seed(m, carry):
            yacc[pl.ds(pl.multiple_of(m * ch, ch), ch), :] = jnp.broadcast_to(bdn_ref[0], (ch, yacc.shape[1]))
            return carry

        lax.fori_loop(0, nch, seed, 0)

    @pl.when(nch > 0)
    def _():
        wgu_b[...] = wgu_ref[0].astype(BF16)
        wdn_b[...] = wdn_ref[0].astype(BF16)
        bgu = bgu_ref[0]
        last = j == nj - 1

        def prefetch(m):
            first_idx = (j * nch + m) * per_body
            gather_rows(row0 + lead + first_idx, first_idx, per_body)

        def gate_up(m):
            rows = pl.ds(pl.multiple_of(m * ch, ch), ch)
            return _dot(xb[rows, :], wgu_b[...]) + bgu

        def finish(m, gu):
            rows = pl.ds(pl.multiple_of(m * ch, ch), ch)
            glu = jnp.minimum(gu, SWIGLU_LIMIT)
            f_glu = glu * _sigmoid(SWIGLU_ALPHA * glu)
            f_lin = jnp.clip(gu, -SWIGLU_LIMIT, SWIGLU_LIMIT) + 1.0
            prod = (pltpu.roll(f_glu, 1, 1) * f_lin).astype(BF16)
            parts = [_dot(prod[:, q * 2 * LANES:(q + 1) * 2 * LANES], sel_ref[...])
                     for q in range(MOE_TN // (2 * LANES))]
            act = jnp.concatenate(parts, axis=1).astype(BF16)
            yacc[rows, :] += _dot(act, wdn_b[...])

        def step(m):
            gu = gu_scr[...]
            gu_scr[...] = gate_up(m + 1)
            finish(m, gu)
            prefetch(m)

        gu_scr[...] = gate_up(0)
        n_steps = nch - 1

        def pair(p, carry):
            step(2 * p)
            step(2 * p + 1)
            return carry

        lax.fori_loop(0, lax.shift_right_logical(n_steps, 1), pair, 0)

        @pl.when(jnp.bitwise_and(n_steps, 1) == 1)
        def _():
            step(n_steps - 1)

        finish(nch - 1, gu_scr[...])
        prefetch(nch - 1)

        @pl.when(last)
        def _():
            def issue(m, carry):
                rows = pl.ds(pl.multiple_of(m * ch, ch), ch)
                dst = pl.ds(pl.multiple_of(row0 + m * ch, ch), ch)
                pltpu.make_async_copy(yacc.at[rows, :], y_hbm.at[dst, :], sems.at[1]).start()
                return carry

            lax.fori_loop(0, nch, issue, 0)

    @pl.when(jnp.logical_and(i == pl.num_programs(0) - 1, j == nj - 1))
    def _():
        out_wait(tail_ref[1])

        def drain(m, carry):
            chunk_wait(tail_ref[0] + m * ch)
            return carry

        lax.fori_loop(0, lead // ch, drain, 0)


def _experts(h2, tok_pad, item_e, item_row0, item_nch, item_valid, item_tail, w_gu, b_gu, w_dn, b_dn, n_rows):
    ne, d, two_de = w_gu.shape
    tn = MOE_TN
    nj = two_de // tn
    assert nj > 1, "the kernel separates its first and last column-tile steps"
    assert MOE_CHUNK % nj == 0 and (MOE_CHUNK // nj) % 8 == 0, "row copies per chunk stage"
    ni = item_e.shape[0]
    ring_rows = MOE_ITEM_ROWS
    sel = np.zeros((2 * LANES, LANES), np.float32)
    sel[2 * np.arange(LANES) + 1, np.arange(LANES)] = 1.0

    def jmap(i, j, iv):
        return jnp.where(iv[i] > 0, j, nj - 1)

    grid_spec = pltpu.PrefetchScalarGridSpec(
        num_scalar_prefetch=5,
        grid=(ni, nj),
        in_specs=[pl.BlockSpec(memory_space=pl.ANY),
                  pl.BlockSpec(memory_space=pl.ANY),
                  pl.BlockSpec((1, d, tn), lambda i, j, ie, ir, ic, iv, it: (ie[i], 0, jmap(i, j, iv))),
                  pl.BlockSpec((1, 1, tn), lambda i, j, ie, ir, ic, iv, it: (ie[i], 0, jmap(i, j, iv))),
                  pl.BlockSpec((1, tn // 2, d), lambda i, j, ie, ir, ic, iv, it: (ie[i], jmap(i, j, iv), 0)),
                  pl.BlockSpec((1, 1, d), lambda i, j, ie, ir, ic, iv, it: (ie[i], 0, 0)),
                  pl.BlockSpec((2 * LANES, LANES), lambda i, j, ie, ir, ic, iv, it: (0, 0))],
        out_specs=pl.BlockSpec(memory_space=pl.ANY),
        scratch_shapes=[pltpu.SMEM((MOE_ITEM_ROWS // LANES, LANES), I32),
                        pltpu.VMEM((ring_rows, d // 2), jnp.uint32),
                        pltpu.VMEM((MOE_ITEM_ROWS, d), BF16),
                        pltpu.VMEM((MOE_ITEM_ROWS, d), F32),
                        pltpu.VMEM((d, tn), BF16),
                        pltpu.VMEM((tn // 2, d), BF16),
                        pltpu.VMEM((MOE_CHUNK, tn), F32),
                        pltpu.SemaphoreType.DMA((ring_rows // MOE_CHUNK,)),
                        pltpu.SemaphoreType.DMA((2,))],
    )
    return pl.pallas_call(
        functools.partial(_expert_kernel, nj),
        grid_spec=grid_spec,
        out_shape=jax.ShapeDtypeStruct((n_rows, d), F32),
        compiler_params=_cparams(("arbitrary", "arbitrary"), BIG_VMEM_LIMIT),
        name="moe_experts",
    )(item_e, item_row0, item_nch, item_valid, item_tail, tok_pad, h2, w_gu, b_gu.reshape(ne, 1, two_de),
      w_dn, b_dn.reshape(ne, 1, d), jnp.asarray(sel, BF16))


def _combine_kernel(alpha, dest_ref, next_ref, y_hbm, gate_ref, x1_ref, g2_ref, lg_ref, lb_ref, out_ref, buf, sem):
    tc = COMBINE_TOKENS
    i = pl.program_id(0)
    group = 8

    def gather(idx_ref, slot):
        def issue(g, carry):
            first = pl.multiple_of(g * group, group)
            for dr in range(group):
                for k in range(TOP_K):
                    src = idx_ref[0, 0, (first + dr) * TOP_K + k]
                    pltpu.make_async_copy(y_hbm.at[pl.ds(src, 1), :], buf.at[slot, k, pl.ds(first + dr, 1), :],
                                          sem.at[slot]).start()
            return carry

        lax.fori_loop(0, tc // group, issue, 0)

    cur = jnp.bitwise_and(i, 1)

    @pl.when(i == 0)
    def _():
        gather(dest_ref, 0)

    @pl.when(i + 1 < pl.num_programs(0))
    def _():
        gather(next_ref, 1 - cur)

    for k in range(TOP_K):
        pltpu.make_async_copy(y_hbm.at[pl.ds(0, tc), :], buf.at[cur, k], sem.at[cur]).wait()

    gates = gate_ref[...]
    y = gates[:, 0:1] * buf[cur, 0]
    for k in range(1, TOP_K):
        y = y + gates[:, k:k + 1] * buf[cur, k]
    z = alpha * x1_ref[...] + (1.0 + g2_ref[0]) * y
    out_ref[...] = _layer_norm(z, lg_ref[...], lb_ref[...])


def _combine(dest, y_pad, gates, x1, g2, ln_g, ln_b, alpha, seq):
    t, d = x1.shape
    tc = COMBINE_TOKENS
    per_b = seq // tc
    row = lambda i: (i, 0)
    const = lambda i: (0, 0)
    n = t // tc
    dest3 = dest.reshape(n, 1, tc * TOP_K)
    return pl.pallas_call(
        functools.partial(_combine_kernel, alpha),
        grid=(n,),
        in_specs=[pl.BlockSpec((1, 1, tc * TOP_K), lambda i: (i, 0, 0), memory_space=pltpu.SMEM),
                  pl.BlockSpec((1, 1, tc * TOP_K), lambda i: (jnp.minimum(i + 1, n - 1), 0, 0),
                               memory_space=pltpu.SMEM),
                  pl.BlockSpec(memory_space=pl.ANY),
                  pl.BlockSpec((tc, LANES), row), pl.BlockSpec((tc, d), row),
                  pl.BlockSpec((1, 1, d), lambda i: (i // per_b, 0, 0)),
                  pl.BlockSpec((1, d), const), pl.BlockSpec((1, d), const)],
        out_specs=pl.BlockSpec((tc, d), row),
        out_shape=jax.ShapeDtypeStruct((t, d), F32),
        scratch_shapes=[pltpu.VMEM((2, TOP_K, tc, d), F32), pltpu.SemaphoreType.DMA((2,))],
        compiler_params=_cparams(("arbitrary",)),
        name="moe_combine_ln",
    )(dest3, dest3, y_pad, gates, x1, g2, ln_g, ln_b)


def _count_le(ends, q):
    return jnp.sum((ends[None, :] <= q[:, None]).astype(I32), axis=1)


def _lookup(table, idx):
    hit = idx[:, None] == jnp.arange(table.shape[0], dtype=I32)
    return jnp.sum(jnp.where(hit, table[None, :], 0), axis=1)


def _routing_tables(top_idx):
    t = top_idx.shape[0]
    tk = t * TOP_K
    experts = jnp.arange(N_EXPERTS, dtype=I32)
    hits = [top_idx[:, k:k + 1] == experts for k in range(TOP_K)]
    onehot = sum(h.astype(I32) for h in hits)
    csum = jnp.cumsum(onehot, axis=0)
    counts = csum[-1]
    earlier = csum - onehot
    starts = jnp.cumsum(counts) - counts
    padded = ((counts + MOE_ROW_PAD - 1) // MOE_ROW_PAD) * MOE_ROW_PAD
    pad_end = jnp.cumsum(padded)
    pad_start = pad_end - padded
    dest = jnp.stack([jnp.sum(jnp.where(h, earlier + pad_start, 0), axis=1) for h in hits], axis=1).reshape(tk)

    n_rows = tk + N_EXPERTS * MOE_ROW_PAD
    n_tab = n_rows + 2 * MOE_ITEM_ROWS
    tok_sorted = (jnp.argsort(top_idx.reshape(tk)) // TOP_K).astype(I32)
    blk = jnp.arange(n_tab // MOE_ROW_PAD, dtype=I32) * MOE_ROW_PAD
    blk_e = jnp.minimum(_count_le(pad_end, blk), N_EXPERTS - 1)
    local = (blk - _lookup(pad_start, blk_e))[:, None] + jnp.arange(MOE_ROW_PAD, dtype=I32)
    src = jnp.clip(_lookup(starts, blk_e)[:, None] + local, 0, tk - 1)
    live = jnp.logical_and(local < _lookup(counts, blk_e)[:, None], (blk < pad_end[-1])[:, None])
    tok_pad = jnp.where(live, tok_sorted[src], 0).reshape(-1, LANES)

    items_per = (padded + MOE_ITEM_ROWS - 1) // MOE_ITEM_ROWS
    item_end = jnp.cumsum(items_per)
    item_start = item_end - items_per
    n_items = N_EXPERTS + n_rows // MOE_ITEM_ROWS
    idx = jnp.arange(n_items, dtype=I32)
    valid = idx < item_end[-1]
    e_i = jnp.minimum(_count_le(item_end, idx), N_EXPERTS - 1)
    k_i = idx - _lookup(item_start, e_i)
    row0 = _lookup(pad_start, e_i) + k_i * MOE_ITEM_ROWS
    nrows = jnp.clip(_lookup(padded, e_i) - k_i * MOE_ITEM_ROWS, 0, MOE_ITEM_ROWS)
    e_last = jnp.sum(jnp.where(idx == item_end[-1] - 1, e_i, 0))
    item_e = jnp.where(valid, e_i, e_last).astype(I32)
    item_row0 = jnp.where(valid, row0, 0).astype(I32)
    item_nch = jnp.where(valid, nrows // MOE_CHUNK, 0).astype(I32)
    item_tail = jnp.stack([pad_end[-1], jnp.sum(jnp.where(idx == item_end[-1] - 1, item_nch, 0))]).astype(I32)
    return tok_pad, dest, item_e, item_row0, item_nch, valid.astype(I32), item_tail, n_rows


def kernel(x, c, w_ada, b_ada, w_in, b_in, conv_w, conv_b, m_norm_g, a_norm_g, w_out, ln1_g, ln1_b,
           router_w, router_b, w_gu, b_gu, w_dn, b_dn, ln2_g, ln2_b):
    bsz, seq, d = x.shape
    depth = w_ada.shape[0]
    t = bsz * seq
    alpha = float((2 * depth) ** 0.25)
    qk_w = 2 * M_HEADS * M_DQK
    mv_w = M_HEADS * M_DV
    aw = A_HEADS * A_DH
    gate_lo = qk_w + 2 * mv_w
    gate_hi = gate_lo + 2 * M_HEADS

    x2 = x.reshape(t, d)
    for l in range(depth):
        mod = _ada_mod(c, w_ada[l], b_ada[l]).reshape(bsz, 6, 1, d)
        sh1, sc1, g1, sh2, sc2, g2 = (mod[:, i] for i in range(6))

        w_mlstm = w_in[l][:, :gate_lo].astype(BF16)
        w_attn = w_in[l][:, gate_hi:].astype(BF16)
        b_main = jnp.concatenate([b_in[l][:gate_lo], b_in[l][gate_hi:]]).reshape(1, -1)
        w_gate = jnp.zeros((d, LANES), BF16).at[:, :2 * M_HEADS].set(w_in[l][:, gate_lo:gate_hi].astype(BF16))
        b_gate = jnp.zeros((1, LANES), F32).at[0, :2 * M_HEADS].set(b_in[l][gate_lo:gate_hi])
        attn_col0 = gate_lo // aw
        proj, gates_c, *qkv_dil = _in_proj(x2, sc1, sh1, w_mlstm, w_attn, b_main, w_gate, b_gate, seq)

        gates_r = gates_c[:, :2 * M_HEADS].reshape(bsz, seq, 2 * M_HEADS).transpose(0, 2, 1)
        y_m = _mlstm(proj, gates_c, gates_r, conv_w[l], conv_b[l].reshape(1, -1),
                     m_norm_g[l].reshape(1, -1), bsz, seq)

        outs, lses = [], []
        for dil in DILATIONS:
            if dil == 1:
                o_d, lse_d = _attn_group(proj.reshape(bsz, 1, seq, -1), dil, attn_col0)
            else:
                o_d, lse_d = _attn_group(qkv_dil[DILATIONS.index(dil) - 1], dil, 0)
            outs.append(o_d)
            lses.append(lse_d)

        rw = jnp.zeros((d, LANES), BF16).at[:, :N_EXPERTS].set(router_w[l].astype(BF16))
        rb = jnp.zeros((1, LANES), F32).at[0, :N_EXPERTS].set(router_b[l])
        x1, h2, top_idx, gates = _out_proj(y_m, outs, lses, a_norm_g[l].reshape(1, -1), w_out[l].astype(BF16),
                                           x2, g1, sc2, sh2, ln1_g[l].reshape(1, -1), ln1_b[l].reshape(1, -1),
                                           rw, rb, alpha, seq)

        (tok_pad, dest, item_e, item_row0, item_nch, item_valid, item_tail,
         n_rows) = _routing_tables(top_idx[:, :TOP_K])
        y_pad = _experts(h2, tok_pad, item_e, item_row0, item_nch, item_valid, item_tail,
                         w_gu[l], b_gu[l], w_dn[l], b_dn[l], n_rows)
        x2 = _combine(dest, y_pad, gates, x1, g2, ln2_g[l].reshape(1, -1), ln2_b[l].reshape(1, -1), alpha, seq)
    return x2.reshape(bsz, seq, d)
```

```python
import functools

import jax
import jax.numpy as jnp
import numpy as np
from jax import lax
from jax.experimental import pallas as pl
from jax.experimental.pallas import tpu as pltpu

F32 = jnp.float32
BF16 = jnp.bfloat16
I32 = jnp.int32

M_HEADS = 4
M_DQK = 128
M_DV = 256
CONV_WIDTH = 4
A_HEADS = 16
A_DH = 64
ATTN_BLOCK = 128
DILATIONS = (1, 4, 16)
N_EXPERTS = 32
TOP_K = 4
SWIGLU_ALPHA = 1.702
SWIGLU_LIMIT = 7.0
EPS = 1e-5

LANES = 128
VMEM_LIMIT = 56 * 1024 * 1024

MLSTM_CHUNK = 256
MOE_ROW_PAD = 256
MOE_CHUNK = 256
MOE_ITEM_ROWS = 1280
MOE_TN = 1024
BIG_VMEM_LIMIT = 60 * 1024 * 1024
COMBINE_TOKENS = 256
DMA_UNROLL = 8


def _cparams(sem, vmem=VMEM_LIMIT):
    return pltpu.CompilerParams(dimension_semantics=sem, vmem_limit_bytes=vmem)


def _sigmoid(x):
    return 1.0 / (1.0 + jnp.exp(-x))


def _log_sigmoid(x):
    return jnp.minimum(x, 0.0) - jnp.log(1.0 + jnp.exp(-jnp.abs(x)))


def _layer_norm(z, g, b):
    mu = jnp.mean(z, axis=-1, keepdims=True)
    zc = z - mu
    var = jnp.mean(zc * zc, axis=-1, keepdims=True)
    return zc * lax.rsqrt(var + EPS) * g + b


def _dot(a, b):
    return jnp.dot(a, b, preferred_element_type=F32)


def _dot_nt(a, b):
    return lax.dot_general(a, b, (((1,), (1,)), ((), ())), preferred_element_type=F32)


HIGH_HALF = np.uint32(0xFFFF0000)


def _pack_bf16_pairs(x):
    half = x.shape[1] // 2
    xb = x.astype(BF16).astype(F32)
    lo = lax.bitcast_convert_type(xb[:, :half], jnp.uint32)
    hi = lax.bitcast_convert_type(xb[:, half:], jnp.uint32)
    return jnp.bitwise_or(lax.shift_right_logical(lo, jnp.uint32(16)), jnp.bitwise_and(hi, HIGH_HALF))


def _unpack_bf16_pairs(w):
    lo = lax.bitcast_convert_type(lax.shift_left(w, jnp.uint32(16)), F32)
    hi = lax.bitcast_convert_type(jnp.bitwise_and(w, HIGH_HALF), F32)
    return lo.astype(BF16), hi.astype(BF16)


def _dot_hilo(a, sel):
    hi = a.astype(BF16)
    lo = (a - hi.astype(F32)).astype(BF16)
    return _dot(hi, sel) + _dot(lo, sel)


def _ada_kernel(c_ref, w_ref, b_ref, o_ref):
    c = c_ref[...]
    cond = c * _sigmoid(c)
    o_ref[...] = _dot(cond.astype(BF16), w_ref[...].astype(BF16)) + b_ref[...]


def _ada_mod(c, w_ada, b_ada):
    bsz, d = c.shape
    n = w_ada.shape[1]
    tn = 1024
    rows = 8
    c_pad = jnp.zeros((rows, d), F32).at[:bsz].set(c)
    out = pl.pallas_call(
        _ada_kernel,
        grid=(n // tn,),
        in_specs=[pl.BlockSpec((rows, d), lambda j: (0, 0)),
                  pl.BlockSpec((d, tn), lambda j: (0, j)),
                  pl.BlockSpec((1, tn), lambda j: (0, j))],
        out_specs=pl.BlockSpec((rows, tn), lambda j: (0, j)),
        out_shape=jax.ShapeDtypeStruct((rows, n), F32),
        compiler_params=_cparams(("arbitrary",)),
        name="ada_mod",
    )(c_pad, w_ada, b_ada.reshape(1, n))
    return out[:bsz]


def _inproj_kernel(attn_col0, x_ref, sc_ref, sh_ref, wm_ref, wa_ref, b_ref, wg_ref, bg_ref, o_ref, g_ref, *rest):
    dil_refs, (h_ref, r_scr) = rest[:-2], rest[-2:]
    j = pl.program_id(1)

    @pl.when(j == 0)
    def _():
        h = x_ref[...] * (1.0 + sc_ref[0]) + sh_ref[0]
        hb = h.astype(BF16)
        h_ref[...] = hb
        g_ref[...] = _dot(hb, wg_ref[...]) + bg_ref[...]

    @pl.when(j < attn_col0)
    def _():
        o_ref[...] = (_dot(h_ref[...], wm_ref[...]) + b_ref[...]).astype(BF16)

    @pl.when(j >= attn_col0)
    def _():
        res = _dot(h_ref[...], wa_ref[...]) + b_ref[...]
        o_ref[...] = res.astype(BF16)
        cols = res.shape[1] // LANES
        for c in range(cols):
            r_scr[c] = res[:, c * LANES:(c + 1) * LANES]
        for ref in dil_refs:
            d, n = ref.shape[1], ref.shape[2]
            for r in range(d):
                for c in range(cols):
                    ref[0, r, :, c * LANES:(c + 1) * LANES] = r_scr[c, pl.ds(r, n, stride=d), :].astype(BF16)


def _in_proj(x2, sc, sh, w_mlstm, w_attn, b_main, w_gate, b_gate, seq):
    t, d = x2.shape
    n = w_mlstm.shape[1] + w_attn.shape[1]
    tm, tn = 1024, 1024
    attn_col0 = w_mlstm.shape[1] // tn
    per_b = seq // tm
    bsz = t // seq
    dils = [dl for dl in DILATIONS if dl > 1]
    aw3 = n - attn_col0 * tn
    dil_specs = [pl.BlockSpec((1, dl, tm // dl, tn),
                              lambda i, j: (i // per_b, 0, i % per_b, jnp.maximum(j - attn_col0, 0))) for dl in dils]
    dil_shapes = [jax.ShapeDtypeStruct((bsz, dl, seq // dl, aw3), BF16) for dl in dils]
    return pl.pallas_call(
        functools.partial(_inproj_kernel, attn_col0),
        grid=(t // tm, n // tn),
        in_specs=[pl.BlockSpec((tm, d), lambda i, j: (i, 0)),
                  pl.BlockSpec((1, 1, d), lambda i, j: (i // per_b, 0, 0)),
                  pl.BlockSpec((1, 1, d), lambda i, j: (i // per_b, 0, 0)),
                  pl.BlockSpec((d, tn), lambda i, j: (0, jnp.minimum(j, attn_col0 - 1))),
                  pl.BlockSpec((d, tn), lambda i, j: (0, jnp.maximum(j - attn_col0, 0))),
                  pl.BlockSpec((1, tn), lambda i, j: (0, j)),
                  pl.BlockSpec((d, LANES), lambda i, j: (0, 0)),
                  pl.BlockSpec((1, LANES), lambda i, j: (0, 0))],
        out_specs=[pl.BlockSpec((tm, tn), lambda i, j: (i, j)),
                   pl.BlockSpec((tm, LANES), lambda i, j: (i, 0))] + dil_specs,
        out_shape=[jax.ShapeDtypeStruct((t, n), BF16),
                   jax.ShapeDtypeStruct((t, LANES), F32)] + dil_shapes,
        scratch_shapes=[pltpu.VMEM((tm, d), BF16), pltpu.VMEM((tn // LANES, tm, LANES), F32)],
        compiler_params=_cparams(("arbitrary", "arbitrary"), BIG_VMEM_LIMIT),
        name="in_proj",
    )(x2, sc, sh, w_mlstm, w_attn, b_main, w_gate, b_gate)


def _mlstm_kernel(qk_ref, v_ref, o_ref, gc_ref, gr_ref, cw_ref, cb_ref, ng_ref, out_ref,
                  ct_ref, n_ref, m_ref, prev_ref):
    c = pl.program_id(1)
    L = MLSTM_CHUNK

    @pl.when(c == 0)
    def _():
        ct_ref[...] = jnp.zeros_like(ct_ref)
        n_ref[...] = jnp.zeros_like(n_ref)
        m_ref[...] = jnp.zeros_like(m_ref)
        prev_ref[...] = jnp.zeros_like(prev_ref)

    x = qk_ref[...].astype(F32)
    prev = prev_ref[...]
    row = lax.broadcasted_iota(I32, (L, 1), 0)
    y = cw_ref[CONV_WIDTH - 1:CONV_WIDTH, :] * x + cb_ref[...]
    for k in range(1, CONV_WIDTH):
        xs = jnp.where(row < k, pltpu.roll(prev, k, 0), pltpu.roll(x, k, 0))
        y = y + cw_ref[CONV_WIDTH - 1 - k:CONV_WIDTH - k, :] * xs
    prev_ref[...] = x
    y = y * _sigmoid(y)

    gc = gc_ref[...]
    gr = gr_ref[0]
    ti = lax.broadcasted_iota(I32, (L, L), 0)
    si = lax.broadcasted_iota(I32, (L, L), 1)
    causal = si <= ti
    tril = causal.astype(F32)
    triu = (ti <= si).astype(F32)
    b_cols = jnp.dot(tril, _log_sigmoid(gc), precision=lax.Precision.HIGHEST, preferred_element_type=F32)
    b_rows = jnp.dot(_log_sigmoid(gr), triu, precision=lax.Precision.HIGHEST, preferred_element_type=F32)

    qk_w = M_HEADS * M_DQK
    heads = range(M_HEADS)
    qf = [y[:, h * M_DQK:(h + 1) * M_DQK] for h in heads]
    kf = [y[:, qk_w + h * M_DQK:qk_w + (h + 1) * M_DQK] * (M_DQK ** -0.5) for h in heads]
    qb = [t.astype(BF16) for t in qf]
    vb = [v_ref[:, h * M_DV:(h + 1) * M_DV] for h in heads]
    bc = [b_cols[:, M_HEADS + h:M_HEADS + h + 1] for h in heads]
    ic = [gc[:, h:h + 1] for h in heads]
    br = [b_rows[M_HEADS + h:M_HEADS + h + 1, :] for h in heads]
    ir = [gr[h:h + 1, :] for h in heads]
    m_prev = [m_ref[h][0:1, 0:1] for h in heads]
    n_prev = [n_ref[h][0:1, :] for h in heads]

    b_last = [bc[h][L - 1:L, :] for h in heads]
    g_col = [b_last[h] - bc[h] + ic[h] for h in heads]
    m_new = [jnp.maximum(b_last[h] + m_prev[h], jnp.max(g_col[h], axis=0, keepdims=True)) for h in heads]
    w_col = [jnp.exp(g_col[h] - m_new[h]) for h in heads]
    decay = [jnp.exp(b_last[h] + m_prev[h] - m_new[h]) for h in heads]

    qk = [_dot_nt(qb[h], kf[h].astype(BF16)) for h in heads]
    q_state = [_dot(qb[h], ct_ref[h].astype(BF16)) for h in heads]
    kv_new = [_dot(kf[h].T.astype(BF16), (w_col[h] * vb[h].astype(F32)).astype(BF16)) for h in heads]

    dm = [jnp.where(causal, bc[h] - br[h] + ir[h], -jnp.inf) for h in heads]
    inter = [bc[h] + m_prev[h] for h in heads]
    mt = [jnp.maximum(inter[h], jnp.max(dm[h], axis=1, keepdims=True)) for h in heads]
    a = [jnp.exp(dm[h] - mt[h]) * qk[h] for h in heads]
    e_int = [jnp.exp(inter[h] - mt[h]) for h in heads]
    av = [_dot(a[h].astype(BF16), vb[h]) for h in heads]

    for h in heads:
        num = av[h] + e_int[h] * q_state[h]
        den = (jnp.sum(a[h], axis=1, keepdims=True)
               + e_int[h] * jnp.sum(qf[h] * n_prev[h], axis=1, keepdims=True))
        hh = num / jnp.maximum(jnp.abs(den), jnp.exp(-mt[h]))

        ct_ref[h] = decay[h] * ct_ref[h] + kv_new[h]
        n_new = decay[h] * n_prev[h] + jnp.sum(w_col[h] * kf[h], axis=0, keepdims=True)
        n_ref[h] = jnp.broadcast_to(n_new, n_ref.shape[1:])
        m_ref[h] = jnp.broadcast_to(m_new[h], m_ref.shape[1:])

        ms = jnp.mean(hh * hh, axis=1, keepdims=True)
        og = o_ref[:, h * M_DV:(h + 1) * M_DV].astype(F32)
        yh = hh * lax.rsqrt(ms + EPS) * ng_ref[:, h * M_DV:(h + 1) * M_DV] * _sigmoid(og)
        out_ref[:, h * M_DV:(h + 1) * M_DV] = yh.astype(BF16)


def _mlstm(proj, gates_c, gates_r, conv_w, conv_b, norm_g, bsz, seq):
    L = MLSTM_CHUNK
    nc = seq // L
    t = bsz * seq
    w = M_HEADS * M_DV
    return pl.pallas_call(
        _mlstm_kernel,
        grid=(bsz, nc),
        in_specs=[pl.BlockSpec((L, w), lambda b, c: (b * nc + c, 0)),
                  pl.BlockSpec((L, w), lambda b, c: (b * nc + c, 1)),
                  pl.BlockSpec((L, w), lambda b, c: (b * nc + c, 2)),
                  pl.BlockSpec((L, LANES), lambda b, c: (b * nc + c, 0)),
                  pl.BlockSpec((1, 8, L), lambda b, c: (b, 0, c)),
                  pl.BlockSpec((CONV_WIDTH, w), lambda b, c: (0, 0)),
                  pl.BlockSpec((1, w), lambda b, c: (0, 0)),
                  pl.BlockSpec((1, w), lambda b, c: (0, 0))],
        out_specs=pl.BlockSpec((L, w), lambda b, c: (b * nc + c, 0)),
        out_shape=jax.ShapeDtypeStruct((t, w), BF16),
        scratch_shapes=[pltpu.VMEM((M_HEADS, M_DQK, M_DV), F32),
                        pltpu.VMEM((M_HEADS, 8, M_DQK), F32),
                        pltpu.VMEM((M_HEADS, 8, LANES), F32),
                        pltpu.VMEM((L, w), F32)],
        compiler_params=_cparams(("arbitrary", "arbitrary")),
        name="mlstm",
    )(proj, proj, proj, gates_c, gates_r, conv_w, conv_b, norm_g)


def _attn_kernel(dilation, has_prev, *refs):
    nq = ATTN_BLOCK
    if has_prev:
        q_ref, kc_ref, vc_ref, kp_ref, vp_ref, o_ref, lse_ref, k_all, v_all = refs
        k_all[0:nq, :] = kp_ref[0, 0]
        k_all[nq:2 * nq, :] = kc_ref[0, 0]
        v_all[0:nq, :] = vp_ref[0, 0]
        v_all[nq:2 * nq, :] = vc_ref[0, 0]
        nk = 2 * nq
    else:
        q_ref, k_all, v_all, o_ref, lse_ref = refs
        k_all, v_all = k_all.at[0, 0], v_all.at[0, 0]
        nk = nq
    n = pl.program_id(2)
    qi = lax.broadcasted_iota(I32, (nq, nk), 0)
    ki = lax.broadcasted_iota(I32, (nq, nk), 1)
    dist = qi - ki + (nk - nq)
    ok = jnp.logical_and(dist >= 0, dist <= nq)
    if has_prev:
        ok = jnp.logical_and(ok, jnp.logical_or(ki >= nq, n > 0))
    dist_f = jnp.where(ok, dist.astype(F32), jnp.inf)
    lane = lax.broadcasted_iota(I32, (nq, LANES), 1)
    left_q = lane < A_DH
    left_k = lax.broadcasted_iota(I32, (nk, LANES), 1) < A_DH
    n_pairs = A_HEADS // 2

    scores = []
    for p in range(n_pairs):
        cols = slice(p * LANES, (p + 1) * LANES)
        qp = q_ref[0, 0, :, cols] * (A_DH ** -0.5)
        kp = k_all[:, cols]
        zero = jnp.zeros_like(qp)
        scores.append(_dot_nt(jnp.where(left_q, qp, zero), kp))
        scores.append(_dot_nt(jnp.where(left_q, zero, qp), kp))
    probs, maxes = [], []
    for h in range(A_HEADS):
        coef = -(2.0 ** (-8.0 * (h + 1) / A_HEADS)) * dilation
        s = scores[h] + dist_f * coef
        m = jnp.max(s, axis=1, keepdims=True)
        probs.append(jnp.exp(s - m).astype(BF16))
        maxes.append(m)
    lse_tile = jnp.zeros((nq, LANES), F32)
    for p in range(n_pairs):
        cols = slice(p * LANES, (p + 1) * LANES)
        vp = v_all[:, cols]
        one = jnp.ones_like(vp)
        pv_e = _dot(probs[2 * p], jnp.where(left_k, vp, one))
        pv_o = _dot(probs[2 * p + 1], jnp.where(left_k, one, vp))
        num = jnp.where(left_q, pv_e, pv_o)
        den = pltpu.roll(jnp.where(left_q, pv_o, pv_e), A_DH, 1)
        o_ref[0, 0, :, cols] = (num / den).astype(BF16)
        lse_tile = jnp.where(lane == 2 * p, maxes[2 * p] + jnp.log(pv_e[:, A_DH:A_DH + 1]), lse_tile)
        lse_tile = jnp.where(lane == 2 * p + 1, maxes[2 * p + 1] + jnp.log(pv_o[:, 0:1]), lse_tile)
    lse_ref[0, 0] = lse_tile


def _attn_group(qkv, dilation, col0):
    bsz, d, ls, _ = qkv.shape
    aw = A_HEADS * A_DH
    nq = ATTN_BLOCK
    nb = ls // nq
    has_prev = nb > 1
    blk = (1, 1, nq, aw)
    in_specs = [pl.BlockSpec(blk, lambda b, r, n: (b, r, n, col0)),
                pl.BlockSpec(blk, lambda b, r, n: (b, r, n, col0 + 1)),
                pl.BlockSpec(blk, lambda b, r, n: (b, r, n, col0 + 2))]
    args = [qkv, qkv, qkv]
    if has_prev:
        in_specs += [pl.BlockSpec(blk, lambda b, r, n: (b, r, jnp.maximum(n - 1, 0), col0 + 1)),
                     pl.BlockSpec(blk, lambda b, r, n: (b, r, jnp.maximum(n - 1, 0), col0 + 2))]
        args += [qkv, qkv]
    return pl.pallas_call(
        functools.partial(_attn_kernel, dilation, has_prev),
        grid=(bsz, d, nb),
        in_specs=in_specs,
        out_specs=[pl.BlockSpec(blk, lambda b, r, n: (b, r, n, 0)),
                   pl.BlockSpec((1, 1, nq, LANES), lambda b, r, n: (b, r, n, 0))],
        out_shape=[jax.ShapeDtypeStruct((bsz, d, ls, aw), BF16),
                   jax.ShapeDtypeStruct((bsz, d, ls, LANES), F32)],
        scratch_shapes=[pltpu.VMEM((2 * nq, aw), BF16)] * 2 if has_prev else [],
        compiler_params=_cparams(("arbitrary", "arbitrary", "arbitrary")),
        name=f"dilated_attn_d{dilation}",
    )(*args)


def _head_maps(n_heads, dh):
    w = n_heads * dh
    e = np.zeros((LANES, w), np.float32)
    for h in range(n_heads):
        e[h, h * dh:(h + 1) * dh] = 1.0
    return jnp.asarray(e, BF16), jnp.asarray(e.T.copy(), BF16)


def _natural_rows(ref, scr):
    d, n, w = ref.shape[1:]
    if d == 1:
        return ref[0, 0].astype(F32)
    cols = w // LANES
    for r in range(d):
        blk = ref[0, r].astype(F32)
        for c in range(cols):
            scr[c, pl.ds(r, n, stride=d), :] = blk[:, c * LANES:(c + 1) * LANES]
    return jnp.concatenate([scr[c] for c in range(cols)], axis=1)


def _merged_heads(o_refs, l_refs, g_ref, e_ref, p_ref, o_scr, l_scr):
    l1, l2, l3 = (_natural_rows(ref, l_scr.at[g]) for g, ref in enumerate(l_refs))
    mx = jnp.maximum(jnp.maximum(l1, l2), l3)
    w1, w2, w3 = jnp.exp(l1 - mx), jnp.exp(l2 - mx), jnp.exp(l3 - mx)
    inv = 1.0 / (w1 + w2 + w3)
    e = e_ref[...]
    o = (_dot_hilo(w1 * inv, e) * _natural_rows(o_refs[0], o_scr.at[0])
         + _dot_hilo(w2 * inv, e) * _natural_rows(o_refs[1], o_scr.at[1])
         + _dot_hilo(w3 * inv, e) * _natural_rows(o_refs[2], o_scr.at[2]))
    ms = _dot_hilo(o * o, p_ref[...]) * (1.0 / A_DH)
    scale = _dot_hilo(lax.rsqrt(ms + EPS), e)
    return (o * scale * g_ref[...]).astype(BF16)


def _outproj_kernel(alpha, ym_ref, o1_ref, o2_ref, o3_ref, l1_ref, l2_ref, l3_ref, ag_ref, e_ref, p_ref,
                    w_ref, x_ref, g1_ref, sc_ref, sh_ref, lg_ref, lb_ref, rw_ref, rb_ref,
                    x1_ref, h2_ref, ti_ref, tg_ref, o_scr, l_scr):
    half = ym_ref.shape[1]
    y_a = _merged_heads((o1_ref, o2_ref, o3_ref), (l1_ref, l2_ref, l3_ref), ag_ref, e_ref, p_ref, o_scr, l_scr)
    y = _dot(ym_ref[...], w_ref[0:half, :]) + _dot(y_a, w_ref[half:2 * half, :])
    z = alpha * x_ref[...] + (1.0 + g1_ref[0]) * y
    x1 = _layer_norm(z, lg_ref[...], lb_ref[...])
    x1_ref[...] = x1
    h2 = x1 * (1.0 + sc_ref[0]) + sh_ref[0]
    h2_ref[...] = _pack_bf16_pairs(h2)
    logits = _dot(h2.astype(BF16), rw_ref[...]) + rb_ref[...]
    lane = lax.broadcasted_iota(I32, logits.shape, 1)
    lane_f = lane.astype(F32)
    work = jnp.where(lane < N_EXPERTS, logits, -jnp.inf)
    idx_tile = jnp.zeros(logits.shape, F32)
    val_tile = jnp.zeros(logits.shape, F32)
    top = None
    denom = None
    for k in range(TOP_K):
        mk = jnp.max(work, axis=1, keepdims=True)
        ik = jnp.min(jnp.where(work == mk, lane_f, float(LANES)), axis=1, keepdims=True)
        work = jnp.where(lane_f == ik, -jnp.inf, work)
        if k == 0:
            top = mk
        ek = jnp.exp(mk - top)
        denom = ek if k == 0 else denom + ek
        idx_tile = jnp.where(lane == k, ik, idx_tile)
        val_tile = jnp.where(lane == k, ek, val_tile)
    ti_ref[...] = idx_tile.astype(I32)
    tg_ref[...] = val_tile / denom


def _out_proj(y_m, outs, lses, norm_g, w_out, x2, g1, sc2, sh2, ln_g, ln_b, rw, rb, alpha, seq):
    t, d = x2.shape
    half = y_m.shape[1]
    aw = outs[0].shape[3]
    tm = 256
    per_b = seq // tm
    row = lambda i: (i, 0)
    const = lambda i: (0, 0)
    mod = lambda i: (i // per_b, 0, 0)
    expand, pool = _head_maps(A_HEADS, A_DH)

    def grouped(arr):
        dl, w = arr.shape[1], arr.shape[3]
        return pl.BlockSpec((1, dl, tm // dl, w), lambda i: (i // per_b, 0, i % per_b, 0))

    return pl.pallas_call(
        functools.partial(_outproj_kernel, alpha),
        grid=(t // tm,),
        in_specs=[pl.BlockSpec((tm, half), row)] + [grouped(a) for a in outs] + [grouped(a) for a in lses]
        + [pl.BlockSpec((1, aw), const), pl.BlockSpec((LANES, aw), const), pl.BlockSpec((aw, LANES), const),
                  pl.BlockSpec((2 * half, d), const), pl.BlockSpec((tm, d), row),
                  pl.BlockSpec((1, 1, d), mod), pl.BlockSpec((1, 1, d), mod), pl.BlockSpec((1, 1, d), mod),
                  pl.BlockSpec((1, d), const), pl.BlockSpec((1, d), const),
                  pl.BlockSpec((d, LANES), const), pl.BlockSpec((1, LANES), const)],
        out_specs=[pl.BlockSpec((tm, d), row), pl.BlockSpec((tm, d // 2), row),
                   pl.BlockSpec((tm, LANES), row), pl.BlockSpec((tm, LANES), row)],
        out_shape=[jax.ShapeDtypeStruct((t, d), F32), jax.ShapeDtypeStruct((t, d // 2), jnp.uint32),
                   jax.ShapeDtypeStruct((t, LANES), I32), jax.ShapeDtypeStruct((t, LANES), F32)],
        scratch_shapes=[pltpu.VMEM((len(outs), aw // LANES, tm, LANES), F32),
                        pltpu.VMEM((len(lses), 1, tm, LANES), F32)],
        compiler_params=_cparams(("arbitrary",)),
        name="out_proj_ln_router",
    )(y_m, *outs, *lses, norm_g, expand, pool, w_out, x2, g1, sc2, sh2, ln_g, ln_b, rw, rb)


def _expert_kernel(nj, ie_ref, ir_ref, ic_ref, iv_ref, tail_ref, tok_hbm, h_hbm, wgu_ref, bgu_ref, wdn_ref,
                   bdn_ref, sel_ref, y_hbm, tok_smem, ring, xb, yacc, wgu_b, wdn_b, gu_scr, gsem, sems):
    i = pl.program_id(0)
    j = pl.program_id(1)
    row0 = ir_ref[i]
    nch = ic_ref[i]
    ch = MOE_CHUNK
    lead = MOE_ITEM_ROWS
    ring_chunks = ring.shape[0] // ch
    per_body = ch // nj

    def chunk_slot(first_row):
        return lax.rem(lax.shift_right_logical(first_row, 8), ring_chunks)

    def load_tokens(first_row):
        rows = pl.ds(lax.shift_right_logical(first_row, 7), lead // LANES)
        cp = pltpu.make_async_copy(tok_hbm.at[rows, :], tok_smem, sems.at[0])
        cp.start()
        cp.wait()

    def gather_rows(first_row, first_idx, count):
        slot = chunk_slot(first_row)
        base = slot * ch + jnp.bitwise_and(first_row, ch - 1)
        for u in range(count):
            idx = first_idx + u
            t = tok_smem[lax.shift_right_logical(idx, 7), jnp.bitwise_and(idx, LANES - 1)]
            pltpu.make_async_copy(h_hbm.at[pl.ds(t, 1), :], ring.at[pl.ds(base + u, 1), :], gsem.at[slot]).start()

    def chunk_wait(first_row):
        slot = chunk_slot(first_row)
        rows = pl.ds(pl.multiple_of(slot * ch, ch), ch)
        pltpu.make_async_copy(h_hbm.at[pl.ds(0, ch), :], ring.at[rows, :], gsem.at[slot]).wait()
        return rows

    @pl.when(jnp.logical_and(i == 0, j == 0))
    def _():
        load_tokens(0)

        def issue(g, carry):
            gather_rows(g * DMA_UNROLL, g * DMA_UNROLL, DMA_UNROLL)
            return carry

        lax.fori_loop(0, lead // DMA_UNROLL, issue, 0)

    def out_wait(count):
        def drain(m, carry):
            pltpu.make_async_copy(yacc.at[pl.ds(0, ch), :], y_hbm.at[pl.ds(0, ch), :], sems.at[1]).wait()
            return carry

        lax.fori_loop(0, count, drain, 0)

    @pl.when(jnp.logical_and(j == 0, nch > 0))
    def _():
        load_tokens(row0 + lead)

        def cast_rows(m, carry):
            src = chunk_wait(row0 + m * ch)
            rows = pl.ds(pl.multiple_of(m * ch, ch), ch)
            half = ring.shape[1]
            xb[rows, 0:half], xb[rows, half:2 * half] = _unpack_bf16_pairs(ring[src, :])
            return carry

        lax.fori_loop(0, nch, cast_rows, 0)

        @pl.when(i > 0)
        def _():
            out_wait(ic_ref[jnp.maximum(i - 1, 0)])

        def seed(m, carry):
            yacc[pl.ds(pl.multiple_of(m * ch, ch), ch), :] = jnp.broadcast_to(bdn_ref[0], (ch, yacc.shape[1]))
            return carry

        lax.fori_loop(0, nch, seed, 0)

    @pl.when(nch > 0)
    def _():
        wgu_b[...] = wgu_ref[0].astype(BF16)
        wdn_b[...] = wdn_ref[0].astype(BF16)
        bgu = bgu_ref[0]
        last = j == nj - 1

        def prefetch(m):
            first_idx = (j * nch + m) * per_body
            gather_rows(row0 + lead + first_idx, first_idx, per_body)

        def gate_up(m):
            rows = pl.ds(pl.multiple_of(m * ch, ch), ch)
            return _dot(xb[rows, :], wgu_b[...]) + bgu

        def finish(m, gu):
            rows = pl.ds(pl.multiple_of(m * ch, ch), ch)
            parts = []
            for q in range(MOE_TN // (2 * LANES)):
                g = gu[:, q * 2 * LANES:(q + 1) * 2 * LANES]
                glu = jnp.minimum(g, SWIGLU_LIMIT)
                f_glu = glu * _sigmoid(SWIGLU_ALPHA * glu)
                f_lin = jnp.clip(g, -SWIGLU_LIMIT, SWIGLU_LIMIT) + 1.0
                prod = (pltpu.roll(f_glu, 1, 1) * f_lin).astype(BF16)
                parts.append(_dot(prod, sel_ref[...]))
            act = jnp.concatenate(parts, axis=1).astype(BF16)
            yacc[rows, :] += _dot(act, wdn_b[...])

        def step(m):
            gu = gu_scr[...]
            gu_scr[...] = gate_up(m + 1)
            finish(m, gu)
            prefetch(m)

        gu_scr[...] = gate_up(0)
        n_steps = nch - 1

        def pair(p, carry):
            step(2 * p)
            step(2 * p + 1)
            return carry

        lax.fori_loop(0, lax.shift_right_logical(n_steps, 1), pair, 0)

        @pl.when(jnp.bitwise_and(n_steps, 1) == 1)
        def _():
            step(n_steps - 1)

        finish(nch - 1, gu_scr[...])
        prefetch(nch - 1)

        @pl.when(last)
        def _():
            def issue(m, carry):
                rows = pl.ds(pl.multiple_of(m * ch, ch), ch)
                dst = pl.ds(pl.multiple_of(row0 + m * ch, ch), ch)
                pltpu.make_async_copy(yacc.at[rows, :], y_hbm.at[dst, :], sems.at[1]).start()
                return carry

            lax.fori_loop(0, nch, issue, 0)

    @pl.when(jnp.logical_and(i == pl.num_programs(0) - 1, j == nj - 1))
    def _():
        out_wait(tail_ref[1])

        def drain(m, carry):
            chunk_wait(tail_ref[0] + m * ch)
            return carry

        lax.fori_loop(0, lead // ch, drain, 0)


def _experts(h2, tok_pad, item_e, item_row0, item_nch, item_valid, item_tail, w_gu, b_gu, w_dn, b_dn, n_rows):
    ne, d, two_de = w_gu.shape
    tn = MOE_TN
    nj = two_de // tn
    assert nj > 1, "the kernel separates its first and last column-tile steps"
    assert MOE_CHUNK % nj == 0 and (MOE_CHUNK // nj) % 8 == 0, "row copies per chunk stage"
    ni = item_e.shape[0]
    ring_rows = MOE_ITEM_ROWS
    sel = np.zeros((2 * LANES, LANES), np.float32)
    sel[2 * np.arange(LANES) + 1, np.arange(LANES)] = 1.0

    def jmap(i, j, iv):
        return jnp.where(iv[i] > 0, j, nj - 1)

    grid_spec = pltpu.PrefetchScalarGridSpec(
        num_scalar_prefetch=5,
        grid=(ni, nj),
        in_specs=[pl.BlockSpec(memory_space=pl.ANY),
                  pl.BlockSpec(memory_space=pl.ANY),
                  pl.BlockSpec((1, d, tn), lambda i, j, ie, ir, ic, iv, it: (ie[i], 0, jmap(i, j, iv))),
                  pl.BlockSpec((1, 1, tn), lambda i, j, ie, ir, ic, iv, it: (ie[i], 0, jmap(i, j, iv))),
                  pl.BlockSpec((1, tn // 2, d), lambda i, j, ie, ir, ic, iv, it: (ie[i], jmap(i, j, iv), 0)),
                  pl.BlockSpec((1, 1, d), lambda i, j, ie, ir, ic, iv, it: (ie[i], 0, 0)),
                  pl.BlockSpec((2 * LANES, LANES), lambda i, j, ie, ir, ic, iv, it: (0, 0))],
        out_specs=pl.BlockSpec(memory_space=pl.ANY),
        scratch_shapes=[pltpu.SMEM((MOE_ITEM_ROWS // LANES, LANES), I32),
                        pltpu.VMEM((ring_rows, d // 2), jnp.uint32),
                        pltpu.VMEM((MOE_ITEM_ROWS, d), BF16),
                        pltpu.VMEM((MOE_ITEM_ROWS, d), F32),
                        pltpu.VMEM((d, tn), BF16),
                        pltpu.VMEM((tn // 2, d), BF16),
                        pltpu.VMEM((MOE_CHUNK, tn), F32),
                        pltpu.SemaphoreType.DMA((ring_rows // MOE_CHUNK,)),
                        pltpu.SemaphoreType.DMA((2,))],
    )
    return pl.pallas_call(
        functools.partial(_expert_kernel, nj),
        grid_spec=grid_spec,
        out_shape=jax.ShapeDtypeStruct((n_rows, d), F32),
        compiler_params=_cparams(("arbitrary", "arbitrary"), BIG_VMEM_LIMIT),
        name="moe_experts",
    )(item_e, item_row0, item_nch, item_valid, item_tail, tok_pad, h2, w_gu, b_gu.reshape(ne, 1, two_de),
      w_dn, b_dn.reshape(ne, 1, d), jnp.asarray(sel, BF16))


def _combine_kernel(alpha, dest_ref, next_ref, y_hbm, gate_ref, x1_ref, g2_ref, lg_ref, lb_ref, out_ref, buf, sem):
    tc = COMBINE_TOKENS
    i = pl.program_id(0)
    group = 8

    def gather(idx_ref, slot):
        def issue(g, carry):
            first = pl.multiple_of(g * group, group)
            for dr in range(group):
                for k in range(TOP_K):
                    src = idx_ref[0, 0, (first + dr) * TOP_K + k]
                    pltpu.make_async_copy(y_hbm.at[pl.ds(src, 1), :], buf.at[slot, k, pl.ds(first + dr, 1), :],
                                          sem.at[slot]).start()
            return carry

        lax.fori_loop(0, tc // group, issue, 0)

    cur = jnp.bitwise_and(i, 1)

    @pl.when(i == 0)
    def _():
        gather(dest_ref, 0)

    @pl.when(i + 1 < pl.num_programs(0))
    def _():
        gather(next_ref, 1 - cur)

    for k in range(TOP_K):
        pltpu.make_async_copy(y_hbm.at[pl.ds(0, tc), :], buf.at[cur, k], sem.at[cur]).wait()

    gates = gate_ref[...]
    y = gates[:, 0:1] * buf[cur, 0]
    for k in range(1, TOP_K):
        y = y + gates[:, k:k + 1] * buf[cur, k]
    z = alpha * x1_ref[...] + (1.0 + g2_ref[0]) * y
    out_ref[...] = _layer_norm(z, lg_ref[...], lb_ref[...])


def _combine(dest, y_pad, gates, x1, g2, ln_g, ln_b, alpha, seq):
    t, d = x1.shape
    tc = COMBINE_TOKENS
    per_b = seq // tc
    row = lambda i: (i, 0)
    const = lambda i: (0, 0)
    n = t // tc
    dest3 = dest.reshape(n, 1, tc * TOP_K)
    return pl.pallas_call(
        functools.partial(_combine_kernel, alpha),
        grid=(n,),
        in_specs=[pl.BlockSpec((1, 1, tc * TOP_K), lambda i: (i, 0, 0), memory_space=pltpu.SMEM),
                  pl.BlockSpec((1, 1, tc * TOP_K), lambda i: (jnp.minimum(i + 1, n - 1), 0, 0),
                               memory_space=pltpu.SMEM),
                  pl.BlockSpec(memory_space=pl.ANY),
                  pl.BlockSpec((tc, LANES), row), pl.BlockSpec((tc, d), row),
                  pl.BlockSpec((1, 1, d), lambda i: (i // per_b, 0, 0)),
                  pl.BlockSpec((1, d), const), pl.BlockSpec((1, d), const)],
        out_specs=pl.BlockSpec((tc, d), row),
        out_shape=jax.ShapeDtypeStruct((t, d), F32),
        scratch_shapes=[pltpu.VMEM((2, TOP_K, tc, d), F32), pltpu.SemaphoreType.DMA((2,))],
        compiler_params=_cparams(("arbitrary",)),
        name="moe_combine_ln",
    )(dest3, dest3, y_pad, gates, x1, g2, ln_g, ln_b)


def _count_le(ends, q):
    return jnp.sum((ends[None, :] <= q[:, None]).astype(I32), axis=1)


def _lookup(table, idx):
    hit = idx[:, None] == jnp.arange(table.shape[0], dtype=I32)
    return jnp.sum(jnp.where(hit, table[None, :], 0), axis=1)


def _routing_tables(top_idx):
    t = top_idx.shape[0]
    tk = t * TOP_K
    experts = jnp.arange(N_EXPERTS, dtype=I32)
    hits = [top_idx[:, k:k + 1] == experts for k in range(TOP_K)]
    onehot = sum(h.astype(I32) for h in hits)
    csum = jnp.cumsum(onehot, axis=0)
    counts = csum[-1]
    earlier = csum - onehot
    starts = jnp.cumsum(counts) - counts
    padded = ((counts + MOE_ROW_PAD - 1) // MOE_ROW_PAD) * MOE_ROW_PAD
    pad_end = jnp.cumsum(padded)
    pad_start = pad_end - padded
    dest = jnp.stack([jnp.sum(jnp.where(h, earlier + pad_start, 0), axis=1) for h in hits], axis=1).reshape(tk)

    n_rows = tk + N_EXPERTS * MOE_ROW_PAD
    n_tab = n_rows + 2 * MOE_ITEM_ROWS
    tok_sorted = (jnp.argsort(top_idx.reshape(tk)) // TOP_K).astype(I32)
    blk = jnp.arange(n_tab // MOE_ROW_PAD, dtype=I32) * MOE_ROW_PAD
    blk_e = jnp.minimum(_count_le(pad_end, blk), N_EXPERTS - 1)
    local = (blk - _lookup(pad_start, blk_e))[:, None] + jnp.arange(MOE_ROW_PAD, dtype=I32)
    src = jnp.clip(_lookup(starts, blk_e)[:, None] + local, 0, tk - 1)
    live = jnp.logical_and(local < _lookup(counts, blk_e)[:, None], (blk < pad_end[-1])[:, None])
    tok_pad = jnp.where(live, tok_sorted[src], 0).reshape(-1, LANES)

    items_per = (padded + MOE_ITEM_ROWS - 1) // MOE_ITEM_ROWS
    item_end = jnp.cumsum(items_per)
    item_start = item_end - items_per
    n_items = N_EXPERTS + n_rows // MOE_ITEM_ROWS
    idx = jnp.arange(n_items, dtype=I32)
    valid = idx < item_end[-1]
    e_i = jnp.minimum(_count_le(item_end, idx), N_EXPERTS - 1)
    k_i = idx - _lookup(item_start, e_i)
    row0 = _lookup(pad_start, e_i) + k_i * MOE_ITEM_ROWS
    nrows = jnp.clip(_lookup(padded, e_i) - k_i * MOE_ITEM_ROWS, 0, MOE_ITEM_ROWS)
    e_last = jnp.sum(jnp.where(idx == item_end[-1] - 1, e_i, 0))
    item_e = jnp.where(valid, e_i, e_last).astype(I32)
    item_row0 = jnp.where(valid, row0, 0).astype(I32)
    item_nch = jnp.where(valid, nrows // MOE_CHUNK, 0).astype(I32)
    item_tail = jnp.stack([pad_end[-1], jnp.sum(jnp.where(idx == item_end[-1] - 1, item_nch, 0))]).astype(I32)
    return tok_pad, dest, item_e, item_row0, item_nch, valid.astype(I32), item_tail, n_rows


def kernel(x, c, w_ada, b_ada, w_in, b_in, conv_w, conv_b, m_norm_g, a_norm_g, w_out, ln1_g, ln1_b,
           router_w, router_b, w_gu, b_gu, w_dn, b_dn, ln2_g, ln2_b):
    bsz, seq, d = x.shape
    depth = w_ada.shape[0]
    t = bsz * seq
    alpha = float((2 * depth) ** 0.25)
    qk_w = 2 * M_HEADS * M_DQK
    mv_w = M_HEADS * M_DV
    aw = A_HEADS * A_DH
    gate_lo = qk_w + 2 * mv_w
    gate_hi = gate_lo + 2 * M_HEADS

    x2 = x.reshape(t, d)
    for l in range(depth):
        mod = _ada_mod(c, w_ada[l], b_ada[l]).reshape(bsz, 6, 1, d)
        sh1, sc1, g1, sh2, sc2, g2 = (mod[:, i] for i in range(6))

        w_mlstm = w_in[l][:, :gate_lo].astype(BF16)
        w_attn = w_in[l][:, gate_hi:].astype(BF16)
        b_main = jnp.concatenate([b_in[l][:gate_lo], b_in[l][gate_hi:]]).reshape(1, -1)
        w_gate = jnp.zeros((d, LANES), BF16).at[:, :2 * M_HEADS].set(w_in[l][:, gate_lo:gate_hi].astype(BF16))
        b_gate = jnp.zeros((1, LANES), F32).at[0, :2 * M_HEADS].set(b_in[l][gate_lo:gate_hi])
        attn_col0 = gate_lo // aw
        proj, gates_c, *qkv_dil = _in_proj(x2, sc1, sh1, w_mlstm, w_attn, b_main, w_gate, b_gate, seq)

        gates_r = gates_c[:, :2 * M_HEADS].reshape(bsz, seq, 2 * M_HEADS).transpose(0, 2, 1)
        y_m = _mlstm(proj, gates_c, gates_r, conv_w[l], conv_b[l].reshape(1, -1),
                     m_norm_g[l].reshape(1, -1), bsz, seq)

        outs, lses = [], []
        for dil in DILATIONS:
            if dil == 1:
                o_d, lse_d = _attn_group(proj.reshape(bsz, 1, seq, -1), dil, attn_col0)
            else:
                o_d, lse_d = _attn_group(qkv_dil[DILATIONS.index(dil) - 1], dil, 0)
            outs.append(o_d)
            lses.append(lse_d)

        rw = jnp.zeros((d, LANES), BF16).at[:, :N_EXPERTS].set(router_w[l].astype(BF16))
        rb = jnp.zeros((1, LANES), F32).at[0, :N_EXPERTS].set(router_b[l])
        x1, h2, top_idx, gates = _out_proj(y_m, outs, lses, a_norm_g[l].reshape(1, -1), w_out[l].astype(BF16),
                                           x2, g1, sc2, sh2, ln1_g[l].reshape(1, -1), ln1_b[l].reshape(1, -1),
                                           rw, rb, alpha, seq)

        (tok_pad, dest, item_e, item_row0, item_nch, item_valid, item_tail,
         n_rows) = _routing_tables(top_idx[:, :TOP_K])
        y_pad = _experts(h2, tok_pad, item_e, item_row0, item_nch, item_valid, item_tail,
                         w_gu[l], b_gu[l], w_dn[l], b_dn[l], n_rows)
        x2 = _combine(dest, y_pad, gates, x1, g2, ln2_g[l].reshape(1, -1), ln2_b[l].reshape(1, -1), alpha, seq)
    return x2.reshape(bsz, seq, d)
```
